```python
import math
import jax, jax.numpy as jnp
from jax import lax
import numpy as np

D_MODEL = 1024
BATCH = 32
SEQ = 2048
DEPTH = 1
DEC_BATCH = 128
DEC_SEQ = 1
PAST_LEN = 8192
PAGE_SIZE = 128

D_MIX = D_MODEL
N_ATT_HEADS = 8
ATT_HEAD_DIM = 64
N_KV_HEADS = 2
D_ATT = N_ATT_HEADS * ATT_HEAD_DIM
KV_DIM = N_KV_HEADS * ATT_HEAD_DIM
N_IDX_HEADS = 16
IDX_DIM = 64
TOPK_MAX = 256
N_GDN_HEADS = 4
GDN_HEAD_DIM = 128
D_GDN = N_GDN_HEADS * GDN_HEAD_DIM
CONV_W = 4
GDN_CHUNK = 64
N_BUCKETS = 32
MAX_DISTANCE = 128
Q_BLOCK = 128
RMS_EPS = 1e-6
NEG_BIG = -1e30
PROJ_SIZES = (D_ATT, KV_DIM, KV_DIM, D_ATT, N_IDX_HEADS * IDX_DIM, IDX_DIM, N_IDX_HEADS,
              3 * D_GDN, D_GDN, N_GDN_HEADS, N_GDN_HEADS)
D_PROJ = sum(PROJ_SIZES)

kernel_name = 'hymba_dsa_gdn_decode_step'


def _rmsnorm(x, g):
    xf = x.astype(jnp.float32)
    y = xf * lax.rsqrt(jnp.mean(xf * xf, axis=-1, keepdims=True) + RMS_EPS)
    return (y * g.astype(jnp.float32)).astype(x.dtype)


def _l2norm(x):
    return x * lax.rsqrt(jnp.sum(x * x, axis=-1, keepdims=True) + 1e-6)


def _rel_bucket(dist):
    n = jnp.maximum(dist, 0)
    max_exact = N_BUCKETS // 2
    nf = jnp.maximum(n, 1).astype(jnp.float32)
    large = max_exact + (jnp.log(nf / max_exact) / math.log(MAX_DISTANCE / max_exact)
                         * (N_BUCKETS - max_exact)).astype(jnp.int32)
    large = jnp.minimum(large, N_BUCKETS - 1)
    return jnp.where(n < max_exact, n, large)


def _mixer_inputs(x, norm_g, w_in):
    B, T, _ = x.shape
    h = _rmsnorm(x, norm_g)
    splits = np.cumsum(PROJ_SIZES)[:-1].tolist()
    q, k, v, z_a, qi, ki, wi, qkv, z_g, a_g, b_g = jnp.split(h @ w_in, splits, axis=-1)
    return (q.reshape(B, T, N_ATT_HEADS, ATT_HEAD_DIM),
            k.reshape(B, T, N_KV_HEADS, ATT_HEAD_DIM),
            v.reshape(B, T, N_KV_HEADS, ATT_HEAD_DIM),
            z_a, qi.reshape(B, T, N_IDX_HEADS, IDX_DIM), ki, wi, qkv, z_g, a_g, b_g)


def _indexer_scores(qi, wi, ki):
    s = jnp.einsum('...thd,...sd->...ths', qi.astype(jnp.float32), ki.astype(jnp.float32)) * (IDX_DIM ** -0.5)
    return jnp.einsum('...ths,...th->...ts', jax.nn.relu(s), wi.astype(jnp.float32)) * (N_IDX_HEADS ** -0.5)


def _sparse_attend(q, k_sel, v_sel, valid, bias):
    *lead, T, H, Dh = q.shape
    K = k_sel.shape[-3]
    G = H // N_KV_HEADS
    qg = q.reshape(*lead, T, N_KV_HEADS, G, Dh).astype(jnp.float32)
    logits = jnp.einsum('...tgrd,...tkgd->...tgrk', qg, k_sel.astype(jnp.float32)) * (Dh ** -0.5)
    b = jnp.moveaxis(bias.astype(jnp.float32).reshape(*lead, T, K, N_KV_HEADS, G), -3, -1)
    logits = jnp.where(valid[..., :, None, None, :], logits + b, NEG_BIG)
    p = jax.nn.softmax(logits, axis=-1)
    o = jnp.einsum('...tgrk,...tkgd->...tgrd', p, v_sel.astype(jnp.float32))
    return o.reshape(*lead, T, H * Dh)


def _dsa_prompt(q, k, v, qi, ki, wi, rel_table):
    B, S = q.shape[:2]
    topk = min(TOPK_MAX, S // 4)
    n_blocks = S // Q_BLOCK
    kpos = jnp.arange(S)

    def per_seq(args):
        q1, k1, v1, qi1, ki1, wi1 = args

        def per_block(bi):
            t0 = bi * Q_BLOCK
            qb = lax.dynamic_slice_in_dim(q1, t0, Q_BLOCK, 0)
            qib = lax.dynamic_slice_in_dim(qi1, t0, Q_BLOCK, 0)
            wib = lax.dynamic_slice_in_dim(wi1, t0, Q_BLOCK, 0)
            qpos = t0 + jnp.arange(Q_BLOCK)
            sc = _indexer_scores(qib, wib, ki1)
            sc = jnp.where(kpos[None, :] <= qpos[:, None], sc, NEG_BIG)
            _, sel = lax.top_k(sc, topk)
            valid = sel <= qpos[:, None]
            bias = rel_table[_rel_bucket(qpos[:, None] - sel)]
            return _sparse_attend(qb, k1[sel], v1[sel], valid, bias)

        return lax.map(per_block, jnp.arange(n_blocks)).reshape(S, D_ATT)

    return lax.map(per_seq, (q, k, v, qi, ki, wi))


def _dsa_sample(q, k_new, v_new, qi, ki_new, wi, cache_k, cache_v, cache_kidx, layer, page_table, rel_table):
    Bd, T = q.shape[:2]
    n_pages = page_table.shape[1]
    P = n_pages * PAGE_SIZE
    L = P + T
    topk = min(TOPK_MAX, L // 4)
    ki_past = cache_kidx[layer, page_table].reshape(Bd, P, IDX_DIM)
    ki_all = jnp.concatenate([ki_past.astype(ki_new.dtype), ki_new], axis=1)
    qpos = P + jnp.arange(T)
    kpos = jnp.arange(L)
    sc = _indexer_scores(qi, wi, ki_all)
    sc = jnp.where(kpos[None, None, :] <= qpos[None, :, None], sc, NEG_BIG)
    _, sel = lax.top_k(sc, topk)
    valid = sel <= qpos[None, :, None]
    in_past = (sel < P)[..., None, None]
    sp = jnp.minimum(sel, P - 1)
    bidx = jnp.arange(Bd)[:, None, None]
    phys = page_table[bidx, sp // PAGE_SIZE]
    off = sp % PAGE_SIZE
    sn = jnp.clip(sel - P, 0, T - 1)
    k_sel = jnp.where(in_past, cache_k[layer, phys, off].astype(k_new.dtype), k_new[bidx, sn])
    v_sel = jnp.where(in_past, cache_v[layer, phys, off].astype(v_new.dtype), v_new[bidx, sn])
    bias = rel_table[_rel_bucket(qpos[None, :, None] - sel)]
    return _sparse_attend(q, k_sel, v_sel, valid, bias)


def _causal_conv(x_ext, w):
    T = x_ext.shape[1] - (CONV_W - 1)
    out = x_ext[:, 0:T] * w[0]
    for j in range(1, CONV_W):
        out = out + x_ext[:, j:j + T] * w[j]
    return jax.nn.silu(out)


def _gdn_features(qkv_c, a_g, b_g, a_log, dt_bias):
    B, T, _ = qkv_c.shape
    q, k, v = jnp.split(qkv_c.astype(jnp.float32), 3, axis=-1)
    q = _l2norm(q.reshape(B, T, N_GDN_HEADS, GDN_HEAD_DIM)) * (GDN_HEAD_DIM ** -0.5)
    k = _l2norm(k.reshape(B, T, N_GDN_HEADS, GDN_HEAD_DIM))
    v = v.reshape(B, T, N_GDN_HEADS, GDN_HEAD_DIM)
    g = -jnp.exp(a_log.astype(jnp.float32)) * jax.nn.softplus(a_g.astype(jnp.float32) + dt_bias.astype(jnp.float32))
    beta = jax.nn.sigmoid(b_g.astype(jnp.float32))
    return q, k, v, g, beta


def _gdn_chunked(q, k, v, g, beta):
    B, T, H, Dk = q.shape
    Dv = v.shape[-1]
    C = GDN_CHUNK
    N = T // C

    def chunkify(x):
        return x.reshape(B, N, C, H, -1).transpose(0, 3, 1, 2, 4)

    qc, kc, vc = chunkify(q), chunkify(k), chunkify(v)
    gc = g.reshape(B, N, C, H).transpose(0, 3, 1, 2)
    bc = beta.reshape(B, N, C, H).transpose(0, 3, 1, 2)
    gcum = jnp.cumsum(gc, axis=-1)
    tril = jnp.tril(jnp.ones((C, C), dtype=bool))
    strict = jnp.tril(jnp.ones((C, C), dtype=bool), -1)
    diff = gcum[..., :, None] - gcum[..., None, :]
    decay = jnp.where(tril, jnp.exp(jnp.where(tril, diff, 0.0)), 0.0)
    kb = kc * bc[..., None]
    vb = vc * bc[..., None]
    a_mat = jnp.where(strict, jnp.einsum('bhncd,bhnsd->bhncs', kb, kc) * decay, 0.0)
    u = lax.linalg.triangular_solve(a_mat, vb, left_side=True, lower=True, unit_diagonal=True)
    w = lax.linalg.triangular_solve(a_mat, kb * jnp.exp(gcum)[..., None], left_side=True, lower=True, unit_diagonal=True)
    attn_qk = jnp.einsum('bhncd,bhnsd->bhncs', qc, kc) * decay
    q_dec = qc * jnp.exp(gcum)[..., None]
    k_dec = kc * jnp.exp(gcum[..., -1:] - gcum)[..., None]
    chunk_dec = jnp.exp(gcum[..., -1])

    def step(S, xs):
        q_d, k_d, u_i, w_i, a_i, dec = xs
        v_new = u_i - jnp.einsum('bhcd,bhde->bhce', w_i, S)
        o = jnp.einsum('bhcd,bhde->bhce', q_d, S) + jnp.einsum('bhcs,bhse->bhce', a_i, v_new)
        S = S * dec[..., None, None] + jnp.einsum('bhcd,bhce->bhde', k_d, v_new)
        return S, o

    xs = tuple(jnp.moveaxis(t, 2, 0) for t in (q_dec, k_dec, u, w, attn_qk, chunk_dec))
    S0 = jnp.zeros((B, H, Dk, Dv), jnp.float32)
    S_fin, o = lax.scan(step, S0, xs)
    o = o.transpose(1, 0, 3, 2, 4).reshape(B, T, H, Dv)
    return o, S_fin


def _gdn_recurrent(q, k, v, g, beta, S0):
    def step(S, xs):
        q_t, k_t, v_t, g_t, b_t = xs
        S = S * jnp.exp(g_t)[..., None, None]
        kv = jnp.einsum('bhd,bhde->bhe', k_t, S)
        delta = (v_t - kv) * b_t[..., None]
        S = S + jnp.einsum('bhd,bhe->bhde', k_t, delta)
        return S, jnp.einsum('bhd,bhde->bhe', q_t, S)

    xs = tuple(jnp.moveaxis(t, 1, 0) for t in (q, k, v, g, beta))
    S_fin, o = lax.scan(step, S0.astype(jnp.float32), xs)
    return jnp.moveaxis(o, 0, 1), S_fin


def _merge(x, att_o, z_a, gdn_o, z_g, gdn_norm_g, w_out):
    B, T = x.shape[:2]
    att = att_o * jax.nn.silu(z_a.astype(jnp.float32))
    gdn = _rmsnorm(gdn_o, gdn_norm_g).reshape(B, T, D_GDN) * jax.nn.silu(z_g.astype(jnp.float32))
    mixed = jnp.concatenate([att, gdn], axis=-1).astype(x.dtype)
    return x + mixed @ w_out


def setup_inputs(seed: int = 0) -> dict:
    key = jax.random.key(seed)
    ks = jax.random.split(key, 20)
    f32 = jnp.float32
    n_pages = PAST_LEN // PAGE_SIZE
    n_pool = (DEC_BATCH * n_pages * 5) // 4
    x_prompt = jax.random.normal(ks[0], (BATCH, SEQ, D_MODEL), f32)
    x_sample = jax.random.normal(ks[1], (DEC_BATCH, DEC_SEQ, D_MODEL), f32)
    cache_k = jax.random.normal(ks[2], (DEPTH, n_pool, PAGE_SIZE, N_KV_HEADS, ATT_HEAD_DIM), f32)
    cache_v = jax.random.normal(ks[3], (DEPTH, n_pool, PAGE_SIZE, N_KV_HEADS, ATT_HEAD_DIM), f32)
    cache_kidx = jax.random.normal(ks[4], (DEPTH, n_pool, PAGE_SIZE, IDX_DIM), f32)
    state_ssm = 0.1 * jax.random.normal(ks[5], (DEPTH, DEC_BATCH, N_GDN_HEADS, GDN_HEAD_DIM, GDN_HEAD_DIM), f32)
    state_conv = jax.random.normal(ks[6], (DEPTH, DEC_BATCH, CONV_W - 1, 3 * D_GDN), f32)
    page_table = jax.random.permutation(ks[7], n_pool)[:DEC_BATCH * n_pages].reshape(DEC_BATCH, n_pages).astype(jnp.int32)
    norm_in = 1.0 + 0.01 * jax.random.normal(ks[8], (DEPTH, D_MODEL), f32)
    w_in = jax.random.normal(ks[9], (DEPTH, D_MODEL, D_PROJ), f32) * (D_MODEL ** -0.5)
    conv_w = jax.random.normal(ks[10], (DEPTH, CONV_W, 3 * D_GDN), f32) * (CONV_W ** -0.5)
    a_log = jnp.log(jax.random.uniform(ks[11], (DEPTH, N_GDN_HEADS), f32, 1.0, 16.0))
    dt = jnp.exp(jax.random.uniform(ks[12], (DEPTH, N_GDN_HEADS), f32, math.log(1e-3), math.log(1e-1)))
    dt_bias = dt + jnp.log(-jnp.expm1(-dt))
    gdn_norm = 1.0 + 0.01 * jax.random.normal(ks[13], (DEPTH, GDN_HEAD_DIM), f32)
    w_out = jax.random.normal(ks[14], (DEPTH, D_MIX, D_MODEL), f32) * (D_MIX ** -0.5)
    rel_table = 0.5 * jax.random.normal(ks[15], (N_BUCKETS, N_ATT_HEADS), f32)
    norm_final = 1.0 + 0.01 * jax.random.normal(ks[16], (D_MODEL,), f32)
    return {'x_prompt': x_prompt, 'x_sample': x_sample, 'cache_k': cache_k, 'cache_v': cache_v,
            'cache_kidx': cache_kidx, 'state_ssm': state_ssm, 'state_conv': state_conv,
            'page_table': page_table, 'norm_in': norm_in, 'w_in': w_in, 'conv_w': conv_w,
            'a_log': a_log, 'dt_bias': dt_bias, 'gdn_norm': gdn_norm, 'w_out': w_out,
            'rel_table': rel_table, 'norm_final': norm_final}


def reference(x_prompt, x_sample, cache_k, cache_v, cache_kidx, state_ssm, state_conv, page_table,
              norm_in, w_in, conv_w, a_log, dt_bias, gdn_norm, w_out, rel_table, norm_final):
    hp, hs = x_prompt, x_sample
    kp_l, vp_l, kip_l, sp_l, cp_l = [], [], [], [], []
    ks_l, vs_l, kis_l, ss_l, cs_l = [], [], [], [], []
    for l in range(DEPTH):
        q, k, v, z_a, qi, ki, wi, qkv, z_g, a_g, b_g = _mixer_inputs(hp, norm_in[l], w_in[l])
        att = _dsa_prompt(q, k, v, qi, ki, wi, rel_table)
        qkv_c = _causal_conv(jnp.pad(qkv, ((0, 0), (CONV_W - 1, 0), (0, 0))), conv_w[l])
        gq, gk, gv, g, beta = _gdn_features(qkv_c, a_g, b_g, a_log[l], dt_bias[l])
        gdn, s_fin = _gdn_chunked(gq, gk, gv, g, beta)
        hp = _merge(hp, att, z_a, gdn, z_g, gdn_norm[l], w_out[l])
        kp_l.append(k)
        vp_l.append(v)
        kip_l.append(ki)
        sp_l.append(s_fin.astype(x_prompt.dtype))
        cp_l.append(qkv[:, -(CONV_W - 1):])
        q, k, v, z_a, qi, ki, wi, qkv, z_g, a_g, b_g = _mixer_inputs(hs, norm_in[l], w_in[l])
        att = _dsa_sample(q, k, v, qi, ki, wi, cache_k, cache_v, cache_kidx, l, page_table, rel_table)
        conv_ext = jnp.concatenate([state_conv[l].astype(qkv.dtype), qkv], axis=1)
        qkv_c = _causal_conv(conv_ext, conv_w[l])
        gq, gk, gv, g, beta = _gdn_features(qkv_c, a_g, b_g, a_log[l], dt_bias[l])
        gdn, s_new = _gdn_recurrent(gq, gk, gv, g, beta, state_ssm[l])
        hs = _merge(hs, att, z_a, gdn, z_g, gdn_norm[l], w_out[l])
        ks_l.append(k)
        vs_l.append(v)
        kis_l.append(ki)
        ss_l.append(s_new.astype(state_ssm.dtype))
        cs_l.append(conv_ext[:, -(CONV_W - 1):])
    y_prompt = _rmsnorm(hp, norm_final)
    y_sample = _rmsnorm(hs, norm_final)
    return (y_prompt, y_sample,
            jnp.stack(kp_l), jnp.stack(vp_l), jnp.stack(kip_l), jnp.stack(sp_l), jnp.stack(cp_l),
            jnp.stack(ks_l), jnp.stack(vs_l), jnp.stack(kis_l), jnp.stack(ss_l), jnp.stack(cs_l))
```

```python
import functools
import math

import numpy as np
import jax
import jax.numpy as jnp
from jax import lax
from jax.experimental import pallas as pl
from jax.experimental.pallas import tpu as pltpu

F32 = jnp.float32
BF16 = jnp.bfloat16
I32 = jnp.int32

N_ATT_HEADS = 8
ATT_HEAD_DIM = 64
N_KV_HEADS = 2
D_ATT = N_ATT_HEADS * ATT_HEAD_DIM
KV_DIM = N_KV_HEADS * ATT_HEAD_DIM
N_IDX_HEADS = 16
IDX_DIM = 64
TOPK_MAX = 256
N_GDN_HEADS = 4
GDN_HEAD_DIM = 128
D_GDN = N_GDN_HEADS * GDN_HEAD_DIM
CONV_W = 4
N_BUCKETS = 32
MAX_DISTANCE = 128
Q_BLOCK = 128
RMS_EPS = 1e-6
NEG_BIG = -1e30
PROJ_SIZES = (D_ATT, KV_DIM, KV_DIM, D_ATT, N_IDX_HEADS * IDX_DIM, IDX_DIM, N_IDX_HEADS,
              3 * D_GDN, D_GDN, N_GDN_HEADS, N_GDN_HEADS)

LANES = 128
SUBLANES = 8
VMEM_LIMIT = 56 * 1024 * 1024

OFF_Q = 0
OFF_K = OFF_Q + D_ATT
OFF_V = OFF_K + KV_DIM
OFF_ZA = OFF_V + KV_DIM
OFF_QI = OFF_ZA + D_ATT
OFF_SM = OFF_QI + N_IDX_HEADS * IDX_DIM
OFF_QKV = OFF_SM + LANES
OFF_ZG = OFF_QKV + 3 * D_GDN
D_PROJ_PAD = OFF_ZG + D_GDN
SM_WI = IDX_DIM
SM_AG = SM_WI + N_IDX_HEADS
SM_BG = SM_AG + N_GDN_HEADS

GDN_C = 128
KEY_NEG_BIG = int(np.array(NEG_BIG, np.float32).view(np.int32)) ^ 0x7FFFFFFF
INT_MIN = -2 ** 31


def _cparams(sem):
    return pltpu.CompilerParams(dimension_semantics=sem, vmem_limit_bytes=VMEM_LIMIT)


def _silu(x):
    return x * (1.0 / (1.0 + jnp.exp(-x)))


def _bdot(a, b):
    return jnp.dot(a.astype(BF16), b.astype(BF16), preferred_element_type=F32)


def _bdot_nt(a, b):
    return lax.dot_general(a.astype(BF16), b.astype(BF16), (((1,), (1,)), ((), ())),
                           preferred_element_type=F32)


def _sort_key(x):
    i = pltpu.bitcast(x, I32)
    return jnp.where(i < 0, i ^ 0x7FFFFFFF, i)


def _inproj_body(x_ref, g_ref, w_ref, q_ref, k_ref, v_ref, za_ref, qi_ref, sm_ref, qkv_ref, zg_ref, *, blocked):
    x = x_ref[...]
    ms = jnp.mean(x * x, axis=-1, keepdims=True)
    h = ((x * lax.rsqrt(ms + RMS_EPS)) * g_ref[...]).astype(BF16)

    def mm(a, b):
        return jnp.dot(h, w_ref[:, a:b], preferred_element_type=F32)

    q = (mm(OFF_Q, OFF_K) * (ATT_HEAD_DIM ** -0.5)).astype(BF16)
    qi = (mm(OFF_QI, OFF_SM) * (IDX_DIM ** -0.5)).astype(BF16)
    if blocked:
        for r in range(x.shape[0] // Q_BLOCK):
            rs = slice(r * Q_BLOCK, (r + 1) * Q_BLOCK)
            for j in range(D_ATT // LANES):
                q_ref[r, j] = q[rs, j * LANES:(j + 1) * LANES]
            for j in range(N_IDX_HEADS * IDX_DIM // LANES):
                qi_ref[r, j] = qi[rs, j * LANES:(j + 1) * LANES]
    else:
        q_ref[...] = q
        qi_ref[...] = qi
    k_ref[...] = mm(OFF_K, OFF_V)
    v_ref[...] = mm(OFF_V, OFF_ZA)
    za_ref[...] = mm(OFF_ZA, OFF_QI)
    sm_ref[...] = mm(OFF_SM, OFF_QKV)
    qkv_ref[...] = mm(OFF_QKV, OFF_ZG)
    zg_ref[...] = mm(OFF_ZG, D_PROJ_PAD)


def _prep_w_in(w):
    splits = np.cumsum(PROJ_SIZES)[:-1].tolist()
    q, k, v, z_a, qi, ki, wi, qkv, z_g, a_g, b_g = jnp.split(w, splits, axis=1)
    pad = jnp.zeros((w.shape[0], LANES - (SM_BG + N_GDN_HEADS)), w.dtype)
    small = jnp.concatenate([ki, wi, a_g, b_g, pad], axis=1)
    return jnp.concatenate([q, k, v, z_a, qi, small, qkv, z_g], axis=1).astype(BF16)


def _inproj(x2d, norm_g, w_pad, tm, blocked):
    t, d = x2d.shape
    nblk = t // tm
    if blocked:
        rb = tm // Q_BLOCK
        q_shape = jax.ShapeDtypeStruct((t // Q_BLOCK, D_ATT // LANES, Q_BLOCK, LANES), BF16)
        qi_shape = jax.ShapeDtypeStruct((t // Q_BLOCK, N_IDX_HEADS * IDX_DIM // LANES, Q_BLOCK, LANES), BF16)
        q_spec = pl.BlockSpec((rb, D_ATT // LANES, Q_BLOCK, LANES), lambda i: (i, 0, 0, 0))
        qi_spec = pl.BlockSpec((rb, N_IDX_HEADS * IDX_DIM // LANES, Q_BLOCK, LANES), lambda i: (i, 0, 0, 0))
    else:
        q_shape = jax.ShapeDtypeStruct((t, D_ATT), BF16)
        qi_shape = jax.ShapeDtypeStruct((t, N_IDX_HEADS * IDX_DIM), BF16)
        q_spec = pl.BlockSpec((tm, D_ATT), lambda i: (i, 0))
        qi_spec = pl.BlockSpec((tm, N_IDX_HEADS * IDX_DIM), lambda i: (i, 0))

    def row(n):
        return pl.BlockSpec((tm, n), lambda i: (i, 0))

    return pl.pallas_call(
        functools.partial(_inproj_body, blocked=blocked),
        grid=(nblk,),
        in_specs=[row(d), pl.BlockSpec((1, d), lambda i: (0, 0)),
                  pl.BlockSpec((d, D_PROJ_PAD), lambda i: (0, 0))],
        out_specs=[q_spec, row(KV_DIM), row(KV_DIM), row(D_ATT), qi_spec, row(LANES), row(3 * D_GDN), row(D_GDN)],
        out_shape=[q_shape, jax.ShapeDtypeStruct((t, KV_DIM), F32), jax.ShapeDtypeStruct((t, KV_DIM), F32),
                   jax.ShapeDtypeStruct((t, D_ATT), F32), qi_shape, jax.ShapeDtypeStruct((t, LANES), F32),
                   jax.ShapeDtypeStruct((t, 3 * D_GDN), F32), jax.ShapeDtypeStruct((t, D_GDN), F32)],
        compiler_params=_cparams(("parallel",)),
        name="inproj_blocked" if blocked else "inproj_rows",
    )(x2d, norm_g.reshape(1, d), w_pad)


def _rel_bucket(dist):
    n = jnp.maximum(dist, 0)
    max_exact = N_BUCKETS // 2
    nf = jnp.maximum(n, 1).astype(F32)
    large = max_exact + (jnp.log(nf / max_exact) / math.log(MAX_DISTANCE / max_exact)
                         * (N_BUCKETS - max_exact)).astype(I32)
    large = jnp.minimum(large, N_BUCKETS - 1)
    return jnp.where(n < max_exact, n, large)


def _far_bucket_checked(first_far):
    d = np.arange(first_far, 1 << 16, dtype=np.float32)
    b = 16 + (np.log(d / 16) / math.log(MAX_DISTANCE / 16) * 16).astype(np.int32)
    assert int(b.min()) >= N_BUCKETS - 1
    return N_BUCKETS - 1


def _kth_key_search(count_ge, rows, k):
    def body(step, t):
        cand = t + jnp.left_shift(jnp.int32(1), 31 - step)
        return jnp.where(count_ge(cand) >= k, cand, t)

    return lax.fori_loop(0, 32, body, jnp.full((rows, 1), INT_MIN, I32))


def _dsa_prompt_body(rel_ref, q_ref, qi_ref, smq_ref, za_ref, k_ref, v_ref, sms_ref, o_ref,
                     kk_ref, vv_ref, ki_ref, bias_ref, key_ref, mask_ref, lg_ref, *, seq, topk):
    b = pl.program_id(0)
    i = pl.program_id(1)
    nt = seq // LANES
    lane = lax.broadcasted_iota(I32, (Q_BLOCK, LANES), 1)
    sub = lax.broadcasted_iota(I32, (Q_BLOCK, LANES), 0)
    far_bucket = _far_bucket_checked(MAX_DISTANCE + 1)

    @pl.when((b == 0) & (i == 0))
    def _():
        for dt in range(2):
            bucket = _rel_bucket(dt * LANES + sub - lane)
            for h in range(N_ATT_HEADS):
                acc = jnp.zeros((Q_BLOCK, LANES), F32)
                for bk in range(N_BUCKETS):
                    acc = jnp.where(bucket == bk, rel_ref[bk, h], acc)
                bias_ref[h, dt] = acc - rel_ref[far_bucket, h]

    @pl.when(i == 0)
    def _():
        lo = lax.broadcasted_iota(I32, (seq, LANES), 1) < ATT_HEAD_DIM
        kf = k_ref[...]
        vf = v_ref[...]
        g0 = jnp.where(lo, kf, 0.0)
        g1 = jnp.where(lo, 0.0, kf)
        kk_ref[0] = g0.astype(BF16)
        kk_ref[1] = pltpu.roll(g0, ATT_HEAD_DIM, 1).astype(BF16)
        kk_ref[2] = pltpu.roll(g1, ATT_HEAD_DIM, 1).astype(BF16)
        kk_ref[3] = g1.astype(BF16)
        w0 = jnp.where(lo, vf, 0.0)
        w1 = jnp.where(lo, 0.0, vf)
        vv_ref[0] = w0.astype(BF16)
        vv_ref[1] = pltpu.roll(w0, ATT_HEAD_DIM, 1).astype(BF16)
        vv_ref[2] = pltpu.roll(w1, ATT_HEAD_DIM, 1).astype(BF16)
        vv_ref[3] = w1.astype(BF16)
        c0 = jnp.where(lo, sms_ref[...], 0.0)
        ki_ref[0] = c0.astype(BF16)
        ki_ref[1] = pltpu.roll(c0, IDX_DIM, 1).astype(BF16)

    n_live = i + 1
    n_live2 = i // 2 + 1
    row_pos = i * Q_BLOCK + sub

    n_pairs = N_IDX_HEADS // 2
    wq = smq_ref[...]
    wcols = [wq[:, SM_WI + h:SM_WI + h + 1] * (N_IDX_HEADS ** -0.5) for h in range(N_IDX_HEADS)]
    qi_all = qi_ref[0].reshape(n_pairs * Q_BLOCK, LANES)

    def score_tile(t2, carry):
        ks = pl.ds(pl.multiple_of(t2 * 2 * LANES, 2 * LANES), 2 * LANES)
        acc = jnp.zeros((Q_BLOCK, 2 * LANES), F32)
        for half in range(2):
            s_all = lax.dot_general(qi_all, ki_ref[half, ks, :], (((1,), (1,)), ((), ())),
                                    preferred_element_type=F32)
            for j in range(n_pairs):
                s = s_all[j * Q_BLOCK:(j + 1) * Q_BLOCK]
                acc = acc + jnp.maximum(s, 0.0) * wcols[2 * j + half]
        acc = acc + 0.0
        for u in range(2):
            col = (t2 * 2 + u) * LANES + lane
            sc = jnp.where(col <= row_pos, acc[:, u * LANES:(u + 1) * LANES], NEG_BIG)
            key_ref[t2 * 2 + u] = _sort_key(sc)
        return carry

    lax.fori_loop(0, n_live2, score_tile, 0)

    @pl.when(i % 2 == 0)
    def _():
        key_ref[i + 1] = jnp.full((Q_BLOCK, LANES), KEY_NEG_BIG, I32)
        mask_ref[i + 1] = jnp.full((Q_BLOCK, LANES), NEG_BIG, F32)

    n_dead = ((nt - 2 * n_live2) * LANES).astype(F32)

    def count_where(pred, upto):
        def body(t, acc):
            return acc + pred(key_ref[t], t).astype(F32)
        acc = lax.fori_loop(0, upto, body, jnp.zeros((Q_BLOCK, LANES), F32))
        return jnp.sum(acc, axis=1, keepdims=True)

    def count_ge(cand):
        live = count_where(lambda kt, t: kt >= cand, 2 * n_live2)
        return live + jnp.where(cand <= KEY_NEG_BIG, n_dead, 0.0)

    thr = _kth_key_search(count_ge, Q_BLOCK, float(topk))
    cnt_gt = count_where(lambda kt, t: kt > thr, 2 * n_live2) + jnp.where(thr < KEY_NEG_BIG, n_dead, 0.0)
    need = float(topk) - cnt_gt

    def causal_of(t):
        return (t * LANES + lane) <= row_pos

    cnt_ceq = count_where(lambda kt, t: (kt == thr) & causal_of(t), n_live)
    any_tie = jnp.max(jnp.where(cnt_ceq > need, 1.0, 0.0)) > 0.0

    @pl.when(jnp.logical_not(any_tie))
    def _():
        def body(t, carry):
            mask_ref[t] = jnp.where((key_ref[t] >= thr) & causal_of(t), 0.0, NEG_BIG)
            return carry
        lax.fori_loop(0, n_live, body, 0)

    @pl.when(any_tie)
    def _():
        upper = (sub <= lane).astype(BF16)
        ones = jnp.ones((LANES, LANES), BF16)

        def body(t, offset):
            kt = key_ref[t]
            cz = causal_of(t)
            eq = ((kt == thr) & cz).astype(BF16)
            prefix = jnp.dot(eq, upper, preferred_element_type=F32) + offset
            sel = ((kt > thr) & cz) | ((eq > 0) & (prefix <= need))
            mask_ref[t] = jnp.where(sel, 0.0, NEG_BIG)
            return offset + jnp.dot(eq, ones, preferred_element_type=F32)

        lax.fori_loop(0, n_live, body, jnp.zeros((Q_BLOCK, LANES), F32))

    za = za_ref[...]
    lo_lane = lane < ATT_HEAD_DIM
    n_qp = N_ATT_HEADS // 2
    for g in range(N_KV_HEADS):
        pairs = [p for p in range(n_qp) if (2 * p) // (N_ATT_HEADS // N_KV_HEADS) == g]
        q_stack = jnp.concatenate([q_ref[0, p] for p in pairs], axis=0)
        for half in range(2):
            heads = [2 * p + half for p in pairs]

            def logit_tile(t2, mx, heads=heads, half=half, g=g):
                ks = pl.ds(pl.multiple_of(t2 * 2 * LANES, 2 * LANES), 2 * LANES)
                l_all = lax.dot_general(q_stack, kk_ref[2 * g + half, ks, :], (((1,), (1,)), ((), ())),
                                        preferred_element_type=F32)
                out = []
                for n, h in enumerate(heads):
                    m_h = mx[n]
                    for u in range(2):
                        t = t2 * 2 + u
                        l = l_all[n * Q_BLOCK:(n + 1) * Q_BLOCK, u * LANES:(u + 1) * LANES] + mask_ref[t]
                        lg_ref[h, t] = l
                        m_h = jnp.maximum(m_h, jnp.where(t < i - 1, l, NEG_BIG))
                    out.append(m_h)
                return tuple(out)

            mx0 = tuple(jnp.full((Q_BLOCK, LANES), NEG_BIG, F32) for _ in heads)
            mx = lax.fori_loop(0, n_live2, logit_tile, mx0)
            for n, h in enumerate(heads):
                l0 = lg_ref[h, i] + bias_ref[h, 0]
                lg_ref[h, i] = l0
                m_h = jnp.maximum(mx[n], l0)
                im1 = jnp.maximum(i - 1, 0)
                l1 = lg_ref[h, im1] + jnp.where(i > 0, bias_ref[h, 1], 0.0)
                lg_ref[h, im1] = l1
                m_h = jnp.maximum(m_h, l1)
                m_row = jnp.max(m_h, axis=1, keepdims=True)

                def pv_tile(t, carry, h=h, half=half, g=g, m_row=m_row):
                    ssum, acc = carry
                    p = jnp.exp(lg_ref[h, t] - m_row)
                    ts = pl.ds(pl.multiple_of(t * LANES, LANES), LANES)
                    acc = acc + jnp.dot(p.astype(BF16), vv_ref[2 * g + half, ts, :], preferred_element_type=F32)
                    return ssum + p, acc

                ssum, acc = lax.fori_loop(0, n_live, pv_tile,
                                          (jnp.zeros((Q_BLOCK, LANES), F32), jnp.zeros((Q_BLOCK, LANES), F32)))
                l_row = jnp.sum(ssum, axis=1, keepdims=True)
                lg_ref[h, nt] = acc / l_row
        for p in pairs:
            o_pair = jnp.where(lo_lane, lg_ref[2 * p, nt], lg_ref[2 * p + 1, nt])
            cs = slice(p * LANES, (p + 1) * LANES)
            o_ref[:, cs] = (o_pair * _silu(za[:, cs])).astype(BF16)


def _dsa_prompt(rel_table, q_blk, qi_blk, small, za, k2d, v2d, batch, seq):
    nb = seq // Q_BLOCK
    nt = seq // LANES
    topk = min(TOPK_MAX, seq // 4)
    n_qp = D_ATT // LANES
    n_ip = N_IDX_HEADS * IDX_DIM // LANES
    t = batch * seq
    return pl.pallas_call(
        functools.partial(_dsa_prompt_body, seq=seq, topk=topk),
        grid=(batch, nb),
        in_specs=[
            pl.BlockSpec(memory_space=pltpu.SMEM),
            pl.BlockSpec((1, n_qp, Q_BLOCK, LANES), lambda b, i: (b * nb + i, 0, 0, 0)),
            pl.BlockSpec((1, n_ip, Q_BLOCK, LANES), lambda b, i: (b * nb + i, 0, 0, 0)),
            pl.BlockSpec((Q_BLOCK, LANES), lambda b, i: (b * nb + i, 0)),
            pl.BlockSpec((Q_BLOCK, D_ATT), lambda b, i: (b * nb + i, 0)),
            pl.BlockSpec((seq, KV_DIM), lambda b, i: (b, 0)),
            pl.BlockSpec((seq, KV_DIM), lambda b, i: (b, 0)),
            pl.BlockSpec((seq, LANES), lambda b, i: (b, 0)),
        ],
        out_specs=pl.BlockSpec((Q_BLOCK, D_ATT), lambda b, i: (b * nb + i, 0)),
        out_shape=jax.ShapeDtypeStruct((t, D_ATT), BF16),
        scratch_shapes=[
            pltpu.VMEM((2 * N_KV_HEADS, seq, LANES), BF16),
            pltpu.VMEM((2 * N_KV_HEADS, seq, LANES), BF16),
            pltpu.VMEM((2, seq, LANES), BF16),
            pltpu.VMEM((N_ATT_HEADS, 2, Q_BLOCK, LANES), F32),
            pltpu.VMEM((nt + 1, Q_BLOCK, LANES), I32),
            pltpu.VMEM((nt + 1, Q_BLOCK, LANES), F32),
            pltpu.VMEM((N_ATT_HEADS, nt + 1, Q_BLOCK, LANES), F32),
        ],
        compiler_params=_cparams(("arbitrary", "arbitrary")),
        name="dsa_prompt",
    )(rel_table, q_blk, qi_blk, small, za, k2d, v2d, small)


def _softplus(x):
    return jnp.maximum(x, 0.0) + jnp.log1p(jnp.exp(-jnp.abs(x)))


def _lane_pick(x, idx):
    lane = lax.broadcasted_iota(I32, x.shape, 1)
    return jnp.sum(jnp.where(lane == idx, x, 0.0), axis=1, keepdims=True)


def _gated_norm(o, gn, z):
    y = o * lax.rsqrt(jnp.mean(o * o, axis=-1, keepdims=True) + RMS_EPS)
    return (y * gn) * _silu(z)


def _gdn_prompt_body(alog_ref, dtb_ref, xq_ref, xk_ref, xv_ref, wq_ref, wk_ref, wv_ref, sm_ref, zg_ref, gn_ref,
                     o_ref, sfin_ref, xs_ref, q_s, k_s, v_s, g_s, b_s, *, seq):
    h = pl.program_id(1)
    hist = SUBLANES

    def conv(x_ref, w_ref):
        xs_ref[0:hist, :] = jnp.zeros((hist, LANES), F32)
        xs_ref[hist:hist + seq, :] = x_ref[...]
        base = hist - (CONV_W - 1)
        acc = xs_ref[base:base + seq, :] * w_ref[0:1, :]
        for j in range(1, CONV_W):
            acc = acc + xs_ref[base + j:base + j + seq, :] * w_ref[j:j + 1, :]
        return _silu(acc)

    def l2n(x):
        return x * lax.rsqrt(jnp.sum(x * x, axis=-1, keepdims=True) + 1e-6)

    q_s[...] = l2n(conv(xq_ref, wq_ref)) * (GDN_HEAD_DIM ** -0.5)
    k_s[...] = l2n(conv(xk_ref, wk_ref))
    v_s[...] = conv(xv_ref, wv_ref)

    sm = sm_ref[...]
    a_col = _lane_pick(sm, SM_AG + h)
    b_col = _lane_pick(sm, SM_BG + h)
    a_neg = -jnp.exp(jnp.zeros((1, 1), F32) + alog_ref[h])
    g_col = a_neg * _softplus(a_col + dtb_ref[h])
    g_s[...] = jnp.broadcast_to(g_col, (seq, LANES))
    b_s[...] = jnp.broadcast_to(1.0 / (1.0 + jnp.exp(-b_col)), (seq, LANES))

    c = GDN_C
    ri = lax.broadcasted_iota(I32, (c, c), 0)
    ci = lax.broadcasted_iota(I32, (c, c), 1)
    tril = ri >= ci
    strict = ri > ci
    tril_f = tril.astype(F32)
    eye = (ri == ci).astype(F32)
    gn = gn_ref[...]
    off_masks = []
    for lg in range(int(math.log2(c))):
        same_pair = (ri >> (lg + 1)) == (ci >> (lg + 1))
        off_masks.append(same_pair & (((ri >> lg) & 1) == 1) & (((ci >> lg) & 1) == 0))

    def chunk(n, s_state):
        sl = pl.ds(pl.multiple_of(n * c, c), c)
        q = q_s[sl, :]
        k = k_s[sl, :]
        v = v_s[sl, :]
        gb = g_s[sl, :]
        bb = b_s[sl, :]
        gcum = jnp.dot(tril_f, gb, precision=lax.Precision.HIGHEST, preferred_element_type=F32)
        gcum_row = gcum.T
        decay = jnp.where(tril, jnp.exp(jnp.where(tril, gcum - gcum_row, 0.0)), 0.0)
        eg = jnp.exp(gcum)
        kb = k * bb
        vb = v * bb
        a_mat = jnp.where(strict, _bdot_nt(kb, k) * decay, 0.0)
        attn = _bdot_nt(q, k) * decay
        x = eye - jnp.where(off_masks[0], a_mat, 0.0)
        for om in off_masks[1:]:
            x = x - _bdot(x, _bdot(jnp.where(om, a_mat, 0.0), x))
        u = _bdot(x, vb)
        w = _bdot(x, kb * eg)
        g_last = gcum[c - 1:c, :]
        k_dec = k * jnp.exp(g_last - gcum)
        v_new = u - _bdot(w, s_state)
        o = _bdot(q * eg, s_state) + _bdot(attn, v_new)
        s_state = s_state * jnp.exp(g_last) + _bdot(k_dec.T, v_new)
        o_ref[sl, :] = _gated_norm(o, gn, zg_ref[sl, :]).astype(BF16)
        return s_state

    sfin_ref[...] = lax.fori_loop(0, seq // c, chunk, jnp.zeros((GDN_HEAD_DIM, GDN_HEAD_DIM), F32))


def _gdn_prompt(a_log, dt_bias, qkv2d, conv_w, small, zg, gdn_norm, batch, seq):
    nh = N_GDN_HEADS
    t = batch * seq
    blk = lambda off: pl.BlockSpec((seq, LANES), lambda b, h: (b, off + h))
    wblk = lambda off: pl.BlockSpec((CONV_W, LANES), lambda b, h: (0, off + h))
    return pl.pallas_call(
        functools.partial(_gdn_prompt_body, seq=seq),
        grid=(batch, nh),
        in_specs=[pl.BlockSpec(memory_space=pltpu.SMEM), pl.BlockSpec(memory_space=pltpu.SMEM),
                  blk(0), blk(nh), blk(2 * nh), wblk(0), wblk(nh), wblk(2 * nh),
                  pl.BlockSpec((seq, LANES), lambda b, h: (b, 0)),
                  pl.BlockSpec((seq, LANES), lambda b, h: (b, h)),
                  pl.BlockSpec((1, LANES), lambda b, h: (0, 0))],
        out_specs=[pl.BlockSpec((seq, LANES), lambda b, h: (b, h)),
                   pl.BlockSpec((None, None, GDN_HEAD_DIM, GDN_HEAD_DIM), lambda b, h: (b, h, 0, 0))],
        out_shape=[jax.ShapeDtypeStruct((t, D_GDN), BF16),
                   jax.ShapeDtypeStruct((batch, nh, GDN_HEAD_DIM, GDN_HEAD_DIM), F32)],
        scratch_shapes=[pltpu.VMEM((seq + 2 * SUBLANES, LANES), F32)] + [pltpu.VMEM((seq, LANES), F32)] * 5,
        compiler_params=_cparams(("parallel", "arbitrary")),
        name="gdn_prompt",
    )(a_log, dt_bias, qkv2d, qkv2d, qkv2d, conv_w, conv_w, conv_w, small, zg, gdn_norm.reshape(1, LANES))


def _outproj_body(*refs, gate_att):
    if gate_att:
        x_ref, att_ref, za_ref, gdn_ref, w_ref, g_ref, y_ref = refs
        att = (att_ref[...] * _silu(za_ref[...])).astype(BF16)
    else:
        x_ref, att_ref, gdn_ref, w_ref, g_ref, y_ref = refs
        att = att_ref[...]
    y = x_ref[...] + jnp.dot(att, w_ref[0:D_ATT, :], preferred_element_type=F32) \
        + jnp.dot(gdn_ref[...], w_ref[D_ATT:D_ATT + D_GDN, :], preferred_element_type=F32)
    y = y * lax.rsqrt(jnp.mean(y * y, axis=-1, keepdims=True) + RMS_EPS)
    y_ref[...] = y * g_ref[...]


def _outproj(x2d, att, gdn, w_bf, norm_g, tm, za=None):
    t, d = x2d.shape
    row = lambda n: pl.BlockSpec((tm, n), lambda i: (i, 0))
    full = lambda a, b: pl.BlockSpec((a, b), lambda i: (0, 0))
    ins = [x2d, att] + ([za] if za is not None else []) + [gdn, w_bf, norm_g.reshape(1, d)]
    specs = [row(d), row(D_ATT)] + ([row(D_ATT)] if za is not None else []) + \
            [row(D_GDN), full(D_ATT + D_GDN, d), full(1, d)]
    return pl.pallas_call(
        functools.partial(_outproj_body, gate_att=za is not None),
        grid=(t // tm,),
        in_specs=specs,
        out_specs=row(d),
        out_shape=jax.ShapeDtypeStruct((t, d), F32),
        compiler_params=_cparams(("parallel",)),
        name="outproj_gated" if za is not None else "outproj",
    )(*ins)


PAGES_PER_STEP = 8
SAMPLE_Q_ROWS = 16


def _idx_scores_body(pt_ref, qi_ref, wi_ref, *rest):
    page_refs = rest[:PAGES_PER_STEP]
    o_ref = rest[PAGES_PER_STEP]
    qi = qi_ref[0]
    wi = wi_ref[0] * (N_IDX_HEADS ** -0.5)
    for j, pr in enumerate(page_refs):
        s = _bdot_nt(qi, pr[...])
        sc = jnp.sum(jnp.maximum(s, 0.0) * wi, axis=0, keepdims=True)
        o_ref[0, j:j + 1, :] = sc + 0.0


def _idx_scores(page_table, qi3, wi3, kidx_pages):
    db, n_pages = page_table.shape
    page = kidx_pages.shape[1]
    pg = PAGES_PER_STEP

    def page_spec(j):
        return pl.BlockSpec((None, page, IDX_DIM), lambda b, s, pt: (pt[b, s * pg + j], 0, 0))

    grid_spec = pltpu.PrefetchScalarGridSpec(
        num_scalar_prefetch=1,
        grid=(db, n_pages // pg),
        in_specs=[pl.BlockSpec((1, N_IDX_HEADS, IDX_DIM), lambda b, s, pt: (b, 0, 0)),
                  pl.BlockSpec((1, N_IDX_HEADS, 1), lambda b, s, pt: (b, 0, 0))]
                 + [page_spec(j) for j in range(pg)],
        out_specs=pl.BlockSpec((1, pg, page), lambda b, s, pt: (b, s, 0)),
    )
    return pl.pallas_call(
        _idx_scores_body,
        grid_spec=grid_spec,
        out_shape=jax.ShapeDtypeStruct((db, n_pages, page), F32),
        compiler_params=_cparams(("parallel", "arbitrary")),
        name="sample_idx_scores",
    )(page_table, qi3, wi3, *([kidx_pages] * pg))


def _sample_select_body(sc_ref, qi_ref, sm_ref, mask_ref, key_ref, *, n_tiles, topk):
    rows = sc_ref.shape[1]
    lane = lax.broadcasted_iota(I32, (rows, LANES), 1)

    def fill(t, carry):
        key_ref[t] = _sort_key(sc_ref[t])
        return carry

    lax.fori_loop(0, n_tiles, fill, 0)
    sm = sm_ref[...]
    ki_new = sm[:, 0:IDX_DIM].astype(BF16).astype(F32)
    acc = jnp.zeros((rows, 1), F32)
    for hd in range(N_IDX_HEADS):
        qh = qi_ref[:, hd * IDX_DIM:(hd + 1) * IDX_DIM].astype(F32)
        s = jnp.sum(qh * ki_new, axis=1, keepdims=True)
        acc = acc + jnp.maximum(s, 0.0) * (sm[:, SM_WI + hd:SM_WI + hd + 1] * (N_IDX_HEADS ** -0.5))
    sc_new = jnp.broadcast_to(acc + 0.0, (rows, LANES))
    key_ref[n_tiles] = jnp.where(lane == 0, _sort_key(sc_new), INT_MIN)

    def count_where(pred):
        def body(t, a):
            return a + pred(key_ref[t]).astype(F32)
        a = lax.fori_loop(0, n_tiles + 1, body, jnp.zeros((rows, LANES), F32))
        return jnp.sum(a, axis=1, keepdims=True)

    thr = _kth_key_search(lambda cand: count_where(lambda kt: kt >= cand), rows, float(topk))
    need = float(topk) - count_where(lambda kt: kt > thr)
    cnt_eq = count_where(lambda kt: kt == thr)
    any_tie = jnp.max(jnp.where(cnt_eq > need, 1.0, 0.0)) > 0.0

    @pl.when(jnp.logical_not(any_tie))
    def _():
        def body(t, carry):
            mask_ref[t] = jnp.where(key_ref[t] >= thr, 0.0, NEG_BIG)
            return carry
        lax.fori_loop(0, n_tiles + 1, body, 0)

    @pl.when(any_tie)
    def _():
        sub = lax.broadcasted_iota(I32, (LANES, LANES), 0)
        lane2 = lax.broadcasted_iota(I32, (LANES, LANES), 1)
        upper = (sub <= lane2).astype(BF16)
        ones = jnp.ones((LANES, LANES), BF16)

        def body(t, offset):
            kt = key_ref[t]
            eq = (kt == thr).astype(BF16)
            prefix = jnp.dot(eq, upper, preferred_element_type=F32) + offset
            sel = (kt > thr) | ((eq > 0) & (prefix <= need))
            mask_ref[t] = jnp.where(sel, 0.0, NEG_BIG)
            return offset + jnp.dot(eq, ones, preferred_element_type=F32)

        lax.fori_loop(0, n_tiles + 1, body, jnp.zeros((rows, LANES), F32))


def _sample_select(scores_t, qi2d, small, topk):
    n_tiles, db, page = scores_t.shape
    vm = pl.BlockSpec(memory_space=pltpu.VMEM)
    return pl.pallas_call(
        functools.partial(_sample_select_body, n_tiles=n_tiles, topk=topk),
        in_specs=[vm, vm, vm],
        out_specs=vm,
        out_shape=jax.ShapeDtypeStruct((n_tiles + 1, db, page), F32),
        scratch_shapes=[pltpu.VMEM((n_tiles + 1, db, page), I32)],
        compiler_params=pltpu.CompilerParams(vmem_limit_bytes=VMEM_LIMIT),
        name="sample_select",
    )(scores_t, qi2d, small)


def _sample_attn_body(pt_ref, q_ref, mask_ref, mnew_ref, knew_ref, vnew_ref, relt_ref, *rest, past_len, page):
    pg = PAGES_PER_STEP
    k_refs = rest[:pg]
    v_refs = rest[pg:2 * pg]
    o_ref, m_s, l_s, acc_s = rest[2 * pg:]
    s = pl.program_id(1)
    n_steps = pl.num_programs(1)
    nh = SAMPLE_Q_ROWS
    far_bucket = _far_bucket_checked(MAX_DISTANCE + 1)

    @pl.when(s == 0)
    def _():
        m_s[...] = jnp.full(m_s.shape, NEG_BIG, F32)
        l_s[...] = jnp.zeros(l_s.shape, F32)
        acc_s[...] = jnp.zeros(acc_s.shape, F32)

    relt = relt_ref[...]
    c_far = relt[:, far_bucket:far_bucket + 1]

    def bias_of(dist):
        bucket = _rel_bucket(dist)
        acc = jnp.zeros((nh, dist.shape[1]), F32)
        for bk in range(N_BUCKETS):
            acc = jnp.where(bucket == bk, relt[:, bk:bk + 1], acc)
        return acc - c_far

    q = q_ref[0].astype(BF16)
    logits = []
    for j in range(pg):
        l = _bdot_nt(q, k_refs[j][...]) + mask_ref[0, j:j + 1, :]
        logits.append(l)
    logits = jnp.concatenate(logits, axis=1)

    kpos = (s * pg) * page + lax.broadcasted_iota(I32, (1, pg * page), 1)
    logits = logits + bias_of(past_len - kpos)

    m_old = m_s[...]
    m_new = jnp.maximum(m_old, jnp.max(logits, axis=1, keepdims=True))
    alpha = jnp.exp(m_old - m_new)
    p = jnp.exp(logits - m_new)
    l_new = l_s[...] * alpha + jnp.sum(p, axis=1, keepdims=True)
    acc = acc_s[...] * alpha
    for j in range(pg):
        acc = acc + _bdot(p[:, j * page:(j + 1) * page], v_refs[j][...])
    m_s[...] = m_new
    l_s[...] = l_new
    acc_s[...] = acc

    @pl.when(s == n_steps - 1)
    def _():
        kn = knew_ref[0].astype(BF16).astype(F32)
        vn = vnew_ref[0].astype(BF16).astype(F32)
        ln = jnp.sum(q.astype(F32) * kn, axis=1, keepdims=True) + bias_of(jnp.zeros((1, 1), I32)) \
            + mnew_ref[0][:, 0:1]
        m_fin = jnp.maximum(m_new, ln)
        a2 = jnp.exp(m_new - m_fin)
        pn = jnp.exp(ln - m_fin)
        l_fin = l_new * a2 + pn
        res = (acc * a2 + pn.astype(BF16).astype(F32) * vn) / l_fin
        row = lax.broadcasted_iota(I32, res.shape, 0)
        hpg = N_ATT_HEADS // N_KV_HEADS
        o_ref[0] = jnp.where((row >= hpg) & (row < 2 * hpg), pltpu.roll(res, ATT_HEAD_DIM, 1), res)


def _sample_attn(page_table, q_lh, mask_pages, mask_new, k_new, v_new, rel_t, k_pages, v_pages, past_len):
    db, n_pages = page_table.shape
    page = k_pages.shape[1]
    pg = PAGES_PER_STEP

    def page_spec(j):
        return pl.BlockSpec((None, page, KV_DIM), lambda b, s, pt: (pt[b, s * pg + j], 0, 0))

    row3 = lambda n: pl.BlockSpec((1, 1, n), lambda b, s, pt: (b, 0, 0))
    grid_spec = pltpu.PrefetchScalarGridSpec(
        num_scalar_prefetch=1,
        grid=(db, n_pages // pg),
        in_specs=[pl.BlockSpec((1, SAMPLE_Q_ROWS, LANES), lambda b, s, pt: (b, 0, 0)),
                  pl.BlockSpec((1, pg, page), lambda b, s, pt: (b, s, 0)),
                  row3(LANES), row3(KV_DIM), row3(KV_DIM),
                  pl.BlockSpec((SAMPLE_Q_ROWS, N_BUCKETS), lambda b, s, pt: (0, 0))]
                 + [page_spec(j) for j in range(pg)] * 2,
        out_specs=pl.BlockSpec((1, SAMPLE_Q_ROWS, LANES), lambda b, s, pt: (b, 0, 0)),
        scratch_shapes=[pltpu.VMEM((SAMPLE_Q_ROWS, 1), F32), pltpu.VMEM((SAMPLE_Q_ROWS, 1), F32),
                        pltpu.VMEM((SAMPLE_Q_ROWS, LANES), F32)],
    )
    return pl.pallas_call(
        functools.partial(_sample_attn_body, past_len=past_len, page=page),
        grid_spec=grid_spec,
        out_shape=jax.ShapeDtypeStruct((db, SAMPLE_Q_ROWS, LANES), F32),
        compiler_params=_cparams(("parallel", "arbitrary")),
        name="sample_attn",
    )(page_table, q_lh, mask_pages, mask_new, k_new, v_new, rel_t, *([k_pages] * pg), *([v_pages] * pg))


def _gdn_sample_body(alog_ref, dtb_ref, x_ref, cst_ref, w_ref, sm_ref, zg_ref, gn_ref, s0_ref,
                     o_ref, s_ref, cnew_ref):
    x = x_ref[0]
    cst = cst_ref[...]
    w = w_ref[...]
    acc = cst[0:1, :] * w[0:1, :]
    for j in range(1, CONV_W - 1):
        acc = acc + cst[j:j + 1, :] * w[j:j + 1, :]
    acc = acc + x * w[CONV_W - 1:CONV_W, :]
    xc = _silu(acc)
    cnew_ref[0:CONV_W - 2, :] = cst[1:CONV_W - 1, :]
    cnew_ref[CONV_W - 2:CONV_W - 1, :] = x
    sm = sm_ref[0]
    gn = gn_ref[...]
    d = GDN_HEAD_DIM
    for h in range(N_GDN_HEADS):
        q = xc[:, h * d:(h + 1) * d]
        k = xc[:, D_GDN + h * d:D_GDN + (h + 1) * d]
        v = xc[:, 2 * D_GDN + h * d:2 * D_GDN + (h + 1) * d]
        q = q * lax.rsqrt(jnp.sum(q * q, axis=-1, keepdims=True) + 1e-6) * (d ** -0.5)
        k = k * lax.rsqrt(jnp.sum(k * k, axis=-1, keepdims=True) + 1e-6)
        a_neg = -jnp.exp(jnp.zeros((1, 1), F32) + alog_ref[h])
        g = a_neg * _softplus(sm[:, SM_AG + h:SM_AG + h + 1] + dtb_ref[h])
        beta = 1.0 / (1.0 + jnp.exp(-sm[:, SM_BG + h:SM_BG + h + 1]))
        st = s0_ref[h] * jnp.exp(g)
        k_col = jnp.broadcast_to(k, (d, d)).T
        q_col = jnp.broadcast_to(q, (d, d)).T
        kv = jnp.sum(k_col * st, axis=0, keepdims=True)
        delta = (v - kv) * beta
        st = st + k_col * delta
        s_ref[h] = st
        o = jnp.sum(q_col * st, axis=0, keepdims=True)
        o_ref[0, :, h * d:(h + 1) * d] = _gated_norm(o, gn, zg_ref[0][:, h * d:(h + 1) * d]).astype(BF16)


def _gdn_sample(a_log, dt_bias, qkv3, state_conv_l, conv_w, small3, zg3, gdn_norm, state_ssm_l):
    db = qkv3.shape[0]
    d = GDN_HEAD_DIM
    nh = N_GDN_HEADS
    row3 = lambda n: pl.BlockSpec((1, 1, n), lambda b: (b, 0, 0))
    return pl.pallas_call(
        _gdn_sample_body,
        grid=(db,),
        in_specs=[pl.BlockSpec(memory_space=pltpu.SMEM), pl.BlockSpec(memory_space=pltpu.SMEM),
                  row3(3 * D_GDN),
                  pl.BlockSpec((None, CONV_W - 1, 3 * D_GDN), lambda b: (b, 0, 0)),
                  pl.BlockSpec((CONV_W, 3 * D_GDN), lambda b: (0, 0)),
                  row3(LANES), row3(D_GDN),
                  pl.BlockSpec((1, LANES), lambda b: (0, 0)),
                  pl.BlockSpec((None, nh, d, d), lambda b: (b, 0, 0, 0))],
        out_specs=[row3(D_GDN),
                   pl.BlockSpec((None, nh, d, d), lambda b: (b, 0, 0, 0)),
                   pl.BlockSpec((None, CONV_W - 1, 3 * D_GDN), lambda b: (b, 0, 0))],
        out_shape=[jax.ShapeDtypeStruct((db, 1, D_GDN), BF16),
                   jax.ShapeDtypeStruct((db, nh, d, d), F32),
                   jax.ShapeDtypeStruct((db, CONV_W - 1, 3 * D_GDN), F32)],
        compiler_params=_cparams(("parallel",)),
        name="gdn_sample",
    )(a_log, dt_bias, qkv3, state_conv_l, conv_w, small3, zg3, gdn_norm.reshape(1, LANES), state_ssm_l)


def kernel(x_prompt, x_sample, cache_k, cache_v, cache_kidx, state_ssm, state_conv, page_table, norm_in, w_in,
           conv_w, a_log, dt_bias, gdn_norm, w_out, rel_table, norm_final):
    depth = w_in.shape[0]
    assert depth == 1, "single-layer model"
    batch, seq, d_model = x_prompt.shape
    db, dec_seq, _ = x_sample.shape
    assert dec_seq == 1 and seq % (2 * LANES) == 0 and seq % GDN_C == 0
    n_pool, page = cache_k.shape[1], cache_k.shape[2]
    n_pages = page_table.shape[1]
    past_len = n_pages * page
    assert page == LANES and n_pages % PAGES_PER_STEP == 0

    lyr = 0
    w_pad = _prep_w_in(w_in[lyr])
    w_out_bf = w_out[lyr].astype(BF16)

    xp = x_prompt.reshape(batch * seq, d_model)
    q_blk, k2d, v2d, za, qi_blk, small, qkv2d, zg = _inproj(xp, norm_in[lyr], w_pad, tm=2 * Q_BLOCK, blocked=True)
    att_g = _dsa_prompt(rel_table, q_blk, qi_blk, small, za, k2d, v2d, batch, seq)
    gdn_g, s_fin = _gdn_prompt(a_log[lyr], dt_bias[lyr], qkv2d, conv_w[lyr], small, zg, gdn_norm[lyr], batch, seq)
    y_prompt = _outproj(xp, att_g, gdn_g, w_out_bf, norm_final, tm=2 * Q_BLOCK).reshape(batch, seq, d_model)
    k_prompt = k2d.reshape(1, batch, seq, N_KV_HEADS, ATT_HEAD_DIM)
    v_prompt = v2d.reshape(1, batch, seq, N_KV_HEADS, ATT_HEAD_DIM)
    kidx_prompt = small[:, :IDX_DIM].reshape(1, batch, seq, IDX_DIM)
    ssm_prompt = s_fin[None]
    conv_prompt = qkv2d.reshape(batch, seq, 3 * D_GDN)[:, seq - (CONV_W - 1):][None]

    xs = x_sample.reshape(db, d_model)
    q_s, k_s, v_s, za_s, qi_s, small_s, qkv_s, zg_s = _inproj(xs, norm_in[lyr], w_pad, tm=db, blocked=False)
    scores = _idx_scores(page_table, qi_s.reshape(db, N_IDX_HEADS, IDX_DIM),
                         small_s[:, SM_WI:SM_WI + N_IDX_HEADS].reshape(db, N_IDX_HEADS, 1),
                         cache_kidx[lyr])
    topk = min(TOPK_MAX, (past_len + dec_seq) // 4)
    mask_t = _sample_select(jnp.transpose(scores, (1, 0, 2)), qi_s, small_s, topk)
    mask_pages = jnp.transpose(mask_t[:n_pages], (1, 0, 2))
    mask_new = mask_t[n_pages].reshape(db, 1, page)
    hpg = N_ATT_HEADS // N_KV_HEADS
    q8 = q_s.reshape(db, N_ATT_HEADS, ATT_HEAD_DIM).astype(F32)
    zq = jnp.zeros((db, hpg, ATT_HEAD_DIM), F32)
    q_lh = jnp.concatenate([jnp.concatenate([q8[:, :hpg], zq], axis=2),
                            jnp.concatenate([zq, q8[:, hpg:]], axis=2),
                            jnp.zeros((db, SAMPLE_Q_ROWS - N_ATT_HEADS, LANES), F32)], axis=1)
    rel_t = jnp.concatenate([rel_table.T, jnp.zeros((SAMPLE_Q_ROWS - N_ATT_HEADS, N_BUCKETS), F32)], axis=0)
    att_raw = _sample_attn(page_table, q_lh, mask_pages, mask_new, k_s.reshape(db, 1, KV_DIM),
                           v_s.reshape(db, 1, KV_DIM), rel_t,
                           cache_k[lyr].reshape(n_pool, page, KV_DIM), cache_v[lyr].reshape(n_pool, page, KV_DIM),
                           past_len)
    att_s = att_raw[:, :N_ATT_HEADS, :ATT_HEAD_DIM].reshape(db, D_ATT)
    gdn_s, s_new, conv_new = _gdn_sample(a_log[lyr], dt_bias[lyr], qkv_s.reshape(db, 1, 3 * D_GDN), state_conv[lyr],
                                         conv_w[lyr], small_s.reshape(db, 1, LANES), zg_s.reshape(db, 1, D_GDN),
                                         gdn_norm[lyr], state_ssm[lyr])
    y_sample = _outproj(xs, att_s, gdn_s.reshape(db, D_GDN), w_out_bf, norm_final, tm=db,
                        za=za_s).reshape(db, 1, d_model)
    k_sample = k_s.reshape(1, db, 1, N_KV_HEADS, ATT_HEAD_DIM)
    v_sample = v_s.reshape(1, db, 1, N_KV_HEADS, ATT_HEAD_DIM)
    kidx_sample = small_s[:, :IDX_DIM].reshape(1, db, 1, IDX_DIM)

    return (y_prompt, y_sample, k_prompt, v_prompt, kidx_prompt, ssm_prompt, conv_prompt,
            k_sample, v_sample, kidx_sample, s_new[None], conv_new[None])
```

```python
import functools
import math

import numpy as np
import jax
import jax.numpy as jnp
from jax import lax
from jax.experimental import pallas as pl
from jax.experimental.pallas import tpu as pltpu

F32 = jnp.float32
BF16 = jnp.bfloat16
I32 = jnp.int32

N_ATT_HEADS = 8
ATT_HEAD_DIM = 64
N_KV_HEADS = 2
D_ATT = N_ATT_HEADS * ATT_HEAD_DIM
KV_DIM = N_KV_HEADS * ATT_HEAD_DIM
N_IDX_HEADS = 16
IDX_DIM = 64
TOPK_MAX = 256
N_GDN_HEADS = 4
GDN_HEAD_DIM = 128
D_GDN = N_GDN_HEADS * GDN_HEAD_DIM
CONV_W = 4
N_BUCKETS = 32
MAX_DISTANCE = 128
Q_BLOCK = 128
RMS_EPS = 1e-6
NEG_BIG = -1e30
PROJ_SIZES = (D_ATT, KV_DIM, KV_DIM, D_ATT, N_IDX_HEADS * IDX_DIM, IDX_DIM, N_IDX_HEADS,
              3 * D_GDN, D_GDN, N_GDN_HEADS, N_GDN_HEADS)

LANES = 128
SUBLANES = 8
VMEM_LIMIT = 56 * 1024 * 1024

OFF_Q = 0
OFF_K = OFF_Q + D_ATT
OFF_V = OFF_K + KV_DIM
OFF_ZA = OFF_V + KV_DIM
OFF_QI = OFF_ZA + D_ATT
OFF_SM = OFF_QI + N_IDX_HEADS * IDX_DIM
OFF_QKV = OFF_SM + LANES
OFF_ZG = OFF_QKV + 3 * D_GDN
D_PROJ_PAD = OFF_ZG + D_GDN
SM_WI = IDX_DIM
SM_AG = SM_WI + N_IDX_HEADS
SM_BG = SM_AG + N_GDN_HEADS

GDN_C = 128
KEY_NEG_BIG = int(np.array(NEG_BIG, np.float32).view(np.int32)) ^ 0x7FFFFFFF
INT_MIN = -2 ** 31


def _cparams(sem):
    return pltpu.CompilerParams(dimension_semantics=sem, vmem_limit_bytes=VMEM_LIMIT)


def _silu(x):
    return x * (1.0 / (1.0 + jnp.exp(-x)))


def _bdot(a, b):
    return jnp.dot(a.astype(BF16), b.astype(BF16), preferred_element_type=F32)


def _bdot_nt(a, b):
    return lax.dot_general(a.astype(BF16), b.astype(BF16), (((1,), (1,)), ((), ())),
                           preferred_element_type=F32)


def _sort_key(x):
    i = pltpu.bitcast(x, I32)
    return jnp.where(i < 0, i ^ 0x7FFFFFFF, i)


def _inproj_body(x_ref, g_ref, w_ref, q_ref, k_ref, v_ref, za_ref, qi_ref, sm_ref, qkv_ref, zg_ref, *, blocked):
    x = x_ref[...]
    ms = jnp.mean(x * x, axis=-1, keepdims=True)
    h = ((x * lax.rsqrt(ms + RMS_EPS)) * g_ref[...]).astype(BF16)

    def mm(a, b):
        return jnp.dot(h, w_ref[:, a:b], preferred_element_type=F32)

    q = (mm(OFF_Q, OFF_K) * (ATT_HEAD_DIM ** -0.5)).astype(BF16)
    qi = (mm(OFF_QI, OFF_SM) * (IDX_DIM ** -0.5)).astype(BF16)
    if blocked:
        for r in range(x.shape[0] // Q_BLOCK):
            rs = slice(r * Q_BLOCK, (r + 1) * Q_BLOCK)
            for j in range(D_ATT // LANES):
                q_ref[r, j] = q[rs, j * LANES:(j + 1) * LANES]
            for j in range(N_IDX_HEADS * IDX_DIM // LANES):
                qi_ref[r, j] = qi[rs, j * LANES:(j + 1) * LANES]
    else:
        q_ref[...] = q
        qi_ref[...] = qi
    k_ref[...] = mm(OFF_K, OFF_V)
    v_ref[...] = mm(OFF_V, OFF_ZA)
    za_ref[...] = mm(OFF_ZA, OFF_QI)
    sm_ref[...] = mm(OFF_SM, OFF_QKV)
    qkv_ref[...] = mm(OFF_QKV, OFF_ZG)
    zg_ref[...] = mm(OFF_ZG, D_PROJ_PAD)


def _prep_w_in(w):
    splits = np.cumsum(PROJ_SIZES)[:-1].tolist()
    q, k, v, z_a, qi, ki, wi, qkv, z_g, a_g, b_g = jnp.split(w, splits, axis=1)
    pad = jnp.zeros((w.shape[0], LANES - (SM_BG + N_GDN_HEADS)), w.dtype)
    small = jnp.concatenate([ki, wi, a_g, b_g, pad], axis=1)
    return jnp.concatenate([q, k, v, z_a, qi, small, qkv, z_g], axis=1).astype(BF16)


def _inproj(x2d, norm_g, w_pad, tm, blocked):
    t, d = x2d.shape
    nblk = t // tm
    if blocked:
        rb = tm // Q_BLOCK
        q_shape = jax.ShapeDtypeStruct((t // Q_BLOCK, D_ATT // LANES, Q_BLOCK, LANES), BF16)
        qi_shape = jax.ShapeDtypeStruct((t // Q_BLOCK, N_IDX_HEADS * IDX_DIM // LANES, Q_BLOCK, LANES), BF16)
        q_spec = pl.BlockSpec((rb, D_ATT // LANES, Q_BLOCK, LANES), lambda i: (i, 0, 0, 0))
        qi_spec = pl.BlockSpec((rb, N_IDX_HEADS * IDX_DIM // LANES, Q_BLOCK, LANES), lambda i: (i, 0, 0, 0))
    else:
        q_shape = jax.ShapeDtypeStruct((t, D_ATT), BF16)
        qi_shape = jax.ShapeDtypeStruct((t, N_IDX_HEADS * IDX_DIM), BF16)
        q_spec = pl.BlockSpec((tm, D_ATT), lambda i: (i, 0))
        qi_spec = pl.BlockSpec((tm, N_IDX_HEADS * IDX_DIM), lambda i: (i, 0))

    def row(n):
        return pl.BlockSpec((tm, n), lambda i: (i, 0))

    return pl.pallas_call(
        functools.partial(_inproj_body, blocked=blocked),
        grid=(nblk,),
        in_specs=[row(d), pl.BlockSpec((1, d), lambda i: (0, 0)),
                  pl.BlockSpec((d, D_PROJ_PAD), lambda i: (0, 0))],
        out_specs=[q_spec, row(KV_DIM), row(KV_DIM), row(D_ATT), qi_spec, row(LANES), row(3 * D_GDN), row(D_GDN)],
        out_shape=[q_shape, jax.ShapeDtypeStruct((t, KV_DIM), F32), jax.ShapeDtypeStruct((t, KV_DIM), F32),
                   jax.ShapeDtypeStruct((t, D_ATT), F32), qi_shape, jax.ShapeDtypeStruct((t, LANES), F32),
                   jax.ShapeDtypeStruct((t, 3 * D_GDN), F32), jax.ShapeDtypeStruct((t, D_GDN), F32)],
        compiler_params=_cparams(("parallel",)),
        name="inproj_blocked" if blocked else "inproj_rows",
    )(x2d, norm_g.reshape(1, d), w_pad)


def _rel_bucket(dist):
    n = jnp.maximum(dist, 0)
    max_exact = N_BUCKETS // 2
    nf = jnp.maximum(n, 1).astype(F32)
    large = max_exact + (jnp.log(nf / max_exact) / math.log(MAX_DISTANCE / max_exact)
                         * (N_BUCKETS - max_exact)).astype(I32)
    large = jnp.minimum(large, N_BUCKETS - 1)
    return jnp.where(n < max_exact, n, large)


def _far_bucket_checked(first_far):
    d = np.arange(first_far, 1 << 16, dtype=np.float32)
    b = 16 + (np.log(d / 16) / math.log(MAX_DISTANCE / 16) * 16).astype(np.int32)
    assert int(b.min()) >= N_BUCKETS - 1
    return N_BUCKETS - 1


def _kth_key_search(count_ge, shape, k):
    def body(step, t):
        cand = t + jnp.left_shift(jnp.int32(1), 31 - step)
        return jnp.where(count_ge(cand) >= k, cand, t)

    return lax.fori_loop(0, 32, body, jnp.full(shape, INT_MIN, I32))


def _dsa_prompt_body(rel_ref, q_ref, qi_ref, smq_ref, za_ref, k_ref, v_ref, sms_ref, o_ref,
                     kk_ref, vv_ref, ki_ref, bias_ref, key_ref, mask_ref, lg_ref, *, seq, topk):
    b = pl.program_id(0)
    i = pl.program_id(1)
    nt = seq // LANES
    lane = lax.broadcasted_iota(I32, (Q_BLOCK, LANES), 1)
    sub = lax.broadcasted_iota(I32, (Q_BLOCK, LANES), 0)
    far_bucket = _far_bucket_checked(MAX_DISTANCE + 1)

    @pl.when((b == 0) & (i == 0))
    def _():
        for dt in range(2):
            bucket = _rel_bucket(dt * LANES + sub - lane)
            for h in range(N_ATT_HEADS):
                acc = jnp.zeros((Q_BLOCK, LANES), F32)
                for bk in range(N_BUCKETS):
                    acc = jnp.where(bucket == bk, rel_ref[bk, h], acc)
                bias_ref[h, dt] = acc - rel_ref[far_bucket, h]

    @pl.when(i == 0)
    def _():
        lo = lax.broadcasted_iota(I32, (seq, LANES), 1) < ATT_HEAD_DIM
        kf = k_ref[...]
        vf = v_ref[...]
        g0 = jnp.where(lo, kf, 0.0)
        g1 = jnp.where(lo, 0.0, kf)
        kk_ref[0] = g0.astype(BF16)
        kk_ref[1] = pltpu.roll(g0, ATT_HEAD_DIM, 1).astype(BF16)
        kk_ref[2] = pltpu.roll(g1, ATT_HEAD_DIM, 1).astype(BF16)
        kk_ref[3] = g1.astype(BF16)
        w0 = jnp.where(lo, vf, 0.0)
        w1 = jnp.where(lo, 0.0, vf)
        vv_ref[0] = w0.astype(BF16)
        vv_ref[1] = pltpu.roll(w0, ATT_HEAD_DIM, 1).astype(BF16)
        vv_ref[2] = pltpu.roll(w1, ATT_HEAD_DIM, 1).astype(BF16)
        vv_ref[3] = w1.astype(BF16)
        c0 = jnp.where(lo, sms_ref[...], 0.0)
        ki_ref[0] = c0.astype(BF16)
        ki_ref[1] = pltpu.roll(c0, IDX_DIM, 1).astype(BF16)

    n_live = i + 1
    n_live2 = i // 2 + 1
    row_pos = i * Q_BLOCK + sub

    n_pairs = N_IDX_HEADS // 2
    wq = smq_ref[...]
    wcols = [wq[:, SM_WI + h:SM_WI + h + 1] * (N_IDX_HEADS ** -0.5) for h in range(N_IDX_HEADS)]
    qi_all = qi_ref[0].reshape(n_pairs * Q_BLOCK, LANES)

    def score_tile(t2, carry):
        ks = pl.ds(pl.multiple_of(t2 * 2 * LANES, 2 * LANES), 2 * LANES)
        acc = jnp.zeros((Q_BLOCK, 2 * LANES), F32)
        for half in range(2):
            s_all = lax.dot_general(qi_all, ki_ref[half, ks, :], (((1,), (1,)), ((), ())),
                                    preferred_element_type=F32)
            for j in range(n_pairs):
                s = s_all[j * Q_BLOCK:(j + 1) * Q_BLOCK]
                acc = acc + jnp.maximum(s, 0.0) * wcols[2 * j + half]
        acc = acc + 0.0
        for u in range(2):
            col = (t2 * 2 + u) * LANES + lane
            sc = jnp.where(col <= row_pos, acc[:, u * LANES:(u + 1) * LANES], NEG_BIG)
            key_ref[t2 * 2 + u] = _sort_key(sc)
        return carry

    lax.fori_loop(0, n_live2, score_tile, 0)

    @pl.when(i % 2 == 0)
    def _():
        key_ref[i + 1] = jnp.full((Q_BLOCK, LANES), KEY_NEG_BIG, I32)
        mask_ref[i + 1] = jnp.full((Q_BLOCK, LANES), NEG_BIG, F32)

    n_dead = ((nt - 2 * n_live2) * LANES).astype(F32)

    def count_where(pred, upto):
        def body(t, acc):
            return acc + pred(key_ref[t], t).astype(F32)
        acc = lax.fori_loop(0, upto, body, jnp.zeros((Q_BLOCK, LANES), F32))
        return jnp.sum(acc, axis=1, keepdims=True)

    def count_ge(cand):
        live = count_where(lambda kt, t: kt >= cand, 2 * n_live2)
        return live + jnp.where(cand <= KEY_NEG_BIG, n_dead, 0.0)

    thr = _kth_key_search(count_ge, Q_BLOCK, float(topk))
    cnt_gt = count_where(lambda kt, t: kt > thr, 2 * n_live2) + jnp.where(thr < KEY_NEG_BIG, n_dead, 0.0)
    need = float(topk) - cnt_gt

    def causal_of(t):
        return (t * LANES + lane) <= row_pos

    cnt_ceq = count_where(lambda kt, t: (kt == thr) & causal_of(t), n_live)
    any_tie = jnp.max(jnp.where(cnt_ceq > need, 1.0, 0.0)) > 0.0

    @pl.when(jnp.logical_not(any_tie))
    def _():
        def body(t, carry):
            mask_ref[t] = jnp.where((key_ref[t] >= thr) & causal_of(t), 0.0, NEG_BIG)
            return carry
        lax.fori_loop(0, n_live, body, 0)

    @pl.when(any_tie)
    def _():
        upper = (sub <= lane).astype(BF16)
        ones = jnp.ones((LANES, LANES), BF16)

        def body(t, offset):
            kt = key_ref[t]
            cz = causal_of(t)
            eq = ((kt == thr) & cz).astype(BF16)
            prefix = jnp.dot(eq, upper, preferred_element_type=F32) + offset
            sel = ((kt > thr) & cz) | ((eq > 0) & (prefix <= need))
            mask_ref[t] = jnp.where(sel, 0.0, NEG_BIG)
            return offset + jnp.dot(eq, ones, preferred_element_type=F32)

        lax.fori_loop(0, n_live, body, jnp.zeros((Q_BLOCK, LANES), F32))

    za = za_ref[...]
    lo_lane = lane < ATT_HEAD_DIM
    n_qp = N_ATT_HEADS // 2
    for g in range(N_KV_HEADS):
        pairs = [p for p in range(n_qp) if (2 * p) // (N_ATT_HEADS // N_KV_HEADS) == g]
        q_stack = jnp.concatenate([q_ref[0, p] for p in pairs], axis=0)
        for half in range(2):
            heads = [2 * p + half for p in pairs]

            def logit_tile(t2, mx, heads=heads, half=half, g=g):
                ks = pl.ds(pl.multiple_of(t2 * 2 * LANES, 2 * LANES), 2 * LANES)
                l_all = lax.dot_general(q_stack, kk_ref[2 * g + half, ks, :], (((1,), (1,)), ((), ())),
                                        preferred_element_type=F32)
                out = []
                for n, h in enumerate(heads):
                    m_h = mx[n]
                    for u in range(2):
                        t = t2 * 2 + u
                        l = l_all[n * Q_BLOCK:(n + 1) * Q_BLOCK, u * LANES:(u + 1) * LANES] + mask_ref[t]
                        lg_ref[h, t] = l
                        m_h = jnp.maximum(m_h, jnp.where(t < i - 1, l, NEG_BIG))
                    out.append(m_h)
                return tuple(out)

            mx0 = tuple(jnp.full((Q_BLOCK, LANES), NEG_BIG, F32) for _ in heads)
            mx = lax.fori_loop(0, n_live2, logit_tile, mx0)
            for n, h in enumerate(heads):
                l0 = lg_ref[h, i] + bias_ref[h, 0]
                lg_ref[h, i] = l0
                m_h = jnp.maximum(mx[n], l0)
                im1 = jnp.maximum(i - 1, 0)
                l1 = lg_ref[h, im1] + jnp.where(i > 0, bias_ref[h, 1], 0.0)
                lg_ref[h, im1] = l1
                m_h = jnp.maximum(m_h, l1)
                m_row = jnp.max(m_h, axis=1, keepdims=True)

                def pv_tile(t, carry, h=h, half=half, g=g, m_row=m_row):
                    ssum, acc = carry
                    p = jnp.exp(lg_ref[h, t] - m_row)
                    ts = pl.ds(pl.multiple_of(t * LANES, LANES), LANES)
                    acc = acc + jnp.dot(p.astype(BF16), vv_ref[2 * g + half, ts, :], preferred_element_type=F32)
                    return ssum + p, acc

                ssum, acc = lax.fori_loop(0, n_live, pv_tile,
                                          (jnp.zeros((Q_BLOCK, LANES), F32), jnp.zeros((Q_BLOCK, LANES), F32)))
                l_row = jnp.sum(ssum, axis=1, keepdims=True)
                lg_ref[h, nt] = acc / l_row
        for p in pairs:
            o_pair = jnp.where(lo_lane, lg_ref[2 * p, nt], lg_ref[2 * p + 1, nt])
            cs = slice(p * LANES, (p + 1) * LANES)
            o_ref[:, cs] = (o_pair * _silu(za[:, cs])).astype(BF16)


def _dsa_prompt(rel_table, q_blk, qi_blk, small, za, k2d, v2d, batch, seq):
    nb = seq // Q_BLOCK
    nt = seq // LANES
    topk = min(TOPK_MAX, seq // 4)
    n_qp = D_ATT // LANES
    n_ip = N_IDX_HEADS * IDX_DIM // LANES
    t = batch * seq
    return pl.pallas_call(
        functools.partial(_dsa_prompt_body, seq=seq, topk=topk),
        grid=(batch, nb),
        in_specs=[
            pl.BlockSpec(memory_space=pltpu.SMEM),
            pl.BlockSpec((1, n_qp, Q_BLOCK, LANES), lambda b, i: (b * nb + i, 0, 0, 0)),
            pl.BlockSpec((1, n_ip, Q_BLOCK, LANES), lambda b, i: (b * nb + i, 0, 0, 0)),
            pl.BlockSpec((Q_BLOCK, LANES), lambda b, i: (b * nb + i, 0)),
            pl.BlockSpec((Q_BLOCK, D_ATT), lambda b, i: (b * nb + i, 0)),
            pl.BlockSpec((seq, KV_DIM), lambda b, i: (b, 0)),
            pl.BlockSpec((seq, KV_DIM), lambda b, i: (b, 0)),
            pl.BlockSpec((seq, LANES), lambda b, i: (b, 0)),
        ],
        out_specs=pl.BlockSpec((Q_BLOCK, D_ATT), lambda b, i: (b * nb + i, 0)),
        out_shape=jax.ShapeDtypeStruct((t, D_ATT), BF16),
        scratch_shapes=[
            pltpu.VMEM((2 * N_KV_HEADS, seq, LANES), BF16),
            pltpu.VMEM((2 * N_KV_HEADS, seq, LANES), BF16),
            pltpu.VMEM((2, seq, LANES), BF16),
            pltpu.VMEM((N_ATT_HEADS, 2, Q_BLOCK, LANES), F32),
            pltpu.VMEM((nt + 1, Q_BLOCK, LANES), I32),
            pltpu.VMEM((nt + 1, Q_BLOCK, LANES), F32),
            pltpu.VMEM((N_ATT_HEADS, nt + 1, Q_BLOCK, LANES), F32),
        ],
        compiler_params=_cparams(("arbitrary", "arbitrary")),
        name="dsa_prompt",
    )(rel_table, q_blk, qi_blk, small, za, k2d, v2d, small)


KT = 2 * LANES


def _fold8(x, op):
    binop = {jnp.sum: jnp.add, jnp.max: jnp.maximum}[op]
    r = x.reshape(x.shape[0] // SUBLANES, SUBLANES, x.shape[1])
    while r.shape[0] > 1:
        half = r.shape[0] // 2
        r = binop(r[:half], r[half:])
    return r[0]


def _dsa_prompt_t_body(rel_ref, q_ref, qi_ref, smq_ref, za_ref, k_ref, v_ref, sms_ref, o_ref,
                       kk_ref, vvt_ref, ki_ref, bias_ref, qt_ref, qit_ref, key_ref, mask_ref, lg_ref,
                       *, seq, topk):
    b = pl.program_id(0)
    i = pl.program_id(1)
    n_t = seq // KT
    hpg = N_ATT_HEADS // N_KV_HEADS
    far_bucket = _far_bucket_checked(MAX_DISTANCE + 1)
    sub1 = lax.broadcasted_iota(I32, (LANES, LANES), 0)
    lane1 = lax.broadcasted_iota(I32, (LANES, LANES), 1)

    @pl.when((b == 0) & (i == 0))
    def _():
        for dt in range(2):
            bucket = _rel_bucket(dt * LANES + lane1 - sub1)
            for h in range(N_ATT_HEADS):
                acc = jnp.zeros((LANES, LANES), F32)
                for bk in range(N_BUCKETS):
                    acc = jnp.where(bucket == bk, rel_ref[bk, h], acc)
                bias_ref[h, dt] = acc - rel_ref[far_bucket, h]

    @pl.when(i == 0)
    def _():
        lo = lax.broadcasted_iota(I32, (seq, LANES), 1) < ATT_HEAD_DIM
        kf = k_ref[...]
        g0 = jnp.where(lo, kf, 0.0)
        g1 = jnp.where(lo, 0.0, kf)
        kk_ref[0] = g0.astype(BF16)
        kk_ref[1] = pltpu.roll(g0, ATT_HEAD_DIM, 1).astype(BF16)
        kk_ref[2] = pltpu.roll(g1, ATT_HEAD_DIM, 1).astype(BF16)
        kk_ref[3] = g1.astype(BF16)
        c0 = jnp.where(lo, sms_ref[...], 0.0)
        ki_ref[0] = c0.astype(BF16)
        ki_ref[1] = pltpu.roll(c0, IDX_DIM, 1).astype(BF16)
        lo_t = lax.broadcasted_iota(I32, (KT, LANES), 1) < ATT_HEAD_DIM
        for t in range(n_t):
            vf = v_ref[t * KT:(t + 1) * KT, :]
            w0 = jnp.where(lo_t, vf, 0.0)
            w1 = jnp.where(lo_t, 0.0, vf)
            vvt_ref[0, t] = w0.T.astype(BF16)
            vvt_ref[1, t] = pltpu.roll(w0, ATT_HEAD_DIM, 1).T.astype(BF16)
            vvt_ref[2, t] = pltpu.roll(w1, ATT_HEAD_DIM, 1).T.astype(BF16)
            vvt_ref[3, t] = w1.T.astype(BF16)

    n_qp = D_ATT // LANES
    n_ip = N_IDX_HEADS * IDX_DIM // LANES
    for p in range(n_qp):
        qt_ref[:, p * LANES:(p + 1) * LANES] = q_ref[0, p].astype(F32).T.astype(BF16)
    for j in range(n_ip):
        qit_ref[:, j * LANES:(j + 1) * LANES] = qi_ref[0, j].astype(F32).T.astype(BF16)
    sm_t = smq_ref[...].T
    wrow = [sm_t[SM_WI + h:SM_WI + h + 1, :] * (N_IDX_HEADS ** -0.5) for h in range(N_IDX_HEADS)]

    n_live = i // 2 + 1
    n_dead = ((n_t - n_live) * KT).astype(F32)
    kidx = lax.broadcasted_iota(I32, (KT, LANES), 0)
    qpos = i * Q_BLOCK + lax.broadcasted_iota(I32, (KT, LANES), 1)

    def causal_of(t):
        return (t * KT + kidx) <= qpos

    def tile_rows(t):
        return pl.ds(pl.multiple_of(t * KT, KT), KT)

    def score_tile(t, carry):
        acc = jnp.zeros((KT, LANES), F32)
        for half in range(2):
            s_all = jnp.dot(ki_ref[half, tile_rows(t), :], qit_ref[...], preferred_element_type=F32)
            for j in range(n_ip):
                acc = acc + jnp.maximum(s_all[:, j * LANES:(j + 1) * LANES], 0.0) * wrow[2 * j + half]
        acc = acc + 0.0
        key_ref[t] = _sort_key(jnp.where(causal_of(t), acc, NEG_BIG))
        return carry

    lax.fori_loop(0, n_live, score_tile, 0)

    def count_where(pred):
        def body(t, acc):
            return acc + _fold8(pred(key_ref[t], t).astype(F32), jnp.sum)
        acc = lax.fori_loop(0, n_live, body, jnp.zeros((SUBLANES, LANES), F32))
        return jnp.sum(acc, axis=0, keepdims=True)

    def count_ge(cand):
        return count_where(lambda kt, t: kt >= cand) + jnp.where(cand <= KEY_NEG_BIG, n_dead, 0.0)

    thr = _kth_key_search(count_ge, (1, LANES), float(topk))
    cnt_gt = count_where(lambda kt, t: kt > thr) + jnp.where(thr < KEY_NEG_BIG, n_dead, 0.0)
    need = float(topk) - cnt_gt
    cnt_ceq = count_where(lambda kt, t: (kt == thr) & causal_of(t))
    any_tie = jnp.max(jnp.where(cnt_ceq > need, 1.0, 0.0)) > 0.0

    @pl.when(jnp.logical_not(any_tie))
    def _():
        def body(t, carry):
            mask_ref[t] = jnp.where((key_ref[t] >= thr) & causal_of(t), 0.0, NEG_BIG)
            return carry
        lax.fori_loop(0, n_live, body, 0)

    @pl.when(any_tie)
    def _():
        rk = lax.broadcasted_iota(I32, (KT, KT), 0)
        ck = lax.broadcasted_iota(I32, (KT, KT), 1)
        lower = (ck <= rk).astype(BF16)

        def body(t, offset):
            kt = key_ref[t]
            cz = causal_of(t)
            eq = ((kt == thr) & cz).astype(F32)
            prefix = jnp.dot(lower, eq.astype(BF16), preferred_element_type=F32) + offset
            sel = ((kt > thr) & cz) | ((eq > 0.0) & (prefix <= need))
            mask_ref[t] = jnp.where(sel, 0.0, NEG_BIG)
            return offset + jnp.sum(_fold8(eq, jnp.sum), axis=0, keepdims=True)

        lax.fori_loop(0, n_live, body, jnp.zeros((1, LANES), F32))

    za = za_ref[...]
    near_lo = jnp.maximum(i - 1, 0) // 2
    neg8 = jnp.full((SUBLANES, LANES), NEG_BIG, F32)
    zero8 = jnp.zeros((SUBLANES, LANES), F32)
    for g in range(N_KV_HEADS):
        qt_g = qt_ref[:, g * 2 * LANES:(g + 1) * 2 * LANES]

        def logit_tile(t, mx, g=g, qt_g=qt_g):
            mx = list(mx)
            far = t < near_lo
            for half in range(2):
                l_all = jnp.dot(kk_ref[2 * g + half, tile_rows(t), :], qt_g, preferred_element_type=F32)
                for n in range(2):
                    hl = 2 * n + half
                    l = l_all[:, n * LANES:(n + 1) * LANES] + mask_ref[t]
                    lg_ref[hl, t] = l
                    mx[hl] = jnp.maximum(mx[hl], jnp.where(far, _fold8(l, jnp.max), NEG_BIG))
            return tuple(mx)

        mx = lax.fori_loop(0, n_live, logit_tile, (neg8,) * hpg)
        m_row = []
        for hl in range(hpg):
            h = g * hpg + hl
            r0 = pl.ds(pl.multiple_of((i % 2) * LANES, LANES), LANES)
            lg_ref[hl, i // 2, r0, :] = lg_ref[hl, i // 2, r0, :] + bias_ref[h, 0]
            im1 = jnp.maximum(i - 1, 0)
            r1 = pl.ds(pl.multiple_of((im1 % 2) * LANES, LANES), LANES)
            lg_ref[hl, im1 // 2, r1, :] = lg_ref[hl, im1 // 2, r1, :] + jnp.where(i > 0, bias_ref[h, 1], 0.0)
            m8 = jnp.maximum(mx[hl], jnp.maximum(_fold8(lg_ref[hl, near_lo], jnp.max),
                                                 _fold8(lg_ref[hl, i // 2], jnp.max)))
            m_row.append(jnp.max(m8, axis=0, keepdims=True))

        def pv_tile(t, carry, g=g, m_row=m_row):
            ssum = list(carry[:hpg])
            acc = list(carry[hpg:])
            for n in range(2):
                for half in range(2):
                    hl = 2 * n + half
                    p = jnp.exp(lg_ref[hl, t] - m_row[hl])
                    ssum[hl] = ssum[hl] + _fold8(p, jnp.sum)
                    acc[n] = acc[n] + jnp.dot(vvt_ref[2 * g + half, t], p.astype(BF16),
                                              preferred_element_type=F32)
            return tuple(ssum) + tuple(acc)

        zacc = jnp.zeros((LANES, LANES), F32)
        res = lax.fori_loop(0, n_live, pv_tile, (zero8,) * hpg + (zacc, zacc))
        for n in range(2):
            l_lo = jnp.sum(res[2 * n], axis=0, keepdims=True)
            l_hi = jnp.sum(res[2 * n + 1], axis=0, keepdims=True)
            inv = jnp.where(sub1 < ATT_HEAD_DIM, 1.0 / l_lo, 1.0 / l_hi)
            o_pair = (res[hpg + n] * inv).T
            cs = slice((2 * g + n) * LANES, (2 * g + n + 1) * LANES)
            o_ref[:, cs] = (o_pair * _silu(za[:, cs])).astype(BF16)


def _dsa_prompt_t(rel_table, q_blk, qi_blk, small, za, k2d, v2d, batch, seq):
    assert N_KV_HEADS == 2 and N_ATT_HEADS // N_KV_HEADS == 4 and seq % KT == 0
    nb = seq // Q_BLOCK
    n_t = seq // KT
    topk = min(TOPK_MAX, seq // 4)
    n_qp = D_ATT // LANES
    n_ip = N_IDX_HEADS * IDX_DIM // LANES
    t = batch * seq
    return pl.pallas_call(
        functools.partial(_dsa_prompt_t_body, seq=seq, topk=topk),
        grid=(batch, nb),
        in_specs=[
            pl.BlockSpec(memory_space=pltpu.SMEM),
            pl.BlockSpec((1, n_qp, Q_BLOCK, LANES), lambda b, i: (b * nb + i, 0, 0, 0)),
            pl.BlockSpec((1, n_ip, Q_BLOCK, LANES), lambda b, i: (b * nb + i, 0, 0, 0)),
            pl.BlockSpec((Q_BLOCK, LANES), lambda b, i: (b * nb + i, 0)),
            pl.BlockSpec((Q_BLOCK, D_ATT), lambda b, i: (b * nb + i, 0)),
            pl.BlockSpec((seq, KV_DIM), lambda b, i: (b, 0)),
            pl.BlockSpec((seq, KV_DIM), lambda b, i: (b, 0)),
            pl.BlockSpec((seq, LANES), lambda b, i: (b, 0)),
        ],
        out_specs=pl.BlockSpec((Q_BLOCK, D_ATT), lambda b, i: (b * nb + i, 0)),
        out_shape=jax.ShapeDtypeStruct((t, D_ATT), BF16),
        scratch_shapes=[
            pltpu.VMEM((2 * N_KV_HEADS, seq, LANES), BF16),
            pltpu.VMEM((2 * N_KV_HEADS, n_t, LANES, KT), BF16),
            pltpu.VMEM((2, seq, LANES), BF16),
            pltpu.VMEM((N_ATT_HEADS, 2, LANES, LANES), F32),
            pltpu.VMEM((LANES, D_ATT), BF16),
            pltpu.VMEM((LANES, N_IDX_HEADS * IDX_DIM), BF16),
            pltpu.VMEM((n_t, KT, LANES), I32),
            pltpu.VMEM((n_t, KT, LANES), F32),
            pltpu.VMEM((N_ATT_HEADS // N_KV_HEADS, n_t, KT, LANES), F32),
        ],
        compiler_params=_cparams(("arbitrary", "arbitrary")),
        name="dsa_prompt",
    )(rel_table, q_blk, qi_blk, small, za, k2d, v2d, small)


def _softplus(x):
    return jnp.maximum(x, 0.0) + jnp.log1p(jnp.exp(-jnp.abs(x)))


def _lane_pick(x, idx):
    lane = lax.broadcasted_iota(I32, x.shape, 1)
    return jnp.sum(jnp.where(lane == idx, x, 0.0), axis=1, keepdims=True)


def _gated_norm(o, gn, z):
    y = o * lax.rsqrt(jnp.mean(o * o, axis=-1, keepdims=True) + RMS_EPS)
    return (y * gn) * _silu(z)


def _gdn_prompt_body(alog_ref, dtb_ref, xq_ref, xk_ref, xv_ref, wq_ref, wk_ref, wv_ref, sm_ref, zg_ref, gn_ref,
                     o_ref, sfin_ref, xs_ref, q_s, k_s, v_s, g_s, b_s, *, seq, hp):
    h0 = pl.program_id(1) * hp
    hist = SUBLANES
    slabs = [slice(hh * LANES, (hh + 1) * LANES) for hh in range(hp)]

    def conv_into(x_ref, w_ref, dst, post):
        xs_ref[0:hist, :] = jnp.zeros((hist, hp * LANES), F32)
        xs_ref[hist:hist + seq, :] = x_ref[...]
        base = hist - (CONV_W - 1)
        for hs in slabs:
            acc = xs_ref[base:base + seq, hs] * w_ref[0:1, hs]
            for j in range(1, CONV_W):
                acc = acc + xs_ref[base + j:base + j + seq, hs] * w_ref[j:j + 1, hs]
            dst[:, hs] = post(_silu(acc))

    def l2n(x):
        return x * lax.rsqrt(jnp.sum(x * x, axis=-1, keepdims=True) + 1e-6)

    conv_into(xq_ref, wq_ref, q_s, lambda x: l2n(x) * (GDN_HEAD_DIM ** -0.5))
    conv_into(xk_ref, wk_ref, k_s, l2n)
    conv_into(xv_ref, wv_ref, v_s, lambda x: x)

    sm = sm_ref[...]
    for hh, hs in enumerate(slabs):
        a_col = _lane_pick(sm, SM_AG + h0 + hh)
        b_col = _lane_pick(sm, SM_BG + h0 + hh)
        a_neg = -jnp.exp(jnp.zeros((1, 1), F32) + alog_ref[h0 + hh])
        g_col = a_neg * _softplus(a_col + dtb_ref[h0 + hh])
        g_s[:, hs] = jnp.broadcast_to(g_col, (seq, LANES))
        b_s[:, hs] = jnp.broadcast_to(1.0 / (1.0 + jnp.exp(-b_col)), (seq, LANES))

    c = GDN_C
    ri = lax.broadcasted_iota(I32, (c, c), 0)
    ci = lax.broadcasted_iota(I32, (c, c), 1)
    tril = ri >= ci
    strict = ri > ci
    tril_f = tril.astype(F32)
    eye = (ri == ci).astype(F32)
    gn = gn_ref[...]
    off_masks = []
    for lg in range(int(math.log2(c))):
        same_pair = (ri >> (lg + 1)) == (ci >> (lg + 1))
        off_masks.append(same_pair & (((ri >> lg) & 1) == 1) & (((ci >> lg) & 1) == 0))

    tril_b = tril.astype(BF16)

    def cumsum_rows(g):
        hi = g.astype(BF16)
        r1 = g - hi.astype(F32)
        mid = r1.astype(BF16)
        lo = (r1 - mid.astype(F32)).astype(BF16)
        return sum(jnp.dot(tril_b, piece, preferred_element_type=F32) for piece in (hi, mid, lo))

    def chunk(n, states):
        heads = range(hp)
        sl = pl.ds(pl.multiple_of(n * c, c), c)
        q = [q_s[sl, hs] for hs in slabs]
        k = [k_s[sl, hs] for hs in slabs]
        v = [v_s[sl, hs] for hs in slabs]
        bb = [b_s[sl, hs] for hs in slabs]
        gcum = [cumsum_rows(g_s[sl, hs]) for hs in slabs]
        gcum_row = [g.T for g in gcum]
        decay = [jnp.where(tril, jnp.exp(jnp.where(tril, gcum[h] - gcum_row[h], 0.0)), 0.0) for h in heads]
        eg = [jnp.exp(g) for g in gcum]
        kb = [k[h] * bb[h] for h in heads]
        vb = [v[h] * bb[h] for h in heads]
        kk = [_bdot_nt(kb[h], k[h]) for h in heads]
        qk = [_bdot_nt(q[h], k[h]) for h in heads]
        a_mat = [jnp.where(strict, kk[h] * decay[h], 0.0) for h in heads]
        attn = [qk[h] * decay[h] for h in heads]
        x = [eye - jnp.where(off_masks[0], a_mat[h], 0.0) for h in heads]
        for om in off_masks[1:]:
            inner = [_bdot(jnp.where(om, a_mat[h], 0.0), x[h]) for h in heads]
            x = [x[h] - _bdot(x[h], inner[h]) for h in heads]
        u = [_bdot(x[h], vb[h]) for h in heads]
        w = [_bdot(x[h], kb[h] * eg[h]) for h in heads]
        g_last = [g[c - 1:c, :] for g in gcum]
        k_dec_t = [(k[h] * jnp.exp(g_last[h] - gcum[h])).T for h in heads]
        ws = [_bdot(w[h], states[h]) for h in heads]
        qs = [_bdot(q[h] * eg[h], states[h]) for h in heads]
        v_new = [u[h] - ws[h] for h in heads]
        av = [_bdot(attn[h], v_new[h]) for h in heads]
        kv = [_bdot(k_dec_t[h], v_new[h]) for h in heads]
        for h, hs in enumerate(slabs):
            o_ref[sl, hs] = _gated_norm(qs[h] + av[h], gn, zg_ref[sl, hs]).astype(BF16)
        return tuple(states[h] * jnp.exp(g_last[h]) + kv[h] for h in heads)

    zero_state = jnp.zeros((GDN_HEAD_DIM, GDN_HEAD_DIM), F32)
    finals = lax.fori_loop(0, seq // c, chunk, (zero_state,) * hp)
    for hh in range(hp):
        sfin_ref[hh] = finals[hh]


GDN_HEADS_PER_STEP = 4


def _gdn_prompt(a_log, dt_bias, qkv2d, conv_w, small, zg, gdn_norm, batch, seq):
    nh = N_GDN_HEADS
    hp = GDN_HEADS_PER_STEP
    ng = nh // hp
    t = batch * seq
    w = hp * LANES
    once = pl.Buffered(1)
    blk = lambda off: pl.BlockSpec((seq, w), lambda b, j: (b, off + j), pipeline_mode=once)
    wblk = lambda off: pl.BlockSpec((CONV_W, w), lambda b, j: (0, off + j))
    return pl.pallas_call(
        functools.partial(_gdn_prompt_body, seq=seq, hp=hp),
        grid=(batch, ng),
        in_specs=[pl.BlockSpec(memory_space=pltpu.SMEM), pl.BlockSpec(memory_space=pltpu.SMEM),
                  blk(0), blk(ng), blk(2 * ng), wblk(0), wblk(ng), wblk(2 * ng),
                  pl.BlockSpec((seq, LANES), lambda b, j: (b, 0)),
                  pl.BlockSpec((seq, w), lambda b, j: (b, j), pipeline_mode=once),
                  pl.BlockSpec((1, LANES), lambda b, j: (0, 0))],
        out_specs=[pl.BlockSpec((seq, w), lambda b, j: (b, j)),
                   pl.BlockSpec((None, hp, GDN_HEAD_DIM, GDN_HEAD_DIM), lambda b, j: (b, j, 0, 0))],
        out_shape=[jax.ShapeDtypeStruct((t, D_GDN), BF16),
                   jax.ShapeDtypeStruct((batch, nh, GDN_HEAD_DIM, GDN_HEAD_DIM), F32)],
        scratch_shapes=[pltpu.VMEM((seq + 2 * SUBLANES, w), F32)] + [pltpu.VMEM((seq, w), F32)] * 5,
        compiler_params=_cparams(("parallel", "arbitrary")),
        name="gdn_prompt",
    )(a_log, dt_bias, qkv2d, qkv2d, qkv2d, conv_w, conv_w, conv_w, small, zg, gdn_norm.reshape(1, LANES))


def _outproj_body(*refs, gate_att):
    if gate_att:
        x_ref, att_ref, za_ref, gdn_ref, w_ref, g_ref, y_ref = refs
        att = (att_ref[...] * _silu(za_ref[...])).astype(BF16)
    else:
        x_ref, att_ref, gdn_ref, w_ref, g_ref, y_ref = refs
        att = att_ref[...]
    y = x_ref[...] + jnp.dot(att, w_ref[0:D_ATT, :], preferred_element_type=F32) \
        + jnp.dot(gdn_ref[...], w_ref[D_ATT:D_ATT + D_GDN, :], preferred_element_type=F32)
    y = y * lax.rsqrt(jnp.mean(y * y, axis=-1, keepdims=True) + RMS_EPS)
    y_ref[...] = y * g_ref[...]


def _outproj(x2d, att, gdn, w_bf, norm_g, tm, za=None):
    t, d = x2d.shape
    row = lambda n: pl.BlockSpec((tm, n), lambda i: (i, 0))
    full = lambda a, b: pl.BlockSpec((a, b), lambda i: (0, 0))
    ins = [x2d, att] + ([za] if za is not None else []) + [gdn, w_bf, norm_g.reshape(1, d)]
    specs = [row(d), row(D_ATT)] + ([row(D_ATT)] if za is not None else []) + \
            [row(D_GDN), full(D_ATT + D_GDN, d), full(1, d)]
    return pl.pallas_call(
        functools.partial(_outproj_body, gate_att=za is not None),
        grid=(t // tm,),
        in_specs=specs,
        out_specs=row(d),
        out_shape=jax.ShapeDtypeStruct((t, d), F32),
        compiler_params=_cparams(("parallel",)),
        name="outproj_gated" if za is not None else "outproj",
    )(*ins)


PAGES_PER_STEP = 8
SAMPLE_Q_ROWS = 16


def _idx_scores_body(pt_ref, qi_ref, wi_ref, *rest):
    page_refs = rest[:PAGES_PER_STEP]
    o_ref = rest[PAGES_PER_STEP]
    qi = qi_ref[0]
    wi = wi_ref[0] * (N_IDX_HEADS ** -0.5)
    for j, pr in enumerate(page_refs):
        s = _bdot(qi, pr[...])
        sc = jnp.sum(jnp.maximum(s, 0.0) * wi, axis=0, keepdims=True)
        o_ref[0, j:j + 1, :] = sc + 0.0


def _idx_scores(page_table, qi3, wi3, kidx_pages):
    db, n_pages = page_table.shape
    page = kidx_pages.shape[2]
    pg = PAGES_PER_STEP

    def page_spec(j):
        return pl.BlockSpec((None, IDX_DIM, page), lambda b, s, pt: (pt[b, s * pg + j], 0, 0))

    grid_spec = pltpu.PrefetchScalarGridSpec(
        num_scalar_prefetch=1,
        grid=(db, n_pages // pg),
        in_specs=[pl.BlockSpec((1, N_IDX_HEADS, IDX_DIM), lambda b, s, pt: (b, 0, 0)),
                  pl.BlockSpec((1, N_IDX_HEADS, 1), lambda b, s, pt: (b, 0, 0))]
                 + [page_spec(j) for j in range(pg)],
        out_specs=pl.BlockSpec((1, pg, page), lambda b, s, pt: (b, s, 0)),
    )
    return pl.pallas_call(
        _idx_scores_body,
        grid_spec=grid_spec,
        out_shape=jax.ShapeDtypeStruct((db, n_pages, page), F32),
        compiler_params=_cparams(("parallel", "arbitrary")),
        name="sample_idx_scores",
    )(page_table, qi3, wi3, *([kidx_pages] * pg))


def _sample_select_body(sc_ref, qi_ref, sm_ref, mask_ref, key_ref, *, n_tiles, topk):
    rows = sc_ref.shape[1]
    lane = lax.broadcasted_iota(I32, (rows, LANES), 1)

    def fill(t, carry):
        key_ref[t] = _sort_key(sc_ref[t])
        return carry

    lax.fori_loop(0, n_tiles, fill, 0)
    sm = sm_ref[...]
    ki_new = sm[:, 0:IDX_DIM].astype(BF16).astype(F32)
    acc = jnp.zeros((rows, 1), F32)
    for hd in range(N_IDX_HEADS):
        qh = qi_ref[:, hd * IDX_DIM:(hd + 1) * IDX_DIM].astype(F32)
        s = jnp.sum(qh * ki_new, axis=1, keepdims=True)
        acc = acc + jnp.maximum(s, 0.0) * (sm[:, SM_WI + hd:SM_WI + hd + 1] * (N_IDX_HEADS ** -0.5))
    sc_new = jnp.broadcast_to(acc + 0.0, (rows, LANES))
    key_ref[n_tiles] = jnp.where(lane == 0, _sort_key(sc_new), INT_MIN)

    def count_where(pred):
        def body(t, a):
            return a + pred(key_ref[t]).astype(F32)
        a = lax.fori_loop(0, n_tiles + 1, body, jnp.zeros((rows, LANES), F32))
        return jnp.sum(a, axis=1, keepdims=True)

    thr = _kth_key_search(lambda cand: count_where(lambda kt: kt >= cand), (rows, 1), float(topk))
    need = float(topk) - count_where(lambda kt: kt > thr)
    cnt_eq = count_where(lambda kt: kt == thr)
    any_tie = jnp.max(jnp.where(cnt_eq > need, 1.0, 0.0)) > 0.0

    @pl.when(jnp.logical_not(any_tie))
    def _():
        def body(t, carry):
            mask_ref[t] = jnp.where(key_ref[t] >= thr, 0.0, NEG_BIG)
            return carry
        lax.fori_loop(0, n_tiles + 1, body, 0)

    @pl.when(any_tie)
    def _():
        sub = lax.broadcasted_iota(I32, (LANES, LANES), 0)
        lane2 = lax.broadcasted_iota(I32, (LANES, LANES), 1)
        upper = (sub <= lane2).astype(BF16)
        ones = jnp.ones((LANES, LANES), BF16)

        def body(t, offset):
            kt = key_ref[t]
            eq = (kt == thr).astype(BF16)
            prefix = jnp.dot(eq, upper, preferred_element_type=F32) + offset
            sel = (kt > thr) | ((eq > 0) & (prefix <= need))
            mask_ref[t] = jnp.where(sel, 0.0, NEG_BIG)
            return offset + jnp.dot(eq, ones, preferred_element_type=F32)

        lax.fori_loop(0, n_tiles + 1, body, jnp.zeros((rows, LANES), F32))


def _sample_select(scores_t, qi2d, small, topk):
    n_tiles, db, page = scores_t.shape
    vm = pl.BlockSpec(memory_space=pltpu.VMEM)
    return pl.pallas_call(
        functools.partial(_sample_select_body, n_tiles=n_tiles, topk=topk),
        in_specs=[vm, vm, vm],
        out_specs=vm,
        out_shape=jax.ShapeDtypeStruct((n_tiles + 1, db, page), F32),
        scratch_shapes=[pltpu.VMEM((n_tiles + 1, db, page), I32)],
        compiler_params=pltpu.CompilerParams(vmem_limit_bytes=VMEM_LIMIT),
        name="sample_select",
    )(scores_t, qi2d, small)


def _sample_attn_body(pt_ref, q_ref, mask_ref, mnew_ref, knew_ref, vnew_ref, relt_ref, *rest, past_len, page):
    pg = PAGES_PER_STEP
    k_refs = rest[:pg]
    v_refs = rest[pg:2 * pg]
    o_ref, m_s, l_s, acc_s = rest[2 * pg:]
    s = pl.program_id(1)
    n_steps = pl.num_programs(1)
    nh = SAMPLE_Q_ROWS
    far_bucket = _far_bucket_checked(MAX_DISTANCE + 1)

    @pl.when(s == 0)
    def _():
        m_s[...] = jnp.full(m_s.shape, NEG_BIG, F32)
        l_s[...] = jnp.zeros(l_s.shape, F32)
        acc_s[...] = jnp.zeros(acc_s.shape, F32)

    relt = relt_ref[...]
    c_far = relt[:, far_bucket:far_bucket + 1]

    def bias_of(dist):
        bucket = _rel_bucket(dist)
        acc = jnp.zeros((nh, dist.shape[1]), F32)
        for bk in range(N_BUCKETS):
            acc = jnp.where(bucket == bk, relt[:, bk:bk + 1], acc)
        return acc - c_far

    q = q_ref[0].astype(BF16)
    logits = []
    for j in range(pg):
        l = _bdot(q, k_refs[j][...]) + mask_ref[0, j:j + 1, :]
        logits.append(l)
    logits = jnp.concatenate(logits, axis=1)

    kpos = (s * pg) * page + lax.broadcasted_iota(I32, (1, pg * page), 1)
    logits = logits + bias_of(past_len - kpos)

    m_old = m_s[...]
    m_new = jnp.maximum(m_old, jnp.max(logits, axis=1, keepdims=True))
    alpha = jnp.exp(m_old - m_new)
    p = jnp.exp(logits - m_new)
    l_new = l_s[...] * alpha + jnp.sum(p, axis=1, keepdims=True)
    acc = acc_s[...] * alpha
    for j in range(pg):
        acc = acc + _bdot_nt(p[:, j * page:(j + 1) * page], v_refs[j][...])
    m_s[...] = m_new
    l_s[...] = l_new
    acc_s[...] = acc

    @pl.when(s == n_steps - 1)
    def _():
        kn = knew_ref[0].astype(BF16).astype(F32)
        vn = vnew_ref[0].astype(BF16).astype(F32)
        ln = jnp.sum(q.astype(F32) * kn, axis=1, keepdims=True) + bias_of(jnp.zeros((1, 1), I32)) \
            + mnew_ref[0][:, 0:1]
        m_fin = jnp.maximum(m_new, ln)
        a2 = jnp.exp(m_new - m_fin)
        pn = jnp.exp(ln - m_fin)
        l_fin = l_new * a2 + pn
        res = (acc * a2 + pn.astype(BF16).astype(F32) * vn) / l_fin
        row = lax.broadcasted_iota(I32, res.shape, 0)
        hpg = N_ATT_HEADS // N_KV_HEADS
        o_ref[0] = jnp.where((row >= hpg) & (row < 2 * hpg), pltpu.roll(res, ATT_HEAD_DIM, 1), res)


def _sample_attn(page_table, q_lh, mask_pages, mask_new, k_new, v_new, rel_t, k_pages, v_pages, past_len):
    db, n_pages = page_table.shape
    page = k_pages.shape[2]
    pg = PAGES_PER_STEP

    def page_spec(j):
        return pl.BlockSpec((None, KV_DIM, page), lambda b, s, pt: (pt[b, s * pg + j], 0, 0))

    row3 = lambda n: pl.BlockSpec((1, 1, n), lambda b, s, pt: (b, 0, 0))
    grid_spec = pltpu.PrefetchScalarGridSpec(
        num_scalar_prefetch=1,
        grid=(db, n_pages // pg),
        in_specs=[pl.BlockSpec((1, SAMPLE_Q_ROWS, LANES), lambda b, s, pt: (b, 0, 0)),
                  pl.BlockSpec((1, pg, page), lambda b, s, pt: (b, s, 0)),
                  row3(LANES), row3(KV_DIM), row3(KV_DIM),
                  pl.BlockSpec((SAMPLE_Q_ROWS, N_BUCKETS), lambda b, s, pt: (0, 0))]
                 + [page_spec(j) for j in range(pg)] * 2,
        out_specs=pl.BlockSpec((1, SAMPLE_Q_ROWS, LANES), lambda b, s, pt: (b, 0, 0)),
        scratch_shapes=[pltpu.VMEM((SAMPLE_Q_ROWS, 1), F32), pltpu.VMEM((SAMPLE_Q_ROWS, 1), F32),
                        pltpu.VMEM((SAMPLE_Q_ROWS, LANES), F32)],
    )
    return pl.pallas_call(
        functools.partial(_sample_attn_body, past_len=past_len, page=page),
        grid_spec=grid_spec,
        out_shape=jax.ShapeDtypeStruct((db, SAMPLE_Q_ROWS, LANES), F32),
        compiler_params=_cparams(("parallel", "arbitrary")),
        name="sample_attn",
    )(page_table, q_lh, mask_pages, mask_new, k_new, v_new, rel_t, *([k_pages] * pg), *([v_pages] * pg))


def _gdn_sample_body(alog_ref, dtb_ref, x_ref, cst_ref, w_ref, sm_ref, zg_ref, gn_ref, s0_ref,
                     o_ref, s_ref, cnew_ref):
    x = x_ref[0]
    cst = cst_ref[...]
    w = w_ref[...]
    acc = cst[0:1, :] * w[0:1, :]
    for j in range(1, CONV_W - 1):
        acc = acc + cst[j:j + 1, :] * w[j:j + 1, :]
    acc = acc + x * w[CONV_W - 1:CONV_W, :]
    xc = _silu(acc)
    cnew_ref[0:CONV_W - 2, :] = cst[1:CONV_W - 1, :]
    cnew_ref[CONV_W - 2:CONV_W - 1, :] = x
    sm = sm_ref[0]
    gn = gn_ref[...]
    d = GDN_HEAD_DIM
    for h in range(N_GDN_HEADS):
        q = xc[:, h * d:(h + 1) * d]
        k = xc[:, D_GDN + h * d:D_GDN + (h + 1) * d]
        v = xc[:, 2 * D_GDN + h * d:2 * D_GDN + (h + 1) * d]
        q = q * lax.rsqrt(jnp.sum(q * q, axis=-1, keepdims=True) + 1e-6) * (d ** -0.5)
        k = k * lax.rsqrt(jnp.sum(k * k, axis=-1, keepdims=True) + 1e-6)
        a_neg = -jnp.exp(jnp.zeros((1, 1), F32) + alog_ref[h])
        g = a_neg * _softplus(sm[:, SM_AG + h:SM_AG + h + 1] + dtb_ref[h])
        beta = 1.0 / (1.0 + jnp.exp(-sm[:, SM_BG + h:SM_BG + h + 1]))
        st = s0_ref[h] * jnp.exp(g)
        k_col = jnp.broadcast_to(k, (d, d)).T
        q_col = jnp.broadcast_to(q, (d, d)).T
        kv = jnp.sum(k_col * st, axis=0, keepdims=True)
        delta = (v - kv) * beta
        st = st + k_col * delta
        s_ref[h] = st
        o = jnp.sum(q_col * st, axis=0, keepdims=True)
        o_ref[0, :, h * d:(h + 1) * d] = _gated_norm(o, gn, zg_ref[0][:, h * d:(h + 1) * d]).astype(BF16)


def _gdn_sample(a_log, dt_bias, qkv3, state_conv_l, conv_w, small3, zg3, gdn_norm, state_ssm_l):
    db = qkv3.shape[0]
    d = GDN_HEAD_DIM
    nh = N_GDN_HEADS
    row3 = lambda n: pl.BlockSpec((1, 1, n), lambda b: (b, 0, 0))
    return pl.pallas_call(
        _gdn_sample_body,
        grid=(db,),
        in_specs=[pl.BlockSpec(memory_space=pltpu.SMEM), pl.BlockSpec(memory_space=pltpu.SMEM),
                  row3(3 * D_GDN),
                  pl.BlockSpec((None, CONV_W - 1, 3 * D_GDN), lambda b: (b, 0, 0)),
                  pl.BlockSpec((CONV_W, 3 * D_GDN), lambda b: (0, 0)),
                  row3(LANES), row3(D_GDN),
                  pl.BlockSpec((1, LANES), lambda b: (0, 0)),
                  pl.BlockSpec((None, nh, d, d), lambda b: (b, 0, 0, 0))],
        out_specs=[row3(D_GDN),
                   pl.BlockSpec((None, nh, d, d), lambda b: (b, 0, 0, 0)),
                   pl.BlockSpec((None, CONV_W - 1, 3 * D_GDN), lambda b: (b, 0, 0))],
        out_shape=[jax.ShapeDtypeStruct((db, 1, D_GDN), BF16),
                   jax.ShapeDtypeStruct((db, nh, d, d), F32),
                   jax.ShapeDtypeStruct((db, CONV_W - 1, 3 * D_GDN), F32)],
        compiler_params=_cparams(("parallel",)),
        name="gdn_sample",
    )(a_log, dt_bias, qkv3, state_conv_l, conv_w, small3, zg3, gdn_norm.reshape(1, LANES), state_ssm_l)


def kernel(x_prompt, x_sample, cache_k, cache_v, cache_kidx, state_ssm, state_conv, page_table, norm_in, w_in,
           conv_w, a_log, dt_bias, gdn_norm, w_out, rel_table, norm_final):
    depth = w_in.shape[0]
    assert depth == 1, "single-layer model"
    batch, seq, d_model = x_prompt.shape
    db, dec_seq, _ = x_sample.shape
    assert dec_seq == 1 and seq % (2 * LANES) == 0 and seq % GDN_C == 0
    n_pool, page = cache_k.shape[1], cache_k.shape[2]
    n_pages = page_table.shape[1]
    past_len = n_pages * page
    assert page == LANES and n_pages % PAGES_PER_STEP == 0

    lyr = 0
    w_pad = _prep_w_in(w_in[lyr])
    w_out_bf = w_out[lyr].astype(BF16)

    xp = x_prompt.reshape(batch * seq, d_model)
    q_blk, k2d, v2d, za, qi_blk, small, qkv2d, zg = _inproj(xp, norm_in[lyr], w_pad, tm=2 * Q_BLOCK, blocked=True)
    att_g = _dsa_prompt_t(rel_table, q_blk, qi_blk, small, za, k2d, v2d, batch, seq)
    gdn_g, s_fin = _gdn_prompt(a_log[lyr], dt_bias[lyr], qkv2d, conv_w[lyr], small, zg, gdn_norm[lyr], batch, seq)
    y_prompt = _outproj(xp, att_g, gdn_g, w_out_bf, norm_final, tm=2 * Q_BLOCK).reshape(batch, seq, d_model)
    k_prompt = k2d.reshape(1, batch, seq, N_KV_HEADS, ATT_HEAD_DIM)
    v_prompt = v2d.reshape(1, batch, seq, N_KV_HEADS, ATT_HEAD_DIM)
    kidx_prompt = small[:, :IDX_DIM].reshape(1, batch, seq, IDX_DIM)
    ssm_prompt = s_fin[None]
    conv_prompt = qkv2d.reshape(batch, seq, 3 * D_GDN)[:, seq - (CONV_W - 1):][None]

    xs = x_sample.reshape(db, d_model)
    q_s, k_s, v_s, za_s, qi_s, small_s, qkv_s, zg_s = _inproj(xs, norm_in[lyr], w_pad, tm=db, blocked=False)
    kidx_t = jnp.transpose(cache_kidx[lyr], (0, 2, 1))
    k_pages_t = jnp.transpose(cache_k[lyr], (0, 2, 3, 1)).reshape(n_pool, KV_DIM, page)
    v_pages_t = jnp.transpose(cache_v[lyr], (0, 2, 3, 1)).reshape(n_pool, KV_DIM, page)
    scores = _idx_scores(page_table, qi_s.reshape(db, N_IDX_HEADS, IDX_DIM),
                         small_s[:, SM_WI:SM_WI + N_IDX_HEADS].reshape(db, N_IDX_HEADS, 1),
                         kidx_t)
    topk = min(TOPK_MAX, (past_len + dec_seq) // 4)
    mask_t = _sample_select(jnp.transpose(scores, (1, 0, 2)), qi_s, small_s, topk)
    mask_pages = jnp.transpose(mask_t[:n_pages], (1, 0, 2))
    mask_new = mask_t[n_pages].reshape(db, 1, page)
    hpg = N_ATT_HEADS // N_KV_HEADS
    q8 = q_s.reshape(db, N_ATT_HEADS, ATT_HEAD_DIM).astype(F32)
    zq = jnp.zeros((db, hpg, ATT_HEAD_DIM), F32)
    q_lh = jnp.concatenate([jnp.concatenate([q8[:, :hpg], zq], axis=2),
                            jnp.concatenate([zq, q8[:, hpg:]], axis=2),
                            jnp.zeros((db, SAMPLE_Q_ROWS - N_ATT_HEADS, LANES), F32)], axis=1)
    rel_t = jnp.concatenate([rel_table.T, jnp.zeros((SAMPLE_Q_ROWS - N_ATT_HEADS, N_BUCKETS), F32)], axis=0)
    att_raw = _sample_attn(page_table, q_lh, mask_pages, mask_new, k_s.reshape(db, 1, KV_DIM),
                           v_s.reshape(db, 1, KV_DIM), rel_t,
                           k_pages_t, v_pages_t, past_len)
    att_s = att_raw[:, :N_ATT_HEADS, :ATT_HEAD_DIM].reshape(db, D_ATT)
    gdn_s, s_new, conv_new = _gdn_sample(a_log[lyr], dt_bias[lyr], qkv_s.reshape(db, 1, 3 * D_GDN), state_conv[lyr],
                                         conv_w[lyr], small_s.reshape(db, 1, LANES), zg_s.reshape(db, 1, D_GDN),
                                         gdn_norm[lyr], state_ssm[lyr])
    y_sample = _outproj(xs, att_s, gdn_s.reshape(db, D_GDN), w_out_bf, norm_final, tm=db,
                        za=za_s).reshape(db, 1, d_model)
    k_sample = k_s.reshape(1, db, 1, N_KV_HEADS, ATT_HEAD_DIM)
    v_sample = v_s.reshape(1, db, 1, N_KV_HEADS, ATT_HEAD_DIM)
    kidx_sample = small_s[:, :IDX_DIM].reshape(1, db, 1, IDX_DIM)

    return (y_prompt, y_sample, k_prompt, v_prompt, kidx_prompt, ssm_prompt, conv_prompt,
            k_sample, v_sample, kidx_sample, s_new[None], conv_new[None])
```

```python
import functools
import math

import numpy as np
import jax
import jax.numpy as jnp
from jax import lax
from jax.experimental import pallas as pl
from jax.experimental.pallas import tpu as pltpu

F32 = jnp.float32
BF16 = jnp.bfloat16
I32 = jnp.int32

N_ATT_HEADS = 8
ATT_HEAD_DIM = 64
N_KV_HEADS = 2
D_ATT = N_ATT_HEADS * ATT_HEAD_DIM
KV_DIM = N_KV_HEADS * ATT_HEAD_DIM
N_IDX_HEADS = 16
IDX_DIM = 64
TOPK_MAX = 256
N_GDN_HEADS = 4
GDN_HEAD_DIM = 128
D_GDN = N_GDN_HEADS * GDN_HEAD_DIM
CONV_W = 4
N_BUCKETS = 32
MAX_DISTANCE = 128
Q_BLOCK = 128
RMS_EPS = 1e-6
NEG_BIG = -1e30
PROJ_SIZES = (D_ATT, KV_DIM, KV_DIM, D_ATT, N_IDX_HEADS * IDX_DIM, IDX_DIM, N_IDX_HEADS,
              3 * D_GDN, D_GDN, N_GDN_HEADS, N_GDN_HEADS)

LANES = 128
SUBLANES = 8
VMEM_LIMIT = 56 * 1024 * 1024

OFF_Q = 0
OFF_K = OFF_Q + D_ATT
OFF_V = OFF_K + KV_DIM
OFF_ZA = OFF_V + KV_DIM
OFF_QI = OFF_ZA + D_ATT
OFF_SM = OFF_QI + N_IDX_HEADS * IDX_DIM
OFF_QKV = OFF_SM + LANES
OFF_ZG = OFF_QKV + 3 * D_GDN
D_PROJ_PAD = OFF_ZG + D_GDN
SM_WI = IDX_DIM
SM_AG = SM_WI + N_IDX_HEADS
SM_BG = SM_AG + N_GDN_HEADS

GDN_C = 128
KEY_NEG_BIG = int(np.array(NEG_BIG, np.float32).view(np.int32)) ^ 0x7FFFFFFF
INT_MIN = -2 ** 31


def _cparams(sem):
    return pltpu.CompilerParams(dimension_semantics=sem, vmem_limit_bytes=VMEM_LIMIT)


def _silu(x):
    return x * (1.0 / (1.0 + jnp.exp(-x)))


def _bdot(a, b):
    return jnp.dot(a.astype(BF16), b.astype(BF16), preferred_element_type=F32)


def _bdot_nt(a, b):
    return lax.dot_general(a.astype(BF16), b.astype(BF16), (((1,), (1,)), ((), ())),
                           preferred_element_type=F32)


def _sort_key(x):
    i = pltpu.bitcast(x, I32)
    return jnp.where(i < 0, i ^ 0x7FFFFFFF, i)


def _inproj_body(x_ref, g_ref, w_ref, q_ref, k_ref, v_ref, za_ref, qi_ref, sm_ref, qkv_ref, zg_ref, *t_refs,
                 blocked):
    x = x_ref[...]
    ms = jnp.mean(x * x, axis=-1, keepdims=True)
    h = ((x * lax.rsqrt(ms + RMS_EPS)) * g_ref[...]).astype(BF16)

    def mm(a, b):
        return jnp.dot(h, w_ref[:, a:b], preferred_element_type=F32)

    q = (mm(OFF_Q, OFF_K) * (ATT_HEAD_DIM ** -0.5)).astype(BF16)
    qi = (mm(OFF_QI, OFF_SM) * (IDX_DIM ** -0.5)).astype(BF16)
    if blocked:
        for r in range(x.shape[0] // Q_BLOCK):
            rs = slice(r * Q_BLOCK, (r + 1) * Q_BLOCK)
            for j in range(D_ATT // LANES):
                q_ref[r, j] = q[rs, j * LANES:(j + 1) * LANES]
            for j in range(N_IDX_HEADS * IDX_DIM // LANES):
                qi_ref[r, j] = qi[rs, j * LANES:(j + 1) * LANES]
    else:
        q_ref[...] = q
        qi_ref[...] = qi
    k = mm(OFF_K, OFF_V)
    v = mm(OFF_V, OFF_ZA)
    sm = mm(OFF_SM, OFF_QKV)
    k_ref[...] = k
    v_ref[...] = v
    sm_ref[...] = sm
    za_ref[...] = mm(OFF_ZA, OFF_QI)
    qkv_ref[...] = mm(OFF_QKV, OFF_ZG)
    zg_ref[...] = mm(OFF_ZG, D_PROJ_PAD)
    if t_refs:
        kt_ref, vt_ref, kit_ref = t_refs
        kt_ref[...] = k.T
        vt_ref[...] = v.T
        kit_ref[...] = sm.T[0:IDX_DIM, :]


def _prep_w_in(w):
    splits = np.cumsum(PROJ_SIZES)[:-1].tolist()
    q, k, v, z_a, qi, ki, wi, qkv, z_g, a_g, b_g = jnp.split(w, splits, axis=1)
    pad = jnp.zeros((w.shape[0], LANES - (SM_BG + N_GDN_HEADS)), w.dtype)
    small = jnp.concatenate([ki, wi, a_g, b_g, pad], axis=1)
    return jnp.concatenate([q, k, v, z_a, qi, small, qkv, z_g], axis=1).astype(BF16)


def _inproj(x2d, norm_g, w_pad, tm, blocked, seq=None):
    t, d = x2d.shape
    nblk = t // tm
    t_shapes, t_specs = [], []
    if blocked:
        spb = seq // tm
        for n in (KV_DIM, KV_DIM, IDX_DIM):
            t_shapes.append(jax.ShapeDtypeStruct((t // seq, n, seq), F32))
            t_specs.append(pl.BlockSpec((None, n, tm), lambda i: (i // spb, 0, i % spb)))
        rb = tm // Q_BLOCK
        q_shape = jax.ShapeDtypeStruct((t // Q_BLOCK, D_ATT // LANES, Q_BLOCK, LANES), BF16)
        qi_shape = jax.ShapeDtypeStruct((t // Q_BLOCK, N_IDX_HEADS * IDX_DIM // LANES, Q_BLOCK, LANES), BF16)
        q_spec = pl.BlockSpec((rb, D_ATT // LANES, Q_BLOCK, LANES), lambda i: (i, 0, 0, 0))
        qi_spec = pl.BlockSpec((rb, N_IDX_HEADS * IDX_DIM // LANES, Q_BLOCK, LANES), lambda i: (i, 0, 0, 0))
    else:
        q_shape = jax.ShapeDtypeStruct((t, D_ATT), BF16)
        qi_shape = jax.ShapeDtypeStruct((t, N_IDX_HEADS * IDX_DIM), BF16)
        q_spec = pl.BlockSpec((tm, D_ATT), lambda i: (i, 0))
        qi_spec = pl.BlockSpec((tm, N_IDX_HEADS * IDX_DIM), lambda i: (i, 0))

    def row(n):
        return pl.BlockSpec((tm, n), lambda i: (i, 0))

    return pl.pallas_call(
        functools.partial(_inproj_body, blocked=blocked),
        grid=(nblk,),
        in_specs=[row(d), pl.BlockSpec((1, d), lambda i: (0, 0)),
                  pl.BlockSpec((d, D_PROJ_PAD), lambda i: (0, 0))],
        out_specs=[q_spec, row(KV_DIM), row(KV_DIM), row(D_ATT), qi_spec, row(LANES), row(3 * D_GDN), row(D_GDN)]
                  + t_specs,
        out_shape=[q_shape, jax.ShapeDtypeStruct((t, KV_DIM), F32), jax.ShapeDtypeStruct((t, KV_DIM), F32),
                   jax.ShapeDtypeStruct((t, D_ATT), F32), qi_shape, jax.ShapeDtypeStruct((t, LANES), F32),
                   jax.ShapeDtypeStruct((t, 3 * D_GDN), F32), jax.ShapeDtypeStruct((t, D_GDN), F32)] + t_shapes,
        compiler_params=_cparams(("parallel",)),
        name="inproj_blocked" if blocked else "inproj_rows",
    )(x2d, norm_g.reshape(1, d), w_pad)


def _rel_bucket(dist):
    n = jnp.maximum(dist, 0)
    max_exact = N_BUCKETS // 2
    nf = jnp.maximum(n, 1).astype(F32)
    large = max_exact + (jnp.log(nf / max_exact) / math.log(MAX_DISTANCE / max_exact)
                         * (N_BUCKETS - max_exact)).astype(I32)
    large = jnp.minimum(large, N_BUCKETS - 1)
    return jnp.where(n < max_exact, n, large)


def _far_bucket_checked(first_far):
    d = np.arange(first_far, 1 << 16, dtype=np.float32)
    b = 16 + (np.log(d / 16) / math.log(MAX_DISTANCE / 16) * 16).astype(np.int32)
    assert int(b.min()) >= N_BUCKETS - 1
    return N_BUCKETS - 1


def _kth_key_search(count_ge, shape, k):
    def body(step, t):
        cand = t + jnp.left_shift(jnp.int32(1), 31 - step)
        return jnp.where(count_ge(cand) >= k, cand, t)

    return lax.fori_loop(0, 32, body, jnp.full(shape, INT_MIN, I32))


def _dsa_prompt_body(rel_ref, q_ref, qi_ref, smq_ref, za_ref, k_ref, v_ref, sms_ref, o_ref,
                     kk_ref, vv_ref, ki_ref, bias_ref, key_ref, mask_ref, lg_ref, *, seq, topk):
    b = pl.program_id(0)
    i = pl.program_id(1)
    nt = seq // LANES
    lane = lax.broadcasted_iota(I32, (Q_BLOCK, LANES), 1)
    sub = lax.broadcasted_iota(I32, (Q_BLOCK, LANES), 0)
    far_bucket = _far_bucket_checked(MAX_DISTANCE + 1)

    @pl.when((b == 0) & (i == 0))
    def _():
        for dt in range(2):
            bucket = _rel_bucket(dt * LANES + sub - lane)
            for h in range(N_ATT_HEADS):
                acc = jnp.zeros((Q_BLOCK, LANES), F32)
                for bk in range(N_BUCKETS):
                    acc = jnp.where(bucket == bk, rel_ref[bk, h], acc)
                bias_ref[h, dt] = acc - rel_ref[far_bucket, h]

    @pl.when(i == 0)
    def _():
        lo = lax.broadcasted_iota(I32, (seq, LANES), 1) < ATT_HEAD_DIM
        kf = k_ref[...]
        vf = v_ref[...]
        g0 = jnp.where(lo, kf, 0.0)
        g1 = jnp.where(lo, 0.0, kf)
        kk_ref[0] = g0.astype(BF16)
        kk_ref[1] = pltpu.roll(g0, ATT_HEAD_DIM, 1).astype(BF16)
        kk_ref[2] = pltpu.roll(g1, ATT_HEAD_DIM, 1).astype(BF16)
        kk_ref[3] = g1.astype(BF16)
        w0 = jnp.where(lo, vf, 0.0)
        w1 = jnp.where(lo, 0.0, vf)
        vv_ref[0] = w0.astype(BF16)
        vv_ref[1] = pltpu.roll(w0, ATT_HEAD_DIM, 1).astype(BF16)
        vv_ref[2] = pltpu.roll(w1, ATT_HEAD_DIM, 1).astype(BF16)
        vv_ref[3] = w1.astype(BF16)
        c0 = jnp.where(lo, sms_ref[...], 0.0)
        ki_ref[0] = c0.astype(BF16)
        ki_ref[1] = pltpu.roll(c0, IDX_DIM, 1).astype(BF16)

    n_live = i + 1
    n_live2 = i // 2 + 1
    row_pos = i * Q_BLOCK + sub

    n_pairs = N_IDX_HEADS // 2
    wq = smq_ref[...]
    wcols = [wq[:, SM_WI + h:SM_WI + h + 1] * (N_IDX_HEADS ** -0.5) for h in range(N_IDX_HEADS)]
    qi_all = qi_ref[0].reshape(n_pairs * Q_BLOCK, LANES)

    def score_tile(t2, carry):
        ks = pl.ds(pl.multiple_of(t2 * 2 * LANES, 2 * LANES), 2 * LANES)
        acc = jnp.zeros((Q_BLOCK, 2 * LANES), F32)
        for half in range(2):
            s_all = lax.dot_general(qi_all, ki_ref[half, ks, :], (((1,), (1,)), ((), ())),
                                    preferred_element_type=F32)
            for j in range(n_pairs):
                s = s_all[j * Q_BLOCK:(j + 1) * Q_BLOCK]
                acc = acc + jnp.maximum(s, 0.0) * wcols[2 * j + half]
        acc = acc + 0.0
        for u in range(2):
            col = (t2 * 2 + u) * LANES + lane
            sc = jnp.where(col <= row_pos, acc[:, u * LANES:(u + 1) * LANES], NEG_BIG)
            key_ref[t2 * 2 + u] = _sort_key(sc)
        return carry

    lax.fori_loop(0, n_live2, score_tile, 0)

    @pl.when(i % 2 == 0)
    def _():
        key_ref[i + 1] = jnp.full((Q_BLOCK, LANES), KEY_NEG_BIG, I32)
        mask_ref[i + 1] = jnp.full((Q_BLOCK, LANES), NEG_BIG, F32)

    n_dead = ((nt - 2 * n_live2) * LANES).astype(F32)

    def count_where(pred, upto):
        def body(t, acc):
            return acc + pred(key_ref[t], t).astype(F32)
        acc = lax.fori_loop(0, upto, body, jnp.zeros((Q_BLOCK, LANES), F32))
        return jnp.sum(acc, axis=1, keepdims=True)

    def count_ge(cand):
        live = count_where(lambda kt, t: kt >= cand, 2 * n_live2)
        return live + jnp.where(cand <= KEY_NEG_BIG, n_dead, 0.0)

    thr = _kth_key_search(count_ge, Q_BLOCK, float(topk))
    cnt_gt = count_where(lambda kt, t: kt > thr, 2 * n_live2) + jnp.where(thr < KEY_NEG_BIG, n_dead, 0.0)
    need = float(topk) - cnt_gt

    def causal_of(t):
        return (t * LANES + lane) <= row_pos

    cnt_ceq = count_where(lambda kt, t: (kt == thr) & causal_of(t), n_live)
    any_tie = jnp.max(jnp.where(cnt_ceq > need, 1.0, 0.0)) > 0.0

    @pl.when(jnp.logical_not(any_tie))
    def _():
        def body(t, carry):
            mask_ref[t] = jnp.where((key_ref[t] >= thr) & causal_of(t), 0.0, NEG_BIG)
            return carry
        lax.fori_loop(0, n_live, body, 0)

    @pl.when(any_tie)
    def _():
        upper = (sub <= lane).astype(BF16)
        ones = jnp.ones((LANES, LANES), BF16)

        def body(t, offset):
            kt = key_ref[t]
            cz = causal_of(t)
            eq = ((kt == thr) & cz).astype(BF16)
            prefix = jnp.dot(eq, upper, preferred_element_type=F32) + offset
            sel = ((kt > thr) & cz) | ((eq > 0) & (prefix <= need))
            mask_ref[t] = jnp.where(sel, 0.0, NEG_BIG)
            return offset + jnp.dot(eq, ones, preferred_element_type=F32)

        lax.fori_loop(0, n_live, body, jnp.zeros((Q_BLOCK, LANES), F32))

    za = za_ref[...]
    lo_lane = lane < ATT_HEAD_DIM
    n_qp = N_ATT_HEADS // 2
    for g in range(N_KV_HEADS):
        pairs = [p for p in range(n_qp) if (2 * p) // (N_ATT_HEADS // N_KV_HEADS) == g]
        q_stack = jnp.concatenate([q_ref[0, p] for p in pairs], axis=0)
        for half in range(2):
            heads = [2 * p + half for p in pairs]

            def logit_tile(t2, mx, heads=heads, half=half, g=g):
                ks = pl.ds(pl.multiple_of(t2 * 2 * LANES, 2 * LANES), 2 * LANES)
                l_all = lax.dot_general(q_stack, kk_ref[2 * g + half, ks, :], (((1,), (1,)), ((), ())),
                                        preferred_element_type=F32)
                out = []
                for n, h in enumerate(heads):
                    m_h = mx[n]
                    for u in range(2):
                        t = t2 * 2 + u
                        l = l_all[n * Q_BLOCK:(n + 1) * Q_BLOCK, u * LANES:(u + 1) * LANES] + mask_ref[t]
                        lg_ref[h, t] = l
                        m_h = jnp.maximum(m_h, jnp.where(t < i - 1, l, NEG_BIG))
                    out.append(m_h)
                return tuple(out)

            mx0 = tuple(jnp.full((Q_BLOCK, LANES), NEG_BIG, F32) for _ in heads)
            mx = lax.fori_loop(0, n_live2, logit_tile, mx0)
            for n, h in enumerate(heads):
                l0 = lg_ref[h, i] + bias_ref[h, 0]
                lg_ref[h, i] = l0
                m_h = jnp.maximum(mx[n], l0)
                im1 = jnp.maximum(i - 1, 0)
                l1 = lg_ref[h, im1] + jnp.where(i > 0, bias_ref[h, 1], 0.0)
                lg_ref[h, im1] = l1
                m_h = jnp.maximum(m_h, l1)
                m_row = jnp.max(m_h, axis=1, keepdims=True)

                def pv_tile(t, carry, h=h, half=half, g=g, m_row=m_row):
                    ssum, acc = carry
                    p = jnp.exp(lg_ref[h, t] - m_row)
                    ts = pl.ds(pl.multiple_of(t * LANES, LANES), LANES)
                    acc = acc + jnp.dot(p.astype(BF16), vv_ref[2 * g + half, ts, :], preferred_element_type=F32)
                    return ssum + p, acc

                ssum, acc = lax.fori_loop(0, n_live, pv_tile,
                                          (jnp.zeros((Q_BLOCK, LANES), F32), jnp.zeros((Q_BLOCK, LANES), F32)))
                l_row = jnp.sum(ssum, axis=1, keepdims=True)
                lg_ref[h, nt] = acc / l_row
        for p in pairs:
            o_pair = jnp.where(lo_lane, lg_ref[2 * p, nt], lg_ref[2 * p + 1, nt])
            cs = slice(p * LANES, (p + 1) * LANES)
            o_ref[:, cs] = (o_pair * _silu(za[:, cs])).astype(BF16)


def _dsa_prompt(rel_table, q_blk, qi_blk, small, za, k2d, v2d, batch, seq):
    nb = seq // Q_BLOCK
    nt = seq // LANES
    topk = min(TOPK_MAX, seq // 4)
    n_qp = D_ATT // LANES
    n_ip = N_IDX_HEADS * IDX_DIM // LANES
    t = batch * seq
    return pl.pallas_call(
        functools.partial(_dsa_prompt_body, seq=seq, topk=topk),
        grid=(batch, nb),
        in_specs=[
            pl.BlockSpec(memory_space=pltpu.SMEM),
            pl.BlockSpec((1, n_qp, Q_BLOCK, LANES), lambda b, i: (b * nb + i, 0, 0, 0)),
            pl.BlockSpec((1, n_ip, Q_BLOCK, LANES), lambda b, i: (b * nb + i, 0, 0, 0)),
            pl.BlockSpec((Q_BLOCK, LANES), lambda b, i: (b * nb + i, 0)),
            pl.BlockSpec((Q_BLOCK, D_ATT), lambda b, i: (b * nb + i, 0)),
            pl.BlockSpec((seq, KV_DIM), lambda b, i: (b, 0)),
            pl.BlockSpec((seq, KV_DIM), lambda b, i: (b, 0)),
            pl.BlockSpec((seq, LANES), lambda b, i: (b, 0)),
        ],
        out_specs=pl.BlockSpec((Q_BLOCK, D_ATT), lambda b, i: (b * nb + i, 0)),
        out_shape=jax.ShapeDtypeStruct((t, D_ATT), BF16),
        scratch_shapes=[
            pltpu.VMEM((2 * N_KV_HEADS, seq, LANES), BF16),
            pltpu.VMEM((2 * N_KV_HEADS, seq, LANES), BF16),
            pltpu.VMEM((2, seq, LANES), BF16),
            pltpu.VMEM((N_ATT_HEADS, 2, Q_BLOCK, LANES), F32),
            pltpu.VMEM((nt + 1, Q_BLOCK, LANES), I32),
            pltpu.VMEM((nt + 1, Q_BLOCK, LANES), F32),
            pltpu.VMEM((N_ATT_HEADS, nt + 1, Q_BLOCK, LANES), F32),
        ],
        compiler_params=_cparams(("arbitrary", "arbitrary")),
        name="dsa_prompt",
    )(rel_table, q_blk, qi_blk, small, za, k2d, v2d, small)


KT = 2 * LANES


def _fold8(x, op):
    binop = {jnp.sum: jnp.add, jnp.max: jnp.maximum}[op]
    r = x.reshape(x.shape[0] // SUBLANES, SUBLANES, x.shape[1])
    while r.shape[0] > 1:
        half = r.shape[0] // 2
        r = binop(r[:half], r[half:])
    return r[0]


def _dsa_prompt_t_body(rel_ref, q_ref, qi_ref, smq_ref, za_ref, k_ref, v_ref, sms_ref, o_ref,
                       kk_ref, vvt_ref, ki_ref, bias_ref, qt_ref, qit_ref, key_ref, mask_ref, lg_ref,
                       *, seq, topk):
    b = pl.program_id(0)
    i = pl.program_id(1)
    n_t = seq // KT
    hpg = N_ATT_HEADS // N_KV_HEADS
    far_bucket = _far_bucket_checked(MAX_DISTANCE + 1)
    sub1 = lax.broadcasted_iota(I32, (LANES, LANES), 0)
    lane1 = lax.broadcasted_iota(I32, (LANES, LANES), 1)

    @pl.when((b == 0) & (i == 0))
    def _():
        for dt in range(2):
            bucket = _rel_bucket(dt * LANES + lane1 - sub1)
            for h in range(N_ATT_HEADS):
                acc = jnp.zeros((LANES, LANES), F32)
                for bk in range(N_BUCKETS):
                    acc = jnp.where(bucket == bk, rel_ref[bk, h], acc)
                bias_ref[h, dt] = acc - rel_ref[far_bucket, h]

    @pl.when(i == 0)
    def _():
        lo = lax.broadcasted_iota(I32, (seq, LANES), 1) < ATT_HEAD_DIM
        kf = k_ref[...]
        g0 = jnp.where(lo, kf, 0.0)
        g1 = jnp.where(lo, 0.0, kf)
        kk_ref[0] = g0.astype(BF16)
        kk_ref[1] = pltpu.roll(g0, ATT_HEAD_DIM, 1).astype(BF16)
        kk_ref[2] = pltpu.roll(g1, ATT_HEAD_DIM, 1).astype(BF16)
        kk_ref[3] = g1.astype(BF16)
        c0 = jnp.where(lo, sms_ref[...], 0.0)
        ki_ref[0] = c0.astype(BF16)
        ki_ref[1] = pltpu.roll(c0, IDX_DIM, 1).astype(BF16)
        lo_t = lax.broadcasted_iota(I32, (KT, LANES), 1) < ATT_HEAD_DIM
        for t in range(n_t):
            vf = v_ref[t * KT:(t + 1) * KT, :]
            w0 = jnp.where(lo_t, vf, 0.0)
            w1 = jnp.where(lo_t, 0.0, vf)
            vvt_ref[0, t] = w0.T.astype(BF16)
            vvt_ref[1, t] = pltpu.roll(w0, ATT_HEAD_DIM, 1).T.astype(BF16)
            vvt_ref[2, t] = pltpu.roll(w1, ATT_HEAD_DIM, 1).T.astype(BF16)
            vvt_ref[3, t] = w1.T.astype(BF16)

    n_qp = D_ATT // LANES
    n_ip = N_IDX_HEADS * IDX_DIM // LANES
    for p in range(n_qp):
        qt_ref[:, p * LANES:(p + 1) * LANES] = q_ref[0, p].astype(F32).T.astype(BF16)
    for j in range(n_ip):
        qit_ref[:, j * LANES:(j + 1) * LANES] = qi_ref[0, j].astype(F32).T.astype(BF16)
    sm_t = smq_ref[...].T
    wrow = [sm_t[SM_WI + h:SM_WI + h + 1, :] * (N_IDX_HEADS ** -0.5) for h in range(N_IDX_HEADS)]

    n_live = i // 2 + 1
    n_dead = ((n_t - n_live) * KT).astype(F32)
    kidx = lax.broadcasted_iota(I32, (KT, LANES), 0)
    qpos = i * Q_BLOCK + lax.broadcasted_iota(I32, (KT, LANES), 1)

    def causal_of(t):
        return (t * KT + kidx) <= qpos

    def tile_rows(t):
        return pl.ds(pl.multiple_of(t * KT, KT), KT)

    def score_tile(t, carry):
        acc = jnp.zeros((KT, LANES), F32)
        for half in range(2):
            s_all = jnp.dot(ki_ref[half, tile_rows(t), :], qit_ref[...], preferred_element_type=F32)
            for j in range(n_ip):
                acc = acc + jnp.maximum(s_all[:, j * LANES:(j + 1) * LANES], 0.0) * wrow[2 * j + half]
        acc = acc + 0.0
        key_ref[t] = _sort_key(jnp.where(causal_of(t), acc, NEG_BIG))
        return carry

    lax.fori_loop(0, n_live, score_tile, 0)

    def count_where(pred):
        def body(t, acc):
            return acc + _fold8(pred(key_ref[t], t).astype(F32), jnp.sum)
        acc = lax.fori_loop(0, n_live, body, jnp.zeros((SUBLANES, LANES), F32))
        return jnp.sum(acc, axis=0, keepdims=True)

    def count_ge(cand):
        return count_where(lambda kt, t: kt >= cand) + jnp.where(cand <= KEY_NEG_BIG, n_dead, 0.0)

    thr = _kth_key_search(count_ge, (1, LANES), float(topk))
    cnt_gt = count_where(lambda kt, t: kt > thr) + jnp.where(thr < KEY_NEG_BIG, n_dead, 0.0)
    need = float(topk) - cnt_gt
    cnt_ceq = count_where(lambda kt, t: (kt == thr) & causal_of(t))
    any_tie = jnp.max(jnp.where(cnt_ceq > need, 1.0, 0.0)) > 0.0

    @pl.when(jnp.logical_not(any_tie))
    def _():
        def body(t, carry):
            mask_ref[t] = jnp.where((key_ref[t] >= thr) & causal_of(t), 0.0, NEG_BIG)
            return carry
        lax.fori_loop(0, n_live, body, 0)

    @pl.when(any_tie)
    def _():
        rk = lax.broadcasted_iota(I32, (KT, KT), 0)
        ck = lax.broadcasted_iota(I32, (KT, KT), 1)
        lower = (ck <= rk).astype(BF16)

        def body(t, offset):
            kt = key_ref[t]
            cz = causal_of(t)
            eq = ((kt == thr) & cz).astype(F32)
            prefix = jnp.dot(lower, eq.astype(BF16), preferred_element_type=F32) + offset
            sel = ((kt > thr) & cz) | ((eq > 0.0) & (prefix <= need))
            mask_ref[t] = jnp.where(sel, 0.0, NEG_BIG)
            return offset + jnp.sum(_fold8(eq, jnp.sum), axis=0, keepdims=True)

        lax.fori_loop(0, n_live, body, jnp.zeros((1, LANES), F32))

    @pl.when(n_live % 2 == 1)
    def _():
        mask_ref[n_live] = jnp.full((KT, LANES), NEG_BIG, F32)

    def pair_loop(body, init):
        return lax.fori_loop(0, (n_live + 1) // 2, lambda p, c: body(2 * p + 1, body(2 * p, c)), init)

    za = za_ref[...]
    near_lo = jnp.maximum(i - 1, 0) // 2
    neg8 = jnp.full((SUBLANES, LANES), NEG_BIG, F32)
    zero8 = jnp.zeros((SUBLANES, LANES), F32)
    for g in range(N_KV_HEADS):
        qt_g = qt_ref[:, g * 2 * LANES:(g + 1) * 2 * LANES]

        def logit_tile(t, mx, g=g, qt_g=qt_g):
            mx = list(mx)
            far = t < near_lo
            for half in range(2):
                l_all = jnp.dot(kk_ref[2 * g + half, tile_rows(t), :], qt_g, preferred_element_type=F32)
                for n in range(2):
                    hl = 2 * n + half
                    l = l_all[:, n * LANES:(n + 1) * LANES] + mask_ref[t]
                    lg_ref[hl, t] = l
                    mx[hl] = jnp.maximum(mx[hl], jnp.where(far, _fold8(l, jnp.max), NEG_BIG))
            return tuple(mx)

        mx = pair_loop(logit_tile, (neg8,) * hpg)
        m_row = []
        for hl in range(hpg):
            h = g * hpg + hl
            r0 = pl.ds(pl.multiple_of((i % 2) * LANES, LANES), LANES)
            lg_ref[hl, i // 2, r0, :] = lg_ref[hl, i // 2, r0, :] + bias_ref[h, 0]
            im1 = jnp.maximum(i - 1, 0)
            r1 = pl.ds(pl.multiple_of((im1 % 2) * LANES, LANES), LANES)
            lg_ref[hl, im1 // 2, r1, :] = lg_ref[hl, im1 // 2, r1, :] + jnp.where(i > 0, bias_ref[h, 1], 0.0)
            m8 = jnp.maximum(mx[hl], jnp.maximum(_fold8(lg_ref[hl, near_lo], jnp.max),
                                                 _fold8(lg_ref[hl, i // 2], jnp.max)))
            m_row.append(jnp.max(m8, axis=0, keepdims=True))

        def pv_tile(t, carry, g=g, m_row=m_row):
            ssum = list(carry[:hpg])
            acc = list(carry[hpg:])
            for n in range(2):
                for half in range(2):
                    hl = 2 * n + half
                    p = jnp.exp(lg_ref[hl, t] - m_row[hl])
                    ssum[hl] = ssum[hl] + _fold8(p, jnp.sum)
                    acc[n] = acc[n] + jnp.dot(vvt_ref[2 * g + half, t], p.astype(BF16),
                                              preferred_element_type=F32)
            return tuple(ssum) + tuple(acc)

        zacc = jnp.zeros((LANES, LANES), F32)
        res = pair_loop(pv_tile, (zero8,) * hpg + (zacc, zacc))
        for n in range(2):
            l_lo = jnp.sum(res[2 * n], axis=0, keepdims=True)
            l_hi = jnp.sum(res[2 * n + 1], axis=0, keepdims=True)
            inv = jnp.where(sub1 < ATT_HEAD_DIM, 1.0 / l_lo, 1.0 / l_hi)
            o_pair = (res[hpg + n] * inv).T
            cs = slice((2 * g + n) * LANES, (2 * g + n + 1) * LANES)
            o_ref[:, cs] = (o_pair * _silu(za[:, cs])).astype(BF16)


def _dsa_prompt_t(rel_table, q_blk, qi_blk, small, za, k2d, v2d, batch, seq):
    assert N_KV_HEADS == 2 and N_ATT_HEADS // N_KV_HEADS == 4 and seq % KT == 0
    nb = seq // Q_BLOCK
    n_t = seq // KT
    topk = min(TOPK_MAX, seq // 4)
    n_qp = D_ATT // LANES
    n_ip = N_IDX_HEADS * IDX_DIM // LANES
    t = batch * seq
    return pl.pallas_call(
        functools.partial(_dsa_prompt_t_body, seq=seq, topk=topk),
        grid=(batch, nb),
        in_specs=[
            pl.BlockSpec(memory_space=pltpu.SMEM),
            pl.BlockSpec((1, n_qp, Q_BLOCK, LANES), lambda b, i: (b * nb + i, 0, 0, 0)),
            pl.BlockSpec((1, n_ip, Q_BLOCK, LANES), lambda b, i: (b * nb + i, 0, 0, 0)),
            pl.BlockSpec((Q_BLOCK, LANES), lambda b, i: (b * nb + i, 0)),
            pl.BlockSpec((Q_BLOCK, D_ATT), lambda b, i: (b * nb + i, 0)),
            pl.BlockSpec((seq, KV_DIM), lambda b, i: (b, 0)),
            pl.BlockSpec((seq, KV_DIM), lambda b, i: (b, 0)),
            pl.BlockSpec((seq, LANES), lambda b, i: (b, 0)),
        ],
        out_specs=pl.BlockSpec((Q_BLOCK, D_ATT), lambda b, i: (b * nb + i, 0)),
        out_shape=jax.ShapeDtypeStruct((t, D_ATT), BF16),
        scratch_shapes=[
            pltpu.VMEM((2 * N_KV_HEADS, seq, LANES), BF16),
            pltpu.VMEM((2 * N_KV_HEADS, n_t, LANES, KT), BF16),
            pltpu.VMEM((2, seq, LANES), BF16),
            pltpu.VMEM((N_ATT_HEADS, 2, LANES, LANES), F32),
            pltpu.VMEM((LANES, D_ATT), BF16),
            pltpu.VMEM((LANES, N_IDX_HEADS * IDX_DIM), BF16),
            pltpu.VMEM((n_t, KT, LANES), I32),
            pltpu.VMEM((n_t, KT, LANES), F32),
            pltpu.VMEM((N_ATT_HEADS // N_KV_HEADS, n_t, KT, LANES), F32),
        ],
        compiler_params=_cparams(("arbitrary", "arbitrary")),
        name="dsa_prompt",
    )(rel_table, q_blk, qi_blk, small, za, k2d, v2d, small)


def _softplus(x):
    return jnp.maximum(x, 0.0) + jnp.log1p(jnp.exp(-jnp.abs(x)))


def _lane_pick(x, idx):
    lane = lax.broadcasted_iota(I32, x.shape, 1)
    return jnp.sum(jnp.where(lane == idx, x, 0.0), axis=1, keepdims=True)


def _gated_norm(o, gn, z):
    y = o * lax.rsqrt(jnp.mean(o * o, axis=-1, keepdims=True) + RMS_EPS)
    return (y * gn) * _silu(z)


def _gdn_prompt_body(alog_ref, dtb_ref, xq_ref, xk_ref, xv_ref, wq_ref, wk_ref, wv_ref, sm_ref, zg_ref, gn_ref,
                     o_ref, sfin_ref, xs_ref, q_s, k_s, v_s, g_s, b_s, *, seq, hp):
    h0 = pl.program_id(1) * hp
    hist = SUBLANES
    slabs = [slice(hh * LANES, (hh + 1) * LANES) for hh in range(hp)]

    def conv_into(x_ref, w_ref, dst, post):
        xs_ref[0:hist, :] = jnp.zeros((hist, hp * LANES), F32)
        xs_ref[hist:hist + seq, :] = x_ref[...]
        base = hist - (CONV_W - 1)
        for hs in slabs:
            acc = xs_ref[base:base + seq, hs] * w_ref[0:1, hs]
            for j in range(1, CONV_W):
                acc = acc + xs_ref[base + j:base + j + seq, hs] * w_ref[j:j + 1, hs]
            dst[:, hs] = post(_silu(acc))

    def l2n(x):
        return x * lax.rsqrt(jnp.sum(x * x, axis=-1, keepdims=True) + 1e-6)

    conv_into(xq_ref, wq_ref, q_s, lambda x: l2n(x) * (GDN_HEAD_DIM ** -0.5))
    conv_into(xk_ref, wk_ref, k_s, l2n)
    conv_into(xv_ref, wv_ref, v_s, lambda x: x)

    sm = sm_ref[...]
    for hh, hs in enumerate(slabs):
        a_col = _lane_pick(sm, SM_AG + h0 + hh)
        b_col = _lane_pick(sm, SM_BG + h0 + hh)
        a_neg = -jnp.exp(jnp.zeros((1, 1), F32) + alog_ref[h0 + hh])
        g_col = a_neg * _softplus(a_col + dtb_ref[h0 + hh])
        g_s[:, hs] = jnp.broadcast_to(g_col, (seq, LANES))
        b_s[:, hs] = jnp.broadcast_to(1.0 / (1.0 + jnp.exp(-b_col)), (seq, LANES))

    c = GDN_C
    ri = lax.broadcasted_iota(I32, (c, c), 0)
    ci = lax.broadcasted_iota(I32, (c, c), 1)
    tril = ri >= ci
    strict = ri > ci
    tril_f = tril.astype(F32)
    eye = (ri == ci).astype(F32)
    gn = gn_ref[...]
    off_masks = []
    for lg in range(int(math.log2(c))):
        same_pair = (ri >> (lg + 1)) == (ci >> (lg + 1))
        off_masks.append(same_pair & (((ri >> lg) & 1) == 1) & (((ci >> lg) & 1) == 0))

    tril_b = tril.astype(BF16)

    def cumsum_rows(g):
        hi = g.astype(BF16)
        r1 = g - hi.astype(F32)
        mid = r1.astype(BF16)
        lo = (r1 - mid.astype(F32)).astype(BF16)
        return sum(jnp.dot(tril_b, piece, preferred_element_type=F32) for piece in (hi, mid, lo))

    def chunk(n, states):
        heads = range(hp)
        sl = pl.ds(pl.multiple_of(n * c, c), c)
        q = [q_s[sl, hs] for hs in slabs]
        k = [k_s[sl, hs] for hs in slabs]
        v = [v_s[sl, hs] for hs in slabs]
        bb = [b_s[sl, hs] for hs in slabs]
        gcum = [cumsum_rows(g_s[sl, hs]) for hs in slabs]
        gcum_row = [g.T for g in gcum]
        decay = [jnp.where(tril, jnp.exp(jnp.where(tril, gcum[h] - gcum_row[h], 0.0)), 0.0) for h in heads]
        eg = [jnp.exp(g) for g in gcum]
        kb = [k[h] * bb[h] for h in heads]
        vb = [v[h] * bb[h] for h in heads]
        kk = [_bdot_nt(kb[h], k[h]) for h in heads]
        qk = [_bdot_nt(q[h], k[h]) for h in heads]
        a_mat = [jnp.where(strict, kk[h] * decay[h], 0.0) for h in heads]
        attn = [qk[h] * decay[h] for h in heads]
        x = [eye - jnp.where(off_masks[0], a_mat[h], 0.0) for h in heads]
        for om in off_masks[1:]:
            inner = [_bdot(jnp.where(om, a_mat[h], 0.0), x[h]) for h in heads]
            x = [x[h] - _bdot(x[h], inner[h]) for h in heads]
        u = [_bdot(x[h], vb[h]) for h in heads]
        w = [_bdot(x[h], kb[h] * eg[h]) for h in heads]
        g_last = [g[c - 1:c, :] for g in gcum]
        k_dec_t = [(k[h] * jnp.exp(g_last[h] - gcum[h])).T for h in heads]
        ws = [_bdot(w[h], states[h]) for h in heads]
        qs = [_bdot(q[h] * eg[h], states[h]) for h in heads]
        v_new = [u[h] - ws[h] for h in heads]
        av = [_bdot(attn[h], v_new[h]) for h in heads]
        kv = [_bdot(k_dec_t[h], v_new[h]) for h in heads]
        for h, hs in enumerate(slabs):
            o_ref[sl, hs] = _gated_norm(qs[h] + av[h], gn, zg_ref[sl, hs]).astype(BF16)
        return tuple(states[h] * jnp.exp(g_last[h]) + kv[h] for h in heads)

    zero_state = jnp.zeros((GDN_HEAD_DIM, GDN_HEAD_DIM), F32)
    finals = lax.fori_loop(0, seq // c, chunk, (zero_state,) * hp)
    for hh in range(hp):
        sfin_ref[hh] = finals[hh]


GDN_HEADS_PER_STEP = 4


def _gdn_prompt(a_log, dt_bias, qkv2d, conv_w, small, zg, gdn_norm, batch, seq):
    nh = N_GDN_HEADS
    hp = GDN_HEADS_PER_STEP
    ng = nh // hp
    t = batch * seq
    w = hp * LANES
    once = pl.Buffered(1)
    blk = lambda off: pl.BlockSpec((seq, w), lambda b, j: (b, off + j), pipeline_mode=once)
    wblk = lambda off: pl.BlockSpec((CONV_W, w), lambda b, j: (0, off + j))
    return pl.pallas_call(
        functools.partial(_gdn_prompt_body, seq=seq, hp=hp),
        grid=(batch, ng),
        in_specs=[pl.BlockSpec(memory_space=pltpu.SMEM), pl.BlockSpec(memory_space=pltpu.SMEM),
                  blk(0), blk(ng), blk(2 * ng), wblk(0), wblk(ng), wblk(2 * ng),
                  pl.BlockSpec((seq, LANES), lambda b, j: (b, 0)),
                  pl.BlockSpec((seq, w), lambda b, j: (b, j), pipeline_mode=once),
                  pl.BlockSpec((1, LANES), lambda b, j: (0, 0))],
        out_specs=[pl.BlockSpec((seq, w), lambda b, j: (b, j)),
                   pl.BlockSpec((None, hp, GDN_HEAD_DIM, GDN_HEAD_DIM), lambda b, j: (b, j, 0, 0))],
        out_shape=[jax.ShapeDtypeStruct((t, D_GDN), BF16),
                   jax.ShapeDtypeStruct((batch, nh, GDN_HEAD_DIM, GDN_HEAD_DIM), F32)],
        scratch_shapes=[pltpu.VMEM((seq + 2 * SUBLANES, w), F32)] + [pltpu.VMEM((seq, w), F32)] * 5,
        compiler_params=_cparams(("parallel", "arbitrary")),
        name="gdn_prompt",
    )(a_log, dt_bias, qkv2d, qkv2d, qkv2d, conv_w, conv_w, conv_w, small, zg, gdn_norm.reshape(1, LANES))


def _outproj_body(*refs, gate_att):
    if gate_att:
        x_ref, att_ref, za_ref, gdn_ref, w_ref, g_ref, y_ref = refs
        att = (att_ref[...] * _silu(za_ref[...])).astype(BF16)
    else:
        x_ref, att_ref, gdn_ref, w_ref, g_ref, y_ref = refs
        att = att_ref[...]
    y = x_ref[...] + jnp.dot(att, w_ref[0:D_ATT, :], preferred_element_type=F32) \
        + jnp.dot(gdn_ref[...], w_ref[D_ATT:D_ATT + D_GDN, :], preferred_element_type=F32)
    y = y * lax.rsqrt(jnp.mean(y * y, axis=-1, keepdims=True) + RMS_EPS)
    y_ref[...] = y * g_ref[...]


def _outproj(x2d, att, gdn, w_bf, norm_g, tm, za=None):
    t, d = x2d.shape
    row = lambda n: pl.BlockSpec((tm, n), lambda i: (i, 0))
    full = lambda a, b: pl.BlockSpec((a, b), lambda i: (0, 0))
    ins = [x2d, att] + ([za] if za is not None else []) + [gdn, w_bf, norm_g.reshape(1, d)]
    specs = [row(d), row(D_ATT)] + ([row(D_ATT)] if za is not None else []) + \
            [row(D_GDN), full(D_ATT + D_GDN, d), full(1, d)]
    return pl.pallas_call(
        functools.partial(_outproj_body, gate_att=za is not None),
        grid=(t // tm,),
        in_specs=specs,
        out_specs=row(d),
        out_shape=jax.ShapeDtypeStruct((t, d), F32),
        compiler_params=_cparams(("parallel",)),
        name="outproj_gated" if za is not None else "outproj",
    )(*ins)


IDX_PAGES_PER_STEP = 32
ATT_PAGES_PER_STEP = 16
SAMPLE_Q_ROWS = 16


def _idx_scores_body(pt_ref, qi_ref, wi_ref, *rest):
    page_refs = rest[:IDX_PAGES_PER_STEP]
    o_ref = rest[IDX_PAGES_PER_STEP]
    qi = qi_ref[0]
    wi = wi_ref[0] * (N_IDX_HEADS ** -0.5)
    for j, pr in enumerate(page_refs):
        s = _bdot(qi, pr[...])
        sc = jnp.sum(jnp.maximum(s, 0.0) * wi, axis=0, keepdims=True)
        o_ref[0, j:j + 1, :] = sc + 0.0


def _idx_scores(page_table, qi3, wi3, kidx_pages):
    db, n_pages = page_table.shape
    page = kidx_pages.shape[2]
    pg = IDX_PAGES_PER_STEP

    def page_spec(j):
        return pl.BlockSpec((None, IDX_DIM, page), lambda b, s, pt: (pt[b, s * pg + j], 0, 0))

    grid_spec = pltpu.PrefetchScalarGridSpec(
        num_scalar_prefetch=1,
        grid=(db, n_pages // pg),
        in_specs=[pl.BlockSpec((1, N_IDX_HEADS, IDX_DIM), lambda b, s, pt: (b, 0, 0)),
                  pl.BlockSpec((1, N_IDX_HEADS, 1), lambda b, s, pt: (b, 0, 0))]
                 + [page_spec(j) for j in range(pg)],
        out_specs=pl.BlockSpec((1, pg, page), lambda b, s, pt: (b, s, 0)),
    )
    return pl.pallas_call(
        _idx_scores_body,
        grid_spec=grid_spec,
        out_shape=jax.ShapeDtypeStruct((db, n_pages, page), F32),
        compiler_params=_cparams(("parallel", "arbitrary")),
        name="sample_idx_scores",
    )(page_table, qi3, wi3, *([kidx_pages] * pg))


def _sample_select_body(sc_ref, qi_ref, sm_ref, mask_ref, key_ref, *, n_tiles, topk):
    rows = sc_ref.shape[1]
    lane = lax.broadcasted_iota(I32, (rows, LANES), 1)

    def fill(t, carry):
        key_ref[t] = _sort_key(sc_ref[t])
        return carry

    lax.fori_loop(0, n_tiles, fill, 0)
    sm = sm_ref[...]
    ki_new = sm[:, 0:IDX_DIM].astype(BF16).astype(F32)
    acc = jnp.zeros((rows, 1), F32)
    for hd in range(N_IDX_HEADS):
        qh = qi_ref[:, hd * IDX_DIM:(hd + 1) * IDX_DIM].astype(F32)
        s = jnp.sum(qh * ki_new, axis=1, keepdims=True)
        acc = acc + jnp.maximum(s, 0.0) * (sm[:, SM_WI + hd:SM_WI + hd + 1] * (N_IDX_HEADS ** -0.5))
    sc_new = jnp.broadcast_to(acc + 0.0, (rows, LANES))
    key_ref[n_tiles] = jnp.where(lane == 0, _sort_key(sc_new), INT_MIN)

    def count_where(pred):
        def body(t, a):
            return a + pred(key_ref[t]).astype(F32)
        a = lax.fori_loop(0, n_tiles + 1, body, jnp.zeros((rows, LANES), F32))
        return jnp.sum(a, axis=1, keepdims=True)

    thr = _kth_key_search(lambda cand: count_where(lambda kt: kt >= cand), (rows, 1), float(topk))
    need = float(topk) - count_where(lambda kt: kt > thr)
    cnt_eq = count_where(lambda kt: kt == thr)
    any_tie = jnp.max(jnp.where(cnt_eq > need, 1.0, 0.0)) > 0.0

    @pl.when(jnp.logical_not(any_tie))
    def _():
        def body(t, carry):
            mask_ref[t] = jnp.where(key_ref[t] >= thr, 0.0, NEG_BIG)
            return carry
        lax.fori_loop(0, n_tiles + 1, body, 0)

    @pl.when(any_tie)
    def _():
        sub = lax.broadcasted_iota(I32, (LANES, LANES), 0)
        lane2 = lax.broadcasted_iota(I32, (LANES, LANES), 1)
        upper = (sub <= lane2).astype(BF16)
        ones = jnp.ones((LANES, LANES), BF16)

        def body(t, offset):
            kt = key_ref[t]
            eq = (kt == thr).astype(BF16)
            prefix = jnp.dot(eq, upper, preferred_element_type=F32) + offset
            sel = (kt > thr) | ((eq > 0) & (prefix <= need))
            mask_ref[t] = jnp.where(sel, 0.0, NEG_BIG)
            return offset + jnp.dot(eq, ones, preferred_element_type=F32)

        lax.fori_loop(0, n_tiles + 1, body, jnp.zeros((rows, LANES), F32))


def _sample_select(scores_t, qi2d, small, topk):
    n_tiles, db, page = scores_t.shape
    vm = pl.BlockSpec(memory_space=pltpu.VMEM)
    return pl.pallas_call(
        functools.partial(_sample_select_body, n_tiles=n_tiles, topk=topk),
        in_specs=[vm, vm, vm],
        out_specs=vm,
        out_shape=jax.ShapeDtypeStruct((n_tiles + 1, db, page), F32),
        scratch_shapes=[pltpu.VMEM((n_tiles + 1, db, page), I32)],
        compiler_params=pltpu.CompilerParams(vmem_limit_bytes=VMEM_LIMIT),
        name="sample_select",
    )(scores_t, qi2d, small)


def _sample_attn_body(pt_ref, q_ref, mask_ref, mnew_ref, knew_ref, vnew_ref, relt_ref, *rest, past_len, page):
    pg = ATT_PAGES_PER_STEP
    k_refs = rest[:pg]
    v_refs = rest[pg:2 * pg]
    o_ref, m_s, l_s, acc_s = rest[2 * pg:]
    s = pl.program_id(1)
    n_steps = pl.num_programs(1)
    nh = SAMPLE_Q_ROWS
    far_bucket = _far_bucket_checked(MAX_DISTANCE + 1)

    @pl.when(s == 0)
    def _():
        m_s[...] = jnp.full(m_s.shape, NEG_BIG, F32)
        l_s[...] = jnp.zeros(l_s.shape, F32)
        acc_s[...] = jnp.zeros(acc_s.shape, F32)

    relt = relt_ref[...]
    c_far = relt[:, far_bucket:far_bucket + 1]

    def bias_of(dist):
        bucket = _rel_bucket(dist)
        acc = jnp.zeros((nh, dist.shape[1]), F32)
        for bk in range(N_BUCKETS):
            acc = jnp.where(bucket == bk, relt[:, bk:bk + 1], acc)
        return acc - c_far

    q = q_ref[0].astype(BF16)
    logits = []
    for j in range(pg):
        l = _bdot(q, k_refs[j][...]) + mask_ref[0, j:j + 1, :]
        logits.append(l)
    logits = jnp.concatenate(logits, axis=1)

    kpos = (s * pg) * page + lax.broadcasted_iota(I32, (1, pg * page), 1)
    near = past_len - ((s + 1) * pg * page - 1) <= MAX_DISTANCE
    logits = logits + lax.cond(near, lambda: bias_of(past_len - kpos),
                               lambda: jnp.zeros((nh, pg * page), F32))

    m_old = m_s[...]
    m_new = jnp.maximum(m_old, jnp.max(logits, axis=1, keepdims=True))
    alpha = jnp.exp(m_old - m_new)
    p = jnp.exp(logits - m_new)
    l_new = l_s[...] * alpha + jnp.sum(p, axis=1, keepdims=True)
    acc = acc_s[...] * alpha
    for j in range(pg):
        acc = acc + _bdot_nt(p[:, j * page:(j + 1) * page], v_refs[j][...])
    m_s[...] = m_new
    l_s[...] = l_new
    acc_s[...] = acc

    @pl.when(s == n_steps - 1)
    def _():
        kn = knew_ref[0].astype(BF16).astype(F32)
        vn = vnew_ref[0].astype(BF16).astype(F32)
        ln = jnp.sum(q.astype(F32) * kn, axis=1, keepdims=True) + bias_of(jnp.zeros((1, 1), I32)) \
            + mnew_ref[0][:, 0:1]
        m_fin = jnp.maximum(m_new, ln)
        a2 = jnp.exp(m_new - m_fin)
        pn = jnp.exp(ln - m_fin)
        l_fin = l_new * a2 + pn
        res = (acc * a2 + pn.astype(BF16).astype(F32) * vn) / l_fin
        row = lax.broadcasted_iota(I32, res.shape, 0)
        hpg = N_ATT_HEADS // N_KV_HEADS
        o_ref[0] = jnp.where((row >= hpg) & (row < 2 * hpg), pltpu.roll(res, ATT_HEAD_DIM, 1), res)


def _sample_attn(page_table, q_lh, mask_pages, mask_new, k_new, v_new, rel_t, k_pages, v_pages, past_len):
    db, n_pages = page_table.shape
    page = k_pages.shape[2]
    pg = ATT_PAGES_PER_STEP

    def page_spec(j):
        return pl.BlockSpec((None, KV_DIM, page), lambda b, s, pt: (pt[b, s * pg + j], 0, 0))

    row3 = lambda n: pl.BlockSpec((1, 1, n), lambda b, s, pt: (b, 0, 0))
    grid_spec = pltpu.PrefetchScalarGridSpec(
        num_scalar_prefetch=1,
        grid=(db, n_pages // pg),
        in_specs=[pl.BlockSpec((1, SAMPLE_Q_ROWS, LANES), lambda b, s, pt: (b, 0, 0)),
                  pl.BlockSpec((1, pg, page), lambda b, s, pt: (b, s, 0)),
                  row3(LANES), row3(KV_DIM), row3(KV_DIM),
                  pl.BlockSpec((SAMPLE_Q_ROWS, N_BUCKETS), lambda b, s, pt: (0, 0))]
                 + [page_spec(j) for j in range(pg)] * 2,
        out_specs=pl.BlockSpec((1, SAMPLE_Q_ROWS, LANES), lambda b, s, pt: (b, 0, 0)),
        scratch_shapes=[pltpu.VMEM((SAMPLE_Q_ROWS, 1), F32), pltpu.VMEM((SAMPLE_Q_ROWS, 1), F32),
                        pltpu.VMEM((SAMPLE_Q_ROWS, LANES), F32)],
    )
    return pl.pallas_call(
        functools.partial(_sample_attn_body, past_len=past_len, page=page),
        grid_spec=grid_spec,
        out_shape=jax.ShapeDtypeStruct((db, SAMPLE_Q_ROWS, LANES), F32),
        compiler_params=_cparams(("parallel", "arbitrary")),
        name="sample_attn",
    )(page_table, q_lh, mask_pages, mask_new, k_new, v_new, rel_t, *([k_pages] * pg), *([v_pages] * pg))


def _gdn_sample_body(alog_ref, dtb_ref, x_ref, cst_ref, w_ref, sm_ref, zg_ref, gn_ref, s0_ref,
                     o_ref, s_ref, cnew_ref):
    x = x_ref[0]
    cst = cst_ref[...]
    w = w_ref[...]
    acc = cst[0:1, :] * w[0:1, :]
    for j in range(1, CONV_W - 1):
        acc = acc + cst[j:j + 1, :] * w[j:j + 1, :]
    acc = acc + x * w[CONV_W - 1:CONV_W, :]
    xc = _silu(acc)
    cnew_ref[0:CONV_W - 2, :] = cst[1:CONV_W - 1, :]
    cnew_ref[CONV_W - 2:CONV_W - 1, :] = x
    sm = sm_ref[0]
    gn = gn_ref[...]
    d = GDN_HEAD_DIM
    for h in range(N_GDN_HEADS):
        q = xc[:, h * d:(h + 1) * d]
        k = xc[:, D_GDN + h * d:D_GDN + (h + 1) * d]
        v = xc[:, 2 * D_GDN + h * d:2 * D_GDN + (h + 1) * d]
        q = q * lax.rsqrt(jnp.sum(q * q, axis=-1, keepdims=True) + 1e-6) * (d ** -0.5)
        k = k * lax.rsqrt(jnp.sum(k * k, axis=-1, keepdims=True) + 1e-6)
        a_neg = -jnp.exp(jnp.zeros((1, 1), F32) + alog_ref[h])
        g = a_neg * _softplus(sm[:, SM_AG + h:SM_AG + h + 1] + dtb_ref[h])
        beta = 1.0 / (1.0 + jnp.exp(-sm[:, SM_BG + h:SM_BG + h + 1]))
        st = s0_ref[h] * jnp.exp(g)
        k_col = jnp.broadcast_to(k, (d, d)).T
        q_col = jnp.broadcast_to(q, (d, d)).T
        kv = jnp.sum(k_col * st, axis=0, keepdims=True)
        delta = (v - kv) * beta
        st = st + k_col * delta
        s_ref[h] = st
        o = jnp.sum(q_col * st, axis=0, keepdims=True)
        o_ref[0, :, h * d:(h + 1) * d] = _gated_norm(o, gn, zg_ref[0][:, h * d:(h + 1) * d]).astype(BF16)


def _gdn_sample(a_log, dt_bias, qkv3, state_conv_l, conv_w, small3, zg3, gdn_norm, state_ssm_l):
    db = qkv3.shape[0]
    d = GDN_HEAD_DIM
    nh = N_GDN_HEADS
    row3 = lambda n: pl.BlockSpec((1, 1, n), lambda b: (b, 0, 0))
    return pl.pallas_call(
        _gdn_sample_body,
        grid=(db,),
        in_specs=[pl.BlockSpec(memory_space=pltpu.SMEM), pl.BlockSpec(memory_space=pltpu.SMEM),
                  row3(3 * D_GDN),
                  pl.BlockSpec((None, CONV_W - 1, 3 * D_GDN), lambda b: (b, 0, 0)),
                  pl.BlockSpec((CONV_W, 3 * D_GDN), lambda b: (0, 0)),
                  row3(LANES), row3(D_GDN),
                  pl.BlockSpec((1, LANES), lambda b: (0, 0)),
                  pl.BlockSpec((None, nh, d, d), lambda b: (b, 0, 0, 0))],
        out_specs=[row3(D_GDN),
                   pl.BlockSpec((None, nh, d, d), lambda b: (b, 0, 0, 0)),
                   pl.BlockSpec((None, CONV_W - 1, 3 * D_GDN), lambda b: (b, 0, 0))],
        out_shape=[jax.ShapeDtypeStruct((db, 1, D_GDN), BF16),
                   jax.ShapeDtypeStruct((db, nh, d, d), F32),
                   jax.ShapeDtypeStruct((db, CONV_W - 1, 3 * D_GDN), F32)],
        compiler_params=_cparams(("parallel",)),
        name="gdn_sample",
    )(a_log, dt_bias, qkv3, state_conv_l, conv_w, small3, zg3, gdn_norm.reshape(1, LANES), state_ssm_l)


def kernel(x_prompt, x_sample, cache_k, cache_v, cache_kidx, state_ssm, state_conv, page_table, norm_in, w_in,
           conv_w, a_log, dt_bias, gdn_norm, w_out, rel_table, norm_final):
    depth = w_in.shape[0]
    assert depth == 1, "single-layer model"
    batch, seq, d_model = x_prompt.shape
    db, dec_seq, _ = x_sample.shape
    assert dec_seq == 1 and seq % (2 * LANES) == 0 and seq % GDN_C == 0
    n_pool, page = cache_k.shape[1], cache_k.shape[2]
    n_pages = page_table.shape[1]
    past_len = n_pages * page
    assert page == LANES and n_pages % IDX_PAGES_PER_STEP == 0 and n_pages % ATT_PAGES_PER_STEP == 0

    lyr = 0
    w_pad = _prep_w_in(w_in[lyr])
    w_out_bf = w_out[lyr].astype(BF16)

    xp = x_prompt.reshape(batch * seq, d_model)
    q_blk, k2d, v2d, za, qi_blk, small, qkv2d, zg, k_t, v_t, ki_t = _inproj(
        xp, norm_in[lyr], w_pad, tm=2 * Q_BLOCK, blocked=True, seq=seq)
    att_g = _dsa_prompt_t(rel_table, q_blk, qi_blk, small, za, k2d, v2d, batch, seq)
    gdn_g, s_fin = _gdn_prompt(a_log[lyr], dt_bias[lyr], qkv2d, conv_w[lyr], small, zg, gdn_norm[lyr], batch, seq)
    y_prompt = _outproj(xp, att_g, gdn_g, w_out_bf, norm_final, tm=2 * Q_BLOCK).reshape(batch, seq, d_model)
    k_prompt = jnp.transpose(k_t.reshape(batch, N_KV_HEADS, ATT_HEAD_DIM, seq), (0, 3, 1, 2))[None]
    v_prompt = jnp.transpose(v_t.reshape(batch, N_KV_HEADS, ATT_HEAD_DIM, seq), (0, 3, 1, 2))[None]
    kidx_prompt = jnp.transpose(ki_t, (0, 2, 1))[None]
    ssm_prompt = s_fin[None]
    conv_prompt = qkv2d.reshape(batch, seq, 3 * D_GDN)[:, seq - (CONV_W - 1):][None]

    xs = x_sample.reshape(db, d_model)
    q_s, k_s, v_s, za_s, qi_s, small_s, qkv_s, zg_s = _inproj(xs, norm_in[lyr], w_pad, tm=db, blocked=False)
    kidx_t = jnp.transpose(cache_kidx[lyr], (0, 2, 1))
    k_pages_t = jnp.transpose(cache_k[lyr], (0, 2, 3, 1)).reshape(n_pool, KV_DIM, page)
    v_pages_t = jnp.transpose(cache_v[lyr], (0, 2, 3, 1)).reshape(n_pool, KV_DIM, page)
    scores = _idx_scores(page_table, qi_s.reshape(db, N_IDX_HEADS, IDX_DIM),
                         small_s[:, SM_WI:SM_WI + N_IDX_HEADS].reshape(db, N_IDX_HEADS, 1),
                         kidx_t)
    topk = min(TOPK_MAX, (past_len + dec_seq) // 4)
    mask_t = _sample_select(jnp.transpose(scores, (1, 0, 2)), qi_s, small_s, topk)
    mask_pages = jnp.transpose(mask_t[:n_pages], (1, 0, 2))
    mask_new = mask_t[n_pages].reshape(db, 1, page)
    hpg = N_ATT_HEADS // N_KV_HEADS
    q8 = q_s.reshape(db, N_ATT_HEADS, ATT_HEAD_DIM).astype(F32)
    zq = jnp.zeros((db, hpg, ATT_HEAD_DIM), F32)
    q_lh = jnp.concatenate([jnp.concatenate([q8[:, :hpg], zq], axis=2),
                            jnp.concatenate([zq, q8[:, hpg:]], axis=2),
                            jnp.zeros((db, SAMPLE_Q_ROWS - N_ATT_HEADS, LANES), F32)], axis=1)
    rel_t = jnp.concatenate([rel_table.T, jnp.zeros((SAMPLE_Q_ROWS - N_ATT_HEADS, N_BUCKETS), F32)], axis=0)
    att_raw = _sample_attn(page_table, q_lh, mask_pages, mask_new, k_s.reshape(db, 1, KV_DIM),
                           v_s.reshape(db, 1, KV_DIM), rel_t,
                           k_pages_t, v_pages_t, past_len)
    att_s = att_raw[:, :N_ATT_HEADS, :ATT_HEAD_DIM].reshape(db, D_ATT)
    gdn_s, s_new, conv_new = _gdn_sample(a_log[lyr], dt_bias[lyr], qkv_s.reshape(db, 1, 3 * D_GDN), state_conv[lyr],
                                         conv_w[lyr], small_s.reshape(db, 1, LANES), zg_s.reshape(db, 1, D_GDN),
                                         gdn_norm[lyr], state_ssm[lyr])
    y_sample = _outproj(xs, att_s, gdn_s.reshape(db, D_GDN), w_out_bf, norm_final, tm=db,
                        za=za_s).reshape(db, 1, d_model)
    k_sample = k_s.reshape(1, db, 1, N_KV_HEADS, ATT_HEAD_DIM)
    v_sample = v_s.reshape(1, db, 1, N_KV_HEADS, ATT_HEAD_DIM)
    kidx_sample = small_s[:, :IDX_DIM].reshape(1, db, 1, IDX_DIM)

    return (y_prompt, y_sample, k_prompt, v_prompt, kidx_prompt, ssm_prompt, conv_prompt,
            k_sample, v_sample, kidx_sample, s_new[None], conv_new[None])
```

```python
import functools
import math

import numpy as np
import jax
import jax.numpy as jnp
from jax import lax
from jax.experimental import pallas as pl
from jax.experimental.pallas import tpu as pltpu

F32 = jnp.float32
BF16 = jnp.bfloat16
I32 = jnp.int32

N_ATT_HEADS = 8
ATT_HEAD_DIM = 64
N_KV_HEADS = 2
D_ATT = N_ATT_HEADS * ATT_HEAD_DIM
KV_DIM = N_KV_HEADS * ATT_HEAD_DIM
N_IDX_HEADS = 16
IDX_DIM = 64
TOPK_MAX = 256
N_GDN_HEADS = 4
GDN_HEAD_DIM = 128
D_GDN = N_GDN_HEADS * GDN_HEAD_DIM
CONV_W = 4
N_BUCKETS = 32
MAX_DISTANCE = 128
Q_BLOCK = 128
RMS_EPS = 1e-6
NEG_BIG = -1e30
PROJ_SIZES = (D_ATT, KV_DIM, KV_DIM, D_ATT, N_IDX_HEADS * IDX_DIM, IDX_DIM, N_IDX_HEADS,
              3 * D_GDN, D_GDN, N_GDN_HEADS, N_GDN_HEADS)

LANES = 128
SUBLANES = 8
VMEM_LIMIT = 56 * 1024 * 1024

OFF_Q = 0
OFF_K = OFF_Q + D_ATT
OFF_V = OFF_K + KV_DIM
OFF_ZA = OFF_V + KV_DIM
OFF_QI = OFF_ZA + D_ATT
OFF_SM = OFF_QI + N_IDX_HEADS * IDX_DIM
OFF_QKV = OFF_SM + LANES
OFF_ZG = OFF_QKV + 3 * D_GDN
D_PROJ_PAD = OFF_ZG + D_GDN
SM_WI = IDX_DIM
SM_AG = SM_WI + N_IDX_HEADS
SM_BG = SM_AG + N_GDN_HEADS

GDN_C = 128
GDN_CHUNKS_PER_ITER = 4
KEY_NEG_BIG = int(np.array(NEG_BIG, np.float32).view(np.int32)) ^ 0x7FFFFFFF
INT_MIN = -2 ** 31


def _cparams(sem):
    return pltpu.CompilerParams(dimension_semantics=sem, vmem_limit_bytes=VMEM_LIMIT)


def _silu(x):
    return x * (1.0 / (1.0 + jnp.exp(-x)))


def _bdot(a, b):
    return jnp.dot(a.astype(BF16), b.astype(BF16), preferred_element_type=F32)


def _bdot_nt(a, b):
    return lax.dot_general(a.astype(BF16), b.astype(BF16), (((1,), (1,)), ((), ())),
                           preferred_element_type=F32)


def _sort_key(x):
    i = pltpu.bitcast(x, I32)
    return jnp.where(i < 0, i ^ 0x7FFFFFFF, i)


def _inproj_body(x_ref, g_ref, w_ref, q_ref, k_ref, v_ref, za_ref, qi_ref, sm_ref, qkv_ref, zg_ref, *t_refs,
                 blocked):
    x = x_ref[...]
    ms = jnp.mean(x * x, axis=-1, keepdims=True)
    h = ((x * lax.rsqrt(ms + RMS_EPS)) * g_ref[...]).astype(BF16)

    def mm(a, b):
        return jnp.dot(h, w_ref[:, a:b], preferred_element_type=F32)

    q = (mm(OFF_Q, OFF_K) * (ATT_HEAD_DIM ** -0.5)).astype(BF16)
    qi = (mm(OFF_QI, OFF_SM) * (IDX_DIM ** -0.5)).astype(BF16)
    if blocked:
        for r in range(x.shape[0] // Q_BLOCK):
            rs = slice(r * Q_BLOCK, (r + 1) * Q_BLOCK)
            for j in range(D_ATT // LANES):
                q_ref[r, j] = q[rs, j * LANES:(j + 1) * LANES]
            for j in range(N_IDX_HEADS * IDX_DIM // LANES):
                qi_ref[r, j] = qi[rs, j * LANES:(j + 1) * LANES]
    else:
        q_ref[...] = q
        qi_ref[...] = qi
    k = mm(OFF_K, OFF_V)
    v = mm(OFF_V, OFF_ZA)
    sm = mm(OFF_SM, OFF_QKV)
    k_ref[...] = k
    v_ref[...] = v
    sm_ref[...] = sm
    za_ref[...] = mm(OFF_ZA, OFF_QI)
    qkv_ref[...] = mm(OFF_QKV, OFF_ZG)
    zg_ref[...] = mm(OFF_ZG, D_PROJ_PAD)
    if t_refs:
        kt_ref, vt_ref, kit_ref = t_refs
        kt_ref[...] = k.T
        vt_ref[...] = v.T
        kit_ref[...] = sm.T[0:IDX_DIM, :]


def _prep_w_in(w):
    splits = np.cumsum(PROJ_SIZES)[:-1].tolist()
    q, k, v, z_a, qi, ki, wi, qkv, z_g, a_g, b_g = jnp.split(w, splits, axis=1)
    pad = jnp.zeros((w.shape[0], LANES - (SM_BG + N_GDN_HEADS)), w.dtype)
    small = jnp.concatenate([ki, wi, a_g, b_g, pad], axis=1)
    return jnp.concatenate([q, k, v, z_a, qi, small, qkv, z_g], axis=1).astype(BF16)


def _inproj(x2d, norm_g, w_pad, tm, blocked, seq=None):
    t, d = x2d.shape
    nblk = t // tm
    t_shapes, t_specs = [], []
    if blocked:
        spb = seq // tm
        for n in (KV_DIM, KV_DIM, IDX_DIM):
            t_shapes.append(jax.ShapeDtypeStruct((t // seq, n, seq), F32))
            t_specs.append(pl.BlockSpec((None, n, tm), lambda i: (i // spb, 0, i % spb)))
        rb = tm // Q_BLOCK
        q_shape = jax.ShapeDtypeStruct((t // Q_BLOCK, D_ATT // LANES, Q_BLOCK, LANES), BF16)
        qi_shape = jax.ShapeDtypeStruct((t // Q_BLOCK, N_IDX_HEADS * IDX_DIM // LANES, Q_BLOCK, LANES), BF16)
        q_spec = pl.BlockSpec((rb, D_ATT // LANES, Q_BLOCK, LANES), lambda i: (i, 0, 0, 0))
        qi_spec = pl.BlockSpec((rb, N_IDX_HEADS * IDX_DIM // LANES, Q_BLOCK, LANES), lambda i: (i, 0, 0, 0))
    else:
        q_shape = jax.ShapeDtypeStruct((t, D_ATT), BF16)
        qi_shape = jax.ShapeDtypeStruct((t, N_IDX_HEADS * IDX_DIM), BF16)
        q_spec = pl.BlockSpec((tm, D_ATT), lambda i: (i, 0))
        qi_spec = pl.BlockSpec((tm, N_IDX_HEADS * IDX_DIM), lambda i: (i, 0))

    def row(n):
        return pl.BlockSpec((tm, n), lambda i: (i, 0))

    return pl.pallas_call(
        functools.partial(_inproj_body, blocked=blocked),
        grid=(nblk,),
        in_specs=[row(d), pl.BlockSpec((1, d), lambda i: (0, 0)),
                  pl.BlockSpec((d, D_PROJ_PAD), lambda i: (0, 0))],
        out_specs=[q_spec, row(KV_DIM), row(KV_DIM), row(D_ATT), qi_spec, row(LANES), row(3 * D_GDN), row(D_GDN)]
                  + t_specs,
        out_shape=[q_shape, jax.ShapeDtypeStruct((t, KV_DIM), F32), jax.ShapeDtypeStruct((t, KV_DIM), F32),
                   jax.ShapeDtypeStruct((t, D_ATT), F32), qi_shape, jax.ShapeDtypeStruct((t, LANES), F32),
                   jax.ShapeDtypeStruct((t, 3 * D_GDN), F32), jax.ShapeDtypeStruct((t, D_GDN), F32)] + t_shapes,
        compiler_params=_cparams(("parallel",)),
        name="inproj_blocked" if blocked else "inproj_rows",
    )(x2d, norm_g.reshape(1, d), w_pad)


def _rel_bucket(dist):
    n = jnp.maximum(dist, 0)
    max_exact = N_BUCKETS // 2
    nf = jnp.maximum(n, 1).astype(F32)
    large = max_exact + (jnp.log(nf / max_exact) / math.log(MAX_DISTANCE / max_exact)
                         * (N_BUCKETS - max_exact)).astype(I32)
    large = jnp.minimum(large, N_BUCKETS - 1)
    return jnp.where(n < max_exact, n, large)


def _far_bucket_checked(first_far):
    d = np.arange(first_far, 1 << 16, dtype=np.float32)
    b = 16 + (np.log(d / 16) / math.log(MAX_DISTANCE / 16) * 16).astype(np.int32)
    assert int(b.min()) >= N_BUCKETS - 1
    return N_BUCKETS - 1


def _kth_key_search(count_ge, shape, k):
    def body(step, t):
        cand = t + jnp.left_shift(jnp.int32(1), 31 - step)
        return jnp.where(count_ge(cand) >= k, cand, t)

    return lax.fori_loop(0, 32, body, jnp.full(shape, INT_MIN, I32))


def _dsa_prompt_body(rel_ref, q_ref, qi_ref, smq_ref, za_ref, k_ref, v_ref, sms_ref, o_ref,
                     kk_ref, vv_ref, ki_ref, bias_ref, key_ref, mask_ref, lg_ref, *, seq, topk):
    b = pl.program_id(0)
    i = pl.program_id(1)
    nt = seq // LANES
    lane = lax.broadcasted_iota(I32, (Q_BLOCK, LANES), 1)
    sub = lax.broadcasted_iota(I32, (Q_BLOCK, LANES), 0)
    far_bucket = _far_bucket_checked(MAX_DISTANCE + 1)

    @pl.when((b == 0) & (i == 0))
    def _():
        for dt in range(2):
            bucket = _rel_bucket(dt * LANES + sub - lane)
            for h in range(N_ATT_HEADS):
                acc = jnp.zeros((Q_BLOCK, LANES), F32)
                for bk in range(N_BUCKETS):
                    acc = jnp.where(bucket == bk, rel_ref[bk, h], acc)
                bias_ref[h, dt] = acc - rel_ref[far_bucket, h]

    @pl.when(i == 0)
    def _():
        lo = lax.broadcasted_iota(I32, (seq, LANES), 1) < ATT_HEAD_DIM
        kf = k_ref[...]
        vf = v_ref[...]
        g0 = jnp.where(lo, kf, 0.0)
        g1 = jnp.where(lo, 0.0, kf)
        kk_ref[0] = g0.astype(BF16)
        kk_ref[1] = pltpu.roll(g0, ATT_HEAD_DIM, 1).astype(BF16)
        kk_ref[2] = pltpu.roll(g1, ATT_HEAD_DIM, 1).astype(BF16)
        kk_ref[3] = g1.astype(BF16)
        w0 = jnp.where(lo, vf, 0.0)
        w1 = jnp.where(lo, 0.0, vf)
        vv_ref[0] = w0.astype(BF16)
        vv_ref[1] = pltpu.roll(w0, ATT_HEAD_DIM, 1).astype(BF16)
        vv_ref[2] = pltpu.roll(w1, ATT_HEAD_DIM, 1).astype(BF16)
        vv_ref[3] = w1.astype(BF16)
        c0 = jnp.where(lo, sms_ref[...], 0.0)
        ki_ref[0] = c0.astype(BF16)
        ki_ref[1] = pltpu.roll(c0, IDX_DIM, 1).astype(BF16)

    n_live = i + 1
    n_live2 = i // 2 + 1
    row_pos = i * Q_BLOCK + sub

    n_pairs = N_IDX_HEADS // 2
    wq = smq_ref[...]
    wcols = [wq[:, SM_WI + h:SM_WI + h + 1] * (N_IDX_HEADS ** -0.5) for h in range(N_IDX_HEADS)]
    qi_all = qi_ref[0].reshape(n_pairs * Q_BLOCK, LANES)

    def score_tile(t2, carry):
        ks = pl.ds(pl.multiple_of(t2 * 2 * LANES, 2 * LANES), 2 * LANES)
        acc = jnp.zeros((Q_BLOCK, 2 * LANES), F32)
        for half in range(2):
            s_all = lax.dot_general(qi_all, ki_ref[half, ks, :], (((1,), (1,)), ((), ())),
                                    preferred_element_type=F32)
            for j in range(n_pairs):
                s = s_all[j * Q_BLOCK:(j + 1) * Q_BLOCK]
                acc = acc + jnp.maximum(s, 0.0) * wcols[2 * j + half]
        acc = acc + 0.0
        for u in range(2):
            col = (t2 * 2 + u) * LANES + lane
            sc = jnp.where(col <= row_pos, acc[:, u * LANES:(u + 1) * LANES], NEG_BIG)
            key_ref[t2 * 2 + u] = _sort_key(sc)
        return carry

    lax.fori_loop(0, n_live2, score_tile, 0)

    @pl.when(i % 2 == 0)
    def _():
        key_ref[i + 1] = jnp.full((Q_BLOCK, LANES), KEY_NEG_BIG, I32)
        mask_ref[i + 1] = jnp.full((Q_BLOCK, LANES), NEG_BIG, F32)

    n_dead = ((nt - 2 * n_live2) * LANES).astype(F32)

    def count_where(pred, upto):
        def body(t, acc):
            return acc + pred(key_ref[t], t).astype(F32)
        acc = lax.fori_loop(0, upto, body, jnp.zeros((Q_BLOCK, LANES), F32))
        return jnp.sum(acc, axis=1, keepdims=True)

    def count_ge(cand):
        live = count_where(lambda kt, t: kt >= cand, 2 * n_live2)
        return live + jnp.where(cand <= KEY_NEG_BIG, n_dead, 0.0)

    thr = _kth_key_search(count_ge, Q_BLOCK, float(topk))
    cnt_gt = count_where(lambda kt, t: kt > thr, 2 * n_live2) + jnp.where(thr < KEY_NEG_BIG, n_dead, 0.0)
    need = float(topk) - cnt_gt

    def causal_of(t):
        return (t * LANES + lane) <= row_pos

    cnt_ceq = count_where(lambda kt, t: (kt == thr) & causal_of(t), n_live)
    any_tie = jnp.max(jnp.where(cnt_ceq > need, 1.0, 0.0)) > 0.0

    @pl.when(jnp.logical_not(any_tie))
    def _():
        def body(t, carry):
            mask_ref[t] = jnp.where((key_ref[t] >= thr) & causal_of(t), 0.0, NEG_BIG)
            return carry
        lax.fori_loop(0, n_live, body, 0)

    @pl.when(any_tie)
    def _():
        upper = (sub <= lane).astype(BF16)
        ones = jnp.ones((LANES, LANES), BF16)

        def body(t, offset):
            kt = key_ref[t]
            cz = causal_of(t)
            eq = ((kt == thr) & cz).astype(BF16)
            prefix = jnp.dot(eq, upper, preferred_element_type=F32) + offset
            sel = ((kt > thr) & cz) | ((eq > 0) & (prefix <= need))
            mask_ref[t] = jnp.where(sel, 0.0, NEG_BIG)
            return offset + jnp.dot(eq, ones, preferred_element_type=F32)

        lax.fori_loop(0, n_live, body, jnp.zeros((Q_BLOCK, LANES), F32))

    za = za_ref[...]
    lo_lane = lane < ATT_HEAD_DIM
    n_qp = N_ATT_HEADS // 2
    for g in range(N_KV_HEADS):
        pairs = [p for p in range(n_qp) if (2 * p) // (N_ATT_HEADS // N_KV_HEADS) == g]
        q_stack = jnp.concatenate([q_ref[0, p] for p in pairs], axis=0)
        for half in range(2):
            heads = [2 * p + half for p in pairs]

            def logit_tile(t2, mx, heads=heads, half=half, g=g):
                ks = pl.ds(pl.multiple_of(t2 * 2 * LANES, 2 * LANES), 2 * LANES)
                l_all = lax.dot_general(q_stack, kk_ref[2 * g + half, ks, :], (((1,), (1,)), ((), ())),
                                        preferred_element_type=F32)
                out = []
                for n, h in enumerate(heads):
                    m_h = mx[n]
                    for u in range(2):
                        t = t2 * 2 + u
                        l = l_all[n * Q_BLOCK:(n + 1) * Q_BLOCK, u * LANES:(u + 1) * LANES] + mask_ref[t]
                        lg_ref[h, t] = l
                        m_h = jnp.maximum(m_h, jnp.where(t < i - 1, l, NEG_BIG))
                    out.append(m_h)
                return tuple(out)

            mx0 = tuple(jnp.full((Q_BLOCK, LANES), NEG_BIG, F32) for _ in heads)
            mx = lax.fori_loop(0, n_live2, logit_tile, mx0)
            for n, h in enumerate(heads):
                l0 = lg_ref[h, i] + bias_ref[h, 0]
                lg_ref[h, i] = l0
                m_h = jnp.maximum(mx[n], l0)
                im1 = jnp.maximum(i - 1, 0)
                l1 = lg_ref[h, im1] + jnp.where(i > 0, bias_ref[h, 1], 0.0)
                lg_ref[h, im1] = l1
                m_h = jnp.maximum(m_h, l1)
                m_row = jnp.max(m_h, axis=1, keepdims=True)

                def pv_tile(t, carry, h=h, half=half, g=g, m_row=m_row):
                    ssum, acc = carry
                    p = jnp.exp(lg_ref[h, t] - m_row)
                    ts = pl.ds(pl.multiple_of(t * LANES, LANES), LANES)
                    acc = acc + jnp.dot(p.astype(BF16), vv_ref[2 * g + half, ts, :], preferred_element_type=F32)
                    return ssum + p, acc

                ssum, acc = lax.fori_loop(0, n_live, pv_tile,
                                          (jnp.zeros((Q_BLOCK, LANES), F32), jnp.zeros((Q_BLOCK, LANES), F32)))
                l_row = jnp.sum(ssum, axis=1, keepdims=True)
                lg_ref[h, nt] = acc / l_row
        for p in pairs:
            o_pair = jnp.where(lo_lane, lg_ref[2 * p, nt], lg_ref[2 * p + 1, nt])
            cs = slice(p * LANES, (p + 1) * LANES)
            o_ref[:, cs] = (o_pair * _silu(za[:, cs])).astype(BF16)


def _dsa_prompt(rel_table, q_blk, qi_blk, small, za, k2d, v2d, batch, seq):
    nb = seq // Q_BLOCK
    nt = seq // LANES
    topk = min(TOPK_MAX, seq // 4)
    n_qp = D_ATT // LANES
    n_ip = N_IDX_HEADS * IDX_DIM // LANES
    t = batch * seq
    return pl.pallas_call(
        functools.partial(_dsa_prompt_body, seq=seq, topk=topk),
        grid=(batch, nb),
        in_specs=[
            pl.BlockSpec(memory_space=pltpu.SMEM),
            pl.BlockSpec((1, n_qp, Q_BLOCK, LANES), lambda b, i: (b * nb + i, 0, 0, 0)),
            pl.BlockSpec((1, n_ip, Q_BLOCK, LANES), lambda b, i: (b * nb + i, 0, 0, 0)),
            pl.BlockSpec((Q_BLOCK, LANES), lambda b, i: (b * nb + i, 0)),
            pl.BlockSpec((Q_BLOCK, D_ATT), lambda b, i: (b * nb + i, 0)),
            pl.BlockSpec((seq, KV_DIM), lambda b, i: (b, 0)),
            pl.BlockSpec((seq, KV_DIM), lambda b, i: (b, 0)),
            pl.BlockSpec((seq, LANES), lambda b, i: (b, 0)),
        ],
        out_specs=pl.BlockSpec((Q_BLOCK, D_ATT), lambda b, i: (b * nb + i, 0)),
        out_shape=jax.ShapeDtypeStruct((t, D_ATT), BF16),
        scratch_shapes=[
            pltpu.VMEM((2 * N_KV_HEADS, seq, LANES), BF16),
            pltpu.VMEM((2 * N_KV_HEADS, seq, LANES), BF16),
            pltpu.VMEM((2, seq, LANES), BF16),
            pltpu.VMEM((N_ATT_HEADS, 2, Q_BLOCK, LANES), F32),
            pltpu.VMEM((nt + 1, Q_BLOCK, LANES), I32),
            pltpu.VMEM((nt + 1, Q_BLOCK, LANES), F32),
            pltpu.VMEM((N_ATT_HEADS, nt + 1, Q_BLOCK, LANES), F32),
        ],
        compiler_params=_cparams(("arbitrary", "arbitrary")),
        name="dsa_prompt",
    )(rel_table, q_blk, qi_blk, small, za, k2d, v2d, small)


KT = 2 * LANES


def _fold8(x, op):
    binop = {jnp.sum: jnp.add, jnp.max: jnp.maximum}[op]
    r = x.reshape(x.shape[0] // SUBLANES, SUBLANES, x.shape[1])
    while r.shape[0] > 1:
        half = r.shape[0] // 2
        r = binop(r[:half], r[half:])
    return r[0]


def _dsa_prompt_t_body(rel_ref, q_ref, qi_ref, smq_ref, za_ref, k_ref, v_ref, sms_ref, o_ref,
                       kk_ref, vvt_ref, ki_ref, bias_ref, qt_ref, qit_ref, key_ref, mask_ref, lg_ref,
                       *, seq, topk):
    b = pl.program_id(0)
    i = pl.program_id(1)
    n_t = seq // KT
    hpg = N_ATT_HEADS // N_KV_HEADS
    far_bucket = _far_bucket_checked(MAX_DISTANCE + 1)
    sub1 = lax.broadcasted_iota(I32, (LANES, LANES), 0)
    lane1 = lax.broadcasted_iota(I32, (LANES, LANES), 1)

    @pl.when((b == 0) & (i == 0))
    def _():
        for dt in range(2):
            bucket = _rel_bucket(dt * LANES + lane1 - sub1)
            for h in range(N_ATT_HEADS):
                acc = jnp.zeros((LANES, LANES), F32)
                for bk in range(N_BUCKETS):
                    acc = jnp.where(bucket == bk, rel_ref[bk, h], acc)
                bias_ref[h, dt] = acc - rel_ref[far_bucket, h]

    @pl.when(i == 0)
    def _():
        lo = lax.broadcasted_iota(I32, (seq, LANES), 1) < ATT_HEAD_DIM
        kf = k_ref[...]
        g0 = jnp.where(lo, kf, 0.0)
        g1 = jnp.where(lo, 0.0, kf)
        kk_ref[0] = g0.astype(BF16)
        kk_ref[1] = pltpu.roll(g0, ATT_HEAD_DIM, 1).astype(BF16)
        kk_ref[2] = pltpu.roll(g1, ATT_HEAD_DIM, 1).astype(BF16)
        kk_ref[3] = g1.astype(BF16)
        c0 = jnp.where(lo, sms_ref[...], 0.0)
        ki_ref[0] = c0.astype(BF16)
        ki_ref[1] = pltpu.roll(c0, IDX_DIM, 1).astype(BF16)
        lo_t = lax.broadcasted_iota(I32, (KT, LANES), 1) < ATT_HEAD_DIM
        for t in range(n_t):
            vf = v_ref[t * KT:(t + 1) * KT, :]
            w0 = jnp.where(lo_t, vf, 0.0)
            w1 = jnp.where(lo_t, 0.0, vf)
            vvt_ref[0, t] = w0.T.astype(BF16)
            vvt_ref[1, t] = pltpu.roll(w0, ATT_HEAD_DIM, 1).T.astype(BF16)
            vvt_ref[2, t] = pltpu.roll(w1, ATT_HEAD_DIM, 1).T.astype(BF16)
            vvt_ref[3, t] = w1.T.astype(BF16)

    n_qp = D_ATT // LANES
    n_ip = N_IDX_HEADS * IDX_DIM // LANES
    for p in range(n_qp):
        qt_ref[:, p * LANES:(p + 1) * LANES] = q_ref[0, p].astype(F32).T.astype(BF16)
    for j in range(n_ip):
        qit_ref[:, j * LANES:(j + 1) * LANES] = qi_ref[0, j].astype(F32).T.astype(BF16)
    sm_t = smq_ref[...].T
    wrow = [sm_t[SM_WI + h:SM_WI + h + 1, :] * (N_IDX_HEADS ** -0.5) for h in range(N_IDX_HEADS)]

    n_live = i // 2 + 1
    n_dead = ((n_t - n_live) * KT).astype(F32)
    kidx = lax.broadcasted_iota(I32, (KT, LANES), 0)
    qpos = i * Q_BLOCK + lax.broadcasted_iota(I32, (KT, LANES), 1)

    def causal_of(t):
        return (t * KT + kidx) <= qpos

    def tile_rows(t):
        return pl.ds(pl.multiple_of(t * KT, KT), KT)

    def score_tile(t, carry):
        acc = jnp.zeros((KT, LANES), F32)
        for half in range(2):
            s_all = jnp.dot(ki_ref[half, tile_rows(t), :], qit_ref[...], preferred_element_type=F32)
            for j in range(n_ip):
                acc = acc + jnp.maximum(s_all[:, j * LANES:(j + 1) * LANES], 0.0) * wrow[2 * j + half]
        acc = acc + 0.0
        key_ref[t] = _sort_key(jnp.where(causal_of(t), acc, NEG_BIG))
        return carry

    lax.fori_loop(0, n_live, score_tile, 0)

    def count_where(pred):
        def body(t, acc):
            return acc + _fold8(pred(key_ref[t], t).astype(F32), jnp.sum)
        acc = lax.fori_loop(0, n_live, body, jnp.zeros((SUBLANES, LANES), F32))
        return jnp.sum(acc, axis=0, keepdims=True)

    def count_ge(cand):
        return count_where(lambda kt, t: kt >= cand) + jnp.where(cand <= KEY_NEG_BIG, n_dead, 0.0)

    thr = _kth_key_search(count_ge, (1, LANES), float(topk))
    cnt_gt = count_where(lambda kt, t: kt > thr) + jnp.where(thr < KEY_NEG_BIG, n_dead, 0.0)
    need = float(topk) - cnt_gt
    cnt_ceq = count_where(lambda kt, t: (kt == thr) & causal_of(t))
    any_tie = jnp.max(jnp.where(cnt_ceq > need, 1.0, 0.0)) > 0.0

    @pl.when(jnp.logical_not(any_tie))
    def _():
        def body(t, carry):
            mask_ref[t] = jnp.where((key_ref[t] >= thr) & causal_of(t), 0.0, NEG_BIG)
            return carry
        lax.fori_loop(0, n_live, body, 0)

    @pl.when(any_tie)
    def _():
        rk = lax.broadcasted_iota(I32, (KT, KT), 0)
        ck = lax.broadcasted_iota(I32, (KT, KT), 1)
        lower = (ck <= rk).astype(BF16)

        def body(t, offset):
            kt = key_ref[t]
            cz = causal_of(t)
            eq = ((kt == thr) & cz).astype(F32)
            prefix = jnp.dot(lower, eq.astype(BF16), preferred_element_type=F32) + offset
            sel = ((kt > thr) & cz) | ((eq > 0.0) & (prefix <= need))
            mask_ref[t] = jnp.where(sel, 0.0, NEG_BIG)
            return offset + jnp.sum(_fold8(eq, jnp.sum), axis=0, keepdims=True)

        lax.fori_loop(0, n_live, body, jnp.zeros((1, LANES), F32))

    @pl.when(n_live % 2 == 1)
    def _():
        mask_ref[n_live] = jnp.full((KT, LANES), NEG_BIG, F32)

    def pair_loop(body, init):
        return lax.fori_loop(0, (n_live + 1) // 2, lambda p, c: body(2 * p + 1, body(2 * p, c)), init)

    za = za_ref[...]
    near_lo = jnp.maximum(i - 1, 0) // 2
    neg8 = jnp.full((SUBLANES, LANES), NEG_BIG, F32)
    zero8 = jnp.zeros((SUBLANES, LANES), F32)
    for g in range(N_KV_HEADS):
        qt_g = qt_ref[:, g * 2 * LANES:(g + 1) * 2 * LANES]

        def logit_tile(t, mx, g=g, qt_g=qt_g):
            mx = list(mx)
            far = t < near_lo
            for half in range(2):
                l_all = jnp.dot(kk_ref[2 * g + half, tile_rows(t), :], qt_g, preferred_element_type=F32)
                for n in range(2):
                    hl = 2 * n + half
                    l = l_all[:, n * LANES:(n + 1) * LANES] + mask_ref[t]
                    lg_ref[hl, t] = l
                    mx[hl] = jnp.maximum(mx[hl], jnp.where(far, _fold8(l, jnp.max), NEG_BIG))
            return tuple(mx)

        mx = pair_loop(logit_tile, (neg8,) * hpg)
        m_row = []
        for hl in range(hpg):
            h = g * hpg + hl
            r0 = pl.ds(pl.multiple_of((i % 2) * LANES, LANES), LANES)
            lg_ref[hl, i // 2, r0, :] = lg_ref[hl, i // 2, r0, :] + bias_ref[h, 0]
            im1 = jnp.maximum(i - 1, 0)
            r1 = pl.ds(pl.multiple_of((im1 % 2) * LANES, LANES), LANES)
            lg_ref[hl, im1 // 2, r1, :] = lg_ref[hl, im1 // 2, r1, :] + jnp.where(i > 0, bias_ref[h, 1], 0.0)
            m8 = jnp.maximum(mx[hl], jnp.maximum(_fold8(lg_ref[hl, near_lo], jnp.max),
                                                 _fold8(lg_ref[hl, i // 2], jnp.max)))
            m_row.append(jnp.max(m8, axis=0, keepdims=True))

        def pv_tile(t, carry, g=g, m_row=m_row):
            ssum = list(carry[:hpg])
            acc = list(carry[hpg:])
            for n in range(2):
                for half in range(2):
                    hl = 2 * n + half
                    p = jnp.exp(lg_ref[hl, t] - m_row[hl])
                    ssum[hl] = ssum[hl] + _fold8(p, jnp.sum)
                    acc[n] = acc[n] + jnp.dot(vvt_ref[2 * g + half, t], p.astype(BF16),
                                              preferred_element_type=F32)
            return tuple(ssum) + tuple(acc)

        zacc = jnp.zeros((LANES, LANES), F32)
        res = pair_loop(pv_tile, (zero8,) * hpg + (zacc, zacc))
        for n in range(2):
            l_lo = jnp.sum(res[2 * n], axis=0, keepdims=True)
            l_hi = jnp.sum(res[2 * n + 1], axis=0, keepdims=True)
            inv = jnp.where(sub1 < ATT_HEAD_DIM, 1.0 / l_lo, 1.0 / l_hi)
            o_pair = (res[hpg + n] * inv).T
            cs = slice((2 * g + n) * LANES, (2 * g + n + 1) * LANES)
            o_ref[:, cs] = (o_pair * _silu(za[:, cs])).astype(BF16)


def _dsa_prompt_t(rel_table, q_blk, qi_blk, small, za, k2d, v2d, batch, seq):
    assert N_KV_HEADS == 2 and N_ATT_HEADS // N_KV_HEADS == 4 and seq % KT == 0
    nb = seq // Q_BLOCK
    n_t = seq // KT
    topk = min(TOPK_MAX, seq // 4)
    n_qp = D_ATT // LANES
    n_ip = N_IDX_HEADS * IDX_DIM // LANES
    t = batch * seq
    return pl.pallas_call(
        functools.partial(_dsa_prompt_t_body, seq=seq, topk=topk),
        grid=(batch, nb),
        in_specs=[
            pl.BlockSpec(memory_space=pltpu.SMEM),
            pl.BlockSpec((1, n_qp, Q_BLOCK, LANES), lambda b, i: (b * nb + i, 0, 0, 0)),
            pl.BlockSpec((1, n_ip, Q_BLOCK, LANES), lambda b, i: (b * nb + i, 0, 0, 0)),
            pl.BlockSpec((Q_BLOCK, LANES), lambda b, i: (b * nb + i, 0)),
            pl.BlockSpec((Q_BLOCK, D_ATT), lambda b, i: (b * nb + i, 0)),
            pl.BlockSpec((seq, KV_DIM), lambda b, i: (b, 0)),
            pl.BlockSpec((seq, KV_DIM), lambda b, i: (b, 0)),
            pl.BlockSpec((seq, LANES), lambda b, i: (b, 0)),
        ],
        out_specs=pl.BlockSpec((Q_BLOCK, D_ATT), lambda b, i: (b * nb + i, 0)),
        out_shape=jax.ShapeDtypeStruct((t, D_ATT), BF16),
        scratch_shapes=[
            pltpu.VMEM((2 * N_KV_HEADS, seq, LANES), BF16),
            pltpu.VMEM((2 * N_KV_HEADS, n_t, LANES, KT), BF16),
            pltpu.VMEM((2, seq, LANES), BF16),
            pltpu.VMEM((N_ATT_HEADS, 2, LANES, LANES), F32),
            pltpu.VMEM((LANES, D_ATT), BF16),
            pltpu.VMEM((LANES, N_IDX_HEADS * IDX_DIM), BF16),
            pltpu.VMEM((n_t, KT, LANES), I32),
            pltpu.VMEM((n_t, KT, LANES), F32),
            pltpu.VMEM((N_ATT_HEADS // N_KV_HEADS, n_t, KT, LANES), F32),
        ],
        compiler_params=_cparams(("arbitrary", "arbitrary")),
        name="dsa_prompt",
    )(rel_table, q_blk, qi_blk, small, za, k2d, v2d, small)


def _softplus(x):
    return jnp.maximum(x, 0.0) + jnp.log1p(jnp.exp(-jnp.abs(x)))


def _lane_pick(x, idx):
    lane = lax.broadcasted_iota(I32, x.shape, 1)
    return jnp.sum(jnp.where(lane == idx, x, 0.0), axis=1, keepdims=True)


def _gated_norm(o, gn, z):
    y = o * lax.rsqrt(jnp.mean(o * o, axis=-1, keepdims=True) + RMS_EPS)
    return (y * gn) * _silu(z)


def _gdn_prompt_body(alog_ref, dtb_ref, xq_ref, xk_ref, xv_ref, wq_ref, wk_ref, wv_ref, sm_ref, zg_ref, gn_ref,
                     o_ref, sfin_ref, xs_ref, q_s, k_s, v_s, g_s, b_s, *, seq, hp):
    h0 = pl.program_id(1) * hp
    hist = SUBLANES
    slabs = [slice(hh * LANES, (hh + 1) * LANES) for hh in range(hp)]

    def conv_into(x_ref, w_ref, dst, post):
        xs_ref[0:hist, :] = jnp.zeros((hist, hp * LANES), F32)
        xs_ref[hist:hist + seq, :] = x_ref[...]
        base = hist - (CONV_W - 1)
        for hs in slabs:
            acc = xs_ref[base:base + seq, hs] * w_ref[0:1, hs]
            for j in range(1, CONV_W):
                acc = acc + xs_ref[base + j:base + j + seq, hs] * w_ref[j:j + 1, hs]
            dst[:, hs] = post(_silu(acc))

    def l2n(x):
        return x * lax.rsqrt(jnp.sum(x * x, axis=-1, keepdims=True) + 1e-6)

    conv_into(xq_ref, wq_ref, q_s, lambda x: l2n(x) * (GDN_HEAD_DIM ** -0.5))
    conv_into(xk_ref, wk_ref, k_s, l2n)
    conv_into(xv_ref, wv_ref, v_s, lambda x: x)

    lane_row = lax.broadcasted_iota(I32, (1, LANES), 1)
    alog_row = jnp.zeros((1, LANES), F32)
    dtb_row = jnp.zeros((1, LANES), F32)
    for hh in range(hp):
        alog_row = jnp.where(lane_row == SM_AG + h0 + hh, alog_ref[h0 + hh], alog_row)
        dtb_row = jnp.where(lane_row == SM_AG + h0 + hh, dtb_ref[h0 + hh], dtb_row)
    sm = sm_ref[...]
    g_all = -jnp.exp(alog_row) * _softplus(sm + dtb_row)
    beta_all = 1.0 / (1.0 + jnp.exp(-sm))
    for hh, hs in enumerate(slabs):
        g_s[:, hs] = jnp.broadcast_to(_lane_pick(g_all, SM_AG + h0 + hh), (seq, LANES))
        b_s[:, hs] = jnp.broadcast_to(_lane_pick(beta_all, SM_BG + h0 + hh), (seq, LANES))

    c = GDN_C
    ri = lax.broadcasted_iota(I32, (c, c), 0)
    ci = lax.broadcasted_iota(I32, (c, c), 1)
    tril = ri >= ci
    strict = ri > ci
    tril_f = tril.astype(F32)
    eye = (ri == ci).astype(F32)
    gn = gn_ref[...]
    off_masks = []
    for lg in range(int(math.log2(c))):
        same_pair = (ri >> (lg + 1)) == (ci >> (lg + 1))
        off_masks.append(same_pair & (((ri >> lg) & 1) == 1) & (((ci >> lg) & 1) == 0))

    tril_b = tril.astype(BF16)

    def cumsum_rows(g):
        hi = g.astype(BF16)
        r1 = g - hi.astype(F32)
        mid = r1.astype(BF16)
        lo = (r1 - mid.astype(F32)).astype(BF16)
        return sum(jnp.dot(tril_b, piece, preferred_element_type=F32) for piece in (hi, mid, lo))

    cpi = GDN_CHUNKS_PER_ITER

    def rows_of(n):
        return pl.ds(pl.multiple_of(n * c, c), c)

    def local_phase(items):
        ids = range(len(items))
        q = [q_s[rows_of(n), hs] for n, hs in items]
        k = [k_s[rows_of(n), hs] for n, hs in items]
        v = [v_s[rows_of(n), hs] for n, hs in items]
        bb = [b_s[rows_of(n), hs] for n, hs in items]
        gcum = [cumsum_rows(g_s[rows_of(n), hs]) for n, hs in items]
        gcum_row = [g.T for g in gcum]
        decay = [jnp.where(tril, jnp.exp(jnp.where(tril, gcum[e] - gcum_row[e], 0.0)), 0.0) for e in ids]
        eg = [jnp.exp(g) for g in gcum]
        kb = [k[e] * bb[e] for e in ids]
        vb = [v[e] * bb[e] for e in ids]
        kk = [_bdot_nt(kb[e], k[e]) for e in ids]
        qk = [_bdot_nt(q[e], k[e]) for e in ids]
        a_mat = [jnp.where(strict, kk[e] * decay[e], 0.0) for e in ids]
        attn = [qk[e] * decay[e] for e in ids]
        x = [eye - jnp.where(off_masks[0], a_mat[e], 0.0) for e in ids]
        for om in off_masks[1:]:
            inner = [_bdot(jnp.where(om, a_mat[e], 0.0), x[e]) for e in ids]
            x = [x[e] - _bdot(x[e], inner[e]) for e in ids]
        u = [_bdot(x[e], vb[e]) for e in ids]
        w = [_bdot(x[e], kb[e] * eg[e]) for e in ids]
        g_last = [g[c - 1:c, :] for g in gcum]
        k_dec_t = [(k[e] * jnp.exp(g_last[e] - gcum[e])).T for e in ids]
        q_dec = [q[e] * eg[e] for e in ids]
        return [dict(u=u[e], w=w[e], attn=attn[e], q_dec=q_dec[e], k_dec_t=k_dec_t[e],
                     s_dec=jnp.exp(g_last[e])) for e in ids]

    def state_phase(n, loc, states):
        heads = range(hp)
        ws = [_bdot(loc[h]["w"], states[h]) for h in heads]
        qs = [_bdot(loc[h]["q_dec"], states[h]) for h in heads]
        v_new = [loc[h]["u"] - ws[h] for h in heads]
        av = [_bdot(loc[h]["attn"], v_new[h]) for h in heads]
        kv = [_bdot(loc[h]["k_dec_t"], v_new[h]) for h in heads]
        for h, hs in enumerate(slabs):
            o_ref[rows_of(n), hs] = _gated_norm(qs[h] + av[h], gn, zg_ref[rows_of(n), hs]).astype(BF16)
        return tuple(states[h] * loc[h]["s_dec"] + kv[h] for h in heads)

    def chunk_group(p, states):
        ns = [p * cpi + r for r in range(cpi)]
        loc = local_phase([(n, hs) for n in ns for hs in slabs])
        for r, n in enumerate(ns):
            states = state_phase(n, loc[r * hp:(r + 1) * hp], states)
        return states

    zero_state = jnp.zeros((GDN_HEAD_DIM, GDN_HEAD_DIM), F32)
    finals = lax.fori_loop(0, seq // (c * cpi), chunk_group, (zero_state,) * hp)
    for hh in range(hp):
        sfin_ref[hh] = finals[hh]


GDN_HEADS_PER_STEP = 4


def _gdn_prompt(a_log, dt_bias, qkv2d, conv_w, small, zg, gdn_norm, batch, seq):
    nh = N_GDN_HEADS
    hp = GDN_HEADS_PER_STEP
    ng = nh // hp
    t = batch * seq
    w = hp * LANES
    once = pl.Buffered(1)
    blk = lambda off: pl.BlockSpec((seq, w), lambda b, j: (b, off + j), pipeline_mode=once)
    wblk = lambda off: pl.BlockSpec((CONV_W, w), lambda b, j: (0, off + j))
    return pl.pallas_call(
        functools.partial(_gdn_prompt_body, seq=seq, hp=hp),
        grid=(batch, ng),
        in_specs=[pl.BlockSpec(memory_space=pltpu.SMEM), pl.BlockSpec(memory_space=pltpu.SMEM),
                  blk(0), blk(ng), blk(2 * ng), wblk(0), wblk(ng), wblk(2 * ng),
                  pl.BlockSpec((seq, LANES), lambda b, j: (b, 0)),
                  pl.BlockSpec((seq, w), lambda b, j: (b, j), pipeline_mode=once),
                  pl.BlockSpec((1, LANES), lambda b, j: (0, 0))],
        out_specs=[pl.BlockSpec((seq, w), lambda b, j: (b, j)),
                   pl.BlockSpec((None, hp, GDN_HEAD_DIM, GDN_HEAD_DIM), lambda b, j: (b, j, 0, 0))],
        out_shape=[jax.ShapeDtypeStruct((t, D_GDN), BF16),
                   jax.ShapeDtypeStruct((batch, nh, GDN_HEAD_DIM, GDN_HEAD_DIM), F32)],
        scratch_shapes=[pltpu.VMEM((seq + 2 * SUBLANES, w), F32)] + [pltpu.VMEM((seq, w), F32)] * 5,
        compiler_params=_cparams(("parallel", "arbitrary")),
        name="gdn_prompt",
    )(a_log, dt_bias, qkv2d, qkv2d, qkv2d, conv_w, conv_w, conv_w, small, zg, gdn_norm.reshape(1, LANES))


def _outproj_body(*refs, gate_att):
    if gate_att:
        x_ref, att_ref, za_ref, gdn_ref, w_ref, g_ref, y_ref = refs
        att = (att_ref[...] * _silu(za_ref[...])).astype(BF16)
    else:
        x_ref, att_ref, gdn_ref, w_ref, g_ref, y_ref = refs
        att = att_ref[...]
    y = x_ref[...] + jnp.dot(att, w_ref[0:D_ATT, :], preferred_element_type=F32) \
        + jnp.dot(gdn_ref[...], w_ref[D_ATT:D_ATT + D_GDN, :], preferred_element_type=F32)
    y = y * lax.rsqrt(jnp.mean(y * y, axis=-1, keepdims=True) + RMS_EPS)
    y_ref[...] = y * g_ref[...]


def _outproj(x2d, att, gdn, w_bf, norm_g, tm, za=None):
    t, d = x2d.shape
    row = lambda n: pl.BlockSpec((tm, n), lambda i: (i, 0))
    full = lambda a, b: pl.BlockSpec((a, b), lambda i: (0, 0))
    ins = [x2d, att] + ([za] if za is not None else []) + [gdn, w_bf, norm_g.reshape(1, d)]
    specs = [row(d), row(D_ATT)] + ([row(D_ATT)] if za is not None else []) + \
            [row(D_GDN), full(D_ATT + D_GDN, d), full(1, d)]
    return pl.pallas_call(
        functools.partial(_outproj_body, gate_att=za is not None),
        grid=(t // tm,),
        in_specs=specs,
        out_specs=row(d),
        out_shape=jax.ShapeDtypeStruct((t, d), F32),
        compiler_params=_cparams(("parallel",)),
        name="outproj_gated" if za is not None else "outproj",
    )(*ins)


IDX_PAGES_PER_STEP = 32
ATT_PAGES_PER_STEP = 16
SAMPLE_Q_ROWS = 16


def _idx_scores_body(pt_ref, qi_ref, wi_ref, *rest):
    page_refs = rest[:IDX_PAGES_PER_STEP]
    o_ref = rest[IDX_PAGES_PER_STEP]
    qi = qi_ref[0]
    wi = wi_ref[0] * (N_IDX_HEADS ** -0.5)
    for j, pr in enumerate(page_refs):
        s = _bdot(qi, pr[...])
        sc = jnp.sum(jnp.maximum(s, 0.0) * wi, axis=0, keepdims=True)
        o_ref[0, j:j + 1, :] = sc + 0.0


def _idx_scores(page_table, qi3, wi3, kidx_pages):
    db, n_pages = page_table.shape
    page = kidx_pages.shape[2]
    pg = IDX_PAGES_PER_STEP

    def page_spec(j):
        return pl.BlockSpec((None, IDX_DIM, page), lambda b, s, pt: (pt[b, s * pg + j], 0, 0))

    grid_spec = pltpu.PrefetchScalarGridSpec(
        num_scalar_prefetch=1,
        grid=(db, n_pages // pg),
        in_specs=[pl.BlockSpec((1, N_IDX_HEADS, IDX_DIM), lambda b, s, pt: (b, 0, 0)),
                  pl.BlockSpec((1, N_IDX_HEADS, 1), lambda b, s, pt: (b, 0, 0))]
                 + [page_spec(j) for j in range(pg)],
        out_specs=pl.BlockSpec((1, pg, page), lambda b, s, pt: (b, s, 0)),
    )
    return pl.pallas_call(
        _idx_scores_body,
        grid_spec=grid_spec,
        out_shape=jax.ShapeDtypeStruct((db, n_pages, page), F32),
        compiler_params=_cparams(("parallel", "arbitrary")),
        name="sample_idx_scores",
    )(page_table, qi3, wi3, *([kidx_pages] * pg))


def _sample_select_body(sc_ref, qi_ref, sm_ref, mask_ref, key_ref, *, n_tiles, topk):
    rows = sc_ref.shape[1]
    lane = lax.broadcasted_iota(I32, (rows, LANES), 1)

    def fill(t, carry):
        key_ref[t] = _sort_key(sc_ref[t])
        return carry

    lax.fori_loop(0, n_tiles, fill, 0)
    sm = sm_ref[...]
    ki_new = sm[:, 0:IDX_DIM].astype(BF16).astype(F32)
    acc = jnp.zeros((rows, 1), F32)
    for hd in range(N_IDX_HEADS):
        qh = qi_ref[:, hd * IDX_DIM:(hd + 1) * IDX_DIM].astype(F32)
        s = jnp.sum(qh * ki_new, axis=1, keepdims=True)
        acc = acc + jnp.maximum(s, 0.0) * (sm[:, SM_WI + hd:SM_WI + hd + 1] * (N_IDX_HEADS ** -0.5))
    sc_new = jnp.broadcast_to(acc + 0.0, (rows, LANES))
    key_ref[n_tiles] = jnp.where(lane == 0, _sort_key(sc_new), INT_MIN)

    def count_where(pred):
        def body(t, a):
            return a + pred(key_ref[t]).astype(F32)
        a = lax.fori_loop(0, n_tiles + 1, body, jnp.zeros((rows, LANES), F32))
        return jnp.sum(a, axis=1, keepdims=True)

    thr = _kth_key_search(lambda cand: count_where(lambda kt: kt >= cand), (rows, 1), float(topk))
    need = float(topk) - count_where(lambda kt: kt > thr)
    cnt_eq = count_where(lambda kt: kt == thr)
    any_tie = jnp.max(jnp.where(cnt_eq > need, 1.0, 0.0)) > 0.0

    @pl.when(jnp.logical_not(any_tie))
    def _():
        def body(t, carry):
            mask_ref[t] = jnp.where(key_ref[t] >= thr, 0.0, NEG_BIG)
            return carry
        lax.fori_loop(0, n_tiles + 1, body, 0)

    @pl.when(any_tie)
    def _():
        sub = lax.broadcasted_iota(I32, (LANES, LANES), 0)
        lane2 = lax.broadcasted_iota(I32, (LANES, LANES), 1)
        upper = (sub <= lane2).astype(BF16)
        ones = jnp.ones((LANES, LANES), BF16)

        def body(t, offset):
            kt = key_ref[t]
            eq = (kt == thr).astype(BF16)
            prefix = jnp.dot(eq, upper, preferred_element_type=F32) + offset
            sel = (kt > thr) | ((eq > 0) & (prefix <= need))
            mask_ref[t] = jnp.where(sel, 0.0, NEG_BIG)
            return offset + jnp.dot(eq, ones, preferred_element_type=F32)

        lax.fori_loop(0, n_tiles + 1, body, jnp.zeros((rows, LANES), F32))


def _sample_select(scores_t, qi2d, small, topk):
    n_tiles, db, page = scores_t.shape
    vm = pl.BlockSpec(memory_space=pltpu.VMEM)
    return pl.pallas_call(
        functools.partial(_sample_select_body, n_tiles=n_tiles, topk=topk),
        in_specs=[vm, vm, vm],
        out_specs=vm,
        out_shape=jax.ShapeDtypeStruct((n_tiles + 1, db, page), F32),
        scratch_shapes=[pltpu.VMEM((n_tiles + 1, db, page), I32)],
        compiler_params=pltpu.CompilerParams(vmem_limit_bytes=VMEM_LIMIT),
        name="sample_select",
    )(scores_t, qi2d, small)


def _sample_attn_body(pt_ref, q_ref, mask_ref, mnew_ref, knew_ref, vnew_ref, relt_ref, *rest, past_len, page):
    pg = ATT_PAGES_PER_STEP
    k_refs = rest[:pg]
    v_refs = rest[pg:2 * pg]
    o_ref, m_s, l_s, acc_s = rest[2 * pg:]
    s = pl.program_id(1)
    n_steps = pl.num_programs(1)
    nh = SAMPLE_Q_ROWS
    far_bucket = _far_bucket_checked(MAX_DISTANCE + 1)

    @pl.when(s == 0)
    def _():
        m_s[...] = jnp.full(m_s.shape, NEG_BIG, F32)
        l_s[...] = jnp.zeros(l_s.shape, F32)
        acc_s[...] = jnp.zeros(acc_s.shape, F32)

    relt = relt_ref[...]
    c_far = relt[:, far_bucket:far_bucket + 1]

    def bias_of(dist):
        bucket = _rel_bucket(dist)
        acc = jnp.zeros((nh, dist.shape[1]), F32)
        for bk in range(N_BUCKETS):
            acc = jnp.where(bucket == bk, relt[:, bk:bk + 1], acc)
        return acc - c_far

    q = q_ref[0].astype(BF16)
    logits = []
    for j in range(pg):
        l = _bdot(q, k_refs[j][...]) + mask_ref[0, j:j + 1, :]
        logits.append(l)
    logits = jnp.concatenate(logits, axis=1)

    kpos = (s * pg) * page + lax.broadcasted_iota(I32, (1, pg * page), 1)
    near = past_len - ((s + 1) * pg * page - 1) <= MAX_DISTANCE
    logits = logits + lax.cond(near, lambda: bias_of(past_len - kpos),
                               lambda: jnp.zeros((nh, pg * page), F32))

    m_old = m_s[...]
    m_new = jnp.maximum(m_old, jnp.max(logits, axis=1, keepdims=True))
    alpha = jnp.exp(m_old - m_new)
    p = jnp.exp(logits - m_new)
    l_new = l_s[...] * alpha + jnp.sum(p, axis=1, keepdims=True)
    acc = acc_s[...] * alpha
    for j in range(pg):
        acc = acc + _bdot_nt(p[:, j * page:(j + 1) * page], v_refs[j][...])
    m_s[...] = m_new
    l_s[...] = l_new
    acc_s[...] = acc

    @pl.when(s == n_steps - 1)
    def _():
        kn = knew_ref[0].astype(BF16).astype(F32)
        vn = vnew_ref[0].astype(BF16).astype(F32)
        ln = jnp.sum(q.astype(F32) * kn, axis=1, keepdims=True) + bias_of(jnp.zeros((1, 1), I32)) \
            + mnew_ref[0][:, 0:1]
        m_fin = jnp.maximum(m_new, ln)
        a2 = jnp.exp(m_new - m_fin)
        pn = jnp.exp(ln - m_fin)
        l_fin = l_new * a2 + pn
        res = (acc * a2 + pn.astype(BF16).astype(F32) * vn) / l_fin
        row = lax.broadcasted_iota(I32, res.shape, 0)
        hpg = N_ATT_HEADS // N_KV_HEADS
        o_ref[0] = jnp.where((row >= hpg) & (row < 2 * hpg), pltpu.roll(res, ATT_HEAD_DIM, 1), res)


def _sample_attn(page_table, q_lh, mask_pages, mask_new, k_new, v_new, rel_t, k_pages, v_pages, past_len):
    db, n_pages = page_table.shape
    page = k_pages.shape[2]
    pg = ATT_PAGES_PER_STEP

    def page_spec(j):
        return pl.BlockSpec((None, KV_DIM, page), lambda b, s, pt: (pt[b, s * pg + j], 0, 0))

    row3 = lambda n: pl.BlockSpec((1, 1, n), lambda b, s, pt: (b, 0, 0))
    grid_spec = pltpu.PrefetchScalarGridSpec(
        num_scalar_prefetch=1,
        grid=(db, n_pages // pg),
        in_specs=[pl.BlockSpec((1, SAMPLE_Q_ROWS, LANES), lambda b, s, pt: (b, 0, 0)),
                  pl.BlockSpec((1, pg, page), lambda b, s, pt: (b, s, 0)),
                  row3(LANES), row3(KV_DIM), row3(KV_DIM),
                  pl.BlockSpec((SAMPLE_Q_ROWS, N_BUCKETS), lambda b, s, pt: (0, 0))]
                 + [page_spec(j) for j in range(pg)] * 2,
        out_specs=pl.BlockSpec((1, SAMPLE_Q_ROWS, LANES), lambda b, s, pt: (b, 0, 0)),
        scratch_shapes=[pltpu.VMEM((SAMPLE_Q_ROWS, 1), F32), pltpu.VMEM((SAMPLE_Q_ROWS, 1), F32),
                        pltpu.VMEM((SAMPLE_Q_ROWS, LANES), F32)],
    )
    return pl.pallas_call(
        functools.partial(_sample_attn_body, past_len=past_len, page=page),
        grid_spec=grid_spec,
        out_shape=jax.ShapeDtypeStruct((db, SAMPLE_Q_ROWS, LANES), F32),
        compiler_params=_cparams(("parallel", "arbitrary")),
        name="sample_attn",
    )(page_table, q_lh, mask_pages, mask_new, k_new, v_new, rel_t, *([k_pages] * pg), *([v_pages] * pg))


def _gdn_sample_body(alog_ref, dtb_ref, x_ref, cst_ref, w_ref, sm_ref, zg_ref, gn_ref, s0_ref,
                     o_ref, s_ref, cnew_ref):
    x = x_ref[0]
    cst = cst_ref[...]
    w = w_ref[...]
    acc = cst[0:1, :] * w[0:1, :]
    for j in range(1, CONV_W - 1):
        acc = acc + cst[j:j + 1, :] * w[j:j + 1, :]
    acc = acc + x * w[CONV_W - 1:CONV_W, :]
    xc = _silu(acc)
    cnew_ref[0:CONV_W - 2, :] = cst[1:CONV_W - 1, :]
    cnew_ref[CONV_W - 2:CONV_W - 1, :] = x
    sm = sm_ref[0]
    gn = gn_ref[...]
    d = GDN_HEAD_DIM
    for h in range(N_GDN_HEADS):
        q = xc[:, h * d:(h + 1) * d]
        k = xc[:, D_GDN + h * d:D_GDN + (h + 1) * d]
        v = xc[:, 2 * D_GDN + h * d:2 * D_GDN + (h + 1) * d]
        q = q * lax.rsqrt(jnp.sum(q * q, axis=-1, keepdims=True) + 1e-6) * (d ** -0.5)
        k = k * lax.rsqrt(jnp.sum(k * k, axis=-1, keepdims=True) + 1e-6)
        a_neg = -jnp.exp(jnp.zeros((1, 1), F32) + alog_ref[h])
        g = a_neg * _softplus(sm[:, SM_AG + h:SM_AG + h + 1] + dtb_ref[h])
        beta = 1.0 / (1.0 + jnp.exp(-sm[:, SM_BG + h:SM_BG + h + 1]))
        st = s0_ref[h] * jnp.exp(g)
        k_col = jnp.broadcast_to(k, (d, d)).T
        q_col = jnp.broadcast_to(q, (d, d)).T
        kv = jnp.sum(k_col * st, axis=0, keepdims=True)
        delta = (v - kv) * beta
        st = st + k_col * delta
        s_ref[h] = st
        o = jnp.sum(q_col * st, axis=0, keepdims=True)
        o_ref[0, :, h * d:(h + 1) * d] = _gated_norm(o, gn, zg_ref[0][:, h * d:(h + 1) * d]).astype(BF16)


def _gdn_sample(a_log, dt_bias, qkv3, state_conv_l, conv_w, small3, zg3, gdn_norm, state_ssm_l):
    db = qkv3.shape[0]
    d = GDN_HEAD_DIM
    nh = N_GDN_HEADS
    row3 = lambda n: pl.BlockSpec((1, 1, n), lambda b: (b, 0, 0))
    return pl.pallas_call(
        _gdn_sample_body,
        grid=(db,),
        in_specs=[pl.BlockSpec(memory_space=pltpu.SMEM), pl.BlockSpec(memory_space=pltpu.SMEM),
                  row3(3 * D_GDN),
                  pl.BlockSpec((None, CONV_W - 1, 3 * D_GDN), lambda b: (b, 0, 0)),
                  pl.BlockSpec((CONV_W, 3 * D_GDN), lambda b: (0, 0)),
                  row3(LANES), row3(D_GDN),
                  pl.BlockSpec((1, LANES), lambda b: (0, 0)),
                  pl.BlockSpec((None, nh, d, d), lambda b: (b, 0, 0, 0))],
        out_specs=[row3(D_GDN),
                   pl.BlockSpec((None, nh, d, d), lambda b: (b, 0, 0, 0)),
                   pl.BlockSpec((None, CONV_W - 1, 3 * D_GDN), lambda b: (b, 0, 0))],
        out_shape=[jax.ShapeDtypeStruct((db, 1, D_GDN), BF16),
                   jax.ShapeDtypeStruct((db, nh, d, d), F32),
                   jax.ShapeDtypeStruct((db, CONV_W - 1, 3 * D_GDN), F32)],
        compiler_params=_cparams(("parallel",)),
        name="gdn_sample",
    )(a_log, dt_bias, qkv3, state_conv_l, conv_w, small3, zg3, gdn_norm.reshape(1, LANES), state_ssm_l)


def kernel(x_prompt, x_sample, cache_k, cache_v, cache_kidx, state_ssm, state_conv, page_table, norm_in, w_in,
           conv_w, a_log, dt_bias, gdn_norm, w_out, rel_table, norm_final):
    depth = w_in.shape[0]
    assert depth == 1, "single-layer model"
    batch, seq, d_model = x_prompt.shape
    db, dec_seq, _ = x_sample.shape
    assert dec_seq == 1 and seq % (2 * LANES) == 0 and seq % GDN_C == 0
    n_pool, page = cache_k.shape[1], cache_k.shape[2]
    n_pages = page_table.shape[1]
    past_len = n_pages * page
    assert page == LANES and n_pages % IDX_PAGES_PER_STEP == 0 and n_pages % ATT_PAGES_PER_STEP == 0

    lyr = 0
    w_pad = _prep_w_in(w_in[lyr])
    w_out_bf = w_out[lyr].astype(BF16)

    xp = x_prompt.reshape(batch * seq, d_model)
    q_blk, k2d, v2d, za, qi_blk, small, qkv2d, zg, k_t, v_t, ki_t = _inproj(
        xp, norm_in[lyr], w_pad, tm=2 * Q_BLOCK, blocked=True, seq=seq)
    att_g = _dsa_prompt_t(rel_table, q_blk, qi_blk, small, za, k2d, v2d, batch, seq)
    gdn_g, s_fin = _gdn_prompt(a_log[lyr], dt_bias[lyr], qkv2d, conv_w[lyr], small, zg, gdn_norm[lyr], batch, seq)
    y_prompt = _outproj(xp, att_g, gdn_g, w_out_bf, norm_final, tm=2 * Q_BLOCK).reshape(batch, seq, d_model)
    k_prompt = jnp.transpose(k_t.reshape(batch, N_KV_HEADS, ATT_HEAD_DIM, seq), (0, 3, 1, 2))[None]
    v_prompt = jnp.transpose(v_t.reshape(batch, N_KV_HEADS, ATT_HEAD_DIM, seq), (0, 3, 1, 2))[None]
    kidx_prompt = jnp.transpose(ki_t, (0, 2, 1))[None]
    ssm_prompt = s_fin[None]
    conv_prompt = qkv2d.reshape(batch, seq, 3 * D_GDN)[:, seq - (CONV_W - 1):][None]

    xs = x_sample.reshape(db, d_model)
    q_s, k_s, v_s, za_s, qi_s, small_s, qkv_s, zg_s = _inproj(xs, norm_in[lyr], w_pad, tm=db, blocked=False)
    kidx_t = jnp.transpose(cache_kidx[lyr], (0, 2, 1))
    k_pages_t = jnp.transpose(cache_k[lyr], (0, 2, 3, 1)).reshape(n_pool, KV_DIM, page)
    v_pages_t = jnp.transpose(cache_v[lyr], (0, 2, 3, 1)).reshape(n_pool, KV_DIM, page)
    scores = _idx_scores(page_table, qi_s.reshape(db, N_IDX_HEADS, IDX_DIM),
                         small_s[:, SM_WI:SM_WI + N_IDX_HEADS].reshape(db, N_IDX_HEADS, 1),
                         kidx_t)
    topk = min(TOPK_MAX, (past_len + dec_seq) // 4)
    mask_t = _sample_select(jnp.transpose(scores, (1, 0, 2)), qi_s, small_s, topk)
    mask_pages = jnp.transpose(mask_t[:n_pages], (1, 0, 2))
    mask_new = mask_t[n_pages].reshape(db, 1, page)
    hpg = N_ATT_HEADS // N_KV_HEADS
    q8 = q_s.reshape(db, N_ATT_HEADS, ATT_HEAD_DIM).astype(F32)
    zq = jnp.zeros((db, hpg, ATT_HEAD_DIM), F32)
    q_lh = jnp.concatenate([jnp.concatenate([q8[:, :hpg], zq], axis=2),
                            jnp.concatenate([zq, q8[:, hpg:]], axis=2),
                            jnp.zeros((db, SAMPLE_Q_ROWS - N_ATT_HEADS, LANES), F32)], axis=1)
    rel_t = jnp.concatenate([rel_table.T, jnp.zeros((SAMPLE_Q_ROWS - N_ATT_HEADS, N_BUCKETS), F32)], axis=0)
    att_raw = _sample_attn(page_table, q_lh, mask_pages, mask_new, k_s.reshape(db, 1, KV_DIM),
                           v_s.reshape(db, 1, KV_DIM), rel_t,
                           k_pages_t, v_pages_t, past_len)
    att_s = att_raw[:, :N_ATT_HEADS, :ATT_HEAD_DIM].reshape(db, D_ATT)
    gdn_s, s_new, conv_new = _gdn_sample(a_log[lyr], dt_bias[lyr], qkv_s.reshape(db, 1, 3 * D_GDN), state_conv[lyr],
                                         conv_w[lyr], small_s.reshape(db, 1, LANES), zg_s.reshape(db, 1, D_GDN),
                                         gdn_norm[lyr], state_ssm[lyr])
    y_sample = _outproj(xs, att_s, gdn_s.reshape(db, D_GDN), w_out_bf, norm_final, tm=db,
                        za=za_s).reshape(db, 1, d_model)
    k_sample = k_s.reshape(1, db, 1, N_KV_HEADS, ATT_HEAD_DIM)
    v_sample = v_s.reshape(1, db, 1, N_KV_HEADS, ATT_HEAD_DIM)
    kidx_sample = small_s[:, :IDX_DIM].reshape(1, db, 1, IDX_DIM)

    return (y_prompt, y_sample, k_prompt, v_prompt, kidx_prompt, ssm_prompt, conv_prompt,
            k_sample, v_sample, kidx_sample, s_new[None], conv_new[None])
```

```python
import functools
import math

import numpy as np
import jax
import jax.numpy as jnp
from jax import lax
from jax.experimental import pallas as pl
from jax.experimental.pallas import tpu as pltpu

F32 = jnp.float32
BF16 = jnp.bfloat16
I32 = jnp.int32

N_ATT_HEADS = 8
ATT_HEAD_DIM = 64
N_KV_HEADS = 2
D_ATT = N_ATT_HEADS * ATT_HEAD_DIM
KV_DIM = N_KV_HEADS * ATT_HEAD_DIM
N_IDX_HEADS = 16
IDX_DIM = 64
TOPK_MAX = 256
N_GDN_HEADS = 4
GDN_HEAD_DIM = 128
D_GDN = N_GDN_HEADS * GDN_HEAD_DIM
CONV_W = 4
N_BUCKETS = 32
MAX_DISTANCE = 128
Q_BLOCK = 128
RMS_EPS = 1e-6
NEG_BIG = -1e30
PROJ_SIZES = (D_ATT, KV_DIM, KV_DIM, D_ATT, N_IDX_HEADS * IDX_DIM, IDX_DIM, N_IDX_HEADS,
              3 * D_GDN, D_GDN, N_GDN_HEADS, N_GDN_HEADS)

LANES = 128
SUBLANES = 8
VMEM_LIMIT = 56 * 1024 * 1024

OFF_Q = 0
OFF_K = OFF_Q + D_ATT
OFF_V = OFF_K + KV_DIM
OFF_ZA = OFF_V + KV_DIM
OFF_QI = OFF_ZA + D_ATT
OFF_SM = OFF_QI + N_IDX_HEADS * IDX_DIM
OFF_QKV = OFF_SM + LANES
OFF_ZG = OFF_QKV + 3 * D_GDN
D_PROJ_PAD = OFF_ZG + D_GDN
SM_WI = IDX_DIM
SM_AG = SM_WI + N_IDX_HEADS
SM_BG = SM_AG + N_GDN_HEADS

PROJ_ROWS = 512
GDN_C = 128
GDN_CHUNKS_PER_ITER = 4
KEY_NEG_BIG = int(np.array(NEG_BIG, np.float32).view(np.int32)) ^ 0x7FFFFFFF
INT_MIN = -2 ** 31


def _cparams(sem):
    return pltpu.CompilerParams(dimension_semantics=sem, vmem_limit_bytes=VMEM_LIMIT)


def _silu(x):
    return x * (1.0 / (1.0 + jnp.exp(-x)))


def _bdot(a, b):
    return jnp.dot(a.astype(BF16), b.astype(BF16), preferred_element_type=F32)


def _bdot_nt(a, b):
    return lax.dot_general(a.astype(BF16), b.astype(BF16), (((1,), (1,)), ((), ())),
                           preferred_element_type=F32)


def _sort_key(x):
    i = pltpu.bitcast(x, I32)
    return jnp.where(i < 0, i ^ 0x7FFFFFFF, i)


def _inproj_body(x_ref, g_ref, w_ref, q_ref, k_ref, v_ref, za_ref, qi_ref, sm_ref, qkv_ref, zg_ref, *t_refs,
                 blocked):
    x = x_ref[...]
    ms = jnp.mean(x * x, axis=-1, keepdims=True)
    h = ((x * lax.rsqrt(ms + RMS_EPS)) * g_ref[...]).astype(BF16)

    def mm(a, b):
        return jnp.dot(h, w_ref[:, a:b], preferred_element_type=F32)

    q = mm(OFF_Q, OFF_K) * (ATT_HEAD_DIM ** -0.5)
    qi = mm(OFF_QI, OFF_SM) * (IDX_DIM ** -0.5)
    if blocked:
        for r in range(x.shape[0] // Q_BLOCK):
            rs = slice(r * Q_BLOCK, (r + 1) * Q_BLOCK)
            for j in range(D_ATT // LANES):
                cs = slice(j * LANES, (j + 1) * LANES)
                q_ref[r, :, cs] = q[rs, cs].T.astype(BF16)
            for j in range(N_IDX_HEADS * IDX_DIM // LANES):
                cs = slice(j * LANES, (j + 1) * LANES)
                qi_ref[r, :, cs] = qi[rs, cs].T.astype(BF16)
    else:
        q_ref[...] = q.astype(BF16)
        qi_ref[...] = qi.astype(BF16)
    k = mm(OFF_K, OFF_V)
    v = mm(OFF_V, OFF_ZA)
    sm = mm(OFF_SM, OFF_QKV)
    k_ref[...] = k
    v_ref[...] = v
    sm_ref[...] = sm
    za_ref[...] = mm(OFF_ZA, OFF_QI)
    qkv_ref[...] = mm(OFF_QKV, OFF_ZG)
    zg_ref[...] = mm(OFF_ZG, D_PROJ_PAD)
    if t_refs:
        kt_ref, vt_ref, kit_ref = t_refs
        kt_ref[...] = k.T
        vt_ref[...] = v.T
        kit_ref[...] = sm.T[0:IDX_DIM, :]


def _prep_w_in(w):
    splits = np.cumsum(PROJ_SIZES)[:-1].tolist()
    q, k, v, z_a, qi, ki, wi, qkv, z_g, a_g, b_g = jnp.split(w, splits, axis=1)
    pad = jnp.zeros((w.shape[0], LANES - (SM_BG + N_GDN_HEADS)), w.dtype)
    small = jnp.concatenate([ki, wi, a_g, b_g, pad], axis=1)
    return jnp.concatenate([q, k, v, z_a, qi, small, qkv, z_g], axis=1).astype(BF16)


def _inproj(x2d, norm_g, w_pad, tm, blocked, seq=None):
    t, d = x2d.shape
    nblk = t // tm
    t_shapes, t_specs = [], []
    if blocked:
        spb = seq // tm
        for n in (KV_DIM, KV_DIM, IDX_DIM):
            t_shapes.append(jax.ShapeDtypeStruct((t // seq, n, seq), F32))
            t_specs.append(pl.BlockSpec((None, n, tm), lambda i: (i // spb, 0, i % spb)))
        rb = tm // Q_BLOCK
        q_shape = jax.ShapeDtypeStruct((t // Q_BLOCK, LANES, D_ATT), BF16)
        qi_shape = jax.ShapeDtypeStruct((t // Q_BLOCK, LANES, N_IDX_HEADS * IDX_DIM), BF16)
        q_spec = pl.BlockSpec((rb, LANES, D_ATT), lambda i: (i, 0, 0))
        qi_spec = pl.BlockSpec((rb, LANES, N_IDX_HEADS * IDX_DIM), lambda i: (i, 0, 0))
    else:
        q_shape = jax.ShapeDtypeStruct((t, D_ATT), BF16)
        qi_shape = jax.ShapeDtypeStruct((t, N_IDX_HEADS * IDX_DIM), BF16)
        q_spec = pl.BlockSpec((tm, D_ATT), lambda i: (i, 0))
        qi_spec = pl.BlockSpec((tm, N_IDX_HEADS * IDX_DIM), lambda i: (i, 0))

    def row(n):
        return pl.BlockSpec((tm, n), lambda i: (i, 0))

    return pl.pallas_call(
        functools.partial(_inproj_body, blocked=blocked),
        grid=(nblk,),
        in_specs=[row(d), pl.BlockSpec((1, d), lambda i: (0, 0)),
                  pl.BlockSpec((d, D_PROJ_PAD), lambda i: (0, 0))],
        out_specs=[q_spec, row(KV_DIM), row(KV_DIM), row(D_ATT), qi_spec, row(LANES), row(3 * D_GDN), row(D_GDN)]
                  + t_specs,
        out_shape=[q_shape, jax.ShapeDtypeStruct((t, KV_DIM), F32), jax.ShapeDtypeStruct((t, KV_DIM), F32),
                   jax.ShapeDtypeStruct((t, D_ATT), F32), qi_shape, jax.ShapeDtypeStruct((t, LANES), F32),
                   jax.ShapeDtypeStruct((t, 3 * D_GDN), F32), jax.ShapeDtypeStruct((t, D_GDN), F32)] + t_shapes,
        compiler_params=_cparams(("parallel",)),
        name="inproj_blocked" if blocked else "inproj_rows",
    )(x2d, norm_g.reshape(1, d), w_pad)


def _rel_bucket(dist):
    n = jnp.maximum(dist, 0)
    max_exact = N_BUCKETS // 2
    nf = jnp.maximum(n, 1).astype(F32)
    large = max_exact + (jnp.log(nf / max_exact) / math.log(MAX_DISTANCE / max_exact)
                         * (N_BUCKETS - max_exact)).astype(I32)
    large = jnp.minimum(large, N_BUCKETS - 1)
    return jnp.where(n < max_exact, n, large)


def _far_bucket_checked(first_far):
    d = np.arange(first_far, 1 << 16, dtype=np.float32)
    b = 16 + (np.log(d / 16) / math.log(MAX_DISTANCE / 16) * 16).astype(np.int32)
    assert int(b.min()) >= N_BUCKETS - 1
    return N_BUCKETS - 1


def _kth_key_search(count_ge, shape, k, total):
    def body(step, carry):
        t, cnt_t = carry
        cand = t + jnp.left_shift(jnp.int32(1), 31 - step)
        cnt = count_ge(cand)
        accept = cnt >= k
        return jnp.where(accept, cand, t), jnp.where(accept, cnt, cnt_t)

    return lax.fori_loop(0, 32, body, (jnp.full(shape, INT_MIN, I32), jnp.full(shape, total, F32)))


KT = 2 * LANES


def _fold8(x, op):
    binop = {jnp.sum: jnp.add, jnp.max: jnp.maximum}[op]
    r = x.reshape(x.shape[0] // SUBLANES, SUBLANES, x.shape[1])
    while r.shape[0] > 1:
        half = r.shape[0] // 2
        r = binop(r[:half], r[half:])
    return r[0]


def _dsa_prompt_t_body(rel_ref, q_ref, qi_ref, smq_ref, za_ref, k_ref, v_ref, sms_ref, o_ref,
                       kk_ref, vvt_ref, ki_ref, bias_ref, key_ref, mask_ref, lg_ref,
                       *, seq, topk):
    b = pl.program_id(0)
    i = pl.program_id(1)
    n_t = seq // KT
    hpg = N_ATT_HEADS // N_KV_HEADS
    far_bucket = _far_bucket_checked(MAX_DISTANCE + 1)
    sub1 = lax.broadcasted_iota(I32, (LANES, LANES), 0)
    lane1 = lax.broadcasted_iota(I32, (LANES, LANES), 1)

    @pl.when((b == 0) & (i == 0))
    def _():
        for dt in range(2):
            bucket = _rel_bucket(dt * LANES + lane1 - sub1)
            for h in range(N_ATT_HEADS):
                acc = jnp.zeros((LANES, LANES), F32)
                for bk in range(N_BUCKETS):
                    acc = jnp.where(bucket == bk, rel_ref[bk, h], acc)
                bias_ref[h, dt] = acc - rel_ref[far_bucket, h]

    @pl.when(i == 0)
    def _():
        lo = lax.broadcasted_iota(I32, (seq, LANES), 1) < ATT_HEAD_DIM
        kf = k_ref[...]
        g0 = jnp.where(lo, kf, 0.0)
        g1 = jnp.where(lo, 0.0, kf)
        kk_ref[0] = g0.astype(BF16)
        kk_ref[1] = pltpu.roll(g0, ATT_HEAD_DIM, 1).astype(BF16)
        kk_ref[2] = pltpu.roll(g1, ATT_HEAD_DIM, 1).astype(BF16)
        kk_ref[3] = g1.astype(BF16)
        c0 = jnp.where(lo, sms_ref[...], 0.0)
        ki_ref[0] = c0.astype(BF16)
        ki_ref[1] = pltpu.roll(c0, IDX_DIM, 1).astype(BF16)
        lo_t = lax.broadcasted_iota(I32, (KT, LANES), 1) < ATT_HEAD_DIM
        for t in range(n_t):
            vf = v_ref[t * KT:(t + 1) * KT, :]
            w0 = jnp.where(lo_t, vf, 0.0)
            w1 = jnp.where(lo_t, 0.0, vf)
            vvt_ref[0, t] = w0.T.astype(BF16)
            vvt_ref[1, t] = pltpu.roll(w0, ATT_HEAD_DIM, 1).T.astype(BF16)
            vvt_ref[2, t] = pltpu.roll(w1, ATT_HEAD_DIM, 1).T.astype(BF16)
            vvt_ref[3, t] = w1.T.astype(BF16)

    n_ip = N_IDX_HEADS * IDX_DIM // LANES
    sm_t = smq_ref[...].T
    wrow = [sm_t[SM_WI + h:SM_WI + h + 1, :] * (N_IDX_HEADS ** -0.5) for h in range(N_IDX_HEADS)]

    n_live = i // 2 + 1
    n_dead = ((n_t - n_live) * KT).astype(F32)
    kidx = lax.broadcasted_iota(I32, (KT, LANES), 0)
    qpos = i * Q_BLOCK + lax.broadcasted_iota(I32, (KT, LANES), 1)

    def causal_of(t):
        return (t * KT + kidx) <= qpos

    def tile_rows(t):
        return pl.ds(pl.multiple_of(t * KT, KT), KT)

    def score_tile(t, carry):
        acc = jnp.zeros((KT, LANES), F32)
        for half in range(2):
            s_all = jnp.dot(ki_ref[half, tile_rows(t), :], qi_ref[...], preferred_element_type=F32)
            for j in range(n_ip):
                acc = acc + jnp.maximum(s_all[:, j * LANES:(j + 1) * LANES], 0.0) * wrow[2 * j + half]
        acc = acc + 0.0
        key_ref[t] = _sort_key(jnp.where(causal_of(t), acc, NEG_BIG))
        return carry

    lax.fori_loop(0, n_live, score_tile, 0)

    @pl.when(n_live % 2 == 1)
    def _():
        key_ref[n_live] = jnp.full((KT, LANES), INT_MIN, I32)
        mask_ref[n_live] = jnp.full((KT, LANES), NEG_BIG, F32)

    def pair_loop(body, init):
        return lax.fori_loop(0, (n_live + 1) // 2, lambda p, c: body(2 * p + 1, body(2 * p, c)), init)

    slabs_per_tile = KT // LANES

    def causal_slab(t, r):
        return (t * KT + r * LANES + sub1) <= (i * Q_BLOCK + lane1)

    def count_where(pred):
        def body(t, acc):
            kt = key_ref[t]
            for r in range(slabs_per_tile):
                acc = jnp.where(pred(kt[r * LANES:(r + 1) * LANES], t, r), acc + 1.0, acc)
            return acc
        acc = pair_loop(body, jnp.zeros((LANES, LANES), F32))
        return jnp.sum(_fold8(acc, jnp.sum), axis=0, keepdims=True)

    def count_ge(cand):
        return count_where(lambda kt, t, r: kt >= cand) + jnp.where(cand <= KEY_NEG_BIG, n_dead, 0.0)

    thr, cnt_thr = _kth_key_search(count_ge, (1, LANES), float(topk), float(seq))

    def write_threshold_masks():
        def body(t, carry):
            mask_ref[t] = jnp.where((key_ref[t] >= thr) & causal_of(t), 0.0, NEG_BIG)
            return carry
        lax.fori_loop(0, n_live, body, 0)

    boundary_dup = jnp.max(jnp.where(cnt_thr > float(topk), 1.0, 0.0)) > 0.0

    @pl.when(jnp.logical_not(boundary_dup))
    def _():
        write_threshold_masks()

    @pl.when(boundary_dup)
    def _():
        cnt_gt = count_where(lambda kt, t, r: kt > thr) + jnp.where(thr < KEY_NEG_BIG, n_dead, 0.0)
        need = float(topk) - cnt_gt
        cnt_ceq = count_where(lambda kt, t, r: (kt == thr) & causal_slab(t, r))
        any_tie = jnp.max(jnp.where(cnt_ceq > need, 1.0, 0.0)) > 0.0

        @pl.when(jnp.logical_not(any_tie))
        def _():
            write_threshold_masks()

        @pl.when(any_tie)
        def _():
            rk = lax.broadcasted_iota(I32, (KT, KT), 0)
            ck = lax.broadcasted_iota(I32, (KT, KT), 1)
            lower = (ck <= rk).astype(BF16)

            def body(t, offset):
                kt = key_ref[t]
                cz = causal_of(t)
                eq = ((kt == thr) & cz).astype(F32)
                prefix = jnp.dot(lower, eq.astype(BF16), preferred_element_type=F32) + offset
                sel = ((kt > thr) & cz) | ((eq > 0.0) & (prefix <= need))
                mask_ref[t] = jnp.where(sel, 0.0, NEG_BIG)
                return offset + jnp.sum(_fold8(eq, jnp.sum), axis=0, keepdims=True)

            lax.fori_loop(0, n_live, body, jnp.zeros((1, LANES), F32))


    za = za_ref[...]
    near_lo = jnp.maximum(i - 1, 0) // 2
    neg8 = jnp.full((SUBLANES, LANES), NEG_BIG, F32)
    zero8 = jnp.zeros((SUBLANES, LANES), F32)
    for g in range(N_KV_HEADS):
        qt_g = q_ref[:, g * 2 * LANES:(g + 1) * 2 * LANES]

        def logit_tile(t, mx, g=g, qt_g=qt_g):
            mx = list(mx)
            far = t < near_lo
            for half in range(2):
                l_all = jnp.dot(kk_ref[2 * g + half, tile_rows(t), :], qt_g, preferred_element_type=F32)
                for n in range(2):
                    hl = 2 * n + half
                    l = l_all[:, n * LANES:(n + 1) * LANES] + mask_ref[t]
                    lg_ref[hl, t] = l
                    mx[hl] = jnp.maximum(mx[hl], jnp.where(far, _fold8(l, jnp.max), NEG_BIG))
            return tuple(mx)

        mx = pair_loop(logit_tile, (neg8,) * hpg)
        m_row = []
        for hl in range(hpg):
            h = g * hpg + hl
            r0 = pl.ds(pl.multiple_of((i % 2) * LANES, LANES), LANES)
            lg_ref[hl, i // 2, r0, :] = lg_ref[hl, i // 2, r0, :] + bias_ref[h, 0]
            im1 = jnp.maximum(i - 1, 0)
            r1 = pl.ds(pl.multiple_of((im1 % 2) * LANES, LANES), LANES)
            lg_ref[hl, im1 // 2, r1, :] = lg_ref[hl, im1 // 2, r1, :] + jnp.where(i > 0, bias_ref[h, 1], 0.0)
            m8 = jnp.maximum(mx[hl], jnp.maximum(_fold8(lg_ref[hl, near_lo], jnp.max),
                                                 _fold8(lg_ref[hl, i // 2], jnp.max)))
            m_row.append(jnp.max(m8, axis=0, keepdims=True))

        def pv_tile(t, carry, g=g, m_row=m_row):
            ssum = list(carry[:hpg])
            acc = list(carry[hpg:])
            for n in range(2):
                for half in range(2):
                    hl = 2 * n + half
                    p = jnp.exp(lg_ref[hl, t] - m_row[hl])
                    ssum[hl] = ssum[hl] + _fold8(p, jnp.sum)
                    acc[n] = acc[n] + jnp.dot(vvt_ref[2 * g + half, t], p.astype(BF16),
                                              preferred_element_type=F32)
            return tuple(ssum) + tuple(acc)

        zacc = jnp.zeros((LANES, LANES), F32)
        res = pair_loop(pv_tile, (zero8,) * hpg + (zacc, zacc))
        for n in range(2):
            l_lo = jnp.sum(res[2 * n], axis=0, keepdims=True)
            l_hi = jnp.sum(res[2 * n + 1], axis=0, keepdims=True)
            inv = jnp.where(sub1 < ATT_HEAD_DIM, 1.0 / l_lo, 1.0 / l_hi)
            o_pair = (res[hpg + n] * inv).T
            cs = slice((2 * g + n) * LANES, (2 * g + n + 1) * LANES)
            o_ref[:, cs] = (o_pair * _silu(za[:, cs])).astype(BF16)


def _dsa_prompt_t(rel_table, q_blk, qi_blk, small, za, k2d, v2d, batch, seq):
    assert N_KV_HEADS == 2 and N_ATT_HEADS // N_KV_HEADS == 4 and seq % KT == 0
    nb = seq // Q_BLOCK
    n_t = seq // KT
    topk = min(TOPK_MAX, seq // 4)
    t = batch * seq
    return pl.pallas_call(
        functools.partial(_dsa_prompt_t_body, seq=seq, topk=topk),
        grid=(batch, nb),
        in_specs=[
            pl.BlockSpec(memory_space=pltpu.SMEM),
            pl.BlockSpec((None, LANES, D_ATT), lambda b, i: (b * nb + i, 0, 0)),
            pl.BlockSpec((None, LANES, N_IDX_HEADS * IDX_DIM), lambda b, i: (b * nb + i, 0, 0)),
            pl.BlockSpec((Q_BLOCK, LANES), lambda b, i: (b * nb + i, 0)),
            pl.BlockSpec((Q_BLOCK, D_ATT), lambda b, i: (b * nb + i, 0)),
            pl.BlockSpec((seq, KV_DIM), lambda b, i: (b, 0)),
            pl.BlockSpec((seq, KV_DIM), lambda b, i: (b, 0)),
            pl.BlockSpec((seq, LANES), lambda b, i: (b, 0)),
        ],
        out_specs=pl.BlockSpec((Q_BLOCK, D_ATT), lambda b, i: (b * nb + i, 0)),
        out_shape=jax.ShapeDtypeStruct((t, D_ATT), BF16),
        scratch_shapes=[
            pltpu.VMEM((2 * N_KV_HEADS, seq, LANES), BF16),
            pltpu.VMEM((2 * N_KV_HEADS, n_t, LANES, KT), BF16),
            pltpu.VMEM((2, seq, LANES), BF16),
            pltpu.VMEM((N_ATT_HEADS, 2, LANES, LANES), F32),
            pltpu.VMEM((n_t, KT, LANES), I32),
            pltpu.VMEM((n_t, KT, LANES), F32),
            pltpu.VMEM((N_ATT_HEADS // N_KV_HEADS, n_t, KT, LANES), F32),
        ],
        compiler_params=_cparams(("arbitrary", "arbitrary")),
        name="dsa_prompt",
    )(rel_table, q_blk, qi_blk, small, za, k2d, v2d, small)


def _softplus(x):
    return jnp.maximum(x, 0.0) + jnp.log1p(jnp.exp(-jnp.abs(x)))


def _lane_pick(x, idx):
    lane = lax.broadcasted_iota(I32, x.shape, 1)
    return jnp.sum(jnp.where(lane == idx, x, 0.0), axis=1, keepdims=True)


def _gated_norm(o, gn, z):
    y = o * lax.rsqrt(jnp.mean(o * o, axis=-1, keepdims=True) + RMS_EPS)
    return (y * gn) * _silu(z)


def _gdn_prompt_body(alog_ref, dtb_ref, xq_ref, xk_ref, xv_ref, wq_ref, wk_ref, wv_ref, sm_ref, zg_ref, gn_ref,
                     o_ref, sfin_ref, xs_ref, q_s, k_s, v_s, g_s, b_s, *, seq, hp):
    h0 = pl.program_id(1) * hp
    hist = SUBLANES
    slabs = [slice(hh * LANES, (hh + 1) * LANES) for hh in range(hp)]

    def conv_into(x_ref, w_ref, dst, post):
        xs_ref[0:hist, :] = jnp.zeros((hist, hp * LANES), F32)
        xs_ref[hist:hist + seq, :] = x_ref[...]
        base = hist - (CONV_W - 1)
        for hs in slabs:
            acc = xs_ref[base:base + seq, hs] * w_ref[0:1, hs]
            for j in range(1, CONV_W):
                acc = acc + xs_ref[base + j:base + j + seq, hs] * w_ref[j:j + 1, hs]
            dst[:, hs] = post(_silu(acc))

    def l2n(x):
        return x * lax.rsqrt(jnp.sum(x * x, axis=-1, keepdims=True) + 1e-6)

    conv_into(xq_ref, wq_ref, q_s, lambda x: l2n(x) * (GDN_HEAD_DIM ** -0.5))
    conv_into(xk_ref, wk_ref, k_s, l2n)
    conv_into(xv_ref, wv_ref, v_s, lambda x: x)

    lane_row = lax.broadcasted_iota(I32, (1, LANES), 1)
    alog_row = jnp.zeros((1, LANES), F32)
    dtb_row = jnp.zeros((1, LANES), F32)
    for hh in range(hp):
        alog_row = jnp.where(lane_row == SM_AG + h0 + hh, alog_ref[h0 + hh], alog_row)
        dtb_row = jnp.where(lane_row == SM_AG + h0 + hh, dtb_ref[h0 + hh], dtb_row)
    sm = sm_ref[...]
    g_all = -jnp.exp(alog_row) * _softplus(sm + dtb_row)
    beta_all = 1.0 / (1.0 + jnp.exp(-sm))
    for hh, hs in enumerate(slabs):
        g_s[:, hs] = jnp.broadcast_to(_lane_pick(g_all, SM_AG + h0 + hh), (seq, LANES))
        b_s[:, hs] = jnp.broadcast_to(_lane_pick(beta_all, SM_BG + h0 + hh), (seq, LANES))

    c = GDN_C
    ri = lax.broadcasted_iota(I32, (c, c), 0)
    ci = lax.broadcasted_iota(I32, (c, c), 1)
    tril = ri >= ci
    strict = ri > ci
    tril_f = tril.astype(F32)
    eye = (ri == ci).astype(F32)
    gn = gn_ref[...]
    off_masks = []
    for lg in range(int(math.log2(c))):
        same_pair = (ri >> (lg + 1)) == (ci >> (lg + 1))
        off_masks.append(same_pair & (((ri >> lg) & 1) == 1) & (((ci >> lg) & 1) == 0))

    tril_b = tril.astype(BF16)

    def cumsum_rows(g):
        hi = g.astype(BF16)
        r1 = g - hi.astype(F32)
        mid = r1.astype(BF16)
        lo = (r1 - mid.astype(F32)).astype(BF16)
        return sum(jnp.dot(tril_b, piece, preferred_element_type=F32) for piece in (hi, mid, lo))

    cpi = GDN_CHUNKS_PER_ITER

    def rows_of(n):
        return pl.ds(pl.multiple_of(n * c, c), c)

    def local_phase(items):
        ids = range(len(items))
        q = [q_s[rows_of(n), hs] for n, hs in items]
        k = [k_s[rows_of(n), hs] for n, hs in items]
        v = [v_s[rows_of(n), hs] for n, hs in items]
        bb = [b_s[rows_of(n), hs] for n, hs in items]
        gcum = [cumsum_rows(g_s[rows_of(n), hs]) for n, hs in items]
        gcum_row = [g.T for g in gcum]
        decay = [jnp.where(tril, jnp.exp(jnp.where(tril, gcum[e] - gcum_row[e], 0.0)), 0.0) for e in ids]
        eg = [jnp.exp(g) for g in gcum]
        kb = [k[e] * bb[e] for e in ids]
        vb = [v[e] * bb[e] for e in ids]
        kk = [_bdot_nt(kb[e], k[e]) for e in ids]
        qk = [_bdot_nt(q[e], k[e]) for e in ids]
        a_mat = [jnp.where(strict, kk[e] * decay[e], 0.0) for e in ids]
        attn = [qk[e] * decay[e] for e in ids]
        x = [eye - jnp.where(off_masks[0], a_mat[e], 0.0) for e in ids]
        for om in off_masks[1:]:
            inner = [_bdot(jnp.where(om, a_mat[e], 0.0), x[e]) for e in ids]
            x = [x[e] - _bdot(x[e], inner[e]) for e in ids]
        u = [_bdot(x[e], vb[e]) for e in ids]
        w = [_bdot(x[e], kb[e] * eg[e]) for e in ids]
        g_last = [g[c - 1:c, :] for g in gcum]
        k_dec_t = [(k[e] * jnp.exp(g_last[e] - gcum[e])).T for e in ids]
        q_dec = [q[e] * eg[e] for e in ids]
        return [dict(u=u[e], w=w[e], attn=attn[e], q_dec=q_dec[e], k_dec_t=k_dec_t[e],
                     s_dec=jnp.exp(g_last[e])) for e in ids]

    def state_phase(n, loc, states):
        heads = range(hp)
        ws = [_bdot(loc[h]["w"], states[h]) for h in heads]
        qs = [_bdot(loc[h]["q_dec"], states[h]) for h in heads]
        v_new = [loc[h]["u"] - ws[h] for h in heads]
        av = [_bdot(loc[h]["attn"], v_new[h]) for h in heads]
        kv = [_bdot(loc[h]["k_dec_t"], v_new[h]) for h in heads]
        for h, hs in enumerate(slabs):
            o_ref[rows_of(n), hs] = _gated_norm(qs[h] + av[h], gn, zg_ref[rows_of(n), hs]).astype(BF16)
        return tuple(states[h] * loc[h]["s_dec"] + kv[h] for h in heads)

    def chunk_group(p, states):
        ns = [p * cpi + r for r in range(cpi)]
        loc = local_phase([(n, hs) for n in ns for hs in slabs])
        for r, n in enumerate(ns):
            states = state_phase(n, loc[r * hp:(r + 1) * hp], states)
        return states

    zero_state = jnp.zeros((GDN_HEAD_DIM, GDN_HEAD_DIM), F32)
    finals = lax.fori_loop(0, seq // (c * cpi), chunk_group, (zero_state,) * hp)
    for hh in range(hp):
        sfin_ref[hh] = finals[hh]


GDN_HEADS_PER_STEP = 4


def _gdn_prompt(a_log, dt_bias, qkv2d, conv_w, small, zg, gdn_norm, batch, seq):
    nh = N_GDN_HEADS
    hp = GDN_HEADS_PER_STEP
    ng = nh // hp
    t = batch * seq
    w = hp * LANES
    once = pl.Buffered(1)
    blk = lambda off: pl.BlockSpec((seq, w), lambda b, j: (b, off + j), pipeline_mode=once)
    wblk = lambda off: pl.BlockSpec((CONV_W, w), lambda b, j: (0, off + j))
    return pl.pallas_call(
        functools.partial(_gdn_prompt_body, seq=seq, hp=hp),
        grid=(batch, ng),
        in_specs=[pl.BlockSpec(memory_space=pltpu.SMEM), pl.BlockSpec(memory_space=pltpu.SMEM),
                  blk(0), blk(ng), blk(2 * ng), wblk(0), wblk(ng), wblk(2 * ng),
                  pl.BlockSpec((seq, LANES), lambda b, j: (b, 0)),
                  pl.BlockSpec((seq, w), lambda b, j: (b, j), pipeline_mode=once),
                  pl.BlockSpec((1, LANES), lambda b, j: (0, 0))],
        out_specs=[pl.BlockSpec((seq, w), lambda b, j: (b, j)),
                   pl.BlockSpec((None, hp, GDN_HEAD_DIM, GDN_HEAD_DIM), lambda b, j: (b, j, 0, 0))],
        out_shape=[jax.ShapeDtypeStruct((t, D_GDN), BF16),
                   jax.ShapeDtypeStruct((batch, nh, GDN_HEAD_DIM, GDN_HEAD_DIM), F32)],
        scratch_shapes=[pltpu.VMEM((seq + 2 * SUBLANES, w), F32)] + [pltpu.VMEM((seq, w), F32)] * 5,
        compiler_params=_cparams(("parallel", "arbitrary")),
        name="gdn_prompt",
    )(a_log, dt_bias, qkv2d, qkv2d, qkv2d, conv_w, conv_w, conv_w, small, zg, gdn_norm.reshape(1, LANES))


def _outproj_body(*refs, gate_att):
    if gate_att:
        x_ref, att_ref, za_ref, gdn_ref, w_ref, g_ref, y_ref = refs
        att = (att_ref[...] * _silu(za_ref[...])).astype(BF16)
    else:
        x_ref, att_ref, gdn_ref, w_ref, g_ref, y_ref = refs
        att = att_ref[...]
    y = x_ref[...] + jnp.dot(att, w_ref[0:D_ATT, :], preferred_element_type=F32) \
        + jnp.dot(gdn_ref[...], w_ref[D_ATT:D_ATT + D_GDN, :], preferred_element_type=F32)
    y = y * lax.rsqrt(jnp.mean(y * y, axis=-1, keepdims=True) + RMS_EPS)
    y_ref[...] = y * g_ref[...]


def _outproj(x2d, att, gdn, w_bf, norm_g, tm, za=None):
    t, d = x2d.shape
    row = lambda n: pl.BlockSpec((tm, n), lambda i: (i, 0))
    full = lambda a, b: pl.BlockSpec((a, b), lambda i: (0, 0))
    ins = [x2d, att] + ([za] if za is not None else []) + [gdn, w_bf, norm_g.reshape(1, d)]
    specs = [row(d), row(D_ATT)] + ([row(D_ATT)] if za is not None else []) + \
            [row(D_GDN), full(D_ATT + D_GDN, d), full(1, d)]
    return pl.pallas_call(
        functools.partial(_outproj_body, gate_att=za is not None),
        grid=(t // tm,),
        in_specs=specs,
        out_specs=row(d),
        out_shape=jax.ShapeDtypeStruct((t, d), F32),
        compiler_params=_cparams(("parallel",)),
        name="outproj_gated" if za is not None else "outproj",
    )(*ins)


IDX_PAGES_PER_STEP = 32
ATT_PAGES_PER_STEP = 16
SAMPLE_Q_ROWS = 16


def _idx_scores_body(pt_ref, qi_ref, wi_ref, *rest):
    page_refs = rest[:IDX_PAGES_PER_STEP]
    o_ref = rest[IDX_PAGES_PER_STEP]
    qi = qi_ref[0]
    wi = wi_ref[0] * (N_IDX_HEADS ** -0.5)
    for j, pr in enumerate(page_refs):
        s = _bdot(qi, pr[...])
        sc = jnp.sum(jnp.maximum(s, 0.0) * wi, axis=0, keepdims=True)
        o_ref[0, j:j + 1, :] = sc + 0.0


def _idx_scores(page_table, qi3, wi3, kidx_pages):
    db, n_pages = page_table.shape
    page = kidx_pages.shape[2]
    pg = IDX_PAGES_PER_STEP

    def page_spec(j):
        return pl.BlockSpec((None, IDX_DIM, page), lambda b, s, pt: (pt[b, s * pg + j], 0, 0))

    grid_spec = pltpu.PrefetchScalarGridSpec(
        num_scalar_prefetch=1,
        grid=(db, n_pages // pg),
        in_specs=[pl.BlockSpec((1, N_IDX_HEADS, IDX_DIM), lambda b, s, pt: (b, 0, 0)),
                  pl.BlockSpec((1, N_IDX_HEADS, 1), lambda b, s, pt: (b, 0, 0))]
                 + [page_spec(j) for j in range(pg)],
        out_specs=pl.BlockSpec((1, pg, page), lambda b, s, pt: (b, s, 0)),
    )
    return pl.pallas_call(
        _idx_scores_body,
        grid_spec=grid_spec,
        out_shape=jax.ShapeDtypeStruct((db, n_pages, page), F32),
        compiler_params=_cparams(("parallel", "arbitrary")),
        name="sample_idx_scores",
    )(page_table, qi3, wi3, *([kidx_pages] * pg))


def _sample_select_body(sc_ref, qi_ref, sm_ref, mask_ref, key_ref, *, n_tiles, topk):
    rows = sc_ref.shape[1]
    lane = lax.broadcasted_iota(I32, (rows, LANES), 1)

    def fill(t, carry):
        key_ref[t] = _sort_key(sc_ref[t])
        return carry

    lax.fori_loop(0, n_tiles, fill, 0)
    sm = sm_ref[...]
    ki_new = sm[:, 0:IDX_DIM].astype(BF16).astype(F32)
    acc = jnp.zeros((rows, 1), F32)
    for hd in range(N_IDX_HEADS):
        qh = qi_ref[:, hd * IDX_DIM:(hd + 1) * IDX_DIM].astype(F32)
        s = jnp.sum(qh * ki_new, axis=1, keepdims=True)
        acc = acc + jnp.maximum(s, 0.0) * (sm[:, SM_WI + hd:SM_WI + hd + 1] * (N_IDX_HEADS ** -0.5))
    sc_new = jnp.broadcast_to(acc + 0.0, (rows, LANES))
    key_ref[n_tiles] = jnp.where(lane == 0, _sort_key(sc_new), INT_MIN)

    def count_where(pred):
        def body(t, a):
            return a + pred(key_ref[t]).astype(F32)
        a = lax.fori_loop(0, n_tiles + 1, body, jnp.zeros((rows, LANES), F32))
        return jnp.sum(a, axis=1, keepdims=True)

    thr, _ = _kth_key_search(lambda cand: count_where(lambda kt: kt >= cand), (rows, 1), float(topk),
                             float(n_tiles * LANES + 1))
    need = float(topk) - count_where(lambda kt: kt > thr)
    cnt_eq = count_where(lambda kt: kt == thr)
    any_tie = jnp.max(jnp.where(cnt_eq > need, 1.0, 0.0)) > 0.0

    @pl.when(jnp.logical_not(any_tie))
    def _():
        def body(t, carry):
            mask_ref[t] = jnp.where(key_ref[t] >= thr, 0.0, NEG_BIG)
            return carry
        lax.fori_loop(0, n_tiles + 1, body, 0)

    @pl.when(any_tie)
    def _():
        sub = lax.broadcasted_iota(I32, (LANES, LANES), 0)
        lane2 = lax.broadcasted_iota(I32, (LANES, LANES), 1)
        upper = (sub <= lane2).astype(BF16)
        ones = jnp.ones((LANES, LANES), BF16)

        def body(t, offset):
            kt = key_ref[t]
            eq = (kt == thr).astype(BF16)
            prefix = jnp.dot(eq, upper, preferred_element_type=F32) + offset
            sel = (kt > thr) | ((eq > 0) & (prefix <= need))
            mask_ref[t] = jnp.where(sel, 0.0, NEG_BIG)
            return offset + jnp.dot(eq, ones, preferred_element_type=F32)

        lax.fori_loop(0, n_tiles + 1, body, jnp.zeros((rows, LANES), F32))


def _sample_select(scores_t, qi2d, small, topk):
    n_tiles, db, page = scores_t.shape
    vm = pl.BlockSpec(memory_space=pltpu.VMEM)
    return pl.pallas_call(
        functools.partial(_sample_select_body, n_tiles=n_tiles, topk=topk),
        in_specs=[vm, vm, vm],
        out_specs=vm,
        out_shape=jax.ShapeDtypeStruct((n_tiles + 1, db, page), F32),
        scratch_shapes=[pltpu.VMEM((n_tiles + 1, db, page), I32)],
        compiler_params=pltpu.CompilerParams(vmem_limit_bytes=VMEM_LIMIT),
        name="sample_select",
    )(scores_t, qi2d, small)


def _sample_attn_body(pt_ref, q_ref, mask_ref, mnew_ref, knew_ref, vnew_ref, relt_ref, *rest, past_len, page):
    pg = ATT_PAGES_PER_STEP
    k_refs = rest[:pg]
    v_refs = rest[pg:2 * pg]
    o_ref, m_s, l_s, acc_s = rest[2 * pg:]
    s = pl.program_id(1)
    n_steps = pl.num_programs(1)
    nh = SAMPLE_Q_ROWS
    far_bucket = _far_bucket_checked(MAX_DISTANCE + 1)

    @pl.when(s == 0)
    def _():
        m_s[...] = jnp.full(m_s.shape, NEG_BIG, F32)
        l_s[...] = jnp.zeros(l_s.shape, F32)
        acc_s[...] = jnp.zeros(acc_s.shape, F32)

    relt = relt_ref[...]
    c_far = relt[:, far_bucket:far_bucket + 1]

    def bias_of(dist):
        bucket = _rel_bucket(dist)
        acc = jnp.zeros((nh, dist.shape[1]), F32)
        for bk in range(N_BUCKETS):
            acc = jnp.where(bucket == bk, relt[:, bk:bk + 1], acc)
        return acc - c_far

    q = q_ref[0].astype(BF16)
    logits = []
    for j in range(pg):
        l = _bdot(q, k_refs[j][...]) + mask_ref[0, j:j + 1, :]
        logits.append(l)
    logits = jnp.concatenate(logits, axis=1)

    kpos = (s * pg) * page + lax.broadcasted_iota(I32, (1, pg * page), 1)
    near = past_len - ((s + 1) * pg * page - 1) <= MAX_DISTANCE
    logits = logits + lax.cond(near, lambda: bias_of(past_len - kpos),
                               lambda: jnp.zeros((nh, pg * page), F32))

    m_old = m_s[...]
    m_new = jnp.maximum(m_old, jnp.max(logits, axis=1, keepdims=True))
    alpha = jnp.exp(m_old - m_new)
    p = jnp.exp(logits - m_new)
    l_new = l_s[...] * alpha + jnp.sum(p, axis=1, keepdims=True)
    acc = acc_s[...] * alpha
    for j in range(pg):
        acc = acc + _bdot_nt(p[:, j * page:(j + 1) * page], v_refs[j][...])
    m_s[...] = m_new
    l_s[...] = l_new
    acc_s[...] = acc

    @pl.when(s == n_steps - 1)
    def _():
        kn = knew_ref[0].astype(BF16).astype(F32)
        vn = vnew_ref[0].astype(BF16).astype(F32)
        ln = jnp.sum(q.astype(F32) * kn, axis=1, keepdims=True) + bias_of(jnp.zeros((1, 1), I32)) \
            + mnew_ref[0][:, 0:1]
        m_fin = jnp.maximum(m_new, ln)
        a2 = jnp.exp(m_new - m_fin)
        pn = jnp.exp(ln - m_fin)
        l_fin = l_new * a2 + pn
        res = (acc * a2 + pn.astype(BF16).astype(F32) * vn) / l_fin
        row = lax.broadcasted_iota(I32, res.shape, 0)
        hpg = N_ATT_HEADS // N_KV_HEADS
        o_ref[0] = jnp.where((row >= hpg) & (row < 2 * hpg), pltpu.roll(res, ATT_HEAD_DIM, 1), res)


def _sample_attn(page_table, q_lh, mask_pages, mask_new, k_new, v_new, rel_t, k_pages, v_pages, past_len):
    db, n_pages = page_table.shape
    page = k_pages.shape[2]
    pg = ATT_PAGES_PER_STEP

    def page_spec(j):
        return pl.BlockSpec((None, KV_DIM, page), lambda b, s, pt: (pt[b, s * pg + j], 0, 0))

    row3 = lambda n: pl.BlockSpec((1, 1, n), lambda b, s, pt: (b, 0, 0))
    grid_spec = pltpu.PrefetchScalarGridSpec(
        num_scalar_prefetch=1,
        grid=(db, n_pages // pg),
        in_specs=[pl.BlockSpec((1, SAMPLE_Q_ROWS, LANES), lambda b, s, pt: (b, 0, 0)),
                  pl.BlockSpec((1, pg, page), lambda b, s, pt: (b, s, 0)),
                  row3(LANES), row3(KV_DIM), row3(KV_DIM),
                  pl.BlockSpec((SAMPLE_Q_ROWS, N_BUCKETS), lambda b, s, pt: (0, 0))]
                 + [page_spec(j) for j in range(pg)] * 2,
        out_specs=pl.BlockSpec((1, SAMPLE_Q_ROWS, LANES), lambda b, s, pt: (b, 0, 0)),
        scratch_shapes=[pltpu.VMEM((SAMPLE_Q_ROWS, 1), F32), pltpu.VMEM((SAMPLE_Q_ROWS, 1), F32),
                        pltpu.VMEM((SAMPLE_Q_ROWS, LANES), F32)],
    )
    return pl.pallas_call(
        functools.partial(_sample_attn_body, past_len=past_len, page=page),
        grid_spec=grid_spec,
        out_shape=jax.ShapeDtypeStruct((db, SAMPLE_Q_ROWS, LANES), F32),
        compiler_params=_cparams(("parallel", "arbitrary")),
        name="sample_attn",
    )(page_table, q_lh, mask_pages, mask_new, k_new, v_new, rel_t, *([k_pages] * pg), *([v_pages] * pg))


def _gdn_sample_body(alog_ref, dtb_ref, x_ref, cst_ref, w_ref, sm_ref, zg_ref, gn_ref, s0_ref,
                     o_ref, s_ref, cnew_ref):
    x = x_ref[0]
    cst = cst_ref[...]
    w = w_ref[...]
    acc = cst[0:1, :] * w[0:1, :]
    for j in range(1, CONV_W - 1):
        acc = acc + cst[j:j + 1, :] * w[j:j + 1, :]
    acc = acc + x * w[CONV_W - 1:CONV_W, :]
    xc = _silu(acc)
    cnew_ref[0:CONV_W - 2, :] = cst[1:CONV_W - 1, :]
    cnew_ref[CONV_W - 2:CONV_W - 1, :] = x
    sm = sm_ref[0]
    gn = gn_ref[...]
    d = GDN_HEAD_DIM
    for h in range(N_GDN_HEADS):
        q = xc[:, h * d:(h + 1) * d]
        k = xc[:, D_GDN + h * d:D_GDN + (h + 1) * d]
        v = xc[:, 2 * D_GDN + h * d:2 * D_GDN + (h + 1) * d]
        q = q * lax.rsqrt(jnp.sum(q * q, axis=-1, keepdims=True) + 1e-6) * (d ** -0.5)
        k = k * lax.rsqrt(jnp.sum(k * k, axis=-1, keepdims=True) + 1e-6)
        a_neg = -jnp.exp(jnp.zeros((1, 1), F32) + alog_ref[h])
        g = a_neg * _softplus(sm[:, SM_AG + h:SM_AG + h + 1] + dtb_ref[h])
        beta = 1.0 / (1.0 + jnp.exp(-sm[:, SM_BG + h:SM_BG + h + 1]))
        st = s0_ref[h] * jnp.exp(g)
        k_col = jnp.broadcast_to(k, (d, d)).T
        q_col = jnp.broadcast_to(q, (d, d)).T
        kv = jnp.sum(k_col * st, axis=0, keepdims=True)
        delta = (v - kv) * beta
        st = st + k_col * delta
        s_ref[h] = st
        o = jnp.sum(q_col * st, axis=0, keepdims=True)
        o_ref[0, :, h * d:(h + 1) * d] = _gated_norm(o, gn, zg_ref[0][:, h * d:(h + 1) * d]).astype(BF16)


def _gdn_sample(a_log, dt_bias, qkv3, state_conv_l, conv_w, small3, zg3, gdn_norm, state_ssm_l):
    db = qkv3.shape[0]
    d = GDN_HEAD_DIM
    nh = N_GDN_HEADS
    row3 = lambda n: pl.BlockSpec((1, 1, n), lambda b: (b, 0, 0))
    return pl.pallas_call(
        _gdn_sample_body,
        grid=(db,),
        in_specs=[pl.BlockSpec(memory_space=pltpu.SMEM), pl.BlockSpec(memory_space=pltpu.SMEM),
                  row3(3 * D_GDN),
                  pl.BlockSpec((None, CONV_W - 1, 3 * D_GDN), lambda b: (b, 0, 0)),
                  pl.BlockSpec((CONV_W, 3 * D_GDN), lambda b: (0, 0)),
                  row3(LANES), row3(D_GDN),
                  pl.BlockSpec((1, LANES), lambda b: (0, 0)),
                  pl.BlockSpec((None, nh, d, d), lambda b: (b, 0, 0, 0))],
        out_specs=[row3(D_GDN),
                   pl.BlockSpec((None, nh, d, d), lambda b: (b, 0, 0, 0)),
                   pl.BlockSpec((None, CONV_W - 1, 3 * D_GDN), lambda b: (b, 0, 0))],
        out_shape=[jax.ShapeDtypeStruct((db, 1, D_GDN), BF16),
                   jax.ShapeDtypeStruct((db, nh, d, d), F32),
                   jax.ShapeDtypeStruct((db, CONV_W - 1, 3 * D_GDN), F32)],
        compiler_params=_cparams(("parallel",)),
        name="gdn_sample",
    )(a_log, dt_bias, qkv3, state_conv_l, conv_w, small3, zg3, gdn_norm.reshape(1, LANES), state_ssm_l)


def kernel(x_prompt, x_sample, cache_k, cache_v, cache_kidx, state_ssm, state_conv, page_table, norm_in, w_in,
           conv_w, a_log, dt_bias, gdn_norm, w_out, rel_table, norm_final):
    depth = w_in.shape[0]
    assert depth == 1, "single-layer model"
    batch, seq, d_model = x_prompt.shape
    db, dec_seq, _ = x_sample.shape
    assert dec_seq == 1 and seq % KT == 0 and seq % (GDN_C * GDN_CHUNKS_PER_ITER) == 0 and seq % PROJ_ROWS == 0
    n_pool, page = cache_k.shape[1], cache_k.shape[2]
    n_pages = page_table.shape[1]
    past_len = n_pages * page
    assert page == LANES and n_pages % IDX_PAGES_PER_STEP == 0 and n_pages % ATT_PAGES_PER_STEP == 0

    lyr = 0
    w_pad = _prep_w_in(w_in[lyr])
    w_out_bf = w_out[lyr].astype(BF16)

    xp = x_prompt.reshape(batch * seq, d_model)
    q_blk, k2d, v2d, za, qi_blk, small, qkv2d, zg, k_t, v_t, ki_t = _inproj(
        xp, norm_in[lyr], w_pad, tm=PROJ_ROWS, blocked=True, seq=seq)
    att_g = _dsa_prompt_t(rel_table, q_blk, qi_blk, small, za, k2d, v2d, batch, seq)
    gdn_g, s_fin = _gdn_prompt(a_log[lyr], dt_bias[lyr], qkv2d, conv_w[lyr], small, zg, gdn_norm[lyr], batch, seq)
    y_prompt = _outproj(xp, att_g, gdn_g, w_out_bf, norm_final, tm=PROJ_ROWS).reshape(batch, seq, d_model)
    k_prompt = jnp.transpose(k_t.reshape(batch, N_KV_HEADS, ATT_HEAD_DIM, seq), (0, 3, 1, 2))[None]
    v_prompt = jnp.transpose(v_t.reshape(batch, N_KV_HEADS, ATT_HEAD_DIM, seq), (0, 3, 1, 2))[None]
    kidx_prompt = jnp.transpose(ki_t, (0, 2, 1))[None]
    ssm_prompt = s_fin[None]
    conv_prompt = qkv2d.reshape(batch, seq, 3 * D_GDN)[:, seq - (CONV_W - 1):][None]

    xs = x_sample.reshape(db, d_model)
    q_s, k_s, v_s, za_s, qi_s, small_s, qkv_s, zg_s = _inproj(xs, norm_in[lyr], w_pad, tm=db, blocked=False)
    kidx_t = jnp.transpose(cache_kidx[lyr], (0, 2, 1))
    k_pages_t = jnp.transpose(cache_k[lyr], (0, 2, 3, 1)).reshape(n_pool, KV_DIM, page)
    v_pages_t = jnp.transpose(cache_v[lyr], (0, 2, 3, 1)).reshape(n_pool, KV_DIM, page)
    scores = _idx_scores(page_table, qi_s.reshape(db, N_IDX_HEADS, IDX_DIM),
                         small_s[:, SM_WI:SM_WI + N_IDX_HEADS].reshape(db, N_IDX_HEADS, 1),
                         kidx_t)
    topk = min(TOPK_MAX, (past_len + dec_seq) // 4)
    mask_t = _sample_select(jnp.transpose(scores, (1, 0, 2)), qi_s, small_s, topk)
    mask_pages = jnp.transpose(mask_t[:n_pages], (1, 0, 2))
    mask_new = mask_t[n_pages].reshape(db, 1, page)
    hpg = N_ATT_HEADS // N_KV_HEADS
    q8 = q_s.reshape(db, N_ATT_HEADS, ATT_HEAD_DIM).astype(F32)
    zq = jnp.zeros((db, hpg, ATT_HEAD_DIM), F32)
    q_lh = jnp.concatenate([jnp.concatenate([q8[:, :hpg], zq], axis=2),
                            jnp.concatenate([zq, q8[:, hpg:]], axis=2),
                            jnp.zeros((db, SAMPLE_Q_ROWS - N_ATT_HEADS, LANES), F32)], axis=1)
    rel_t = jnp.concatenate([rel_table.T, jnp.zeros((SAMPLE_Q_ROWS - N_ATT_HEADS, N_BUCKETS), F32)], axis=0)
    att_raw = _sample_attn(page_table, q_lh, mask_pages, mask_new, k_s.reshape(db, 1, KV_DIM),
                           v_s.reshape(db, 1, KV_DIM), rel_t,
                           k_pages_t, v_pages_t, past_len)
    att_s = att_raw[:, :N_ATT_HEADS, :ATT_HEAD_DIM].reshape(db, D_ATT)
    gdn_s, s_new, conv_new = _gdn_sample(a_log[lyr], dt_bias[lyr], qkv_s.reshape(db, 1, 3 * D_GDN), state_conv[lyr],
                                         conv_w[lyr], small_s.reshape(db, 1, LANES), zg_s.reshape(db, 1, D_GDN),
                                         gdn_norm[lyr], state_ssm[lyr])
    y_sample = _outproj(xs, att_s, gdn_s.reshape(db, D_GDN), w_out_bf, norm_final, tm=db,
                        za=za_s).reshape(db, 1, d_model)
    k_sample = k_s.reshape(1, db, 1, N_KV_HEADS, ATT_HEAD_DIM)
    v_sample = v_s.reshape(1, db, 1, N_KV_HEADS, ATT_HEAD_DIM)
    kidx_sample = small_s[:, :IDX_DIM].reshape(1, db, 1, IDX_DIM)

    return (y_prompt, y_sample, k_prompt, v_prompt, kidx_prompt, ssm_prompt, conv_prompt,
            k_sample, v_sample, kidx_sample, s_new[None], conv_new[None])
```

```python
import functools
import math

import numpy as np
import jax
import jax.numpy as jnp
from jax import lax
from jax.experimental import pallas as pl
from jax.experimental.pallas import tpu as pltpu

F32 = jnp.float32
BF16 = jnp.bfloat16
I32 = jnp.int32

N_ATT_HEADS = 8
ATT_HEAD_DIM = 64
N_KV_HEADS = 2
D_ATT = N_ATT_HEADS * ATT_HEAD_DIM
KV_DIM = N_KV_HEADS * ATT_HEAD_DIM
N_IDX_HEADS = 16
IDX_DIM = 64
TOPK_MAX = 256
N_GDN_HEADS = 4
GDN_HEAD_DIM = 128
D_GDN = N_GDN_HEADS * GDN_HEAD_DIM
CONV_W = 4
N_BUCKETS = 32
MAX_DISTANCE = 128
Q_BLOCK = 128
RMS_EPS = 1e-6
NEG_BIG = -1e30
PROJ_SIZES = (D_ATT, KV_DIM, KV_DIM, D_ATT, N_IDX_HEADS * IDX_DIM, IDX_DIM, N_IDX_HEADS,
              3 * D_GDN, D_GDN, N_GDN_HEADS, N_GDN_HEADS)

LANES = 128
SUBLANES = 8
VMEM_LIMIT = 56 * 1024 * 1024

OFF_Q = 0
OFF_K = OFF_Q + D_ATT
OFF_V = OFF_K + KV_DIM
OFF_ZA = OFF_V + KV_DIM
OFF_QI = OFF_ZA + D_ATT
OFF_SM = OFF_QI + N_IDX_HEADS * IDX_DIM
OFF_QKV = OFF_SM + LANES
OFF_ZG = OFF_QKV + 3 * D_GDN
D_PROJ_PAD = OFF_ZG + D_GDN
SM_WI = IDX_DIM
SM_AG = SM_WI + N_IDX_HEADS
SM_BG = SM_AG + N_GDN_HEADS

PROJ_ROWS = 512
GDN_C = 128
GDN_CHUNKS_PER_ITER = 4
KEY_NEG_BIG = int(np.array(NEG_BIG, np.float32).view(np.int32)) ^ 0x7FFFFFFF
INT_MIN = -2 ** 31


def _cparams(sem):
    return pltpu.CompilerParams(dimension_semantics=sem, vmem_limit_bytes=VMEM_LIMIT)


def _silu(x):
    return x * (1.0 / (1.0 + jnp.exp(-x)))


def _bdot(a, b):
    return jnp.dot(a.astype(BF16), b.astype(BF16), preferred_element_type=F32)


def _bdot_nt(a, b):
    return lax.dot_general(a.astype(BF16), b.astype(BF16), (((1,), (1,)), ((), ())),
                           preferred_element_type=F32)


def _sort_key(x):
    i = pltpu.bitcast(x, I32)
    return jnp.where(i < 0, i ^ 0x7FFFFFFF, i)


def _inproj_body(x_ref, g_ref, w_ref, q_ref, k_ref, v_ref, za_ref, qi_ref, sm_ref, qkv_ref, zg_ref, *t_refs,
                 blocked):
    x = x_ref[...]
    ms = jnp.mean(x * x, axis=-1, keepdims=True)
    h = ((x * lax.rsqrt(ms + RMS_EPS)) * g_ref[...]).astype(BF16)

    def mm(a, b):
        return jnp.dot(h, w_ref[:, a:b], preferred_element_type=F32)

    q = mm(OFF_Q, OFF_K) * (ATT_HEAD_DIM ** -0.5)
    qi = mm(OFF_QI, OFF_SM) * (IDX_DIM ** -0.5)
    if blocked:
        for r in range(x.shape[0] // Q_BLOCK):
            rs = slice(r * Q_BLOCK, (r + 1) * Q_BLOCK)
            for j in range(D_ATT // LANES):
                cs = slice(j * LANES, (j + 1) * LANES)
                q_ref[r, :, cs] = q[rs, cs].T.astype(BF16)
            for j in range(N_IDX_HEADS * IDX_DIM // LANES):
                cs = slice(j * LANES, (j + 1) * LANES)
                qi_ref[r, :, cs] = qi[rs, cs].T.astype(BF16)
    else:
        q_ref[...] = q.astype(BF16)
        qi_ref[...] = qi.astype(BF16)
    k = mm(OFF_K, OFF_V)
    v = mm(OFF_V, OFF_ZA)
    sm = mm(OFF_SM, OFF_QKV)
    k_ref[...] = k
    v_ref[...] = v
    sm_ref[...] = sm
    za_ref[...] = mm(OFF_ZA, OFF_QI)
    qkv_ref[...] = mm(OFF_QKV, OFF_ZG)
    zg_ref[...] = mm(OFF_ZG, D_PROJ_PAD)
    if t_refs:
        kt_ref, vt_ref, kit_ref = t_refs
        kt_ref[...] = k.T
        vt_ref[...] = v.T
        kit_ref[...] = sm.T[0:IDX_DIM, :]


def _prep_w_in(w):
    splits = np.cumsum(PROJ_SIZES)[:-1].tolist()
    q, k, v, z_a, qi, ki, wi, qkv, z_g, a_g, b_g = jnp.split(w, splits, axis=1)
    pad = jnp.zeros((w.shape[0], LANES - (SM_BG + N_GDN_HEADS)), w.dtype)
    small = jnp.concatenate([ki, wi, a_g, b_g, pad], axis=1)
    return jnp.concatenate([q, k, v, z_a, qi, small, qkv, z_g], axis=1).astype(BF16)


def _inproj(x2d, norm_g, w_pad, tm, blocked, seq=None):
    t, d = x2d.shape
    nblk = t // tm
    t_shapes, t_specs = [], []
    if blocked:
        spb = seq // tm
        for n in (KV_DIM, KV_DIM, IDX_DIM):
            t_shapes.append(jax.ShapeDtypeStruct((t // seq, n, seq), F32))
            t_specs.append(pl.BlockSpec((None, n, tm), lambda i: (i // spb, 0, i % spb)))
        rb = tm // Q_BLOCK
        q_shape = jax.ShapeDtypeStruct((t // Q_BLOCK, LANES, D_ATT), BF16)
        qi_shape = jax.ShapeDtypeStruct((t // Q_BLOCK, LANES, N_IDX_HEADS * IDX_DIM), BF16)
        q_spec = pl.BlockSpec((rb, LANES, D_ATT), lambda i: (i, 0, 0))
        qi_spec = pl.BlockSpec((rb, LANES, N_IDX_HEADS * IDX_DIM), lambda i: (i, 0, 0))
    else:
        q_shape = jax.ShapeDtypeStruct((t, D_ATT), BF16)
        qi_shape = jax.ShapeDtypeStruct((t, N_IDX_HEADS * IDX_DIM), BF16)
        q_spec = pl.BlockSpec((tm, D_ATT), lambda i: (i, 0))
        qi_spec = pl.BlockSpec((tm, N_IDX_HEADS * IDX_DIM), lambda i: (i, 0))

    def row(n):
        return pl.BlockSpec((tm, n), lambda i: (i, 0))

    return pl.pallas_call(
        functools.partial(_inproj_body, blocked=blocked),
        grid=(nblk,),
        in_specs=[row(d), pl.BlockSpec((1, d), lambda i: (0, 0)),
                  pl.BlockSpec((d, D_PROJ_PAD), lambda i: (0, 0))],
        out_specs=[q_spec, row(KV_DIM), row(KV_DIM), row(D_ATT), qi_spec, row(LANES), row(3 * D_GDN), row(D_GDN)]
                  + t_specs,
        out_shape=[q_shape, jax.ShapeDtypeStruct((t, KV_DIM), F32), jax.ShapeDtypeStruct((t, KV_DIM), F32),
                   jax.ShapeDtypeStruct((t, D_ATT), F32), qi_shape, jax.ShapeDtypeStruct((t, LANES), F32),
                   jax.ShapeDtypeStruct((t, 3 * D_GDN), F32), jax.ShapeDtypeStruct((t, D_GDN), F32)] + t_shapes,
        compiler_params=_cparams(("parallel",)),
        name="inproj_blocked" if blocked else "inproj_rows",
    )(x2d, norm_g.reshape(1, d), w_pad)


def _rel_bucket(dist):
    n = jnp.maximum(dist, 0)
    max_exact = N_BUCKETS // 2
    nf = jnp.maximum(n, 1).astype(F32)
    large = max_exact + (jnp.log(nf / max_exact) / math.log(MAX_DISTANCE / max_exact)
                         * (N_BUCKETS - max_exact)).astype(I32)
    large = jnp.minimum(large, N_BUCKETS - 1)
    return jnp.where(n < max_exact, n, large)


def _far_bucket_checked(first_far):
    d = np.arange(first_far, 1 << 16, dtype=np.float32)
    b = 16 + (np.log(d / 16) / math.log(MAX_DISTANCE / 16) * 16).astype(np.int32)
    assert int(b.min()) >= N_BUCKETS - 1
    return N_BUCKETS - 1


def _kth_key_search(count_ge, shape, k, total):
    def body(step, carry):
        t, cnt_t = carry
        cand = t + jnp.left_shift(jnp.int32(1), 31 - step)
        cnt = count_ge(cand)
        accept = cnt >= k
        return jnp.where(accept, cand, t), jnp.where(accept, cnt, cnt_t)

    return lax.fori_loop(0, 32, body, (jnp.full(shape, INT_MIN, I32), jnp.full(shape, total, F32)))


KT = 2 * LANES


def _fold8(x, op):
    binop = {jnp.sum: jnp.add, jnp.max: jnp.maximum}[op]
    r = x.reshape(x.shape[0] // SUBLANES, SUBLANES, x.shape[1])
    while r.shape[0] > 1:
        half = r.shape[0] // 2
        r = binop(r[:half], r[half:])
    return r[0]


def _dsa_prompt_t_body(rel_ref, q_ref, qi_ref, smq_ref, za_ref, k_ref, v_ref, sms_ref, o_ref,
                       kk_ref, vvt_ref, ki_ref, bias_ref, key_ref, mask_ref, lg_ref,
                       *, seq, topk):
    b = pl.program_id(0)
    i = pl.program_id(1)
    n_t = seq // KT
    hpg = N_ATT_HEADS // N_KV_HEADS
    far_bucket = _far_bucket_checked(MAX_DISTANCE + 1)
    sub1 = lax.broadcasted_iota(I32, (LANES, LANES), 0)
    lane1 = lax.broadcasted_iota(I32, (LANES, LANES), 1)

    @pl.when((b == 0) & (i == 0))
    def _():
        for dt in range(2):
            bucket = _rel_bucket(dt * LANES + lane1 - sub1)
            for h in range(N_ATT_HEADS):
                acc = jnp.zeros((LANES, LANES), F32)
                for bk in range(N_BUCKETS):
                    acc = jnp.where(bucket == bk, rel_ref[bk, h], acc)
                bias_ref[h, dt] = acc - rel_ref[far_bucket, h]

    @pl.when(i == 0)
    def _():
        lo = lax.broadcasted_iota(I32, (seq, LANES), 1) < ATT_HEAD_DIM
        kf = k_ref[...]
        g0 = jnp.where(lo, kf, 0.0)
        g1 = jnp.where(lo, 0.0, kf)
        kk_ref[0] = g0.astype(BF16)
        kk_ref[1] = pltpu.roll(g0, ATT_HEAD_DIM, 1).astype(BF16)
        kk_ref[2] = pltpu.roll(g1, ATT_HEAD_DIM, 1).astype(BF16)
        kk_ref[3] = g1.astype(BF16)
        c0 = jnp.where(lo, sms_ref[...], 0.0)
        ki_ref[0] = c0.astype(BF16)
        ki_ref[1] = pltpu.roll(c0, IDX_DIM, 1).astype(BF16)
        lo_t = lax.broadcasted_iota(I32, (KT, LANES), 1) < ATT_HEAD_DIM
        for t in range(n_t):
            vf = v_ref[t * KT:(t + 1) * KT, :]
            w0 = jnp.where(lo_t, vf, 0.0)
            w1 = jnp.where(lo_t, 0.0, vf)
            vvt_ref[0, t] = w0.T.astype(BF16)
            vvt_ref[1, t] = pltpu.roll(w0, ATT_HEAD_DIM, 1).T.astype(BF16)
            vvt_ref[2, t] = pltpu.roll(w1, ATT_HEAD_DIM, 1).T.astype(BF16)
            vvt_ref[3, t] = w1.T.astype(BF16)

    n_ip = N_IDX_HEADS * IDX_DIM // LANES
    sm_t = smq_ref[...].T
    wrow = [sm_t[SM_WI + h:SM_WI + h + 1, :] * (N_IDX_HEADS ** -0.5) for h in range(N_IDX_HEADS)]

    n_live = i // 2 + 1
    n_dead = ((n_t - n_live) * KT).astype(F32)
    kidx = lax.broadcasted_iota(I32, (KT, LANES), 0)
    qpos = i * Q_BLOCK + lax.broadcasted_iota(I32, (KT, LANES), 1)

    def causal_of(t):
        return (t * KT + kidx) <= qpos

    def tile_rows(t):
        return pl.ds(pl.multiple_of(t * KT, KT), KT)

    def score_tile(t, carry):
        acc = jnp.zeros((KT, LANES), F32)
        for half in range(2):
            s_all = jnp.dot(ki_ref[half, tile_rows(t), :], qi_ref[...], preferred_element_type=F32)
            for j in range(n_ip):
                acc = acc + jnp.maximum(s_all[:, j * LANES:(j + 1) * LANES], 0.0) * wrow[2 * j + half]
        acc = acc + 0.0
        key_ref[t] = _sort_key(jnp.where(causal_of(t), acc, NEG_BIG))
        return carry

    def pair_loop(body, init):
        return lax.fori_loop(0, (n_live + 1) // 2, lambda p, c: body(2 * p + 1, body(2 * p, c)), init)

    pair_loop(score_tile, 0)

    @pl.when(n_live % 2 == 1)
    def _():
        key_ref[n_live] = jnp.full((KT, LANES), INT_MIN, I32)
        mask_ref[n_live] = jnp.full((KT, LANES), NEG_BIG, F32)

    slabs_per_tile = KT // LANES

    def causal_slab(t, r):
        return (t * KT + r * LANES + sub1) <= (i * Q_BLOCK + lane1)

    def count_where(pred):
        def body(t, acc):
            kt = key_ref[t]
            for r in range(slabs_per_tile):
                acc = jnp.where(pred(kt[r * LANES:(r + 1) * LANES], t, r), acc + 1.0, acc)
            return acc
        acc = pair_loop(body, jnp.zeros((LANES, LANES), F32))
        return jnp.sum(_fold8(acc, jnp.sum), axis=0, keepdims=True)

    def count_ge(cand):
        return count_where(lambda kt, t, r: kt >= cand) + jnp.where(cand <= KEY_NEG_BIG, n_dead, 0.0)

    def write_threshold_masks(thr):
        def body(t, carry):
            mask_ref[t] = jnp.where((key_ref[t] >= thr) & causal_of(t), 0.0, NEG_BIG)
            return carry
        lax.fori_loop(0, n_live, body, 0)

    few_keys = (i + 1) * Q_BLOCK <= topk

    @pl.when(few_keys)
    def _():
        write_threshold_masks(jnp.full((1, LANES), KEY_NEG_BIG, I32))

    @pl.when(jnp.logical_not(few_keys))
    def _():
        thr, cnt_thr = _kth_key_search(count_ge, (1, LANES), float(topk), float(seq))
        boundary_dup = jnp.max(jnp.where(cnt_thr > float(topk), 1.0, 0.0)) > 0.0

        @pl.when(jnp.logical_not(boundary_dup))
        def _():
            write_threshold_masks(thr)

        @pl.when(boundary_dup)
        def _():
            cnt_gt = count_where(lambda kt, t, r: kt > thr) + jnp.where(thr < KEY_NEG_BIG, n_dead, 0.0)
            need = float(topk) - cnt_gt
            cnt_ceq = count_where(lambda kt, t, r: (kt == thr) & causal_slab(t, r))
            any_tie = jnp.max(jnp.where(cnt_ceq > need, 1.0, 0.0)) > 0.0

            @pl.when(jnp.logical_not(any_tie))
            def _():
                write_threshold_masks(thr)

            @pl.when(any_tie)
            def _():
                rk = lax.broadcasted_iota(I32, (KT, KT), 0)
                ck = lax.broadcasted_iota(I32, (KT, KT), 1)
                lower = (ck <= rk).astype(BF16)

                def body(t, offset):
                    kt = key_ref[t]
                    cz = causal_of(t)
                    eq = ((kt == thr) & cz).astype(F32)
                    prefix = jnp.dot(lower, eq.astype(BF16), preferred_element_type=F32) + offset
                    sel = ((kt > thr) & cz) | ((eq > 0.0) & (prefix <= need))
                    mask_ref[t] = jnp.where(sel, 0.0, NEG_BIG)
                    return offset + jnp.sum(_fold8(eq, jnp.sum), axis=0, keepdims=True)

                lax.fori_loop(0, n_live, body, jnp.zeros((1, LANES), F32))


    za = za_ref[...]
    near_lo = jnp.maximum(i - 1, 0) // 2
    neg8 = jnp.full((SUBLANES, LANES), NEG_BIG, F32)
    zero8 = jnp.zeros((SUBLANES, LANES), F32)
    for g in range(N_KV_HEADS):
        qt_g = q_ref[:, g * 2 * LANES:(g + 1) * 2 * LANES]

        def logit_tile(t, mx, g=g, qt_g=qt_g):
            mx = list(mx)
            far = t < near_lo
            for half in range(2):
                l_all = jnp.dot(kk_ref[2 * g + half, tile_rows(t), :], qt_g, preferred_element_type=F32)
                for n in range(2):
                    hl = 2 * n + half
                    l = l_all[:, n * LANES:(n + 1) * LANES] + mask_ref[t]
                    lg_ref[hl, t] = l
                    mx[hl] = jnp.maximum(mx[hl], jnp.where(far, _fold8(l, jnp.max), NEG_BIG))
            return tuple(mx)

        mx = pair_loop(logit_tile, (neg8,) * hpg)
        m_row = []
        for hl in range(hpg):
            h = g * hpg + hl
            r0 = pl.ds(pl.multiple_of((i % 2) * LANES, LANES), LANES)
            lg_ref[hl, i // 2, r0, :] = lg_ref[hl, i // 2, r0, :] + bias_ref[h, 0]
            im1 = jnp.maximum(i - 1, 0)
            r1 = pl.ds(pl.multiple_of((im1 % 2) * LANES, LANES), LANES)
            lg_ref[hl, im1 // 2, r1, :] = lg_ref[hl, im1 // 2, r1, :] + jnp.where(i > 0, bias_ref[h, 1], 0.0)
            m8 = jnp.maximum(mx[hl], jnp.maximum(_fold8(lg_ref[hl, near_lo], jnp.max),
                                                 _fold8(lg_ref[hl, i // 2], jnp.max)))
            m_row.append(jnp.max(m8, axis=0, keepdims=True))

        def pv_tile(t, carry, g=g, m_row=m_row):
            ssum = list(carry[:hpg])
            acc = list(carry[hpg:])
            for n in range(2):
                for half in range(2):
                    hl = 2 * n + half
                    p = jnp.exp(lg_ref[hl, t] - m_row[hl])
                    ssum[hl] = ssum[hl] + _fold8(p, jnp.sum)
                    acc[n] = acc[n] + jnp.dot(vvt_ref[2 * g + half, t], p.astype(BF16),
                                              preferred_element_type=F32)
            return tuple(ssum) + tuple(acc)

        zacc = jnp.zeros((LANES, LANES), F32)
        res = pair_loop(pv_tile, (zero8,) * hpg + (zacc, zacc))
        for n in range(2):
            l_lo = jnp.sum(res[2 * n], axis=0, keepdims=True)
            l_hi = jnp.sum(res[2 * n + 1], axis=0, keepdims=True)
            inv = jnp.where(sub1 < ATT_HEAD_DIM, 1.0 / l_lo, 1.0 / l_hi)
            o_pair = (res[hpg + n] * inv).T
            cs = slice((2 * g + n) * LANES, (2 * g + n + 1) * LANES)
            o_ref[:, cs] = (o_pair * _silu(za[:, cs])).astype(BF16)


def _dsa_prompt_t(rel_table, q_blk, qi_blk, small, za, k2d, v2d, batch, seq):
    assert N_KV_HEADS == 2 and N_ATT_HEADS // N_KV_HEADS == 4 and seq % KT == 0
    nb = seq // Q_BLOCK
    n_t = seq // KT
    topk = min(TOPK_MAX, seq // 4)
    assert seq >= 2 * topk
    t = batch * seq
    return pl.pallas_call(
        functools.partial(_dsa_prompt_t_body, seq=seq, topk=topk),
        grid=(batch, nb),
        in_specs=[
            pl.BlockSpec(memory_space=pltpu.SMEM),
            pl.BlockSpec((None, LANES, D_ATT), lambda b, i: (b * nb + i, 0, 0)),
            pl.BlockSpec((None, LANES, N_IDX_HEADS * IDX_DIM), lambda b, i: (b * nb + i, 0, 0)),
            pl.BlockSpec((Q_BLOCK, LANES), lambda b, i: (b * nb + i, 0)),
            pl.BlockSpec((Q_BLOCK, D_ATT), lambda b, i: (b * nb + i, 0)),
            pl.BlockSpec((seq, KV_DIM), lambda b, i: (b, 0)),
            pl.BlockSpec((seq, KV_DIM), lambda b, i: (b, 0)),
            pl.BlockSpec((seq, LANES), lambda b, i: (b, 0)),
        ],
        out_specs=pl.BlockSpec((Q_BLOCK, D_ATT), lambda b, i: (b * nb + i, 0)),
        out_shape=jax.ShapeDtypeStruct((t, D_ATT), BF16),
        scratch_shapes=[
            pltpu.VMEM((2 * N_KV_HEADS, seq, LANES), BF16),
            pltpu.VMEM((2 * N_KV_HEADS, n_t, LANES, KT), BF16),
            pltpu.VMEM((2, seq, LANES), BF16),
            pltpu.VMEM((N_ATT_HEADS, 2, LANES, LANES), F32),
            pltpu.VMEM((n_t, KT, LANES), I32),
            pltpu.VMEM((n_t, KT, LANES), F32),
            pltpu.VMEM((N_ATT_HEADS // N_KV_HEADS, n_t, KT, LANES), F32),
        ],
        compiler_params=_cparams(("arbitrary", "arbitrary")),
        name="dsa_prompt",
    )(rel_table, q_blk, qi_blk, small, za, k2d, v2d, small)


def _softplus(x):
    return jnp.maximum(x, 0.0) + jnp.log1p(jnp.exp(-jnp.abs(x)))


def _lane_pick(x, idx):
    lane = lax.broadcasted_iota(I32, x.shape, 1)
    return jnp.sum(jnp.where(lane == idx, x, 0.0), axis=1, keepdims=True)


def _gated_norm(o, gn, z):
    y = o * lax.rsqrt(jnp.mean(o * o, axis=-1, keepdims=True) + RMS_EPS)
    return (y * gn) * _silu(z)


def _gdn_prompt_body(alog_ref, dtb_ref, xq_ref, xk_ref, xv_ref, wq_ref, wk_ref, wv_ref, sm_ref, zg_ref, gn_ref,
                     o_ref, sfin_ref, xs_ref, q_s, k_s, v_s, g_s, b_s, *, seq, hp):
    h0 = pl.program_id(1) * hp
    hist = SUBLANES
    slabs = [slice(hh * LANES, (hh + 1) * LANES) for hh in range(hp)]

    def conv_into(x_ref, w_ref, dst, post):
        xs_ref[0:hist, :] = jnp.zeros((hist, hp * LANES), F32)
        xs_ref[hist:hist + seq, :] = x_ref[...]
        base = hist - (CONV_W - 1)
        for hs in slabs:
            acc = xs_ref[base:base + seq, hs] * w_ref[0:1, hs]
            for j in range(1, CONV_W):
                acc = acc + xs_ref[base + j:base + j + seq, hs] * w_ref[j:j + 1, hs]
            dst[:, hs] = post(_silu(acc))

    def l2n(x):
        return x * lax.rsqrt(jnp.sum(x * x, axis=-1, keepdims=True) + 1e-6)

    conv_into(xq_ref, wq_ref, q_s, lambda x: l2n(x) * (GDN_HEAD_DIM ** -0.5))
    conv_into(xk_ref, wk_ref, k_s, l2n)
    conv_into(xv_ref, wv_ref, v_s, lambda x: x)

    lane_row = lax.broadcasted_iota(I32, (1, LANES), 1)
    alog_row = jnp.zeros((1, LANES), F32)
    dtb_row = jnp.zeros((1, LANES), F32)
    for hh in range(hp):
        alog_row = jnp.where(lane_row == SM_AG + h0 + hh, alog_ref[h0 + hh], alog_row)
        dtb_row = jnp.where(lane_row == SM_AG + h0 + hh, dtb_ref[h0 + hh], dtb_row)
    sm = sm_ref[...]
    g_all = -jnp.exp(alog_row) * _softplus(sm + dtb_row)
    beta_all = 1.0 / (1.0 + jnp.exp(-sm))
    for hh, hs in enumerate(slabs):
        g_s[:, hs] = jnp.broadcast_to(_lane_pick(g_all, SM_AG + h0 + hh), (seq, LANES))
        b_s[:, hs] = jnp.broadcast_to(_lane_pick(beta_all, SM_BG + h0 + hh), (seq, LANES))

    c = GDN_C
    ri = lax.broadcasted_iota(I32, (c, c), 0)
    ci = lax.broadcasted_iota(I32, (c, c), 1)
    tril = ri >= ci
    strict = ri > ci
    tril_f = tril.astype(F32)
    eye = (ri == ci).astype(F32)
    gn = gn_ref[...]
    off_masks = []
    for lg in range(int(math.log2(c))):
        same_pair = (ri >> (lg + 1)) == (ci >> (lg + 1))
        off_masks.append(same_pair & (((ri >> lg) & 1) == 1) & (((ci >> lg) & 1) == 0))

    tril_b = tril.astype(BF16)

    def cumsum_rows(g):
        hi = g.astype(BF16)
        r1 = g - hi.astype(F32)
        mid = r1.astype(BF16)
        lo = (r1 - mid.astype(F32)).astype(BF16)
        return sum(jnp.dot(tril_b, piece, preferred_element_type=F32) for piece in (hi, mid, lo))

    cpi = GDN_CHUNKS_PER_ITER

    def rows_of(n):
        return pl.ds(pl.multiple_of(n * c, c), c)

    def local_phase(items):
        ids = range(len(items))
        q = [q_s[rows_of(n), hs] for n, hs in items]
        k = [k_s[rows_of(n), hs] for n, hs in items]
        v = [v_s[rows_of(n), hs] for n, hs in items]
        bb = [b_s[rows_of(n), hs] for n, hs in items]
        gcum = [cumsum_rows(g_s[rows_of(n), hs]) for n, hs in items]
        gcum_row = [g.T for g in gcum]
        decay = [jnp.where(tril, jnp.exp(jnp.where(tril, gcum[e] - gcum_row[e], 0.0)), 0.0) for e in ids]
        eg = [jnp.exp(g) for g in gcum]
        kb = [k[e] * bb[e] for e in ids]
        vb = [v[e] * bb[e] for e in ids]
        kk = [_bdot_nt(kb[e], k[e]) for e in ids]
        qk = [_bdot_nt(q[e], k[e]) for e in ids]
        a_mat = [jnp.where(strict, kk[e] * decay[e], 0.0) for e in ids]
        attn = [qk[e] * decay[e] for e in ids]
        x = [eye - jnp.where(off_masks[0], a_mat[e], 0.0) for e in ids]
        for om in off_masks[1:]:
            inner = [_bdot(jnp.where(om, a_mat[e], 0.0), x[e]) for e in ids]
            x = [x[e] - _bdot(x[e], inner[e]) for e in ids]
        u = [_bdot(x[e], vb[e]) for e in ids]
        w = [_bdot(x[e], kb[e] * eg[e]) for e in ids]
        g_last = [g[c - 1:c, :] for g in gcum]
        k_dec_t = [(k[e] * jnp.exp(g_last[e] - gcum[e])).T for e in ids]
        q_dec = [q[e] * eg[e] for e in ids]
        return [dict(u=u[e], w=w[e], attn=attn[e], q_dec=q_dec[e], k_dec_t=k_dec_t[e],
                     s_dec=jnp.exp(g_last[e])) for e in ids]

    def state_phase(n, loc, states):
        heads = range(hp)
        ws = [_bdot(loc[h]["w"], states[h]) for h in heads]
        qs = [_bdot(loc[h]["q_dec"], states[h]) for h in heads]
        v_new = [loc[h]["u"] - ws[h] for h in heads]
        av = [_bdot(loc[h]["attn"], v_new[h]) for h in heads]
        kv = [_bdot(loc[h]["k_dec_t"], v_new[h]) for h in heads]
        for h, hs in enumerate(slabs):
            o_ref[rows_of(n), hs] = _gated_norm(qs[h] + av[h], gn, zg_ref[rows_of(n), hs]).astype(BF16)
        return tuple(states[h] * loc[h]["s_dec"] + kv[h] for h in heads)

    def chunk_group(p, states):
        ns = [p * cpi + r for r in range(cpi)]
        loc = local_phase([(n, hs) for n in ns for hs in slabs])
        for r, n in enumerate(ns):
            states = state_phase(n, loc[r * hp:(r + 1) * hp], states)
        return states

    zero_state = jnp.zeros((GDN_HEAD_DIM, GDN_HEAD_DIM), F32)
    finals = lax.fori_loop(0, seq // (c * cpi), chunk_group, (zero_state,) * hp)
    for hh in range(hp):
        sfin_ref[hh] = finals[hh]


GDN_HEADS_PER_STEP = 4


def _gdn_prompt(a_log, dt_bias, qkv2d, conv_w, small, zg, gdn_norm, batch, seq):
    nh = N_GDN_HEADS
    hp = GDN_HEADS_PER_STEP
    ng = nh // hp
    t = batch * seq
    w = hp * LANES
    once = pl.Buffered(1)
    blk = lambda off: pl.BlockSpec((seq, w), lambda b, j: (b, off + j), pipeline_mode=once)
    wblk = lambda off: pl.BlockSpec((CONV_W, w), lambda b, j: (0, off + j))
    return pl.pallas_call(
        functools.partial(_gdn_prompt_body, seq=seq, hp=hp),
        grid=(batch, ng),
        in_specs=[pl.BlockSpec(memory_space=pltpu.SMEM), pl.BlockSpec(memory_space=pltpu.SMEM),
                  blk(0), blk(ng), blk(2 * ng), wblk(0), wblk(ng), wblk(2 * ng),
                  pl.BlockSpec((seq, LANES), lambda b, j: (b, 0)),
                  pl.BlockSpec((seq, w), lambda b, j: (b, j), pipeline_mode=once),
                  pl.BlockSpec((1, LANES), lambda b, j: (0, 0))],
        out_specs=[pl.BlockSpec((seq, w), lambda b, j: (b, j)),
                   pl.BlockSpec((None, hp, GDN_HEAD_DIM, GDN_HEAD_DIM), lambda b, j: (b, j, 0, 0))],
        out_shape=[jax.ShapeDtypeStruct((t, D_GDN), BF16),
                   jax.ShapeDtypeStruct((batch, nh, GDN_HEAD_DIM, GDN_HEAD_DIM), F32)],
        scratch_shapes=[pltpu.VMEM((seq + 2 * SUBLANES, w), F32)] + [pltpu.VMEM((seq, w), F32)] * 5,
        compiler_params=_cparams(("parallel", "arbitrary")),
        name="gdn_prompt",
    )(a_log, dt_bias, qkv2d, qkv2d, qkv2d, conv_w, conv_w, conv_w, small, zg, gdn_norm.reshape(1, LANES))


def _outproj_body(*refs, gate_att):
    if gate_att:
        x_ref, att_ref, za_ref, gdn_ref, w_ref, g_ref, y_ref = refs
        att = (att_ref[...] * _silu(za_ref[...])).astype(BF16)
    else:
        x_ref, att_ref, gdn_ref, w_ref, g_ref, y_ref = refs
        att = att_ref[...]
    y = x_ref[...] + jnp.dot(att, w_ref[0:D_ATT, :], preferred_element_type=F32) \
        + jnp.dot(gdn_ref[...], w_ref[D_ATT:D_ATT + D_GDN, :], preferred_element_type=F32)
    y = y * lax.rsqrt(jnp.mean(y * y, axis=-1, keepdims=True) + RMS_EPS)
    y_ref[...] = y * g_ref[...]


def _outproj(x2d, att, gdn, w_bf, norm_g, tm, za=None):
    t, d = x2d.shape
    row = lambda n: pl.BlockSpec((tm, n), lambda i: (i, 0))
    full = lambda a, b: pl.BlockSpec((a, b), lambda i: (0, 0))
    ins = [x2d, att] + ([za] if za is not None else []) + [gdn, w_bf, norm_g.reshape(1, d)]
    specs = [row(d), row(D_ATT)] + ([row(D_ATT)] if za is not None else []) + \
            [row(D_GDN), full(D_ATT + D_GDN, d), full(1, d)]
    return pl.pallas_call(
        functools.partial(_outproj_body, gate_att=za is not None),
        grid=(t // tm,),
        in_specs=specs,
        out_specs=row(d),
        out_shape=jax.ShapeDtypeStruct((t, d), F32),
        compiler_params=_cparams(("parallel",)),
        name="outproj_gated" if za is not None else "outproj",
    )(*ins)


IDX_PAGES_PER_STEP = 32
ATT_PAGES_PER_STEP = 16
SAMPLE_Q_ROWS = 16


def _idx_scores_body(pt_ref, qi_ref, wi_ref, *rest):
    page_refs = rest[:IDX_PAGES_PER_STEP]
    o_ref = rest[IDX_PAGES_PER_STEP]
    qi = qi_ref[0]
    wi = wi_ref[0] * (N_IDX_HEADS ** -0.5)
    for j, pr in enumerate(page_refs):
        s = _bdot(qi, pr[...])
        sc = jnp.sum(jnp.maximum(s, 0.0) * wi, axis=0, keepdims=True)
        o_ref[0, j:j + 1, :] = sc + 0.0


def _idx_scores(page_table, qi3, wi3, kidx_pages):
    db, n_pages = page_table.shape
    page = kidx_pages.shape[2]
    pg = IDX_PAGES_PER_STEP

    def page_spec(j):
        return pl.BlockSpec((None, IDX_DIM, page), lambda b, s, pt: (pt[b, s * pg + j], 0, 0))

    grid_spec = pltpu.PrefetchScalarGridSpec(
        num_scalar_prefetch=1,
        grid=(db, n_pages // pg),
        in_specs=[pl.BlockSpec((1, N_IDX_HEADS, IDX_DIM), lambda b, s, pt: (b, 0, 0)),
                  pl.BlockSpec((1, N_IDX_HEADS, 1), lambda b, s, pt: (b, 0, 0))]
                 + [page_spec(j) for j in range(pg)],
        out_specs=pl.BlockSpec((1, pg, page), lambda b, s, pt: (b, s, 0)),
    )
    return pl.pallas_call(
        _idx_scores_body,
        grid_spec=grid_spec,
        out_shape=jax.ShapeDtypeStruct((db, n_pages, page), F32),
        compiler_params=_cparams(("parallel", "arbitrary")),
        name="sample_idx_scores",
    )(page_table, qi3, wi3, *([kidx_pages] * pg))


def _sample_select_body(sc_ref, qi_ref, sm_ref, mask_ref, key_ref, *, n_tiles, topk):
    rows = sc_ref.shape[1]
    lane = lax.broadcasted_iota(I32, (rows, LANES), 1)

    def fill(t, carry):
        key_ref[t] = _sort_key(sc_ref[t])
        return carry

    lax.fori_loop(0, n_tiles, fill, 0)
    sm = sm_ref[...]
    ki_new = sm[:, 0:IDX_DIM].astype(BF16).astype(F32)
    acc = jnp.zeros((rows, 1), F32)
    for hd in range(N_IDX_HEADS):
        qh = qi_ref[:, hd * IDX_DIM:(hd + 1) * IDX_DIM].astype(F32)
        s = jnp.sum(qh * ki_new, axis=1, keepdims=True)
        acc = acc + jnp.maximum(s, 0.0) * (sm[:, SM_WI + hd:SM_WI + hd + 1] * (N_IDX_HEADS ** -0.5))
    sc_new = jnp.broadcast_to(acc + 0.0, (rows, LANES))
    key_ref[n_tiles] = jnp.where(lane == 0, _sort_key(sc_new), INT_MIN)

    def count_where(pred):
        def body(t, a):
            return a + pred(key_ref[t]).astype(F32)
        a = lax.fori_loop(0, n_tiles + 1, body, jnp.zeros((rows, LANES), F32))
        return jnp.sum(a, axis=1, keepdims=True)

    thr, _ = _kth_key_search(lambda cand: count_where(lambda kt: kt >= cand), (rows, 1), float(topk),
                             float(n_tiles * LANES + 1))
    need = float(topk) - count_where(lambda kt: kt > thr)
    cnt_eq = count_where(lambda kt: kt == thr)
    any_tie = jnp.max(jnp.where(cnt_eq > need, 1.0, 0.0)) > 0.0

    @pl.when(jnp.logical_not(any_tie))
    def _():
        def body(t, carry):
            mask_ref[t] = jnp.where(key_ref[t] >= thr, 0.0, NEG_BIG)
            return carry
        lax.fori_loop(0, n_tiles + 1, body, 0)

    @pl.when(any_tie)
    def _():
        sub = lax.broadcasted_iota(I32, (LANES, LANES), 0)
        lane2 = lax.broadcasted_iota(I32, (LANES, LANES), 1)
        upper = (sub <= lane2).astype(BF16)
        ones = jnp.ones((LANES, LANES), BF16)

        def body(t, offset):
            kt = key_ref[t]
            eq = (kt == thr).astype(BF16)
            prefix = jnp.dot(eq, upper, preferred_element_type=F32) + offset
            sel = (kt > thr) | ((eq > 0) & (prefix <= need))
            mask_ref[t] = jnp.where(sel, 0.0, NEG_BIG)
            return offset + jnp.dot(eq, ones, preferred_element_type=F32)

        lax.fori_loop(0, n_tiles + 1, body, jnp.zeros((rows, LANES), F32))


def _sample_select(scores_t, qi2d, small, topk):
    n_tiles, db, page = scores_t.shape
    vm = pl.BlockSpec(memory_space=pltpu.VMEM)
    return pl.pallas_call(
        functools.partial(_sample_select_body, n_tiles=n_tiles, topk=topk),
        in_specs=[vm, vm, vm],
        out_specs=vm,
        out_shape=jax.ShapeDtypeStruct((n_tiles + 1, db, page), F32),
        scratch_shapes=[pltpu.VMEM((n_tiles + 1, db, page), I32)],
        compiler_params=pltpu.CompilerParams(vmem_limit_bytes=VMEM_LIMIT),
        name="sample_select",
    )(scores_t, qi2d, small)


def _sample_attn_body(pt_ref, q_ref, mask_ref, mnew_ref, knew_ref, vnew_ref, relt_ref, *rest, past_len, page):
    pg = ATT_PAGES_PER_STEP
    k_refs = rest[:pg]
    v_refs = rest[pg:2 * pg]
    o_ref, m_s, l_s, acc_s = rest[2 * pg:]
    s = pl.program_id(1)
    n_steps = pl.num_programs(1)
    nh = SAMPLE_Q_ROWS
    far_bucket = _far_bucket_checked(MAX_DISTANCE + 1)

    @pl.when(s == 0)
    def _():
        m_s[...] = jnp.full(m_s.shape, NEG_BIG, F32)
        l_s[...] = jnp.zeros(l_s.shape, F32)
        acc_s[...] = jnp.zeros(acc_s.shape, F32)

    relt = relt_ref[...]
    c_far = relt[:, far_bucket:far_bucket + 1]

    def bias_of(dist):
        bucket = _rel_bucket(dist)
        acc = jnp.zeros((nh, dist.shape[1]), F32)
        for bk in range(N_BUCKETS):
            acc = jnp.where(bucket == bk, relt[:, bk:bk + 1], acc)
        return acc - c_far

    q = q_ref[0].astype(BF16)
    logits = []
    for j in range(pg):
        l = _bdot(q, k_refs[j][...]) + mask_ref[0, j:j + 1, :]
        logits.append(l)
    logits = jnp.concatenate(logits, axis=1)

    kpos = (s * pg) * page + lax.broadcasted_iota(I32, (1, pg * page), 1)
    near = past_len - ((s + 1) * pg * page - 1) <= MAX_DISTANCE
    logits = logits + lax.cond(near, lambda: bias_of(past_len - kpos),
                               lambda: jnp.zeros((nh, pg * page), F32))

    m_old = m_s[...]
    m_new = jnp.maximum(m_old, jnp.max(logits, axis=1, keepdims=True))
    alpha = jnp.exp(m_old - m_new)
    p = jnp.exp(logits - m_new)
    l_new = l_s[...] * alpha + jnp.sum(p, axis=1, keepdims=True)
    acc = acc_s[...] * alpha
    for j in range(pg):
        acc = acc + _bdot_nt(p[:, j * page:(j + 1) * page], v_refs[j][...])
    m_s[...] = m_new
    l_s[...] = l_new
    acc_s[...] = acc

    @pl.when(s == n_steps - 1)
    def _():
        kn = knew_ref[0].astype(BF16).astype(F32)
        vn = vnew_ref[0].astype(BF16).astype(F32)
        ln = jnp.sum(q.astype(F32) * kn, axis=1, keepdims=True) + bias_of(jnp.zeros((1, 1), I32)) \
            + mnew_ref[0][:, 0:1]
        m_fin = jnp.maximum(m_new, ln)
        a2 = jnp.exp(m_new - m_fin)
        pn = jnp.exp(ln - m_fin)
        l_fin = l_new * a2 + pn
        res = (acc * a2 + pn.astype(BF16).astype(F32) * vn) / l_fin
        row = lax.broadcasted_iota(I32, res.shape, 0)
        hpg = N_ATT_HEADS // N_KV_HEADS
        o_ref[0] = jnp.where((row >= hpg) & (row < 2 * hpg), pltpu.roll(res, ATT_HEAD_DIM, 1), res)


def _sample_attn(page_table, q_lh, mask_pages, mask_new, k_new, v_new, rel_t, k_pages, v_pages, past_len):
    db, n_pages = page_table.shape
    page = k_pages.shape[2]
    pg = ATT_PAGES_PER_STEP

    def page_spec(j):
        return pl.BlockSpec((None, KV_DIM, page), lambda b, s, pt: (pt[b, s * pg + j], 0, 0))

    row3 = lambda n: pl.BlockSpec((1, 1, n), lambda b, s, pt: (b, 0, 0))
    grid_spec = pltpu.PrefetchScalarGridSpec(
        num_scalar_prefetch=1,
        grid=(db, n_pages // pg),
        in_specs=[pl.BlockSpec((1, SAMPLE_Q_ROWS, LANES), lambda b, s, pt: (b, 0, 0)),
                  pl.BlockSpec((1, pg, page), lambda b, s, pt: (b, s, 0)),
                  row3(LANES), row3(KV_DIM), row3(KV_DIM),
                  pl.BlockSpec((SAMPLE_Q_ROWS, N_BUCKETS), lambda b, s, pt: (0, 0))]
                 + [page_spec(j) for j in range(pg)] * 2,
        out_specs=pl.BlockSpec((1, SAMPLE_Q_ROWS, LANES), lambda b, s, pt: (b, 0, 0)),
        scratch_shapes=[pltpu.VMEM((SAMPLE_Q_ROWS, 1), F32), pltpu.VMEM((SAMPLE_Q_ROWS, 1), F32),
                        pltpu.VMEM((SAMPLE_Q_ROWS, LANES), F32)],
    )
    return pl.pallas_call(
        functools.partial(_sample_attn_body, past_len=past_len, page=page),
        grid_spec=grid_spec,
        out_shape=jax.ShapeDtypeStruct((db, SAMPLE_Q_ROWS, LANES), F32),
        compiler_params=_cparams(("parallel", "arbitrary")),
        name="sample_attn",
    )(page_table, q_lh, mask_pages, mask_new, k_new, v_new, rel_t, *([k_pages] * pg), *([v_pages] * pg))


def _gdn_sample_body(alog_ref, dtb_ref, x_ref, cst_ref, w_ref, sm_ref, zg_ref, gn_ref, s0_ref,
                     o_ref, s_ref, cnew_ref):
    x = x_ref[0]
    cst = cst_ref[...]
    w = w_ref[...]
    acc = cst[0:1, :] * w[0:1, :]
    for j in range(1, CONV_W - 1):
        acc = acc + cst[j:j + 1, :] * w[j:j + 1, :]
    acc = acc + x * w[CONV_W - 1:CONV_W, :]
    xc = _silu(acc)
    cnew_ref[0:CONV_W - 2, :] = cst[1:CONV_W - 1, :]
    cnew_ref[CONV_W - 2:CONV_W - 1, :] = x
    sm = sm_ref[0]
    gn = gn_ref[...]
    d = GDN_HEAD_DIM
    for h in range(N_GDN_HEADS):
        q = xc[:, h * d:(h + 1) * d]
        k = xc[:, D_GDN + h * d:D_GDN + (h + 1) * d]
        v = xc[:, 2 * D_GDN + h * d:2 * D_GDN + (h + 1) * d]
        q = q * lax.rsqrt(jnp.sum(q * q, axis=-1, keepdims=True) + 1e-6) * (d ** -0.5)
        k = k * lax.rsqrt(jnp.sum(k * k, axis=-1, keepdims=True) + 1e-6)
        a_neg = -jnp.exp(jnp.zeros((1, 1), F32) + alog_ref[h])
        g = a_neg * _softplus(sm[:, SM_AG + h:SM_AG + h + 1] + dtb_ref[h])
        beta = 1.0 / (1.0 + jnp.exp(-sm[:, SM_BG + h:SM_BG + h + 1]))
        st = s0_ref[h] * jnp.exp(g)
        k_col = jnp.broadcast_to(k, (d, d)).T
        q_col = jnp.broadcast_to(q, (d, d)).T
        kv = jnp.sum(k_col * st, axis=0, keepdims=True)
        delta = (v - kv) * beta
        st = st + k_col * delta
        s_ref[h] = st
        o = jnp.sum(q_col * st, axis=0, keepdims=True)
        o_ref[0, :, h * d:(h + 1) * d] = _gated_norm(o, gn, zg_ref[0][:, h * d:(h + 1) * d]).astype(BF16)


def _gdn_sample(a_log, dt_bias, qkv3, state_conv_l, conv_w, small3, zg3, gdn_norm, state_ssm_l):
    db = qkv3.shape[0]
    d = GDN_HEAD_DIM
    nh = N_GDN_HEADS
    row3 = lambda n: pl.BlockSpec((1, 1, n), lambda b: (b, 0, 0))
    return pl.pallas_call(
        _gdn_sample_body,
        grid=(db,),
        in_specs=[pl.BlockSpec(memory_space=pltpu.SMEM), pl.BlockSpec(memory_space=pltpu.SMEM),
                  row3(3 * D_GDN),
                  pl.BlockSpec((None, CONV_W - 1, 3 * D_GDN), lambda b: (b, 0, 0)),
                  pl.BlockSpec((CONV_W, 3 * D_GDN), lambda b: (0, 0)),
                  row3(LANES), row3(D_GDN),
                  pl.BlockSpec((1, LANES), lambda b: (0, 0)),
                  pl.BlockSpec((None, nh, d, d), lambda b: (b, 0, 0, 0))],
        out_specs=[row3(D_GDN),
                   pl.BlockSpec((None, nh, d, d), lambda b: (b, 0, 0, 0)),
                   pl.BlockSpec((None, CONV_W - 1, 3 * D_GDN), lambda b: (b, 0, 0))],
        out_shape=[jax.ShapeDtypeStruct((db, 1, D_GDN), BF16),
                   jax.ShapeDtypeStruct((db, nh, d, d), F32),
                   jax.ShapeDtypeStruct((db, CONV_W - 1, 3 * D_GDN), F32)],
        compiler_params=_cparams(("parallel",)),
        name="gdn_sample",
    )(a_log, dt_bias, qkv3, state_conv_l, conv_w, small3, zg3, gdn_norm.reshape(1, LANES), state_ssm_l)


def kernel(x_prompt, x_sample, cache_k, cache_v, cache_kidx, state_ssm, state_conv, page_table, norm_in, w_in,
           conv_w, a_log, dt_bias, gdn_norm, w_out, rel_table, norm_final):
    depth = w_in.shape[0]
    assert depth == 1, "single-layer model"
    batch, seq, d_model = x_prompt.shape
    db, dec_seq, _ = x_sample.shape
    assert dec_seq == 1 and seq % KT == 0 and seq % (GDN_C * GDN_CHUNKS_PER_ITER) == 0 and seq % PROJ_ROWS == 0
    assert (batch * seq) % (2 * PROJ_ROWS) == 0
    n_pool, page = cache_k.shape[1], cache_k.shape[2]
    n_pages = page_table.shape[1]
    past_len = n_pages * page
    assert page == LANES and n_pages % IDX_PAGES_PER_STEP == 0 and n_pages % ATT_PAGES_PER_STEP == 0

    lyr = 0
    w_pad = _prep_w_in(w_in[lyr])
    w_out_bf = w_out[lyr].astype(BF16)

    xp = x_prompt.reshape(batch * seq, d_model)
    q_blk, k2d, v2d, za, qi_blk, small, qkv2d, zg, k_t, v_t, ki_t = _inproj(
        xp, norm_in[lyr], w_pad, tm=PROJ_ROWS, blocked=True, seq=seq)
    att_g = _dsa_prompt_t(rel_table, q_blk, qi_blk, small, za, k2d, v2d, batch, seq)
    gdn_g, s_fin = _gdn_prompt(a_log[lyr], dt_bias[lyr], qkv2d, conv_w[lyr], small, zg, gdn_norm[lyr], batch, seq)
    y_prompt = _outproj(xp, att_g, gdn_g, w_out_bf, norm_final, tm=2 * PROJ_ROWS).reshape(batch, seq, d_model)
    k_prompt = jnp.transpose(k_t.reshape(batch, N_KV_HEADS, ATT_HEAD_DIM, seq), (0, 3, 1, 2))[None]
    v_prompt = jnp.transpose(v_t.reshape(batch, N_KV_HEADS, ATT_HEAD_DIM, seq), (0, 3, 1, 2))[None]
    kidx_prompt = jnp.transpose(ki_t, (0, 2, 1))[None]
    ssm_prompt = s_fin[None]
    conv_prompt = qkv2d.reshape(batch, seq, 3 * D_GDN)[:, seq - (CONV_W - 1):][None]

    xs = x_sample.reshape(db, d_model)
    q_s, k_s, v_s, za_s, qi_s, small_s, qkv_s, zg_s = _inproj(xs, norm_in[lyr], w_pad, tm=db, blocked=False)
    kidx_t = jnp.transpose(cache_kidx[lyr], (0, 2, 1))
    k_pages_t = jnp.transpose(cache_k[lyr], (0, 2, 3, 1)).reshape(n_pool, KV_DIM, page)
    v_pages_t = jnp.transpose(cache_v[lyr], (0, 2, 3, 1)).reshape(n_pool, KV_DIM, page)
    scores = _idx_scores(page_table, qi_s.reshape(db, N_IDX_HEADS, IDX_DIM),
                         small_s[:, SM_WI:SM_WI + N_IDX_HEADS].reshape(db, N_IDX_HEADS, 1),
                         kidx_t)
    topk = min(TOPK_MAX, (past_len + dec_seq) // 4)
    mask_t = _sample_select(jnp.transpose(scores, (1, 0, 2)), qi_s, small_s, topk)
    mask_pages = jnp.transpose(mask_t[:n_pages], (1, 0, 2))
    mask_new = mask_t[n_pages].reshape(db, 1, page)
    hpg = N_ATT_HEADS // N_KV_HEADS
    q8 = q_s.reshape(db, N_ATT_HEADS, ATT_HEAD_DIM).astype(F32)
    zq = jnp.zeros((db, hpg, ATT_HEAD_DIM), F32)
    q_lh = jnp.concatenate([jnp.concatenate([q8[:, :hpg], zq], axis=2),
                            jnp.concatenate([zq, q8[:, hpg:]], axis=2),
                            jnp.zeros((db, SAMPLE_Q_ROWS - N_ATT_HEADS, LANES), F32)], axis=1)
    rel_t = jnp.concatenate([rel_table.T, jnp.zeros((SAMPLE_Q_ROWS - N_ATT_HEADS, N_BUCKETS), F32)], axis=0)
    att_raw = _sample_attn(page_table, q_lh, mask_pages, mask_new, k_s.reshape(db, 1, KV_DIM),
                           v_s.reshape(db, 1, KV_DIM), rel_t,
                           k_pages_t, v_pages_t, past_len)
    att_s = att_raw[:, :N_ATT_HEADS, :ATT_HEAD_DIM].reshape(db, D_ATT)
    gdn_s, s_new, conv_new = _gdn_sample(a_log[lyr], dt_bias[lyr], qkv_s.reshape(db, 1, 3 * D_GDN), state_conv[lyr],
                                         conv_w[lyr], small_s.reshape(db, 1, LANES), zg_s.reshape(db, 1, D_GDN),
                                         gdn_norm[lyr], state_ssm[lyr])
    y_sample = _outproj(xs, att_s, gdn_s.reshape(db, D_GDN), w_out_bf, norm_final, tm=db,
                        za=za_s).reshape(db, 1, d_model)
    k_sample = k_s.reshape(1, db, 1, N_KV_HEADS, ATT_HEAD_DIM)
    v_sample = v_s.reshape(1, db, 1, N_KV_HEADS, ATT_HEAD_DIM)
    kidx_sample = small_s[:, :IDX_DIM].reshape(1, db, 1, IDX_DIM)

    return (y_prompt, y_sample, k_prompt, v_prompt, kidx_prompt, ssm_prompt, conv_prompt,
            k_sample, v_sample, kidx_sample, s_new[None], conv_new[None])
```

```python
import functools
import math

import numpy as np
import jax
import jax.numpy as jnp
from jax import lax
from jax.experimental import pallas as pl
from jax.experimental.pallas import tpu as pltpu

F32 = jnp.float32
BF16 = jnp.bfloat16
I32 = jnp.int32

N_ATT_HEADS = 8
ATT_HEAD_DIM = 64
N_KV_HEADS = 2
D_ATT = N_ATT_HEADS * ATT_HEAD_DIM
KV_DIM = N_KV_HEADS * ATT_HEAD_DIM
N_IDX_HEADS = 16
IDX_DIM = 64
TOPK_MAX = 256
N_GDN_HEADS = 4
GDN_HEAD_DIM = 128
D_GDN = N_GDN_HEADS * GDN_HEAD_DIM
CONV_W = 4
N_BUCKETS = 32
MAX_DISTANCE = 128
Q_BLOCK = 128
RMS_EPS = 1e-6
NEG_BIG = -1e30
PROJ_SIZES = (D_ATT, KV_DIM, KV_DIM, D_ATT, N_IDX_HEADS * IDX_DIM, IDX_DIM, N_IDX_HEADS,
              3 * D_GDN, D_GDN, N_GDN_HEADS, N_GDN_HEADS)

LANES = 128
SUBLANES = 8
VMEM_LIMIT = 56 * 1024 * 1024

OFF_Q = 0
OFF_K = OFF_Q + D_ATT
OFF_V = OFF_K + KV_DIM
OFF_ZA = OFF_V + KV_DIM
OFF_QI = OFF_ZA + D_ATT
OFF_SM = OFF_QI + N_IDX_HEADS * IDX_DIM
OFF_QKV = OFF_SM + LANES
OFF_ZG = OFF_QKV + 3 * D_GDN
D_PROJ_PAD = OFF_ZG + D_GDN
SM_WI = IDX_DIM
SM_AG = SM_WI + N_IDX_HEADS
SM_BG = SM_AG + N_GDN_HEADS

PROJ_ROWS = 512
GDN_C = 128
GDN_CHUNKS_PER_ITER = 4
KEY_NEG_BIG = int(np.array(NEG_BIG, np.float32).view(np.int32)) ^ 0x7FFFFFFF
INT_MIN = -2 ** 31


def _cparams(sem):
    return pltpu.CompilerParams(dimension_semantics=sem, vmem_limit_bytes=VMEM_LIMIT)


def _silu(x):
    return x * (1.0 / (1.0 + jnp.exp(-x)))


def _bdot(a, b):
    return jnp.dot(a.astype(BF16), b.astype(BF16), preferred_element_type=F32)


def _bdot_nt(a, b):
    return lax.dot_general(a.astype(BF16), b.astype(BF16), (((1,), (1,)), ((), ())),
                           preferred_element_type=F32)


def _sort_key(x):
    i = pltpu.bitcast(x, I32)
    return jnp.where(i < 0, i ^ 0x7FFFFFFF, i)


def _inproj_body(x_ref, g_ref, w_ref, q_ref, k_ref, v_ref, za_ref, qi_ref, sm_ref, qkv_ref, zg_ref, *t_refs,
                 blocked):
    x = x_ref[...]
    ms = jnp.mean(x * x, axis=-1, keepdims=True)
    h = ((x * lax.rsqrt(ms + RMS_EPS)) * g_ref[...]).astype(BF16)

    def mm(a, b):
        return jnp.dot(h, w_ref[:, a:b], preferred_element_type=F32)

    q = mm(OFF_Q, OFF_K) * (ATT_HEAD_DIM ** -0.5)
    qi = mm(OFF_QI, OFF_SM) * (IDX_DIM ** -0.5)
    if blocked:
        for r in range(x.shape[0] // Q_BLOCK):
            rs = slice(r * Q_BLOCK, (r + 1) * Q_BLOCK)
            for j in range(D_ATT // LANES):
                cs = slice(j * LANES, (j + 1) * LANES)
                q_ref[r, :, cs] = q[rs, cs].T.astype(BF16)
            for j in range(N_IDX_HEADS * IDX_DIM // LANES):
                cs = slice(j * LANES, (j + 1) * LANES)
                qi_ref[r, :, cs] = qi[rs, cs].T.astype(BF16)
    else:
        q_ref[...] = q.astype(BF16)
        qi_ref[...] = qi.astype(BF16)
    k = mm(OFF_K, OFF_V)
    v = mm(OFF_V, OFF_ZA)
    sm = mm(OFF_SM, OFF_QKV)
    k_ref[...] = k
    v_ref[...] = v
    sm_ref[...] = sm
    za_ref[...] = mm(OFF_ZA, OFF_QI)
    qkv_ref[...] = mm(OFF_QKV, OFF_ZG)
    zg_ref[...] = mm(OFF_ZG, D_PROJ_PAD)
    if t_refs:
        kt_ref, vt_ref, kit_ref = t_refs
        kt_ref[...] = k.T
        vt_ref[...] = v.T
        kit_ref[...] = sm.T[0:IDX_DIM, :]


def _prep_w_in(w):
    splits = np.cumsum(PROJ_SIZES)[:-1].tolist()
    q, k, v, z_a, qi, ki, wi, qkv, z_g, a_g, b_g = jnp.split(w, splits, axis=1)
    pad = jnp.zeros((w.shape[0], LANES - (SM_BG + N_GDN_HEADS)), w.dtype)
    small = jnp.concatenate([ki, wi, a_g, b_g, pad], axis=1)
    return jnp.concatenate([q, k, v, z_a, qi, small, qkv, z_g], axis=1).astype(BF16)


def _inproj(x2d, norm_g, w_pad, tm, blocked, seq=None):
    t, d = x2d.shape
    nblk = t // tm
    t_shapes, t_specs = [], []
    if blocked:
        spb = seq // tm
        for n in (KV_DIM, KV_DIM, IDX_DIM):
            t_shapes.append(jax.ShapeDtypeStruct((t // seq, n, seq), F32))
            t_specs.append(pl.BlockSpec((None, n, tm), lambda i: (i // spb, 0, i % spb)))
        rb = tm // Q_BLOCK
        q_shape = jax.ShapeDtypeStruct((t // Q_BLOCK, LANES, D_ATT), BF16)
        qi_shape = jax.ShapeDtypeStruct((t // Q_BLOCK, LANES, N_IDX_HEADS * IDX_DIM), BF16)
        q_spec = pl.BlockSpec((rb, LANES, D_ATT), lambda i: (i, 0, 0))
        qi_spec = pl.BlockSpec((rb, LANES, N_IDX_HEADS * IDX_DIM), lambda i: (i, 0, 0))
    else:
        q_shape = jax.ShapeDtypeStruct((t, D_ATT), BF16)
        qi_shape = jax.ShapeDtypeStruct((t, N_IDX_HEADS * IDX_DIM), BF16)
        q_spec = pl.BlockSpec((tm, D_ATT), lambda i: (i, 0))
        qi_spec = pl.BlockSpec((tm, N_IDX_HEADS * IDX_DIM), lambda i: (i, 0))

    def row(n):
        return pl.BlockSpec((tm, n), lambda i: (i, 0))

    return pl.pallas_call(
        functools.partial(_inproj_body, blocked=blocked),
        grid=(nblk,),
        in_specs=[row(d), pl.BlockSpec((1, d), lambda i: (0, 0)),
                  pl.BlockSpec((d, D_PROJ_PAD), lambda i: (0, 0))],
        out_specs=[q_spec, row(KV_DIM), row(KV_DIM), row(D_ATT), qi_spec, row(LANES), row(3 * D_GDN), row(D_GDN)]
                  + t_specs,
        out_shape=[q_shape, jax.ShapeDtypeStruct((t, KV_DIM), F32), jax.ShapeDtypeStruct((t, KV_DIM), F32),
                   jax.ShapeDtypeStruct((t, D_ATT), F32), qi_shape, jax.ShapeDtypeStruct((t, LANES), F32),
                   jax.ShapeDtypeStruct((t, 3 * D_GDN), F32), jax.ShapeDtypeStruct((t, D_GDN), F32)] + t_shapes,
        compiler_params=_cparams(("parallel",)),
        name="inproj_blocked" if blocked else "inproj_rows",
    )(x2d, norm_g.reshape(1, d), w_pad)


def _rel_bucket(dist):
    n = jnp.maximum(dist, 0)
    max_exact = N_BUCKETS // 2
    nf = jnp.maximum(n, 1).astype(F32)
    large = max_exact + (jnp.log(nf / max_exact) / math.log(MAX_DISTANCE / max_exact)
                         * (N_BUCKETS - max_exact)).astype(I32)
    large = jnp.minimum(large, N_BUCKETS - 1)
    return jnp.where(n < max_exact, n, large)


def _far_bucket_checked(first_far):
    d = np.arange(first_far, 1 << 16, dtype=np.float32)
    b = 16 + (np.log(d / 16) / math.log(MAX_DISTANCE / 16) * 16).astype(np.int32)
    assert int(b.min()) >= N_BUCKETS - 1
    return N_BUCKETS - 1


def _kth_key_search(count_ge, shape, k, total):
    def body(step, carry):
        t, cnt_t = carry
        cand = t + jnp.left_shift(jnp.int32(1), 31 - step)
        cnt = count_ge(cand)
        accept = cnt >= k
        return jnp.where(accept, cand, t), jnp.where(accept, cnt, cnt_t)

    return lax.fori_loop(0, 32, body, (jnp.full(shape, INT_MIN, I32), jnp.full(shape, total, F32)))


KT = 2 * LANES


def _fold8(x, op):
    binop = {jnp.sum: jnp.add, jnp.max: jnp.maximum}[op]
    r = x.reshape(x.shape[0] // SUBLANES, SUBLANES, x.shape[1])
    while r.shape[0] > 1:
        half = r.shape[0] // 2
        r = binop(r[:half], r[half:])
    return r[0]


def _dsa_prompt_t_body(rel_ref, q_ref, qi_ref, smq_ref, za_ref, k_ref, v_ref, sms_ref, o_ref,
                       kk_ref, vvt_ref, ki_ref, bias_ref, key_ref, mask_ref, lg_ref,
                       *, seq, topk):
    b = pl.program_id(0)
    i = pl.program_id(1)
    n_t = seq // KT
    hpg = N_ATT_HEADS // N_KV_HEADS
    far_bucket = _far_bucket_checked(MAX_DISTANCE + 1)
    sub1 = lax.broadcasted_iota(I32, (LANES, LANES), 0)
    lane1 = lax.broadcasted_iota(I32, (LANES, LANES), 1)

    @pl.when((b == 0) & (i == 0))
    def _():
        for dt in range(2):
            bucket = _rel_bucket(dt * LANES + lane1 - sub1)
            for h in range(N_ATT_HEADS):
                acc = jnp.zeros((LANES, LANES), F32)
                for bk in range(N_BUCKETS):
                    acc = jnp.where(bucket == bk, rel_ref[bk, h], acc)
                bias_ref[h, dt] = acc - rel_ref[far_bucket, h]

    @pl.when(i == 0)
    def _():
        lo = lax.broadcasted_iota(I32, (seq, LANES), 1) < ATT_HEAD_DIM
        kf = k_ref[...]
        g0 = jnp.where(lo, kf, 0.0)
        g1 = jnp.where(lo, 0.0, kf)
        kk_ref[0] = g0.astype(BF16)
        kk_ref[1] = pltpu.roll(g0, ATT_HEAD_DIM, 1).astype(BF16)
        kk_ref[2] = pltpu.roll(g1, ATT_HEAD_DIM, 1).astype(BF16)
        kk_ref[3] = g1.astype(BF16)
        c0 = jnp.where(lo, sms_ref[...], 0.0)
        ki_ref[0] = c0.astype(BF16)
        ki_ref[1] = pltpu.roll(c0, IDX_DIM, 1).astype(BF16)
        lo_t = lax.broadcasted_iota(I32, (KT, LANES), 1) < ATT_HEAD_DIM
        for t in range(n_t):
            vf = v_ref[t * KT:(t + 1) * KT, :]
            w0 = jnp.where(lo_t, vf, 0.0)
            w1 = jnp.where(lo_t, 0.0, vf)
            vvt_ref[0, t] = w0.T.astype(BF16)
            vvt_ref[1, t] = pltpu.roll(w0, ATT_HEAD_DIM, 1).T.astype(BF16)
            vvt_ref[2, t] = pltpu.roll(w1, ATT_HEAD_DIM, 1).T.astype(BF16)
            vvt_ref[3, t] = w1.T.astype(BF16)

    n_ip = N_IDX_HEADS * IDX_DIM // LANES
    sm_t = smq_ref[...].T
    wrow = [sm_t[SM_WI + h:SM_WI + h + 1, :] * (N_IDX_HEADS ** -0.5) for h in range(N_IDX_HEADS)]

    n_live = i // 2 + 1
    n_dead = ((n_t - n_live) * KT).astype(F32)
    kidx = lax.broadcasted_iota(I32, (KT, LANES), 0)
    qpos = i * Q_BLOCK + lax.broadcasted_iota(I32, (KT, LANES), 1)

    def causal_of(t):
        return (t * KT + kidx) <= qpos

    def tile_rows(t):
        return pl.ds(pl.multiple_of(t * KT, KT), KT)

    def score_tile(t, carry):
        acc = jnp.zeros((KT, LANES), F32)
        for half in range(2):
            s_all = jnp.dot(ki_ref[half, tile_rows(t), :], qi_ref[...], preferred_element_type=F32)
            for j in range(n_ip):
                acc = acc + jnp.maximum(s_all[:, j * LANES:(j + 1) * LANES], 0.0) * wrow[2 * j + half]
        acc = acc + 0.0
        key_ref[t] = _sort_key(jnp.where(causal_of(t), acc, NEG_BIG))
        return carry

    def pair_loop(body, init):
        return lax.fori_loop(0, (n_live + 1) // 2, lambda p, c: body(2 * p + 1, body(2 * p, c)), init)

    pair_loop(score_tile, 0)

    @pl.when(n_live % 2 == 1)
    def _():
        key_ref[n_live] = jnp.full((KT, LANES), INT_MIN, I32)
        mask_ref[n_live] = jnp.full((KT, LANES), NEG_BIG, F32)

    slabs_per_tile = KT // LANES

    def causal_slab(t, r):
        return (t * KT + r * LANES + sub1) <= (i * Q_BLOCK + lane1)

    def count_where(pred):
        def body(t, acc):
            kt = key_ref[t]
            for r in range(slabs_per_tile):
                acc = jnp.where(pred(kt[r * LANES:(r + 1) * LANES], t, r), acc + 1.0, acc)
            return acc
        acc = pair_loop(body, jnp.zeros((LANES, LANES), F32))
        return jnp.sum(_fold8(acc, jnp.sum), axis=0, keepdims=True)

    def count_ge(cand):
        return count_where(lambda kt, t, r: kt >= cand) + jnp.where(cand <= KEY_NEG_BIG, n_dead, 0.0)

    def write_threshold_masks(thr):
        def body(t, carry):
            mask_ref[t] = jnp.where((key_ref[t] >= thr) & causal_of(t), 0.0, NEG_BIG)
            return carry
        lax.fori_loop(0, n_live, body, 0)

    few_keys = (i + 1) * Q_BLOCK <= topk

    @pl.when(few_keys)
    def _():
        write_threshold_masks(jnp.full((1, LANES), KEY_NEG_BIG, I32))

    @pl.when(jnp.logical_not(few_keys))
    def _():
        thr, cnt_thr = _kth_key_search(count_ge, (1, LANES), float(topk), float(seq))
        boundary_dup = jnp.max(jnp.where(cnt_thr > float(topk), 1.0, 0.0)) > 0.0

        @pl.when(jnp.logical_not(boundary_dup))
        def _():
            write_threshold_masks(thr)

        @pl.when(boundary_dup)
        def _():
            cnt_gt = count_where(lambda kt, t, r: kt > thr) + jnp.where(thr < KEY_NEG_BIG, n_dead, 0.0)
            need = float(topk) - cnt_gt
            cnt_ceq = count_where(lambda kt, t, r: (kt == thr) & causal_slab(t, r))
            any_tie = jnp.max(jnp.where(cnt_ceq > need, 1.0, 0.0)) > 0.0

            @pl.when(jnp.logical_not(any_tie))
            def _():
                write_threshold_masks(thr)

            @pl.when(any_tie)
            def _():
                rk = lax.broadcasted_iota(I32, (KT, KT), 0)
                ck = lax.broadcasted_iota(I32, (KT, KT), 1)
                lower = (ck <= rk).astype(BF16)

                def body(t, offset):
                    kt = key_ref[t]
                    cz = causal_of(t)
                    eq = ((kt == thr) & cz).astype(F32)
                    prefix = jnp.dot(lower, eq.astype(BF16), preferred_element_type=F32) + offset
                    sel = ((kt > thr) & cz) | ((eq > 0.0) & (prefix <= need))
                    mask_ref[t] = jnp.where(sel, 0.0, NEG_BIG)
                    return offset + jnp.sum(_fold8(eq, jnp.sum), axis=0, keepdims=True)

                lax.fori_loop(0, n_live, body, jnp.zeros((1, LANES), F32))


    za = za_ref[...]
    near_lo = jnp.maximum(i - 1, 0) // 2
    neg8 = jnp.full((SUBLANES, LANES), NEG_BIG, F32)
    zero8 = jnp.zeros((SUBLANES, LANES), F32)
    for g in range(N_KV_HEADS):
        qt_g = q_ref[:, g * 2 * LANES:(g + 1) * 2 * LANES]

        def logit_tile(t, mx, g=g, qt_g=qt_g):
            mx = list(mx)
            far = t < near_lo
            for half in range(2):
                l_all = jnp.dot(kk_ref[2 * g + half, tile_rows(t), :], qt_g, preferred_element_type=F32)
                for n in range(2):
                    hl = 2 * n + half
                    l = l_all[:, n * LANES:(n + 1) * LANES] + mask_ref[t]
                    lg_ref[hl, t] = l
                    mx[hl] = jnp.maximum(mx[hl], jnp.where(far, _fold8(l, jnp.max), NEG_BIG))
            return tuple(mx)

        mx = pair_loop(logit_tile, (neg8,) * hpg)
        m_row = []
        for hl in range(hpg):
            h = g * hpg + hl
            r0 = pl.ds(pl.multiple_of((i % 2) * LANES, LANES), LANES)
            lg_ref[hl, i // 2, r0, :] = lg_ref[hl, i // 2, r0, :] + bias_ref[h, 0]
            im1 = jnp.maximum(i - 1, 0)
            r1 = pl.ds(pl.multiple_of((im1 % 2) * LANES, LANES), LANES)
            lg_ref[hl, im1 // 2, r1, :] = lg_ref[hl, im1 // 2, r1, :] + jnp.where(i > 0, bias_ref[h, 1], 0.0)
            m8 = jnp.maximum(mx[hl], jnp.maximum(_fold8(lg_ref[hl, near_lo], jnp.max),
                                                 _fold8(lg_ref[hl, i // 2], jnp.max)))
            m_row.append(jnp.max(m8, axis=0, keepdims=True))

        def pv_tile(t, carry, g=g, m_row=m_row):
            ssum = list(carry[:hpg])
            acc = list(carry[hpg:])
            for n in range(2):
                for half in range(2):
                    hl = 2 * n + half
                    p = jnp.exp(lg_ref[hl, t] - m_row[hl])
                    ssum[hl] = ssum[hl] + _fold8(p, jnp.sum)
                    acc[n] = acc[n] + jnp.dot(vvt_ref[2 * g + half, t], p.astype(BF16),
                                              preferred_element_type=F32)
            return tuple(ssum) + tuple(acc)

        zacc = jnp.zeros((LANES, LANES), F32)
        res = pair_loop(pv_tile, (zero8,) * hpg + (zacc, zacc))
        for n in range(2):
            l_lo = jnp.sum(res[2 * n], axis=0, keepdims=True)
            l_hi = jnp.sum(res[2 * n + 1], axis=0, keepdims=True)
            inv = jnp.where(sub1 < ATT_HEAD_DIM, 1.0 / l_lo, 1.0 / l_hi)
            o_pair = (res[hpg + n] * inv).T
            cs = slice((2 * g + n) * LANES, (2 * g + n + 1) * LANES)
            o_ref[:, cs] = (o_pair * _silu(za[:, cs])).astype(BF16)


def _dsa_prompt_t(rel_table, q_blk, qi_blk, small, za, k2d, v2d, batch, seq):
    assert N_KV_HEADS == 2 and N_ATT_HEADS // N_KV_HEADS == 4 and seq % KT == 0
    nb = seq // Q_BLOCK
    n_t = seq // KT
    topk = min(TOPK_MAX, seq // 4)
    assert seq >= 2 * topk
    t = batch * seq
    return pl.pallas_call(
        functools.partial(_dsa_prompt_t_body, seq=seq, topk=topk),
        grid=(batch, nb),
        in_specs=[
            pl.BlockSpec(memory_space=pltpu.SMEM),
            pl.BlockSpec((None, LANES, D_ATT), lambda b, i: (b * nb + i, 0, 0)),
            pl.BlockSpec((None, LANES, N_IDX_HEADS * IDX_DIM), lambda b, i: (b * nb + i, 0, 0)),
            pl.BlockSpec((Q_BLOCK, LANES), lambda b, i: (b * nb + i, 0)),
            pl.BlockSpec((Q_BLOCK, D_ATT), lambda b, i: (b * nb + i, 0)),
            pl.BlockSpec((seq, KV_DIM), lambda b, i: (b, 0)),
            pl.BlockSpec((seq, KV_DIM), lambda b, i: (b, 0)),
            pl.BlockSpec((seq, LANES), lambda b, i: (b, 0)),
        ],
        out_specs=pl.BlockSpec((Q_BLOCK, D_ATT), lambda b, i: (b * nb + i, 0)),
        out_shape=jax.ShapeDtypeStruct((t, D_ATT), BF16),
        scratch_shapes=[
            pltpu.VMEM((2 * N_KV_HEADS, seq, LANES), BF16),
            pltpu.VMEM((2 * N_KV_HEADS, n_t, LANES, KT), BF16),
            pltpu.VMEM((2, seq, LANES), BF16),
            pltpu.VMEM((N_ATT_HEADS, 2, LANES, LANES), F32),
            pltpu.VMEM((n_t, KT, LANES), I32),
            pltpu.VMEM((n_t, KT, LANES), F32),
            pltpu.VMEM((N_ATT_HEADS // N_KV_HEADS, n_t, KT, LANES), F32),
        ],
        compiler_params=_cparams(("arbitrary", "arbitrary")),
        name="dsa_prompt",
    )(rel_table, q_blk, qi_blk, small, za, k2d, v2d, small)


def _softplus(x):
    return jnp.maximum(x, 0.0) + jnp.log1p(jnp.exp(-jnp.abs(x)))


def _lane_pick(x, idx):
    lane = lax.broadcasted_iota(I32, x.shape, 1)
    return jnp.sum(jnp.where(lane == idx, x, 0.0), axis=1, keepdims=True)


def _gated_norm(o, gn, z):
    y = o * lax.rsqrt(jnp.mean(o * o, axis=-1, keepdims=True) + RMS_EPS)
    return (y * gn) * _silu(z)


def _gdn_prompt_body(alog_ref, dtb_ref, xq_ref, xk_ref, xv_ref, wq_ref, wk_ref, wv_ref, sm_ref, zg_ref, gn_ref,
                     o_ref, sfin_ref, xs_ref, q_s, k_s, v_s, g_s, b_s, *, seq, hp):
    h0 = pl.program_id(1) * hp
    hist = SUBLANES
    slabs = [slice(hh * LANES, (hh + 1) * LANES) for hh in range(hp)]

    def conv_into(x_ref, w_ref, dst, post):
        xs_ref[0:hist, :] = jnp.zeros((hist, hp * LANES), F32)
        xs_ref[hist:hist + seq, :] = x_ref[...]
        base = hist - (CONV_W - 1)
        for hs in slabs:
            acc = xs_ref[base:base + seq, hs] * w_ref[0:1, hs]
            for j in range(1, CONV_W):
                acc = acc + xs_ref[base + j:base + j + seq, hs] * w_ref[j:j + 1, hs]
            dst[:, hs] = post(_silu(acc))

    def l2n(x):
        return x * lax.rsqrt(jnp.sum(x * x, axis=-1, keepdims=True) + 1e-6)

    conv_into(xq_ref, wq_ref, q_s, lambda x: l2n(x) * (GDN_HEAD_DIM ** -0.5))
    conv_into(xk_ref, wk_ref, k_s, l2n)
    conv_into(xv_ref, wv_ref, v_s, lambda x: x)

    lane_row = lax.broadcasted_iota(I32, (1, LANES), 1)
    alog_row = jnp.zeros((1, LANES), F32)
    dtb_row = jnp.zeros((1, LANES), F32)
    for hh in range(hp):
        alog_row = jnp.where(lane_row == SM_AG + h0 + hh, alog_ref[h0 + hh], alog_row)
        dtb_row = jnp.where(lane_row == SM_AG + h0 + hh, dtb_ref[h0 + hh], dtb_row)
    sm = sm_ref[...]
    g_all = -jnp.exp(alog_row) * _softplus(sm + dtb_row)
    beta_all = 1.0 / (1.0 + jnp.exp(-sm))
    for hh, hs in enumerate(slabs):
        g_s[:, hs] = jnp.broadcast_to(_lane_pick(g_all, SM_AG + h0 + hh), (seq, LANES))
        b_s[:, hs] = jnp.broadcast_to(_lane_pick(beta_all, SM_BG + h0 + hh), (seq, LANES))

    c = GDN_C
    ri = lax.broadcasted_iota(I32, (c, c), 0)
    ci = lax.broadcasted_iota(I32, (c, c), 1)
    tril = ri >= ci
    strict = ri > ci
    tril_f = tril.astype(F32)
    eye = (ri == ci).astype(F32)
    gn = gn_ref[...]
    off_masks = []
    for lg in range(int(math.log2(c))):
        same_pair = (ri >> (lg + 1)) == (ci >> (lg + 1))
        off_masks.append(same_pair & (((ri >> lg) & 1) == 1) & (((ci >> lg) & 1) == 0))

    tril_b = tril.astype(BF16)

    def cumsum_rows(g):
        hi = g.astype(BF16)
        r1 = g - hi.astype(F32)
        mid = r1.astype(BF16)
        lo = (r1 - mid.astype(F32)).astype(BF16)
        return sum(jnp.dot(tril_b, piece, preferred_element_type=F32) for piece in (hi, mid, lo))

    cpi = GDN_CHUNKS_PER_ITER

    def rows_of(n):
        return pl.ds(pl.multiple_of(n * c, c), c)

    def local_phase(items):
        ids = range(len(items))
        q = [q_s[rows_of(n), hs] for n, hs in items]
        k = [k_s[rows_of(n), hs] for n, hs in items]
        v = [v_s[rows_of(n), hs] for n, hs in items]
        bb = [b_s[rows_of(n), hs] for n, hs in items]
        gcum = [cumsum_rows(g_s[rows_of(n), hs]) for n, hs in items]
        gcum_row = [g.T for g in gcum]
        decay = [jnp.where(tril, jnp.exp(jnp.where(tril, gcum[e] - gcum_row[e], 0.0)), 0.0) for e in ids]
        eg = [jnp.exp(g) for g in gcum]
        kb = [k[e] * bb[e] for e in ids]
        vb = [v[e] * bb[e] for e in ids]
        kq = [_bdot_nt(jnp.concatenate([kb[e], q[e]], axis=0), k[e]) for e in ids]
        a_mat = [jnp.where(strict, kq[e][:c] * decay[e], 0.0) for e in ids]
        attn = [kq[e][c:] * decay[e] for e in ids]
        x = [eye - jnp.where(off_masks[0], a_mat[e], 0.0) for e in ids]
        for om in off_masks[1:]:
            inner = [_bdot(jnp.where(om, a_mat[e], 0.0), x[e]) for e in ids]
            x = [x[e] - _bdot(x[e], inner[e]) for e in ids]
        uw = [_bdot(x[e], jnp.concatenate([vb[e], kb[e] * eg[e]], axis=1)) for e in ids]
        g_last = [g[c - 1:c, :] for g in gcum]
        k_dec_t = [(k[e] * jnp.exp(g_last[e] - gcum[e])).T for e in ids]
        wq = [jnp.concatenate([uw[e][:, GDN_HEAD_DIM:], q[e] * eg[e]], axis=0) for e in ids]
        ak = [jnp.concatenate([attn[e], k_dec_t[e]], axis=0) for e in ids]
        return [dict(u=uw[e][:, :GDN_HEAD_DIM], wq=wq[e], ak=ak[e], s_dec=jnp.exp(g_last[e])) for e in ids]

    def state_phase(n, loc, states):
        heads = range(hp)
        wq_s = [_bdot(loc[h]["wq"], states[h]) for h in heads]
        v_new = [loc[h]["u"] - wq_s[h][:c] for h in heads]
        ak_v = [_bdot(loc[h]["ak"], v_new[h]) for h in heads]
        for h, hs in enumerate(slabs):
            o = wq_s[h][c:] + ak_v[h][:c]
            o_ref[rows_of(n), hs] = _gated_norm(o, gn, zg_ref[rows_of(n), hs]).astype(BF16)
        return tuple(states[h] * loc[h]["s_dec"] + ak_v[h][c:] for h in heads)

    def chunk_group(p, states):
        ns = [p * cpi + r for r in range(cpi)]
        loc = local_phase([(n, hs) for n in ns for hs in slabs])
        for r, n in enumerate(ns):
            states = state_phase(n, loc[r * hp:(r + 1) * hp], states)
        return states

    zero_state = jnp.zeros((GDN_HEAD_DIM, GDN_HEAD_DIM), F32)
    finals = lax.fori_loop(0, seq // (c * cpi), chunk_group, (zero_state,) * hp)
    for hh in range(hp):
        sfin_ref[hh] = finals[hh]


GDN_HEADS_PER_STEP = 4


def _gdn_prompt(a_log, dt_bias, qkv2d, conv_w, small, zg, gdn_norm, batch, seq):
    nh = N_GDN_HEADS
    hp = GDN_HEADS_PER_STEP
    ng = nh // hp
    t = batch * seq
    w = hp * LANES
    once = pl.Buffered(1)
    blk = lambda off: pl.BlockSpec((seq, w), lambda b, j: (b, off + j), pipeline_mode=once)
    wblk = lambda off: pl.BlockSpec((CONV_W, w), lambda b, j: (0, off + j))
    return pl.pallas_call(
        functools.partial(_gdn_prompt_body, seq=seq, hp=hp),
        grid=(batch, ng),
        in_specs=[pl.BlockSpec(memory_space=pltpu.SMEM), pl.BlockSpec(memory_space=pltpu.SMEM),
                  blk(0), blk(ng), blk(2 * ng), wblk(0), wblk(ng), wblk(2 * ng),
                  pl.BlockSpec((seq, LANES), lambda b, j: (b, 0)),
                  pl.BlockSpec((seq, w), lambda b, j: (b, j), pipeline_mode=once),
                  pl.BlockSpec((1, LANES), lambda b, j: (0, 0))],
        out_specs=[pl.BlockSpec((seq, w), lambda b, j: (b, j)),
                   pl.BlockSpec((None, hp, GDN_HEAD_DIM, GDN_HEAD_DIM), lambda b, j: (b, j, 0, 0))],
        out_shape=[jax.ShapeDtypeStruct((t, D_GDN), BF16),
                   jax.ShapeDtypeStruct((batch, nh, GDN_HEAD_DIM, GDN_HEAD_DIM), F32)],
        scratch_shapes=[pltpu.VMEM((seq + 2 * SUBLANES, w), F32)] + [pltpu.VMEM((seq, w), F32)] * 5,
        compiler_params=_cparams(("parallel", "arbitrary")),
        name="gdn_prompt",
    )(a_log, dt_bias, qkv2d, qkv2d, qkv2d, conv_w, conv_w, conv_w, small, zg, gdn_norm.reshape(1, LANES))


def _outproj_body(*refs, gate_att):
    if gate_att:
        x_ref, att_ref, za_ref, gdn_ref, w_ref, g_ref, y_ref = refs
        att = (att_ref[...] * _silu(za_ref[...])).astype(BF16)
    else:
        x_ref, att_ref, gdn_ref, w_ref, g_ref, y_ref = refs
        att = att_ref[...]
    y = x_ref[...] + jnp.dot(att, w_ref[0:D_ATT, :], preferred_element_type=F32) \
        + jnp.dot(gdn_ref[...], w_ref[D_ATT:D_ATT + D_GDN, :], preferred_element_type=F32)
    y = y * lax.rsqrt(jnp.mean(y * y, axis=-1, keepdims=True) + RMS_EPS)
    y_ref[...] = y * g_ref[...]


def _outproj(x2d, att, gdn, w_bf, norm_g, tm, za=None):
    t, d = x2d.shape
    row = lambda n: pl.BlockSpec((tm, n), lambda i: (i, 0))
    full = lambda a, b: pl.BlockSpec((a, b), lambda i: (0, 0))
    ins = [x2d, att] + ([za] if za is not None else []) + [gdn, w_bf, norm_g.reshape(1, d)]
    specs = [row(d), row(D_ATT)] + ([row(D_ATT)] if za is not None else []) + \
            [row(D_GDN), full(D_ATT + D_GDN, d), full(1, d)]
    return pl.pallas_call(
        functools.partial(_outproj_body, gate_att=za is not None),
        grid=(t // tm,),
        in_specs=specs,
        out_specs=row(d),
        out_shape=jax.ShapeDtypeStruct((t, d), F32),
        compiler_params=_cparams(("parallel",)),
        name="outproj_gated" if za is not None else "outproj",
    )(*ins)


IDX_PAGES_PER_STEP = 64
ATT_PAGES_PER_STEP = 32
SAMPLE_Q_ROWS = 16


def _idx_scores_body(pt_ref, qi_ref, wi_ref, *rest):
    page_refs = rest[:IDX_PAGES_PER_STEP]
    o_ref = rest[IDX_PAGES_PER_STEP]
    qi = qi_ref[0]
    wi = wi_ref[0] * (N_IDX_HEADS ** -0.5)
    for j, pr in enumerate(page_refs):
        s = _bdot(qi, pr[...])
        sc = jnp.sum(jnp.maximum(s, 0.0) * wi, axis=0, keepdims=True)
        o_ref[0, j:j + 1, :] = sc + 0.0


def _idx_scores(page_table, qi3, wi3, kidx_pages):
    db, n_pages = page_table.shape
    page = kidx_pages.shape[2]
    pg = IDX_PAGES_PER_STEP

    def page_spec(j):
        return pl.BlockSpec((None, IDX_DIM, page), lambda b, s, pt: (pt[b, s * pg + j], 0, 0))

    grid_spec = pltpu.PrefetchScalarGridSpec(
        num_scalar_prefetch=1,
        grid=(db, n_pages // pg),
        in_specs=[pl.BlockSpec((1, N_IDX_HEADS, IDX_DIM), lambda b, s, pt: (b, 0, 0)),
                  pl.BlockSpec((1, N_IDX_HEADS, 1), lambda b, s, pt: (b, 0, 0))]
                 + [page_spec(j) for j in range(pg)],
        out_specs=pl.BlockSpec((1, pg, page), lambda b, s, pt: (b, s, 0)),
    )
    return pl.pallas_call(
        _idx_scores_body,
        grid_spec=grid_spec,
        out_shape=jax.ShapeDtypeStruct((db, n_pages, page), F32),
        compiler_params=_cparams(("parallel", "arbitrary")),
        name="sample_idx_scores",
    )(page_table, qi3, wi3, *([kidx_pages] * pg))


def _sample_select_body(sc_ref, qi_ref, sm_ref, mask_ref, key_ref, *, n_tiles, topk):
    rows = sc_ref.shape[1]
    lane = lax.broadcasted_iota(I32, (rows, LANES), 1)

    def fill(t, carry):
        key_ref[t] = _sort_key(sc_ref[t])
        return carry

    lax.fori_loop(0, n_tiles, fill, 0)
    sm = sm_ref[...]
    ki_new = sm[:, 0:IDX_DIM].astype(BF16).astype(F32)
    acc = jnp.zeros((rows, 1), F32)
    for hd in range(N_IDX_HEADS):
        qh = qi_ref[:, hd * IDX_DIM:(hd + 1) * IDX_DIM].astype(F32)
        s = jnp.sum(qh * ki_new, axis=1, keepdims=True)
        acc = acc + jnp.maximum(s, 0.0) * (sm[:, SM_WI + hd:SM_WI + hd + 1] * (N_IDX_HEADS ** -0.5))
    sc_new = jnp.broadcast_to(acc + 0.0, (rows, LANES))
    key_ref[n_tiles] = jnp.where(lane == 0, _sort_key(sc_new), INT_MIN)

    def count_where(pred):
        def body(t, a):
            return a + pred(key_ref[t]).astype(F32)
        a = lax.fori_loop(0, n_tiles + 1, body, jnp.zeros((rows, LANES), F32))
        return jnp.sum(a, axis=1, keepdims=True)

    thr, _ = _kth_key_search(lambda cand: count_where(lambda kt: kt >= cand), (rows, 1), float(topk),
                             float(n_tiles * LANES + 1))
    need = float(topk) - count_where(lambda kt: kt > thr)
    cnt_eq = count_where(lambda kt: kt == thr)
    any_tie = jnp.max(jnp.where(cnt_eq > need, 1.0, 0.0)) > 0.0

    @pl.when(jnp.logical_not(any_tie))
    def _():
        def body(t, carry):
            mask_ref[t] = jnp.where(key_ref[t] >= thr, 0.0, NEG_BIG)
            return carry
        lax.fori_loop(0, n_tiles + 1, body, 0)

    @pl.when(any_tie)
    def _():
        sub = lax.broadcasted_iota(I32, (LANES, LANES), 0)
        lane2 = lax.broadcasted_iota(I32, (LANES, LANES), 1)
        upper = (sub <= lane2).astype(BF16)
        ones = jnp.ones((LANES, LANES), BF16)

        def body(t, offset):
            kt = key_ref[t]
            eq = (kt == thr).astype(BF16)
            prefix = jnp.dot(eq, upper, preferred_element_type=F32) + offset
            sel = (kt > thr) | ((eq > 0) & (prefix <= need))
            mask_ref[t] = jnp.where(sel, 0.0, NEG_BIG)
            return offset + jnp.dot(eq, ones, preferred_element_type=F32)

        lax.fori_loop(0, n_tiles + 1, body, jnp.zeros((rows, LANES), F32))


def _sample_select(scores_t, qi2d, small, topk):
    n_tiles, db, page = scores_t.shape
    vm = pl.BlockSpec(memory_space=pltpu.VMEM)
    return pl.pallas_call(
        functools.partial(_sample_select_body, n_tiles=n_tiles, topk=topk),
        in_specs=[vm, vm, vm],
        out_specs=vm,
        out_shape=jax.ShapeDtypeStruct((n_tiles + 1, db, page), F32),
        scratch_shapes=[pltpu.VMEM((n_tiles + 1, db, page), I32)],
        compiler_params=pltpu.CompilerParams(vmem_limit_bytes=VMEM_LIMIT),
        name="sample_select",
    )(scores_t, qi2d, small)


def _sample_attn_body(pt_ref, q_ref, mask_ref, mnew_ref, knew_ref, vnew_ref, relt_ref, *rest, past_len, page):
    pg = ATT_PAGES_PER_STEP
    k_refs = rest[:pg]
    v_refs = rest[pg:2 * pg]
    o_ref, m_s, l_s, acc_s = rest[2 * pg:]
    s = pl.program_id(1)
    n_steps = pl.num_programs(1)
    nh = SAMPLE_Q_ROWS
    far_bucket = _far_bucket_checked(MAX_DISTANCE + 1)

    @pl.when(s == 0)
    def _():
        m_s[...] = jnp.full(m_s.shape, NEG_BIG, F32)
        l_s[...] = jnp.zeros(l_s.shape, F32)
        acc_s[...] = jnp.zeros(acc_s.shape, F32)

    relt = relt_ref[...]
    c_far = relt[:, far_bucket:far_bucket + 1]

    def bias_of(dist):
        bucket = _rel_bucket(dist)
        acc = jnp.zeros((nh, dist.shape[1]), F32)
        for bk in range(N_BUCKETS):
            acc = jnp.where(bucket == bk, relt[:, bk:bk + 1], acc)
        return acc - c_far

    q = q_ref[0].astype(BF16)
    logits = []
    for j in range(pg):
        l = _bdot(q, k_refs[j][...]) + mask_ref[0, j:j + 1, :]
        logits.append(l)
    logits = jnp.concatenate(logits, axis=1)

    kpos = (s * pg) * page + lax.broadcasted_iota(I32, (1, pg * page), 1)
    near = past_len - ((s + 1) * pg * page - 1) <= MAX_DISTANCE
    logits = logits + lax.cond(near, lambda: bias_of(past_len - kpos),
                               lambda: jnp.zeros((nh, pg * page), F32))

    m_old = m_s[...]
    m_new = jnp.maximum(m_old, jnp.max(logits, axis=1, keepdims=True))
    alpha = jnp.exp(m_old - m_new)
    p = jnp.exp(logits - m_new)
    l_new = l_s[...] * alpha + jnp.sum(p, axis=1, keepdims=True)
    acc = acc_s[...] * alpha
    for j in range(pg):
        acc = acc + _bdot_nt(p[:, j * page:(j + 1) * page], v_refs[j][...])
    m_s[...] = m_new
    l_s[...] = l_new
    acc_s[...] = acc

    @pl.when(s == n_steps - 1)
    def _():
        kn = knew_ref[0].astype(BF16).astype(F32)
        vn = vnew_ref[0].astype(BF16).astype(F32)
        ln = jnp.sum(q.astype(F32) * kn, axis=1, keepdims=True) + bias_of(jnp.zeros((1, 1), I32)) \
            + mnew_ref[0][:, 0:1]
        m_fin = jnp.maximum(m_new, ln)
        a2 = jnp.exp(m_new - m_fin)
        pn = jnp.exp(ln - m_fin)
        l_fin = l_new * a2 + pn
        res = (acc * a2 + pn.astype(BF16).astype(F32) * vn) / l_fin
        row = lax.broadcasted_iota(I32, res.shape, 0)
        hpg = N_ATT_HEADS // N_KV_HEADS
        o_ref[0] = jnp.where((row >= hpg) & (row < 2 * hpg), pltpu.roll(res, ATT_HEAD_DIM, 1), res)


def _sample_attn(page_table, q_lh, mask_pages, mask_new, k_new, v_new, rel_t, k_pages, v_pages, past_len):
    db, n_pages = page_table.shape
    page = k_pages.shape[2]
    pg = ATT_PAGES_PER_STEP

    def page_spec(j):
        return pl.BlockSpec((None, KV_DIM, page), lambda b, s, pt: (pt[b, s * pg + j], 0, 0))

    row3 = lambda n: pl.BlockSpec((1, 1, n), lambda b, s, pt: (b, 0, 0))
    grid_spec = pltpu.PrefetchScalarGridSpec(
        num_scalar_prefetch=1,
        grid=(db, n_pages // pg),
        in_specs=[pl.BlockSpec((1, SAMPLE_Q_ROWS, LANES), lambda b, s, pt: (b, 0, 0)),
                  pl.BlockSpec((1, pg, page), lambda b, s, pt: (b, s, 0)),
                  row3(LANES), row3(KV_DIM), row3(KV_DIM),
                  pl.BlockSpec((SAMPLE_Q_ROWS, N_BUCKETS), lambda b, s, pt: (0, 0))]
                 + [page_spec(j) for j in range(pg)] * 2,
        out_specs=pl.BlockSpec((1, SAMPLE_Q_ROWS, LANES), lambda b, s, pt: (b, 0, 0)),
        scratch_shapes=[pltpu.VMEM((SAMPLE_Q_ROWS, 1), F32), pltpu.VMEM((SAMPLE_Q_ROWS, 1), F32),
                        pltpu.VMEM((SAMPLE_Q_ROWS, LANES), F32)],
    )
    return pl.pallas_call(
        functools.partial(_sample_attn_body, past_len=past_len, page=page),
        grid_spec=grid_spec,
        out_shape=jax.ShapeDtypeStruct((db, SAMPLE_Q_ROWS, LANES), F32),
        compiler_params=_cparams(("parallel", "arbitrary")),
        name="sample_attn",
    )(page_table, q_lh, mask_pages, mask_new, k_new, v_new, rel_t, *([k_pages] * pg), *([v_pages] * pg))


def _gdn_sample_body(alog_ref, dtb_ref, x_ref, cst_ref, w_ref, sm_ref, zg_ref, gn_ref, s0_ref,
                     o_ref, s_ref, cnew_ref):
    x = x_ref[0]
    cst = cst_ref[...]
    w = w_ref[...]
    acc = cst[0:1, :] * w[0:1, :]
    for j in range(1, CONV_W - 1):
        acc = acc + cst[j:j + 1, :] * w[j:j + 1, :]
    acc = acc + x * w[CONV_W - 1:CONV_W, :]
    xc = _silu(acc)
    cnew_ref[0:CONV_W - 2, :] = cst[1:CONV_W - 1, :]
    cnew_ref[CONV_W - 2:CONV_W - 1, :] = x
    sm = sm_ref[0]
    gn = gn_ref[...]
    d = GDN_HEAD_DIM
    for h in range(N_GDN_HEADS):
        q = xc[:, h * d:(h + 1) * d]
        k = xc[:, D_GDN + h * d:D_GDN + (h + 1) * d]
        v = xc[:, 2 * D_GDN + h * d:2 * D_GDN + (h + 1) * d]
        q = q * lax.rsqrt(jnp.sum(q * q, axis=-1, keepdims=True) + 1e-6) * (d ** -0.5)
        k = k * lax.rsqrt(jnp.sum(k * k, axis=-1, keepdims=True) + 1e-6)
        a_neg = -jnp.exp(jnp.zeros((1, 1), F32) + alog_ref[h])
        g = a_neg * _softplus(sm[:, SM_AG + h:SM_AG + h + 1] + dtb_ref[h])
        beta = 1.0 / (1.0 + jnp.exp(-sm[:, SM_BG + h:SM_BG + h + 1]))
        st = s0_ref[h] * jnp.exp(g)
        k_col = jnp.broadcast_to(k, (d, d)).T
        q_col = jnp.broadcast_to(q, (d, d)).T
        kv = jnp.sum(k_col * st, axis=0, keepdims=True)
        delta = (v - kv) * beta
        st = st + k_col * delta
        s_ref[h] = st
        o = jnp.sum(q_col * st, axis=0, keepdims=True)
        o_ref[0, :, h * d:(h + 1) * d] = _gated_norm(o, gn, zg_ref[0][:, h * d:(h + 1) * d]).astype(BF16)


def _gdn_sample(a_log, dt_bias, qkv3, state_conv_l, conv_w, small3, zg3, gdn_norm, state_ssm_l):
    db = qkv3.shape[0]
    d = GDN_HEAD_DIM
    nh = N_GDN_HEADS
    row3 = lambda n: pl.BlockSpec((1, 1, n), lambda b: (b, 0, 0))
    return pl.pallas_call(
        _gdn_sample_body,
        grid=(db,),
        in_specs=[pl.BlockSpec(memory_space=pltpu.SMEM), pl.BlockSpec(memory_space=pltpu.SMEM),
                  row3(3 * D_GDN),
                  pl.BlockSpec((None, CONV_W - 1, 3 * D_GDN), lambda b: (b, 0, 0)),
                  pl.BlockSpec((CONV_W, 3 * D_GDN), lambda b: (0, 0)),
                  row3(LANES), row3(D_GDN),
                  pl.BlockSpec((1, LANES), lambda b: (0, 0)),
                  pl.BlockSpec((None, nh, d, d), lambda b: (b, 0, 0, 0))],
        out_specs=[row3(D_GDN),
                   pl.BlockSpec((None, nh, d, d), lambda b: (b, 0, 0, 0)),
                   pl.BlockSpec((None, CONV_W - 1, 3 * D_GDN), lambda b: (b, 0, 0))],
        out_shape=[jax.ShapeDtypeStruct((db, 1, D_GDN), BF16),
                   jax.ShapeDtypeStruct((db, nh, d, d), F32),
                   jax.ShapeDtypeStruct((db, CONV_W - 1, 3 * D_GDN), F32)],
        compiler_params=_cparams(("parallel",)),
        name="gdn_sample",
    )(a_log, dt_bias, qkv3, state_conv_l, conv_w, small3, zg3, gdn_norm.reshape(1, LANES), state_ssm_l)


def kernel(x_prompt, x_sample, cache_k, cache_v, cache_kidx, state_ssm, state_conv, page_table, norm_in, w_in,
           conv_w, a_log, dt_bias, gdn_norm, w_out, rel_table, norm_final):
    depth = w_in.shape[0]
    assert depth == 1, "single-layer model"
    batch, seq, d_model = x_prompt.shape
    db, dec_seq, _ = x_sample.shape
    assert dec_seq == 1 and seq % KT == 0 and seq % (GDN_C * GDN_CHUNKS_PER_ITER) == 0 and seq % PROJ_ROWS == 0
    assert (batch * seq) % (2 * PROJ_ROWS) == 0
    n_pool, page = cache_k.shape[1], cache_k.shape[2]
    n_pages = page_table.shape[1]
    past_len = n_pages * page
    assert page == LANES and n_pages % IDX_PAGES_PER_STEP == 0 and n_pages % ATT_PAGES_PER_STEP == 0

    lyr = 0
    w_pad = _prep_w_in(w_in[lyr])
    w_out_bf = w_out[lyr].astype(BF16)

    xp = x_prompt.reshape(batch * seq, d_model)
    q_blk, k2d, v2d, za, qi_blk, small, qkv2d, zg, k_t, v_t, ki_t = _inproj(
        xp, norm_in[lyr], w_pad, tm=PROJ_ROWS, blocked=True, seq=seq)
    att_g = _dsa_prompt_t(rel_table, q_blk, qi_blk, small, za, k2d, v2d, batch, seq)
    gdn_g, s_fin = _gdn_prompt(a_log[lyr], dt_bias[lyr], qkv2d, conv_w[lyr], small, zg, gdn_norm[lyr], batch, seq)
    y_prompt = _outproj(xp, att_g, gdn_g, w_out_bf, norm_final, tm=2 * PROJ_ROWS).reshape(batch, seq, d_model)
    k_prompt = jnp.transpose(k_t.reshape(batch, N_KV_HEADS, ATT_HEAD_DIM, seq), (0, 3, 1, 2))[None]
    v_prompt = jnp.transpose(v_t.reshape(batch, N_KV_HEADS, ATT_HEAD_DIM, seq), (0, 3, 1, 2))[None]
    kidx_prompt = jnp.transpose(ki_t, (0, 2, 1))[None]
    ssm_prompt = s_fin[None]
    conv_prompt = qkv2d.reshape(batch, seq, 3 * D_GDN)[:, seq - (CONV_W - 1):][None]

    xs = x_sample.reshape(db, d_model)
    q_s, k_s, v_s, za_s, qi_s, small_s, qkv_s, zg_s = _inproj(xs, norm_in[lyr], w_pad, tm=db, blocked=False)
    kidx_t = jnp.transpose(cache_kidx[lyr], (0, 2, 1))
    k_pages_t = jnp.transpose(cache_k[lyr], (0, 2, 3, 1)).reshape(n_pool, KV_DIM, page)
    v_pages_t = jnp.transpose(cache_v[lyr], (0, 2, 3, 1)).reshape(n_pool, KV_DIM, page)
    scores = _idx_scores(page_table, qi_s.reshape(db, N_IDX_HEADS, IDX_DIM),
                         small_s[:, SM_WI:SM_WI + N_IDX_HEADS].reshape(db, N_IDX_HEADS, 1),
                         kidx_t)
    topk = min(TOPK_MAX, (past_len + dec_seq) // 4)
    mask_t = _sample_select(jnp.transpose(scores, (1, 0, 2)), qi_s, small_s, topk)
    mask_pages = jnp.transpose(mask_t[:n_pages], (1, 0, 2))
    mask_new = mask_t[n_pages].reshape(db, 1, page)
    hpg = N_ATT_HEADS // N_KV_HEADS
    q8 = q_s.reshape(db, N_ATT_HEADS, ATT_HEAD_DIM).astype(F32)
    zq = jnp.zeros((db, hpg, ATT_HEAD_DIM), F32)
    q_lh = jnp.concatenate([jnp.concatenate([q8[:, :hpg], zq], axis=2),
                            jnp.concatenate([zq, q8[:, hpg:]], axis=2),
                            jnp.zeros((db, SAMPLE_Q_ROWS - N_ATT_HEADS, LANES), F32)], axis=1)
    rel_t = jnp.concatenate([rel_table.T, jnp.zeros((SAMPLE_Q_ROWS - N_ATT_HEADS, N_BUCKETS), F32)], axis=0)
    att_raw = _sample_attn(page_table, q_lh, mask_pages, mask_new, k_s.reshape(db, 1, KV_DIM),
                           v_s.reshape(db, 1, KV_DIM), rel_t,
                           k_pages_t, v_pages_t, past_len)
    att_s = att_raw[:, :N_ATT_HEADS, :ATT_HEAD_DIM].reshape(db, D_ATT)
    gdn_s, s_new, conv_new = _gdn_sample(a_log[lyr], dt_bias[lyr], qkv_s.reshape(db, 1, 3 * D_GDN), state_conv[lyr],
                                         conv_w[lyr], small_s.reshape(db, 1, LANES), zg_s.reshape(db, 1, D_GDN),
                                         gdn_norm[lyr], state_ssm[lyr])
    y_sample = _outproj(xs, att_s, gdn_s.reshape(db, D_GDN), w_out_bf, norm_final, tm=db,
                        za=za_s).reshape(db, 1, d_model)
    k_sample = k_s.reshape(1, db, 1, N_KV_HEADS, ATT_HEAD_DIM)
    v_sample = v_s.reshape(1, db, 1, N_KV_HEADS, ATT_HEAD_DIM)
    kidx_sample = small_s[:, :IDX_DIM].reshape(1, db, 1, IDX_DIM)

    return (y_prompt, y_sample, k_prompt, v_prompt, kidx_prompt, ssm_prompt, conv_prompt,
            k_sample, v_sample, kidx_sample, s_new[None], conv_new[None])
```

```python
import functools
import math

import numpy as np
import jax
import jax.numpy as jnp
from jax import lax
from jax.experimental import pallas as pl
from jax.experimental.pallas import tpu as pltpu

F32 = jnp.float32
BF16 = jnp.bfloat16
I32 = jnp.int32

N_ATT_HEADS = 8
ATT_HEAD_DIM = 64
N_KV_HEADS = 2
D_ATT = N_ATT_HEADS * ATT_HEAD_DIM
KV_DIM = N_KV_HEADS * ATT_HEAD_DIM
N_IDX_HEADS = 16
IDX_DIM = 64
TOPK_MAX = 256
N_GDN_HEADS = 4
GDN_HEAD_DIM = 128
D_GDN = N_GDN_HEADS * GDN_HEAD_DIM
CONV_W = 4
N_BUCKETS = 32
MAX_DISTANCE = 128
Q_BLOCK = 128
RMS_EPS = 1e-6
NEG_BIG = -1e30
PROJ_SIZES = (D_ATT, KV_DIM, KV_DIM, D_ATT, N_IDX_HEADS * IDX_DIM, IDX_DIM, N_IDX_HEADS,
              3 * D_GDN, D_GDN, N_GDN_HEADS, N_GDN_HEADS)

LANES = 128
SUBLANES = 8
VMEM_LIMIT = 56 * 1024 * 1024

OFF_Q = 0
OFF_K = OFF_Q + D_ATT
OFF_V = OFF_K + KV_DIM
OFF_ZA = OFF_V + KV_DIM
OFF_QI = OFF_ZA + D_ATT
OFF_SM = OFF_QI + N_IDX_HEADS * IDX_DIM
OFF_QKV = OFF_SM + LANES
OFF_ZG = OFF_QKV + 3 * D_GDN
D_PROJ_PAD = OFF_ZG + D_GDN
SM_WI = IDX_DIM
SM_AG = SM_WI + N_IDX_HEADS
SM_BG = SM_AG + N_GDN_HEADS

PROJ_ROWS = 512
GDN_C = 128
GDN_CHUNKS_PER_ITER = 4
KEY_NEG_BIG = int(np.array(NEG_BIG, np.float32).view(np.int32)) ^ 0x7FFFFFFF
INT_MIN = -2 ** 31
HI16_MASK = -(1 << 16)
BF16_LOWEST = float(jnp.finfo(jnp.bfloat16).min)


def _cparams(sem):
    return pltpu.CompilerParams(dimension_semantics=sem, vmem_limit_bytes=VMEM_LIMIT)


def _silu(x):
    return x * (1.0 / (1.0 + jnp.exp(-x)))


def _bdot(a, b):
    return jnp.dot(a.astype(BF16), b.astype(BF16), preferred_element_type=F32)


def _bdot_nt(a, b):
    return lax.dot_general(a.astype(BF16), b.astype(BF16), (((1,), (1,)), ((), ())),
                           preferred_element_type=F32)


def _sort_key(x):
    i = pltpu.bitcast(x, I32)
    return jnp.where(i < 0, i ^ 0x7FFFFFFF, i)


def _inproj_body(x_ref, g_ref, w_ref, q_ref, k_ref, v_ref, za_ref, qi_ref, sm_ref, qkv_ref, zg_ref, *t_refs,
                 blocked):
    x = x_ref[...]
    ms = jnp.mean(x * x, axis=-1, keepdims=True)
    h = ((x * lax.rsqrt(ms + RMS_EPS)) * g_ref[...]).astype(BF16)

    def mm(a, b):
        return jnp.dot(h, w_ref[:, a:b], preferred_element_type=F32)

    q = mm(OFF_Q, OFF_K) * (ATT_HEAD_DIM ** -0.5)
    qi = mm(OFF_QI, OFF_SM) * (IDX_DIM ** -0.5)
    if blocked:
        for r in range(x.shape[0] // Q_BLOCK):
            rs = slice(r * Q_BLOCK, (r + 1) * Q_BLOCK)
            for j in range(D_ATT // LANES):
                cs = slice(j * LANES, (j + 1) * LANES)
                q_ref[r, :, cs] = q[rs, cs].T.astype(BF16)
            for j in range(N_IDX_HEADS * IDX_DIM // LANES):
                cs = slice(j * LANES, (j + 1) * LANES)
                qi_ref[r, :, cs] = qi[rs, cs].T.astype(BF16)
    else:
        q_ref[...] = q.astype(BF16)
        qi_ref[...] = qi.astype(BF16)
    k = mm(OFF_K, OFF_V)
    v = mm(OFF_V, OFF_ZA)
    sm = mm(OFF_SM, OFF_QKV)
    k_ref[...] = k
    v_ref[...] = v
    sm_ref[...] = sm
    za_ref[...] = mm(OFF_ZA, OFF_QI)
    qkv_ref[...] = mm(OFF_QKV, OFF_ZG)
    zg_ref[...] = mm(OFF_ZG, D_PROJ_PAD)
    if t_refs:
        kt_ref, vt_ref, kit_ref = t_refs
        kt_ref[...] = k.T
        vt_ref[...] = v.T
        kit_ref[...] = sm.T[0:IDX_DIM, :]


def _prep_w_in(w):
    splits = np.cumsum(PROJ_SIZES)[:-1].tolist()
    q, k, v, z_a, qi, ki, wi, qkv, z_g, a_g, b_g = jnp.split(w, splits, axis=1)
    pad = jnp.zeros((w.shape[0], LANES - (SM_BG + N_GDN_HEADS)), w.dtype)
    small = jnp.concatenate([ki, wi, a_g, b_g, pad], axis=1)
    return jnp.concatenate([q, k, v, z_a, qi, small, qkv, z_g], axis=1).astype(BF16)


def _inproj(x2d, norm_g, w_pad, tm, blocked, seq=None):
    t, d = x2d.shape
    nblk = t // tm
    t_shapes, t_specs = [], []
    if blocked:
        spb = seq // tm
        for n in (KV_DIM, KV_DIM, IDX_DIM):
            t_shapes.append(jax.ShapeDtypeStruct((t // seq, n, seq), F32))
            t_specs.append(pl.BlockSpec((None, n, tm), lambda i: (i // spb, 0, i % spb)))
        rb = tm // Q_BLOCK
        q_shape = jax.ShapeDtypeStruct((t // Q_BLOCK, LANES, D_ATT), BF16)
        qi_shape = jax.ShapeDtypeStruct((t // Q_BLOCK, LANES, N_IDX_HEADS * IDX_DIM), BF16)
        q_spec = pl.BlockSpec((rb, LANES, D_ATT), lambda i: (i, 0, 0))
        qi_spec = pl.BlockSpec((rb, LANES, N_IDX_HEADS * IDX_DIM), lambda i: (i, 0, 0))
    else:
        q_shape = jax.ShapeDtypeStruct((t, D_ATT), BF16)
        qi_shape = jax.ShapeDtypeStruct((t, N_IDX_HEADS * IDX_DIM), BF16)
        q_spec = pl.BlockSpec((tm, D_ATT), lambda i: (i, 0))
        qi_spec = pl.BlockSpec((tm, N_IDX_HEADS * IDX_DIM), lambda i: (i, 0))

    def row(n):
        return pl.BlockSpec((tm, n), lambda i: (i, 0))

    return pl.pallas_call(
        functools.partial(_inproj_body, blocked=blocked),
        grid=(nblk,),
        in_specs=[row(d), pl.BlockSpec((1, d), lambda i: (0, 0)),
                  pl.BlockSpec((d, D_PROJ_PAD), lambda i: (0, 0))],
        out_specs=[q_spec, row(KV_DIM), row(KV_DIM), row(D_ATT), qi_spec, row(LANES), row(3 * D_GDN), row(D_GDN)]
                  + t_specs,
        out_shape=[q_shape, jax.ShapeDtypeStruct((t, KV_DIM), F32), jax.ShapeDtypeStruct((t, KV_DIM), F32),
                   jax.ShapeDtypeStruct((t, D_ATT), F32), qi_shape, jax.ShapeDtypeStruct((t, LANES), F32),
                   jax.ShapeDtypeStruct((t, 3 * D_GDN), F32), jax.ShapeDtypeStruct((t, D_GDN), F32)] + t_shapes,
        compiler_params=_cparams(("parallel",)),
        name="inproj_blocked" if blocked else "inproj_rows",
    )(x2d, norm_g.reshape(1, d), w_pad)


def _rel_bucket(dist):
    n = jnp.maximum(dist, 0)
    max_exact = N_BUCKETS // 2
    nf = jnp.maximum(n, 1).astype(F32)
    large = max_exact + (jnp.log(nf / max_exact) / math.log(MAX_DISTANCE / max_exact)
                         * (N_BUCKETS - max_exact)).astype(I32)
    large = jnp.minimum(large, N_BUCKETS - 1)
    return jnp.where(n < max_exact, n, large)


def _far_bucket_checked(first_far):
    d = np.arange(first_far, 1 << 16, dtype=np.float32)
    b = 16 + (np.log(d / 16) / math.log(MAX_DISTANCE / 16) * 16).astype(np.int32)
    assert int(b.min()) >= N_BUCKETS - 1
    return N_BUCKETS - 1


def _bit_search(count_ge, t0, cnt0, k, nbits):
    def body(step, carry):
        t, cnt_t = carry
        cand = t + jnp.left_shift(jnp.int32(1), nbits - 1 - step)
        cnt = count_ge(cand)
        accept = cnt >= k
        return jnp.where(accept, cand, t), jnp.where(accept, cnt, cnt_t)

    return lax.fori_loop(0, nbits, body, (t0, cnt0))


def _kth_key_search(count_ge, shape, k, total):
    return _bit_search(count_ge, jnp.full(shape, INT_MIN, I32), jnp.full(shape, total, F32), k, 32)


KT = 2 * LANES


def _fold8(x, op):
    binop = {jnp.sum: jnp.add, jnp.max: jnp.maximum}[op]
    r = x.reshape(x.shape[0] // SUBLANES, SUBLANES, x.shape[1])
    while r.shape[0] > 1:
        half = r.shape[0] // 2
        r = binop(r[:half], r[half:])
    return r[0]


def _dsa_prompt_t_body(rel_ref, q_ref, qi_ref, smq_ref, za_ref, k_ref, v_ref, sms_ref, o_ref,
                       kk_ref, vvt_ref, ki_ref, bias_ref, key_ref, hi_ref, mask_ref, lg_ref,
                       *, seq, topk):
    b = pl.program_id(0)
    i = pl.program_id(1)
    n_t = seq // KT
    hpg = N_ATT_HEADS // N_KV_HEADS
    far_bucket = _far_bucket_checked(MAX_DISTANCE + 1)
    sub1 = lax.broadcasted_iota(I32, (LANES, LANES), 0)
    lane1 = lax.broadcasted_iota(I32, (LANES, LANES), 1)

    @pl.when((b == 0) & (i == 0))
    def _():
        for dt in range(2):
            bucket = _rel_bucket(dt * LANES + lane1 - sub1)
            for h in range(N_ATT_HEADS):
                acc = jnp.zeros((LANES, LANES), F32)
                for bk in range(N_BUCKETS):
                    acc = jnp.where(bucket == bk, rel_ref[bk, h], acc)
                bias_ref[h, dt] = acc - rel_ref[far_bucket, h]

    @pl.when(i == 0)
    def _():
        lo = lax.broadcasted_iota(I32, (seq, LANES), 1) < ATT_HEAD_DIM
        kf = k_ref[...]
        g0 = jnp.where(lo, kf, 0.0)
        g1 = jnp.where(lo, 0.0, kf)
        kk_ref[0] = g0.astype(BF16)
        kk_ref[1] = pltpu.roll(g0, ATT_HEAD_DIM, 1).astype(BF16)
        kk_ref[2] = pltpu.roll(g1, ATT_HEAD_DIM, 1).astype(BF16)
        kk_ref[3] = g1.astype(BF16)
        c0 = jnp.where(lo, sms_ref[...], 0.0)
        ki_ref[0] = c0.astype(BF16)
        ki_ref[1] = pltpu.roll(c0, IDX_DIM, 1).astype(BF16)
        lo_t = lax.broadcasted_iota(I32, (KT, LANES), 1) < ATT_HEAD_DIM
        for t in range(n_t):
            vf = v_ref[t * KT:(t + 1) * KT, :]
            w0 = jnp.where(lo_t, vf, 0.0)
            w1 = jnp.where(lo_t, 0.0, vf)
            vvt_ref[0, t] = w0.T.astype(BF16)
            vvt_ref[1, t] = pltpu.roll(w0, ATT_HEAD_DIM, 1).T.astype(BF16)
            vvt_ref[2, t] = pltpu.roll(w1, ATT_HEAD_DIM, 1).T.astype(BF16)
            vvt_ref[3, t] = w1.T.astype(BF16)

    n_ip = N_IDX_HEADS * IDX_DIM // LANES
    sm_t = smq_ref[...].T
    wrow = [sm_t[SM_WI + h:SM_WI + h + 1, :] * (N_IDX_HEADS ** -0.5) for h in range(N_IDX_HEADS)]

    n_live = i // 2 + 1
    n_dead = ((n_t - n_live) * KT).astype(F32)
    kidx = lax.broadcasted_iota(I32, (KT, LANES), 0)
    qpos = i * Q_BLOCK + lax.broadcasted_iota(I32, (KT, LANES), 1)

    def causal_of(t):
        return (t * KT + kidx) <= qpos

    def tile_rows(t):
        return pl.ds(pl.multiple_of(t * KT, KT), KT)

    def score_tile(t, carry):
        acc = jnp.zeros((KT, LANES), F32)
        for half in range(2):
            s_all = jnp.dot(ki_ref[half, tile_rows(t), :], qi_ref[...], preferred_element_type=F32)
            for j in range(n_ip):
                acc = acc + jnp.maximum(s_all[:, j * LANES:(j + 1) * LANES], 0.0) * wrow[2 * j + half]
        acc = acc + 0.0
        sc = jnp.where(causal_of(t), acc, NEG_BIG)
        key_ref[t] = _sort_key(sc)
        hi_ref[t] = pltpu.bitcast(pltpu.bitcast(sc, I32) & HI16_MASK, F32).astype(BF16)
        return carry

    def pair_loop(body, init):
        return lax.fori_loop(0, (n_live + 1) // 2, lambda p, c: body(2 * p + 1, body(2 * p, c)), init)

    pair_loop(score_tile, 0)

    @pl.when(n_live % 2 == 1)
    def _():
        key_ref[n_live] = jnp.full((KT, LANES), INT_MIN, I32)
        hi_ref[n_live] = jnp.full((KT, LANES), BF16_LOWEST, BF16)
        mask_ref[n_live] = jnp.full((KT, LANES), NEG_BIG, F32)

    slabs_per_tile = KT // LANES

    def causal_slab(t, r):
        return (t * KT + r * LANES + sub1) <= (i * Q_BLOCK + lane1)

    def count_where(pred):
        def body(t, acc):
            kt = key_ref[t]
            for r in range(slabs_per_tile):
                acc = jnp.where(pred(kt[r * LANES:(r + 1) * LANES], t, r), acc + 1.0, acc)
            return acc
        acc = pair_loop(body, jnp.zeros((LANES, LANES), F32))
        return jnp.sum(_fold8(acc, jnp.sum), axis=0, keepdims=True)

    def count_ge(cand):
        return count_where(lambda kt, t, r: kt >= cand) + jnp.where(cand <= KEY_NEG_BIG, n_dead, 0.0)

    def write_threshold_masks(thr):
        def body(t, carry):
            mask_ref[t] = jnp.where((key_ref[t] >= thr) & causal_of(t), 0.0, NEG_BIG)
            return carry
        lax.fori_loop(0, n_live, body, 0)

    few_keys = (i + 1) * Q_BLOCK <= topk

    @pl.when(few_keys)
    def _():
        write_threshold_masks(jnp.full((1, LANES), KEY_NEG_BIG, I32))

    @pl.when(jnp.logical_not(few_keys))
    def _():
        def count_hi_ge(cand_hi):
            pattern = jnp.where(cand_hi < 0, cand_hi ^ 0x7FFF, cand_hi)
            cand_b = pltpu.bitcast(jnp.left_shift(pattern, 16), F32).astype(BF16)

            def body(t, acc):
                return jnp.where(hi_ref[t] >= cand_b, acc + 1.0, acc)
            acc = pair_loop(body, jnp.zeros((KT, LANES), BF16))
            live = jnp.sum(_fold8(acc.astype(F32), jnp.sum), axis=0, keepdims=True)
            return live + jnp.where(cand_hi <= (KEY_NEG_BIG >> 16), n_dead, 0.0)

        one = (1, LANES)
        hi, cnt_hi = _bit_search(count_hi_ge, jnp.full(one, -(1 << 15), I32), jnp.full(one, float(seq), F32),
                                 float(topk), 16)
        thr, cnt_thr = _bit_search(count_ge, jnp.left_shift(hi, 16), cnt_hi, float(topk), 16)
        boundary_dup = jnp.max(jnp.where(cnt_thr > float(topk), 1.0, 0.0)) > 0.0

        @pl.when(jnp.logical_not(boundary_dup))
        def _():
            write_threshold_masks(thr)

        @pl.when(boundary_dup)
        def _():
            cnt_gt = count_where(lambda kt, t, r: kt > thr) + jnp.where(thr < KEY_NEG_BIG, n_dead, 0.0)
            need = float(topk) - cnt_gt
            cnt_ceq = count_where(lambda kt, t, r: (kt == thr) & causal_slab(t, r))
            any_tie = jnp.max(jnp.where(cnt_ceq > need, 1.0, 0.0)) > 0.0

            @pl.when(jnp.logical_not(any_tie))
            def _():
                write_threshold_masks(thr)

            @pl.when(any_tie)
            def _():
                rk = lax.broadcasted_iota(I32, (KT, KT), 0)
                ck = lax.broadcasted_iota(I32, (KT, KT), 1)
                lower = (ck <= rk).astype(BF16)

                def body(t, offset):
                    kt = key_ref[t]
                    cz = causal_of(t)
                    eq = ((kt == thr) & cz).astype(F32)
                    prefix = jnp.dot(lower, eq.astype(BF16), preferred_element_type=F32) + offset
                    sel = ((kt > thr) & cz) | ((eq > 0.0) & (prefix <= need))
                    mask_ref[t] = jnp.where(sel, 0.0, NEG_BIG)
                    return offset + jnp.sum(_fold8(eq, jnp.sum), axis=0, keepdims=True)

                lax.fori_loop(0, n_live, body, jnp.zeros((1, LANES), F32))


    za = za_ref[...]
    near_lo = jnp.maximum(i - 1, 0) // 2
    neg8 = jnp.full((SUBLANES, LANES), NEG_BIG, F32)
    zero8 = jnp.zeros((SUBLANES, LANES), F32)
    for g in range(N_KV_HEADS):
        qt_g = q_ref[:, g * 2 * LANES:(g + 1) * 2 * LANES]

        def logit_tile(t, mx, g=g, qt_g=qt_g):
            mx = list(mx)
            far = t < near_lo
            for half in range(2):
                l_all = jnp.dot(kk_ref[2 * g + half, tile_rows(t), :], qt_g, preferred_element_type=F32)
                for n in range(2):
                    hl = 2 * n + half
                    l = l_all[:, n * LANES:(n + 1) * LANES] + mask_ref[t]
                    lg_ref[hl, t] = l
                    mx[hl] = jnp.maximum(mx[hl], jnp.where(far, _fold8(l, jnp.max), NEG_BIG))
            return tuple(mx)

        mx = pair_loop(logit_tile, (neg8,) * hpg)
        m_row = []
        for hl in range(hpg):
            h = g * hpg + hl
            r0 = pl.ds(pl.multiple_of((i % 2) * LANES, LANES), LANES)
            lg_ref[hl, i // 2, r0, :] = lg_ref[hl, i // 2, r0, :] + bias_ref[h, 0]
            im1 = jnp.maximum(i - 1, 0)
            r1 = pl.ds(pl.multiple_of((im1 % 2) * LANES, LANES), LANES)
            lg_ref[hl, im1 // 2, r1, :] = lg_ref[hl, im1 // 2, r1, :] + jnp.where(i > 0, bias_ref[h, 1], 0.0)
            m8 = jnp.maximum(mx[hl], jnp.maximum(_fold8(lg_ref[hl, near_lo], jnp.max),
                                                 _fold8(lg_ref[hl, i // 2], jnp.max)))
            m_row.append(jnp.max(m8, axis=0, keepdims=True))

        def pv_tile(t, carry, g=g, m_row=m_row):
            ssum = list(carry[:hpg])
            acc = list(carry[hpg:])
            for n in range(2):
                for half in range(2):
                    hl = 2 * n + half
                    p = jnp.exp(lg_ref[hl, t] - m_row[hl])
                    ssum[hl] = ssum[hl] + _fold8(p, jnp.sum)
                    acc[n] = acc[n] + jnp.dot(vvt_ref[2 * g + half, t], p.astype(BF16),
                                              preferred_element_type=F32)
            return tuple(ssum) + tuple(acc)

        zacc = jnp.zeros((LANES, LANES), F32)
        res = pair_loop(pv_tile, (zero8,) * hpg + (zacc, zacc))
        for n in range(2):
            l_lo = jnp.sum(res[2 * n], axis=0, keepdims=True)
            l_hi = jnp.sum(res[2 * n + 1], axis=0, keepdims=True)
            inv = jnp.where(sub1 < ATT_HEAD_DIM, 1.0 / l_lo, 1.0 / l_hi)
            o_pair = (res[hpg + n] * inv).T
            cs = slice((2 * g + n) * LANES, (2 * g + n + 1) * LANES)
            o_ref[:, cs] = (o_pair * _silu(za[:, cs])).astype(BF16)


def _dsa_prompt_t(rel_table, q_blk, qi_blk, small, za, k2d, v2d, batch, seq):
    assert N_KV_HEADS == 2 and N_ATT_HEADS // N_KV_HEADS == 4 and seq % KT == 0
    nb = seq // Q_BLOCK
    n_t = seq // KT
    topk = min(TOPK_MAX, seq // 4)
    assert seq >= 2 * topk
    t = batch * seq
    return pl.pallas_call(
        functools.partial(_dsa_prompt_t_body, seq=seq, topk=topk),
        grid=(batch, nb),
        in_specs=[
            pl.BlockSpec(memory_space=pltpu.SMEM),
            pl.BlockSpec((None, LANES, D_ATT), lambda b, i: (b * nb + i, 0, 0)),
            pl.BlockSpec((None, LANES, N_IDX_HEADS * IDX_DIM), lambda b, i: (b * nb + i, 0, 0)),
            pl.BlockSpec((Q_BLOCK, LANES), lambda b, i: (b * nb + i, 0)),
            pl.BlockSpec((Q_BLOCK, D_ATT), lambda b, i: (b * nb + i, 0)),
            pl.BlockSpec((seq, KV_DIM), lambda b, i: (b, 0)),
            pl.BlockSpec((seq, KV_DIM), lambda b, i: (b, 0)),
            pl.BlockSpec((seq, LANES), lambda b, i: (b, 0)),
        ],
        out_specs=pl.BlockSpec((Q_BLOCK, D_ATT), lambda b, i: (b * nb + i, 0)),
        out_shape=jax.ShapeDtypeStruct((t, D_ATT), BF16),
        scratch_shapes=[
            pltpu.VMEM((2 * N_KV_HEADS, seq, LANES), BF16),
            pltpu.VMEM((2 * N_KV_HEADS, n_t, LANES, KT), BF16),
            pltpu.VMEM((2, seq, LANES), BF16),
            pltpu.VMEM((N_ATT_HEADS, 2, LANES, LANES), F32),
            pltpu.VMEM((n_t, KT, LANES), I32),
            pltpu.VMEM((n_t, KT, LANES), BF16),
            pltpu.VMEM((n_t, KT, LANES), F32),
            pltpu.VMEM((N_ATT_HEADS // N_KV_HEADS, n_t, KT, LANES), F32),
        ],
        compiler_params=_cparams(("arbitrary", "arbitrary")),
        name="dsa_prompt",
    )(rel_table, q_blk, qi_blk, small, za, k2d, v2d, small)


def _softplus(x):
    return jnp.maximum(x, 0.0) + jnp.log1p(jnp.exp(-jnp.abs(x)))


def _lane_pick(x, idx):
    lane = lax.broadcasted_iota(I32, x.shape, 1)
    return jnp.sum(jnp.where(lane == idx, x, 0.0), axis=1, keepdims=True)


def _gated_norm(o, gn, z):
    y = o * lax.rsqrt(jnp.mean(o * o, axis=-1, keepdims=True) + RMS_EPS)
    return (y * gn) * _silu(z)


def _gdn_prompt_body(alog_ref, dtb_ref, xq_ref, xk_ref, xv_ref, wq_ref, wk_ref, wv_ref, sm_ref, zg_ref, gn_ref,
                     o_ref, sfin_ref, xs_ref, q_s, k_s, v_s, g_s, b_s, *, seq, hp):
    h0 = pl.program_id(1) * hp
    hist = SUBLANES
    slabs = [slice(hh * LANES, (hh + 1) * LANES) for hh in range(hp)]

    def conv_into(x_ref, w_ref, dst, post):
        xs_ref[0:hist, :] = jnp.zeros((hist, hp * LANES), F32)
        xs_ref[hist:hist + seq, :] = x_ref[...]
        base = hist - (CONV_W - 1)
        for hs in slabs:
            acc = xs_ref[base:base + seq, hs] * w_ref[0:1, hs]
            for j in range(1, CONV_W):
                acc = acc + xs_ref[base + j:base + j + seq, hs] * w_ref[j:j + 1, hs]
            dst[:, hs] = post(_silu(acc))

    def l2n(x):
        return x * lax.rsqrt(jnp.sum(x * x, axis=-1, keepdims=True) + 1e-6)

    conv_into(xq_ref, wq_ref, q_s, lambda x: l2n(x) * (GDN_HEAD_DIM ** -0.5))
    conv_into(xk_ref, wk_ref, k_s, l2n)
    conv_into(xv_ref, wv_ref, v_s, lambda x: x)

    lane_row = lax.broadcasted_iota(I32, (1, LANES), 1)
    alog_row = jnp.zeros((1, LANES), F32)
    dtb_row = jnp.zeros((1, LANES), F32)
    for hh in range(hp):
        alog_row = jnp.where(lane_row == SM_AG + h0 + hh, alog_ref[h0 + hh], alog_row)
        dtb_row = jnp.where(lane_row == SM_AG + h0 + hh, dtb_ref[h0 + hh], dtb_row)
    sm = sm_ref[...]
    g_all = -jnp.exp(alog_row) * _softplus(sm + dtb_row)
    beta_all = 1.0 / (1.0 + jnp.exp(-sm))
    for hh, hs in enumerate(slabs):
        g_s[:, hs] = jnp.broadcast_to(_lane_pick(g_all, SM_AG + h0 + hh), (seq, LANES))
        b_s[:, hs] = jnp.broadcast_to(_lane_pick(beta_all, SM_BG + h0 + hh), (seq, LANES))

    c = GDN_C
    ri = lax.broadcasted_iota(I32, (c, c), 0)
    ci = lax.broadcasted_iota(I32, (c, c), 1)
    tril = ri >= ci
    strict = ri > ci
    tril_f = tril.astype(F32)
    eye = (ri == ci).astype(F32)
    gn = gn_ref[...]
    off_masks = []
    for lg in range(int(math.log2(c))):
        same_pair = (ri >> (lg + 1)) == (ci >> (lg + 1))
        off_masks.append(same_pair & (((ri >> lg) & 1) == 1) & (((ci >> lg) & 1) == 0))

    tril_b = tril.astype(BF16)

    def cumsum_rows(g):
        hi = g.astype(BF16)
        r1 = g - hi.astype(F32)
        mid = r1.astype(BF16)
        lo = (r1 - mid.astype(F32)).astype(BF16)
        return sum(jnp.dot(tril_b, piece, preferred_element_type=F32) for piece in (hi, mid, lo))

    cpi = GDN_CHUNKS_PER_ITER

    def rows_of(n):
        return pl.ds(pl.multiple_of(n * c, c), c)

    def local_phase(items):
        ids = range(len(items))
        q = [q_s[rows_of(n), hs] for n, hs in items]
        k = [k_s[rows_of(n), hs] for n, hs in items]
        v = [v_s[rows_of(n), hs] for n, hs in items]
        bb = [b_s[rows_of(n), hs] for n, hs in items]
        gcum = [cumsum_rows(g_s[rows_of(n), hs]) for n, hs in items]
        gcum_row = [g.T for g in gcum]
        decay = [jnp.where(tril, jnp.exp(jnp.where(tril, gcum[e] - gcum_row[e], 0.0)), 0.0) for e in ids]
        eg = [jnp.exp(g) for g in gcum]
        kb = [k[e] * bb[e] for e in ids]
        vb = [v[e] * bb[e] for e in ids]
        kq = [_bdot_nt(jnp.concatenate([kb[e], q[e]], axis=0), k[e]) for e in ids]
        a_mat = [jnp.where(strict, kq[e][:c] * decay[e], 0.0) for e in ids]
        attn = [kq[e][c:] * decay[e] for e in ids]
        x = [eye - jnp.where(off_masks[0], a_mat[e], 0.0) for e in ids]
        for om in off_masks[1:]:
            inner = [_bdot(jnp.where(om, a_mat[e], 0.0), x[e]) for e in ids]
            x = [x[e] - _bdot(x[e], inner[e]) for e in ids]
        uw = [_bdot(x[e], jnp.concatenate([vb[e], kb[e] * eg[e]], axis=1)) for e in ids]
        g_last = [g[c - 1:c, :] for g in gcum]
        k_dec_t = [(k[e] * jnp.exp(g_last[e] - gcum[e])).T for e in ids]
        wq = [jnp.concatenate([uw[e][:, GDN_HEAD_DIM:], q[e] * eg[e]], axis=0) for e in ids]
        ak = [jnp.concatenate([attn[e], k_dec_t[e]], axis=0) for e in ids]
        return [dict(u=uw[e][:, :GDN_HEAD_DIM], wq=wq[e], ak=ak[e], s_dec=jnp.exp(g_last[e])) for e in ids]

    def state_phase(n, loc, states):
        heads = range(hp)
        wq_s = [_bdot(loc[h]["wq"], states[h]) for h in heads]
        v_new = [loc[h]["u"] - wq_s[h][:c] for h in heads]
        ak_v = [_bdot(loc[h]["ak"], v_new[h]) for h in heads]
        for h, hs in enumerate(slabs):
            o = wq_s[h][c:] + ak_v[h][:c]
            o_ref[rows_of(n), hs] = _gated_norm(o, gn, zg_ref[rows_of(n), hs]).astype(BF16)
        return tuple(states[h] * loc[h]["s_dec"] + ak_v[h][c:] for h in heads)

    def chunk_group(p, states):
        ns = [p * cpi + r for r in range(cpi)]
        loc = local_phase([(n, hs) for n in ns for hs in slabs])
        for r, n in enumerate(ns):
            states = state_phase(n, loc[r * hp:(r + 1) * hp], states)
        return states

    zero_state = jnp.zeros((GDN_HEAD_DIM, GDN_HEAD_DIM), F32)
    finals = lax.fori_loop(0, seq // (c * cpi), chunk_group, (zero_state,) * hp)
    for hh in range(hp):
        sfin_ref[hh] = finals[hh]


GDN_HEADS_PER_STEP = 4


def _gdn_prompt(a_log, dt_bias, qkv2d, conv_w, small, zg, gdn_norm, batch, seq):
    nh = N_GDN_HEADS
    hp = GDN_HEADS_PER_STEP
    ng = nh // hp
    t = batch * seq
    w = hp * LANES
    once = pl.Buffered(1)
    blk = lambda off: pl.BlockSpec((seq, w), lambda b, j: (b, off + j), pipeline_mode=once)
    wblk = lambda off: pl.BlockSpec((CONV_W, w), lambda b, j: (0, off + j))
    return pl.pallas_call(
        functools.partial(_gdn_prompt_body, seq=seq, hp=hp),
        grid=(batch, ng),
        in_specs=[pl.BlockSpec(memory_space=pltpu.SMEM), pl.BlockSpec(memory_space=pltpu.SMEM),
                  blk(0), blk(ng), blk(2 * ng), wblk(0), wblk(ng), wblk(2 * ng),
                  pl.BlockSpec((seq, LANES), lambda b, j: (b, 0)),
                  pl.BlockSpec((seq, w), lambda b, j: (b, j), pipeline_mode=once),
                  pl.BlockSpec((1, LANES), lambda b, j: (0, 0))],
        out_specs=[pl.BlockSpec((seq, w), lambda b, j: (b, j)),
                   pl.BlockSpec((None, hp, GDN_HEAD_DIM, GDN_HEAD_DIM), lambda b, j: (b, j, 0, 0))],
        out_shape=[jax.ShapeDtypeStruct((t, D_GDN), BF16),
                   jax.ShapeDtypeStruct((batch, nh, GDN_HEAD_DIM, GDN_HEAD_DIM), F32)],
        scratch_shapes=[pltpu.VMEM((seq + 2 * SUBLANES, w), F32)] + [pltpu.VMEM((seq, w), F32)] * 5,
        compiler_params=_cparams(("parallel", "arbitrary")),
        name="gdn_prompt",
    )(a_log, dt_bias, qkv2d, qkv2d, qkv2d, conv_w, conv_w, conv_w, small, zg, gdn_norm.reshape(1, LANES))


def _outproj_body(*refs, gate_att):
    if gate_att:
        x_ref, att_ref, za_ref, gdn_ref, w_ref, g_ref, y_ref = refs
        att = (att_ref[...] * _silu(za_ref[...])).astype(BF16)
    else:
        x_ref, att_ref, gdn_ref, w_ref, g_ref, y_ref = refs
        att = att_ref[...]
    y = x_ref[...] + jnp.dot(att, w_ref[0:D_ATT, :], preferred_element_type=F32) \
        + jnp.dot(gdn_ref[...], w_ref[D_ATT:D_ATT + D_GDN, :], preferred_element_type=F32)
    y = y * lax.rsqrt(jnp.mean(y * y, axis=-1, keepdims=True) + RMS_EPS)
    y_ref[...] = y * g_ref[...]


def _outproj(x2d, att, gdn, w_bf, norm_g, tm, za=None):
    t, d = x2d.shape
    row = lambda n: pl.BlockSpec((tm, n), lambda i: (i, 0))
    full = lambda a, b: pl.BlockSpec((a, b), lambda i: (0, 0))
    ins = [x2d, att] + ([za] if za is not None else []) + [gdn, w_bf, norm_g.reshape(1, d)]
    specs = [row(d), row(D_ATT)] + ([row(D_ATT)] if za is not None else []) + \
            [row(D_GDN), full(D_ATT + D_GDN, d), full(1, d)]
    return pl.pallas_call(
        functools.partial(_outproj_body, gate_att=za is not None),
        grid=(t // tm,),
        in_specs=specs,
        out_specs=row(d),
        out_shape=jax.ShapeDtypeStruct((t, d), F32),
        compiler_params=_cparams(("parallel",)),
        name="outproj_gated" if za is not None else "outproj",
    )(*ins)


IDX_PAGES_PER_STEP = 64
ATT_PAGES_PER_STEP = 32
SAMPLE_Q_ROWS = 16


def _idx_scores_body(pt_ref, qi_ref, wi_ref, *rest):
    page_refs = rest[:IDX_PAGES_PER_STEP]
    o_ref = rest[IDX_PAGES_PER_STEP]
    qi = qi_ref[0]
    wi = wi_ref[0] * (N_IDX_HEADS ** -0.5)
    for j, pr in enumerate(page_refs):
        s = _bdot(qi, pr[...])
        sc = jnp.sum(jnp.maximum(s, 0.0) * wi, axis=0, keepdims=True)
        o_ref[0, j:j + 1, :] = sc + 0.0


def _idx_scores(page_table, qi3, wi3, kidx_pages):
    db, n_pages = page_table.shape
    page = kidx_pages.shape[2]
    pg = IDX_PAGES_PER_STEP

    def page_spec(j):
        return pl.BlockSpec((None, IDX_DIM, page), lambda b, s, pt: (pt[b, s * pg + j], 0, 0))

    grid_spec = pltpu.PrefetchScalarGridSpec(
        num_scalar_prefetch=1,
        grid=(db, n_pages // pg),
        in_specs=[pl.BlockSpec((1, N_IDX_HEADS, IDX_DIM), lambda b, s, pt: (b, 0, 0)),
                  pl.BlockSpec((1, N_IDX_HEADS, 1), lambda b, s, pt: (b, 0, 0))]
                 + [page_spec(j) for j in range(pg)],
        out_specs=pl.BlockSpec((1, pg, page), lambda b, s, pt: (b, s, 0)),
    )
    return pl.pallas_call(
        _idx_scores_body,
        grid_spec=grid_spec,
        out_shape=jax.ShapeDtypeStruct((db, n_pages, page), F32),
        compiler_params=_cparams(("parallel", "arbitrary")),
        name="sample_idx_scores",
    )(page_table, qi3, wi3, *([kidx_pages] * pg))


def _sample_select_body(sc_ref, qi_ref, sm_ref, mask_ref, key_ref, *, n_tiles, topk):
    rows = sc_ref.shape[1]
    lane = lax.broadcasted_iota(I32, (rows, LANES), 1)

    def fill(t, carry):
        key_ref[t] = _sort_key(sc_ref[t])
        return carry

    lax.fori_loop(0, n_tiles, fill, 0)
    sm = sm_ref[...]
    ki_new = sm[:, 0:IDX_DIM].astype(BF16).astype(F32)
    acc = jnp.zeros((rows, 1), F32)
    for hd in range(N_IDX_HEADS):
        qh = qi_ref[:, hd * IDX_DIM:(hd + 1) * IDX_DIM].astype(F32)
        s = jnp.sum(qh * ki_new, axis=1, keepdims=True)
        acc = acc + jnp.maximum(s, 0.0) * (sm[:, SM_WI + hd:SM_WI + hd + 1] * (N_IDX_HEADS ** -0.5))
    sc_new = jnp.broadcast_to(acc + 0.0, (rows, LANES))
    key_ref[n_tiles] = jnp.where(lane == 0, _sort_key(sc_new), INT_MIN)

    def count_where(pred):
        def body(t, a):
            return a + pred(key_ref[t]).astype(F32)
        a = lax.fori_loop(0, n_tiles + 1, body, jnp.zeros((rows, LANES), F32))
        return jnp.sum(a, axis=1, keepdims=True)

    thr, _ = _kth_key_search(lambda cand: count_where(lambda kt: kt >= cand), (rows, 1), float(topk),
                             float(n_tiles * LANES + 1))
    need = float(topk) - count_where(lambda kt: kt > thr)
    cnt_eq = count_where(lambda kt: kt == thr)
    any_tie = jnp.max(jnp.where(cnt_eq > need, 1.0, 0.0)) > 0.0

    @pl.when(jnp.logical_not(any_tie))
    def _():
        def body(t, carry):
            mask_ref[t] = jnp.where(key_ref[t] >= thr, 0.0, NEG_BIG)
            return carry
        lax.fori_loop(0, n_tiles + 1, body, 0)

    @pl.when(any_tie)
    def _():
        sub = lax.broadcasted_iota(I32, (LANES, LANES), 0)
        lane2 = lax.broadcasted_iota(I32, (LANES, LANES), 1)
        upper = (sub <= lane2).astype(BF16)
        ones = jnp.ones((LANES, LANES), BF16)

        def body(t, offset):
            kt = key_ref[t]
            eq = (kt == thr).astype(BF16)
            prefix = jnp.dot(eq, upper, preferred_element_type=F32) + offset
            sel = (kt > thr) | ((eq > 0) & (prefix <= need))
            mask_ref[t] = jnp.where(sel, 0.0, NEG_BIG)
            return offset + jnp.dot(eq, ones, preferred_element_type=F32)

        lax.fori_loop(0, n_tiles + 1, body, jnp.zeros((rows, LANES), F32))


def _sample_select(scores_t, qi2d, small, topk):
    n_tiles, db, page = scores_t.shape
    vm = pl.BlockSpec(memory_space=pltpu.VMEM)
    return pl.pallas_call(
        functools.partial(_sample_select_body, n_tiles=n_tiles, topk=topk),
        in_specs=[vm, vm, vm],
        out_specs=vm,
        out_shape=jax.ShapeDtypeStruct((n_tiles + 1, db, page), F32),
        scratch_shapes=[pltpu.VMEM((n_tiles + 1, db, page), I32)],
        compiler_params=pltpu.CompilerParams(vmem_limit_bytes=VMEM_LIMIT),
        name="sample_select",
    )(scores_t, qi2d, small)


def _sample_attn_body(pt_ref, q_ref, mask_ref, mnew_ref, knew_ref, vnew_ref, relt_ref, *rest, past_len, page):
    pg = ATT_PAGES_PER_STEP
    k_refs = rest[:pg]
    v_refs = rest[pg:2 * pg]
    o_ref, m_s, l_s, acc_s = rest[2 * pg:]
    s = pl.program_id(1)
    n_steps = pl.num_programs(1)
    nh = SAMPLE_Q_ROWS
    far_bucket = _far_bucket_checked(MAX_DISTANCE + 1)

    @pl.when(s == 0)
    def _():
        m_s[...] = jnp.full(m_s.shape, NEG_BIG, F32)
        l_s[...] = jnp.zeros(l_s.shape, F32)
        acc_s[...] = jnp.zeros(acc_s.shape, F32)

    relt = relt_ref[...]
    c_far = relt[:, far_bucket:far_bucket + 1]

    def bias_of(dist):
        bucket = _rel_bucket(dist)
        acc = jnp.zeros((nh, dist.shape[1]), F32)
        for bk in range(N_BUCKETS):
            acc = jnp.where(bucket == bk, relt[:, bk:bk + 1], acc)
        return acc - c_far

    q = q_ref[0].astype(BF16)
    logits = []
    for j in range(pg):
        l = _bdot(q, k_refs[j][...]) + mask_ref[0, j:j + 1, :]
        logits.append(l)
    logits = jnp.concatenate(logits, axis=1)

    kpos = (s * pg) * page + lax.broadcasted_iota(I32, (1, pg * page), 1)
    near = past_len - ((s + 1) * pg * page - 1) <= MAX_DISTANCE
    logits = logits + lax.cond(near, lambda: bias_of(past_len - kpos),
                               lambda: jnp.zeros((nh, pg * page), F32))

    m_old = m_s[...]
    m_new = jnp.maximum(m_old, jnp.max(logits, axis=1, keepdims=True))
    alpha = jnp.exp(m_old - m_new)
    p = jnp.exp(logits - m_new)
    l_new = l_s[...] * alpha + jnp.sum(p, axis=1, keepdims=True)
    acc = acc_s[...] * alpha
    for j in range(pg):
        acc = acc + _bdot_nt(p[:, j * page:(j + 1) * page], v_refs[j][...])
    m_s[...] = m_new
    l_s[...] = l_new
    acc_s[...] = acc

    @pl.when(s == n_steps - 1)
    def _():
        kn = knew_ref[0].astype(BF16).astype(F32)
        vn = vnew_ref[0].astype(BF16).astype(F32)
        ln = jnp.sum(q.astype(F32) * kn, axis=1, keepdims=True) + bias_of(jnp.zeros((1, 1), I32)) \
            + mnew_ref[0][:, 0:1]
        m_fin = jnp.maximum(m_new, ln)
        a2 = jnp.exp(m_new - m_fin)
        pn = jnp.exp(ln - m_fin)
        l_fin = l_new * a2 + pn
        res = (acc * a2 + pn.astype(BF16).astype(F32) * vn) / l_fin
        row = lax.broadcasted_iota(I32, res.shape, 0)
        hpg = N_ATT_HEADS // N_KV_HEADS
        o_ref[0] = jnp.where((row >= hpg) & (row < 2 * hpg), pltpu.roll(res, ATT_HEAD_DIM, 1), res)


def _sample_attn(page_table, q_lh, mask_pages, mask_new, k_new, v_new, rel_t, k_pages, v_pages, past_len):
    db, n_pages = page_table.shape
    page = k_pages.shape[2]
    pg = ATT_PAGES_PER_STEP

    def page_spec(j):
        return pl.BlockSpec((None, KV_DIM, page), lambda b, s, pt: (pt[b, s * pg + j], 0, 0))

    row3 = lambda n: pl.BlockSpec((1, 1, n), lambda b, s, pt: (b, 0, 0))
    grid_spec = pltpu.PrefetchScalarGridSpec(
        num_scalar_prefetch=1,
        grid=(db, n_pages // pg),
        in_specs=[pl.BlockSpec((1, SAMPLE_Q_ROWS, LANES), lambda b, s, pt: (b, 0, 0)),
                  pl.BlockSpec((1, pg, page), lambda b, s, pt: (b, s, 0)),
                  row3(LANES), row3(KV_DIM), row3(KV_DIM),
                  pl.BlockSpec((SAMPLE_Q_ROWS, N_BUCKETS), lambda b, s, pt: (0, 0))]
                 + [page_spec(j) for j in range(pg)] * 2,
        out_specs=pl.BlockSpec((1, SAMPLE_Q_ROWS, LANES), lambda b, s, pt: (b, 0, 0)),
        scratch_shapes=[pltpu.VMEM((SAMPLE_Q_ROWS, 1), F32), pltpu.VMEM((SAMPLE_Q_ROWS, 1), F32),
                        pltpu.VMEM((SAMPLE_Q_ROWS, LANES), F32)],
    )
    return pl.pallas_call(
        functools.partial(_sample_attn_body, past_len=past_len, page=page),
        grid_spec=grid_spec,
        out_shape=jax.ShapeDtypeStruct((db, SAMPLE_Q_ROWS, LANES), F32),
        compiler_params=_cparams(("parallel", "arbitrary")),
        name="sample_attn",
    )(page_table, q_lh, mask_pages, mask_new, k_new, v_new, rel_t, *([k_pages] * pg), *([v_pages] * pg))


def _gdn_sample_body(alog_ref, dtb_ref, x_ref, cst_ref, w_ref, sm_ref, zg_ref, gn_ref, s0_ref,
                     o_ref, s_ref, cnew_ref):
    x = x_ref[0]
    cst = cst_ref[...]
    w = w_ref[...]
    acc = cst[0:1, :] * w[0:1, :]
    for j in range(1, CONV_W - 1):
        acc = acc + cst[j:j + 1, :] * w[j:j + 1, :]
    acc = acc + x * w[CONV_W - 1:CONV_W, :]
    xc = _silu(acc)
    cnew_ref[0:CONV_W - 2, :] = cst[1:CONV_W - 1, :]
    cnew_ref[CONV_W - 2:CONV_W - 1, :] = x
    sm = sm_ref[0]
    gn = gn_ref[...]
    d = GDN_HEAD_DIM
    for h in range(N_GDN_HEADS):
        q = xc[:, h * d:(h + 1) * d]
        k = xc[:, D_GDN + h * d:D_GDN + (h + 1) * d]
        v = xc[:, 2 * D_GDN + h * d:2 * D_GDN + (h + 1) * d]
        q = q * lax.rsqrt(jnp.sum(q * q, axis=-1, keepdims=True) + 1e-6) * (d ** -0.5)
        k = k * lax.rsqrt(jnp.sum(k * k, axis=-1, keepdims=True) + 1e-6)
        a_neg = -jnp.exp(jnp.zeros((1, 1), F32) + alog_ref[h])
        g = a_neg * _softplus(sm[:, SM_AG + h:SM_AG + h + 1] + dtb_ref[h])
        beta = 1.0 / (1.0 + jnp.exp(-sm[:, SM_BG + h:SM_BG + h + 1]))
        st = s0_ref[h] * jnp.exp(g)
        k_col = jnp.broadcast_to(k, (d, d)).T
        q_col = jnp.broadcast_to(q, (d, d)).T
        kv = jnp.sum(k_col * st, axis=0, keepdims=True)
        delta = (v - kv) * beta
        st = st + k_col * delta
        s_ref[h] = st
        o = jnp.sum(q_col * st, axis=0, keepdims=True)
        o_ref[0, :, h * d:(h + 1) * d] = _gated_norm(o, gn, zg_ref[0][:, h * d:(h + 1) * d]).astype(BF16)


def _gdn_sample(a_log, dt_bias, qkv3, state_conv_l, conv_w, small3, zg3, gdn_norm, state_ssm_l):
    db = qkv3.shape[0]
    d = GDN_HEAD_DIM
    nh = N_GDN_HEADS
    row3 = lambda n: pl.BlockSpec((1, 1, n), lambda b: (b, 0, 0))
    return pl.pallas_call(
        _gdn_sample_body,
        grid=(db,),
        in_specs=[pl.BlockSpec(memory_space=pltpu.SMEM), pl.BlockSpec(memory_space=pltpu.SMEM),
                  row3(3 * D_GDN),
                  pl.BlockSpec((None, CONV_W - 1, 3 * D_GDN), lambda b: (b, 0, 0)),
                  pl.BlockSpec((CONV_W, 3 * D_GDN), lambda b: (0, 0)),
                  row3(LANES), row3(D_GDN),
                  pl.BlockSpec((1, LANES), lambda b: (0, 0)),
                  pl.BlockSpec((None, nh, d, d), lambda b: (b, 0, 0, 0))],
        out_specs=[row3(D_GDN),
                   pl.BlockSpec((None, nh, d, d), lambda b: (b, 0, 0, 0)),
                   pl.BlockSpec((None, CONV_W - 1, 3 * D_GDN), lambda b: (b, 0, 0))],
        out_shape=[jax.ShapeDtypeStruct((db, 1, D_GDN), BF16),
                   jax.ShapeDtypeStruct((db, nh, d, d), F32),
                   jax.ShapeDtypeStruct((db, CONV_W - 1, 3 * D_GDN), F32)],
        compiler_params=_cparams(("parallel",)),
        name="gdn_sample",
    )(a_log, dt_bias, qkv3, state_conv_l, conv_w, small3, zg3, gdn_norm.reshape(1, LANES), state_ssm_l)


def kernel(x_prompt, x_sample, cache_k, cache_v, cache_kidx, state_ssm, state_conv, page_table, norm_in, w_in,
           conv_w, a_log, dt_bias, gdn_norm, w_out, rel_table, norm_final):
    depth = w_in.shape[0]
    assert depth == 1, "single-layer model"
    batch, seq, d_model = x_prompt.shape
    db, dec_seq, _ = x_sample.shape
    assert dec_seq == 1 and seq % KT == 0 and seq % (GDN_C * GDN_CHUNKS_PER_ITER) == 0 and seq % PROJ_ROWS == 0
    assert (batch * seq) % (2 * PROJ_ROWS) == 0
    n_pool, page = cache_k.shape[1], cache_k.shape[2]
    n_pages = page_table.shape[1]
    past_len = n_pages * page
    assert page == LANES and n_pages % IDX_PAGES_PER_STEP == 0 and n_pages % ATT_PAGES_PER_STEP == 0

    lyr = 0
    w_pad = _prep_w_in(w_in[lyr])
    w_out_bf = w_out[lyr].astype(BF16)

    xp = x_prompt.reshape(batch * seq, d_model)
    q_blk, k2d, v2d, za, qi_blk, small, qkv2d, zg, k_t, v_t, ki_t = _inproj(
        xp, norm_in[lyr], w_pad, tm=PROJ_ROWS, blocked=True, seq=seq)
    att_g = _dsa_prompt_t(rel_table, q_blk, qi_blk, small, za, k2d, v2d, batch, seq)
    gdn_g, s_fin = _gdn_prompt(a_log[lyr], dt_bias[lyr], qkv2d, conv_w[lyr], small, zg, gdn_norm[lyr], batch, seq)
    y_prompt = _outproj(xp, att_g, gdn_g, w_out_bf, norm_final, tm=2 * PROJ_ROWS).reshape(batch, seq, d_model)
    k_prompt = jnp.transpose(k_t.reshape(batch, N_KV_HEADS, ATT_HEAD_DIM, seq), (0, 3, 1, 2))[None]
    v_prompt = jnp.transpose(v_t.reshape(batch, N_KV_HEADS, ATT_HEAD_DIM, seq), (0, 3, 1, 2))[None]
    kidx_prompt = jnp.transpose(ki_t, (0, 2, 1))[None]
    ssm_prompt = s_fin[None]
    conv_prompt = qkv2d.reshape(batch, seq, 3 * D_GDN)[:, seq - (CONV_W - 1):][None]

    xs = x_sample.reshape(db, d_model)
    q_s, k_s, v_s, za_s, qi_s, small_s, qkv_s, zg_s = _inproj(xs, norm_in[lyr], w_pad, tm=db, blocked=False)
    kidx_t = jnp.transpose(cache_kidx[lyr], (0, 2, 1))
    k_pages_t = jnp.transpose(cache_k[lyr], (0, 2, 3, 1)).reshape(n_pool, KV_DIM, page)
    v_pages_t = jnp.transpose(cache_v[lyr], (0, 2, 3, 1)).reshape(n_pool, KV_DIM, page)
    scores = _idx_scores(page_table, qi_s.reshape(db, N_IDX_HEADS, IDX_DIM),
                         small_s[:, SM_WI:SM_WI + N_IDX_HEADS].reshape(db, N_IDX_HEADS, 1),
                         kidx_t)
    topk = min(TOPK_MAX, (past_len + dec_seq) // 4)
    mask_t = _sample_select(jnp.transpose(scores, (1, 0, 2)), qi_s, small_s, topk)
    mask_pages = jnp.transpose(mask_t[:n_pages], (1, 0, 2))
    mask_new = mask_t[n_pages].reshape(db, 1, page)
    hpg = N_ATT_HEADS // N_KV_HEADS
    q8 = q_s.reshape(db, N_ATT_HEADS, ATT_HEAD_DIM).astype(F32)
    zq = jnp.zeros((db, hpg, ATT_HEAD_DIM), F32)
    q_lh = jnp.concatenate([jnp.concatenate([q8[:, :hpg], zq], axis=2),
                            jnp.concatenate([zq, q8[:, hpg:]], axis=2),
                            jnp.zeros((db, SAMPLE_Q_ROWS - N_ATT_HEADS, LANES), F32)], axis=1)
    rel_t = jnp.concatenate([rel_table.T, jnp.zeros((SAMPLE_Q_ROWS - N_ATT_HEADS, N_BUCKETS), F32)], axis=0)
    att_raw = _sample_attn(page_table, q_lh, mask_pages, mask_new, k_s.reshape(db, 1, KV_DIM),
                           v_s.reshape(db, 1, KV_DIM), rel_t,
                           k_pages_t, v_pages_t, past_len)
    att_s = att_raw[:, :N_ATT_HEADS, :ATT_HEAD_DIM].reshape(db, D_ATT)
    gdn_s, s_new, conv_new = _gdn_sample(a_log[lyr], dt_bias[lyr], qkv_s.reshape(db, 1, 3 * D_GDN), state_conv[lyr],
                                         conv_w[lyr], small_s.reshape(db, 1, LANES), zg_s.reshape(db, 1, D_GDN),
                                         gdn_norm[lyr], state_ssm[lyr])
    y_sample = _outproj(xs, att_s, gdn_s.reshape(db, D_GDN), w_out_bf, norm_final, tm=db,
                        za=za_s).reshape(db, 1, d_model)
    k_sample = k_s.reshape(1, db, 1, N_KV_HEADS, ATT_HEAD_DIM)
    v_sample = v_s.reshape(1, db, 1, N_KV_HEADS, ATT_HEAD_DIM)
    kidx_sample = small_s[:, :IDX_DIM].reshape(1, db, 1, IDX_DIM)

    return (y_prompt, y_sample, k_prompt, v_prompt, kidx_prompt, ssm_prompt, conv_prompt,
            k_sample, v_sample, kidx_sample, s_new[None], conv_new[None])
```

```python
import functools
import math

import numpy as np
import jax
import jax.numpy as jnp
from jax import lax
from jax.experimental import pallas as pl
from jax.experimental.pallas import tpu as pltpu

F32 = jnp.float32
BF16 = jnp.bfloat16
I32 = jnp.int32

N_ATT_HEADS = 8
ATT_HEAD_DIM = 64
N_KV_HEADS = 2
D_ATT = N_ATT_HEADS * ATT_HEAD_DIM
KV_DIM = N_KV_HEADS * ATT_HEAD_DIM
N_IDX_HEADS = 16
IDX_DIM = 64
TOPK_MAX = 256
N_GDN_HEADS = 4
GDN_HEAD_DIM = 128
D_GDN = N_GDN_HEADS * GDN_HEAD_DIM
CONV_W = 4
N_BUCKETS = 32
MAX_DISTANCE = 128
Q_BLOCK = 128
RMS_EPS = 1e-6
NEG_BIG = -1e30
PROJ_SIZES = (D_ATT, KV_DIM, KV_DIM, D_ATT, N_IDX_HEADS * IDX_DIM, IDX_DIM, N_IDX_HEADS,
              3 * D_GDN, D_GDN, N_GDN_HEADS, N_GDN_HEADS)

LANES = 128
SUBLANES = 8
VMEM_LIMIT = 56 * 1024 * 1024

OFF_Q = 0
OFF_K = OFF_Q + D_ATT
OFF_V = OFF_K + KV_DIM
OFF_ZA = OFF_V + KV_DIM
OFF_QI = OFF_ZA + D_ATT
OFF_SM = OFF_QI + N_IDX_HEADS * IDX_DIM
OFF_QKV = OFF_SM + LANES
OFF_ZG = OFF_QKV + 3 * D_GDN
D_PROJ_PAD = OFF_ZG + D_GDN
SM_WI = IDX_DIM
SM_AG = SM_WI + N_IDX_HEADS
SM_BG = SM_AG + N_GDN_HEADS

PROJ_ROWS = 512
GDN_C = 128
GDN_CHUNKS_PER_ITER = 4
KEY_NEG_BIG = int(np.array(NEG_BIG, np.float32).view(np.int32)) ^ 0x7FFFFFFF
INT_MIN = -2 ** 31


def _cparams(sem):
    return pltpu.CompilerParams(dimension_semantics=sem, vmem_limit_bytes=VMEM_LIMIT)


def _silu(x):
    return x * (1.0 / (1.0 + jnp.exp(-x)))


def _bdot(a, b):
    return jnp.dot(a.astype(BF16), b.astype(BF16), preferred_element_type=F32)


def _bdot_nt(a, b):
    return lax.dot_general(a.astype(BF16), b.astype(BF16), (((1,), (1,)), ((), ())),
                           preferred_element_type=F32)


def _sort_key(x):
    i = pltpu.bitcast(x, I32)
    return jnp.where(i < 0, i ^ 0x7FFFFFFF, i)


def _inproj_body(x_ref, g_ref, w_ref, q_ref, k_ref, v_ref, za_ref, qi_ref, sm_ref, qkv_ref, zg_ref, *t_refs,
                 blocked):
    x = x_ref[...]
    ms = jnp.mean(x * x, axis=-1, keepdims=True)
    h = ((x * lax.rsqrt(ms + RMS_EPS)) * g_ref[...]).astype(BF16)

    def mm(a, b):
        return jnp.dot(h, w_ref[:, a:b], preferred_element_type=F32)

    q = mm(OFF_Q, OFF_K) * (ATT_HEAD_DIM ** -0.5)
    qi = mm(OFF_QI, OFF_SM) * (IDX_DIM ** -0.5)
    if blocked:
        for r in range(x.shape[0] // Q_BLOCK):
            rs = slice(r * Q_BLOCK, (r + 1) * Q_BLOCK)
            for j in range(D_ATT // LANES):
                cs = slice(j * LANES, (j + 1) * LANES)
                q_ref[r, :, cs] = q[rs, cs].T.astype(BF16)
            for j in range(N_IDX_HEADS * IDX_DIM // LANES):
                cs = slice(j * LANES, (j + 1) * LANES)
                qi_ref[r, :, cs] = qi[rs, cs].T.astype(BF16)
    else:
        q_ref[...] = q.astype(BF16)
        qi_ref[...] = qi.astype(BF16)
    k = mm(OFF_K, OFF_V)
    v = mm(OFF_V, OFF_ZA)
    sm = mm(OFF_SM, OFF_QKV)
    k_ref[...] = k
    v_ref[...] = v
    sm_ref[...] = sm
    za_ref[...] = mm(OFF_ZA, OFF_QI)
    qkv_ref[...] = mm(OFF_QKV, OFF_ZG)
    zg_ref[...] = mm(OFF_ZG, D_PROJ_PAD)
    if t_refs:
        kt_ref, vt_ref, kit_ref = t_refs
        kt_ref[...] = k.T
        vt_ref[...] = v.T
        kit_ref[...] = sm.T[0:IDX_DIM, :]


def _prep_w_in(w):
    splits = np.cumsum(PROJ_SIZES)[:-1].tolist()
    q, k, v, z_a, qi, ki, wi, qkv, z_g, a_g, b_g = jnp.split(w, splits, axis=1)
    pad = jnp.zeros((w.shape[0], LANES - (SM_BG + N_GDN_HEADS)), w.dtype)
    small = jnp.concatenate([ki, wi, a_g, b_g, pad], axis=1)
    return jnp.concatenate([q, k, v, z_a, qi, small, qkv, z_g], axis=1).astype(BF16)


def _inproj(x2d, norm_g, w_pad, tm, blocked, seq=None):
    t, d = x2d.shape
    nblk = t // tm
    t_shapes, t_specs = [], []
    if blocked:
        spb = seq // tm
        for n in (KV_DIM, KV_DIM, IDX_DIM):
            t_shapes.append(jax.ShapeDtypeStruct((t // seq, n, seq), F32))
            t_specs.append(pl.BlockSpec((None, n, tm), lambda i: (i // spb, 0, i % spb)))
        rb = tm // Q_BLOCK
        q_shape = jax.ShapeDtypeStruct((t // Q_BLOCK, LANES, D_ATT), BF16)
        qi_shape = jax.ShapeDtypeStruct((t // Q_BLOCK, LANES, N_IDX_HEADS * IDX_DIM), BF16)
        q_spec = pl.BlockSpec((rb, LANES, D_ATT), lambda i: (i, 0, 0))
        qi_spec = pl.BlockSpec((rb, LANES, N_IDX_HEADS * IDX_DIM), lambda i: (i, 0, 0))
    else:
        q_shape = jax.ShapeDtypeStruct((t, D_ATT), BF16)
        qi_shape = jax.ShapeDtypeStruct((t, N_IDX_HEADS * IDX_DIM), BF16)
        q_spec = pl.BlockSpec((tm, D_ATT), lambda i: (i, 0))
        qi_spec = pl.BlockSpec((tm, N_IDX_HEADS * IDX_DIM), lambda i: (i, 0))

    def row(n):
        return pl.BlockSpec((tm, n), lambda i: (i, 0))

    return pl.pallas_call(
        functools.partial(_inproj_body, blocked=blocked),
        grid=(nblk,),
        in_specs=[row(d), pl.BlockSpec((1, d), lambda i: (0, 0)),
                  pl.BlockSpec((d, D_PROJ_PAD), lambda i: (0, 0))],
        out_specs=[q_spec, row(KV_DIM), row(KV_DIM), row(D_ATT), qi_spec, row(LANES), row(3 * D_GDN), row(D_GDN)]
                  + t_specs,
        out_shape=[q_shape, jax.ShapeDtypeStruct((t, KV_DIM), F32), jax.ShapeDtypeStruct((t, KV_DIM), F32),
                   jax.ShapeDtypeStruct((t, D_ATT), F32), qi_shape, jax.ShapeDtypeStruct((t, LANES), F32),
                   jax.ShapeDtypeStruct((t, 3 * D_GDN), F32), jax.ShapeDtypeStruct((t, D_GDN), F32)] + t_shapes,
        compiler_params=_cparams(("parallel",)),
        name="inproj_blocked" if blocked else "inproj_rows",
    )(x2d, norm_g.reshape(1, d), w_pad)


def _rel_bucket(dist):
    n = jnp.maximum(dist, 0)
    max_exact = N_BUCKETS // 2
    nf = jnp.maximum(n, 1).astype(F32)
    large = max_exact + (jnp.log(nf / max_exact) / math.log(MAX_DISTANCE / max_exact)
                         * (N_BUCKETS - max_exact)).astype(I32)
    large = jnp.minimum(large, N_BUCKETS - 1)
    return jnp.where(n < max_exact, n, large)


def _far_bucket_checked(first_far):
    d = np.arange(first_far, 1 << 16, dtype=np.float32)
    b = 16 + (np.log(d / 16) / math.log(MAX_DISTANCE / 16) * 16).astype(np.int32)
    assert int(b.min()) >= N_BUCKETS - 1
    return N_BUCKETS - 1


def _bit_search(count_ge, t0, cnt0, k, nbits):
    def body(step, carry):
        t, cnt_t = carry
        cand = t + jnp.left_shift(jnp.int32(1), nbits - 1 - step)
        cnt = count_ge(cand)
        accept = cnt >= k
        return jnp.where(accept, cand, t), jnp.where(accept, cnt, cnt_t)

    return lax.fori_loop(0, nbits, body, (t0, cnt0))


def _kth_key_search(count_ge, shape, k, total):
    return _bit_search(count_ge, jnp.full(shape, INT_MIN, I32), jnp.full(shape, total, F32), k, 32)


KT = 2 * LANES


def _fold8(x, op):
    binop = {jnp.sum: jnp.add, jnp.max: jnp.maximum}[op]
    r = x.reshape(x.shape[0] // SUBLANES, SUBLANES, x.shape[1])
    while r.shape[0] > 1:
        half = r.shape[0] // 2
        r = binop(r[:half], r[half:])
    return r[0]


def _dsa_prompt_t_body(rel_ref, q_ref, qi_ref, smq_ref, za_ref, k_ref, v_ref, sms_ref, o_ref,
                       kk_ref, vvt_ref, ki_ref, bias_ref, key_ref, mask_ref, lg_ref,
                       *, seq, topk):
    b = pl.program_id(0)
    i = pl.program_id(1)
    n_t = seq // KT
    hpg = N_ATT_HEADS // N_KV_HEADS
    far_bucket = _far_bucket_checked(MAX_DISTANCE + 1)
    sub1 = lax.broadcasted_iota(I32, (LANES, LANES), 0)
    lane1 = lax.broadcasted_iota(I32, (LANES, LANES), 1)

    @pl.when((b == 0) & (i == 0))
    def _():
        for dt in range(2):
            bucket = _rel_bucket(dt * LANES + lane1 - sub1)
            for h in range(N_ATT_HEADS):
                acc = jnp.zeros((LANES, LANES), F32)
                for bk in range(N_BUCKETS):
                    acc = jnp.where(bucket == bk, rel_ref[bk, h], acc)
                bias_ref[h, dt] = acc - rel_ref[far_bucket, h]

    @pl.when(i == 0)
    def _():
        lo = lax.broadcasted_iota(I32, (seq, LANES), 1) < ATT_HEAD_DIM
        kf = k_ref[...]
        g0 = jnp.where(lo, kf, 0.0)
        g1 = jnp.where(lo, 0.0, kf)
        kk_ref[0] = g0.astype(BF16)
        kk_ref[1] = pltpu.roll(g0, ATT_HEAD_DIM, 1).astype(BF16)
        kk_ref[2] = pltpu.roll(g1, ATT_HEAD_DIM, 1).astype(BF16)
        kk_ref[3] = g1.astype(BF16)
        c0 = jnp.where(lo, sms_ref[...], 0.0)
        ki_ref[0] = c0.astype(BF16)
        ki_ref[1] = pltpu.roll(c0, IDX_DIM, 1).astype(BF16)
        lo_t = lax.broadcasted_iota(I32, (KT, LANES), 1) < ATT_HEAD_DIM
        for t in range(n_t):
            vf = v_ref[t * KT:(t + 1) * KT, :]
            w0 = jnp.where(lo_t, vf, 0.0)
            w1 = jnp.where(lo_t, 0.0, vf)
            vvt_ref[0, t] = w0.T.astype(BF16)
            vvt_ref[1, t] = pltpu.roll(w0, ATT_HEAD_DIM, 1).T.astype(BF16)
            vvt_ref[2, t] = pltpu.roll(w1, ATT_HEAD_DIM, 1).T.astype(BF16)
            vvt_ref[3, t] = w1.T.astype(BF16)

    n_ip = N_IDX_HEADS * IDX_DIM // LANES
    sm_t = smq_ref[...].T
    wrow = [sm_t[SM_WI + h:SM_WI + h + 1, :] * (N_IDX_HEADS ** -0.5) for h in range(N_IDX_HEADS)]

    n_live = i // 2 + 1
    n_dead = ((n_t - n_live) * KT).astype(F32)
    kidx = lax.broadcasted_iota(I32, (KT, LANES), 0)
    qpos = i * Q_BLOCK + lax.broadcasted_iota(I32, (KT, LANES), 1)

    def causal_of(t):
        return (t * KT + kidx) <= qpos

    def tile_rows(t):
        return pl.ds(pl.multiple_of(t * KT, KT), KT)

    def score_tile(t, carry):
        acc = jnp.zeros((KT, LANES), F32)
        for half in range(2):
            s_all = jnp.dot(ki_ref[half, tile_rows(t), :], qi_ref[...], preferred_element_type=F32)
            for j in range(n_ip):
                acc = acc + jnp.maximum(s_all[:, j * LANES:(j + 1) * LANES], 0.0) * wrow[2 * j + half]
        acc = acc + 0.0
        key_ref[t] = _sort_key(jnp.where(causal_of(t), acc, NEG_BIG))
        return carry

    def pair_loop(body, init):
        return lax.fori_loop(0, (n_live + 1) // 2, lambda p, c: body(2 * p + 1, body(2 * p, c)), init)

    pair_loop(score_tile, 0)

    @pl.when(n_live % 2 == 1)
    def _():
        key_ref[n_live] = jnp.full((KT, LANES), INT_MIN, I32)
        mask_ref[n_live] = jnp.full((KT, LANES), NEG_BIG, F32)

    slabs_per_tile = KT // LANES

    def causal_slab(t, r):
        return (t * KT + r * LANES + sub1) <= (i * Q_BLOCK + lane1)

    def count_where(pred):
        def body(t, acc):
            kt = key_ref[t]
            for r in range(slabs_per_tile):
                acc = jnp.where(pred(kt[r * LANES:(r + 1) * LANES], t, r), acc + 1.0, acc)
            return acc
        acc = pair_loop(body, jnp.zeros((LANES, LANES), F32))
        return jnp.sum(_fold8(acc, jnp.sum), axis=0, keepdims=True)

    def count_ge(cand):
        return count_where(lambda kt, t, r: kt >= cand) + jnp.where(cand <= KEY_NEG_BIG, n_dead, 0.0)

    def write_threshold_masks(thr):
        def body(t, carry):
            mask_ref[t] = jnp.where((key_ref[t] >= thr) & causal_of(t), 0.0, NEG_BIG)
            return carry
        lax.fori_loop(0, n_live, body, 0)

    few_keys = (i + 1) * Q_BLOCK <= topk

    @pl.when(few_keys)
    def _():
        write_threshold_masks(jnp.full((1, LANES), KEY_NEG_BIG, I32))

    @pl.when(jnp.logical_not(few_keys))
    def _():
        thr, cnt_thr = _kth_key_search(count_ge, (1, LANES), float(topk), float(seq))
        boundary_dup = jnp.max(jnp.where(cnt_thr > float(topk), 1.0, 0.0)) > 0.0

        @pl.when(jnp.logical_not(boundary_dup))
        def _():
            write_threshold_masks(thr)

        @pl.when(boundary_dup)
        def _():
            cnt_gt = count_where(lambda kt, t, r: kt > thr) + jnp.where(thr < KEY_NEG_BIG, n_dead, 0.0)
            need = float(topk) - cnt_gt
            cnt_ceq = count_where(lambda kt, t, r: (kt == thr) & causal_slab(t, r))
            any_tie = jnp.max(jnp.where(cnt_ceq > need, 1.0, 0.0)) > 0.0

            @pl.when(jnp.logical_not(any_tie))
            def _():
                write_threshold_masks(thr)

            @pl.when(any_tie)
            def _():
                rk = lax.broadcasted_iota(I32, (KT, KT), 0)
                ck = lax.broadcasted_iota(I32, (KT, KT), 1)
                lower = (ck <= rk).astype(BF16)

                def body(t, offset):
                    kt = key_ref[t]
                    cz = causal_of(t)
                    eq = ((kt == thr) & cz).astype(F32)
                    prefix = jnp.dot(lower, eq.astype(BF16), preferred_element_type=F32) + offset
                    sel = ((kt > thr) & cz) | ((eq > 0.0) & (prefix <= need))
                    mask_ref[t] = jnp.where(sel, 0.0, NEG_BIG)
                    return offset + jnp.sum(_fold8(eq, jnp.sum), axis=0, keepdims=True)

                lax.fori_loop(0, n_live, body, jnp.zeros((1, LANES), F32))


    za = za_ref[...]
    near_lo = jnp.maximum(i - 1, 0) // 2
    neg8 = jnp.full((SUBLANES, LANES), NEG_BIG, F32)
    zero8 = jnp.zeros((SUBLANES, LANES), F32)
    def logit_tile(t, mx):
        mx = list(mx)
        far = t < near_lo
        for g in range(N_KV_HEADS):
            qt_g = q_ref[:, g * 2 * LANES:(g + 1) * 2 * LANES]
            for half in range(2):
                l_all = jnp.dot(kk_ref[2 * g + half, tile_rows(t), :], qt_g, preferred_element_type=F32)
                for n in range(2):
                    h = g * hpg + 2 * n + half
                    l = l_all[:, n * LANES:(n + 1) * LANES] + mask_ref[t]
                    lg_ref[h, t] = l
                    mx[h] = jnp.maximum(mx[h], jnp.where(far, _fold8(l, jnp.max), NEG_BIG))
        return tuple(mx)

    mx = pair_loop(logit_tile, (neg8,) * N_ATT_HEADS)
    m_all = []
    for h in range(N_ATT_HEADS):
        r0 = pl.ds(pl.multiple_of((i % 2) * LANES, LANES), LANES)
        lg_ref[h, i // 2, r0, :] = lg_ref[h, i // 2, r0, :] + bias_ref[h, 0]
        im1 = jnp.maximum(i - 1, 0)
        r1 = pl.ds(pl.multiple_of((im1 % 2) * LANES, LANES), LANES)
        lg_ref[h, im1 // 2, r1, :] = lg_ref[h, im1 // 2, r1, :] + jnp.where(i > 0, bias_ref[h, 1], 0.0)
        m8 = jnp.maximum(mx[h], jnp.maximum(_fold8(lg_ref[h, near_lo], jnp.max),
                                            _fold8(lg_ref[h, i // 2], jnp.max)))
        m_all.append(jnp.max(m8, axis=0, keepdims=True))

    for g in range(N_KV_HEADS):
        m_row = m_all[g * hpg:(g + 1) * hpg]

        def pv_tile(t, carry, g=g, m_row=m_row):
            ssum = list(carry[:hpg])
            acc = list(carry[hpg:])
            for n in range(2):
                for half in range(2):
                    hl = 2 * n + half
                    p = jnp.exp(lg_ref[g * hpg + hl, t] - m_row[hl])
                    ssum[hl] = ssum[hl] + _fold8(p, jnp.sum)
                    acc[n] = acc[n] + jnp.dot(vvt_ref[2 * g + half, t], p.astype(BF16),
                                              preferred_element_type=F32)
            return tuple(ssum) + tuple(acc)

        zacc = jnp.zeros((LANES, LANES), F32)
        res = pair_loop(pv_tile, (zero8,) * hpg + (zacc, zacc))
        for n in range(2):
            l_lo = jnp.sum(res[2 * n], axis=0, keepdims=True)
            l_hi = jnp.sum(res[2 * n + 1], axis=0, keepdims=True)
            inv = jnp.where(sub1 < ATT_HEAD_DIM, 1.0 / l_lo, 1.0 / l_hi)
            o_pair = (res[hpg + n] * inv).T
            cs = slice((2 * g + n) * LANES, (2 * g + n + 1) * LANES)
            o_ref[:, cs] = (o_pair * _silu(za[:, cs])).astype(BF16)


def _dsa_prompt_t(rel_table, q_blk, qi_blk, small, za, k2d, v2d, batch, seq):
    assert N_KV_HEADS == 2 and N_ATT_HEADS // N_KV_HEADS == 4 and seq % KT == 0
    nb = seq // Q_BLOCK
    n_t = seq // KT
    topk = min(TOPK_MAX, seq // 4)
    assert seq >= 2 * topk
    t = batch * seq
    return pl.pallas_call(
        functools.partial(_dsa_prompt_t_body, seq=seq, topk=topk),
        grid=(batch, nb),
        in_specs=[
            pl.BlockSpec(memory_space=pltpu.SMEM),
            pl.BlockSpec((None, LANES, D_ATT), lambda b, i: (b * nb + i, 0, 0)),
            pl.BlockSpec((None, LANES, N_IDX_HEADS * IDX_DIM), lambda b, i: (b * nb + i, 0, 0)),
            pl.BlockSpec((Q_BLOCK, LANES), lambda b, i: (b * nb + i, 0)),
            pl.BlockSpec((Q_BLOCK, D_ATT), lambda b, i: (b * nb + i, 0)),
            pl.BlockSpec((seq, KV_DIM), lambda b, i: (b, 0)),
            pl.BlockSpec((seq, KV_DIM), lambda b, i: (b, 0)),
            pl.BlockSpec((seq, LANES), lambda b, i: (b, 0)),
        ],
        out_specs=pl.BlockSpec((Q_BLOCK, D_ATT), lambda b, i: (b * nb + i, 0)),
        out_shape=jax.ShapeDtypeStruct((t, D_ATT), BF16),
        scratch_shapes=[
            pltpu.VMEM((2 * N_KV_HEADS, seq, LANES), BF16),
            pltpu.VMEM((2 * N_KV_HEADS, n_t, LANES, KT), BF16),
            pltpu.VMEM((2, seq, LANES), BF16),
            pltpu.VMEM((N_ATT_HEADS, 2, LANES, LANES), F32),
            pltpu.VMEM((n_t, KT, LANES), I32),
            pltpu.VMEM((n_t, KT, LANES), F32),
            pltpu.VMEM((N_ATT_HEADS, n_t, KT, LANES), F32),
        ],
        compiler_params=_cparams(("arbitrary", "arbitrary")),
        name="dsa_prompt",
    )(rel_table, q_blk, qi_blk, small, za, k2d, v2d, small)


def _softplus(x):
    return jnp.maximum(x, 0.0) + jnp.log1p(jnp.exp(-jnp.abs(x)))


def _lane_pick(x, idx):
    lane = lax.broadcasted_iota(I32, x.shape, 1)
    return jnp.sum(jnp.where(lane == idx, x, 0.0), axis=1, keepdims=True)


def _gated_norm(o, gn, z):
    y = o * lax.rsqrt(jnp.mean(o * o, axis=-1, keepdims=True) + RMS_EPS)
    return (y * gn) * _silu(z)


def _gdn_prompt_body(alog_ref, dtb_ref, xq_ref, xk_ref, xv_ref, wq_ref, wk_ref, wv_ref, sm_ref, zg_ref, gn_ref,
                     o_ref, sfin_ref, xs_ref, q_s, k_s, v_s, g_s, b_s, *, seq, hp):
    h0 = pl.program_id(1) * hp
    hist = SUBLANES
    slabs = [slice(hh * LANES, (hh + 1) * LANES) for hh in range(hp)]

    def conv_into(x_ref, w_ref, dst, post):
        xs_ref[0:hist, :] = jnp.zeros((hist, hp * LANES), F32)
        xs_ref[hist:hist + seq, :] = x_ref[...]
        base = hist - (CONV_W - 1)
        for hs in slabs:
            acc = xs_ref[base:base + seq, hs] * w_ref[0:1, hs]
            for j in range(1, CONV_W):
                acc = acc + xs_ref[base + j:base + j + seq, hs] * w_ref[j:j + 1, hs]
            dst[:, hs] = post(_silu(acc))

    def l2n(x):
        return x * lax.rsqrt(jnp.sum(x * x, axis=-1, keepdims=True) + 1e-6)

    conv_into(xq_ref, wq_ref, q_s, lambda x: l2n(x) * (GDN_HEAD_DIM ** -0.5))
    conv_into(xk_ref, wk_ref, k_s, l2n)
    conv_into(xv_ref, wv_ref, v_s, lambda x: x)

    lane_row = lax.broadcasted_iota(I32, (1, LANES), 1)
    alog_row = jnp.zeros((1, LANES), F32)
    dtb_row = jnp.zeros((1, LANES), F32)
    for hh in range(hp):
        alog_row = jnp.where(lane_row == SM_AG + h0 + hh, alog_ref[h0 + hh], alog_row)
        dtb_row = jnp.where(lane_row == SM_AG + h0 + hh, dtb_ref[h0 + hh], dtb_row)
    sm = sm_ref[...]
    g_all = -jnp.exp(alog_row) * _softplus(sm + dtb_row)
    beta_all = 1.0 / (1.0 + jnp.exp(-sm))
    for hh, hs in enumerate(slabs):
        g_s[:, hs] = jnp.broadcast_to(_lane_pick(g_all, SM_AG + h0 + hh), (seq, LANES))
        b_s[:, hs] = jnp.broadcast_to(_lane_pick(beta_all, SM_BG + h0 + hh), (seq, LANES))

    c = GDN_C
    ri = lax.broadcasted_iota(I32, (c, c), 0)
    ci = lax.broadcasted_iota(I32, (c, c), 1)
    tril = ri >= ci
    strict = ri > ci
    tril_f = tril.astype(F32)
    eye = (ri == ci).astype(F32)
    gn = gn_ref[...]
    off_masks = []
    for lg in range(int(math.log2(c))):
        same_pair = (ri >> (lg + 1)) == (ci >> (lg + 1))
        off_masks.append(same_pair & (((ri >> lg) & 1) == 1) & (((ci >> lg) & 1) == 0))

    tril_b = tril.astype(BF16)

    def cumsum_rows(g):
        hi = g.astype(BF16)
        r1 = g - hi.astype(F32)
        mid = r1.astype(BF16)
        lo = (r1 - mid.astype(F32)).astype(BF16)
        return sum(jnp.dot(tril_b, piece, preferred_element_type=F32) for piece in (hi, mid, lo))

    cpi = GDN_CHUNKS_PER_ITER

    def rows_of(n):
        return pl.ds(pl.multiple_of(n * c, c), c)

    def local_phase(items):
        ids = range(len(items))
        q = [q_s[rows_of(n), hs] for n, hs in items]
        k = [k_s[rows_of(n), hs] for n, hs in items]
        v = [v_s[rows_of(n), hs] for n, hs in items]
        bb = [b_s[rows_of(n), hs] for n, hs in items]
        gcum = [cumsum_rows(g_s[rows_of(n), hs]) for n, hs in items]
        gcum_row = [g.T for g in gcum]
        decay = [jnp.where(tril, jnp.exp(jnp.where(tril, gcum[e] - gcum_row[e], 0.0)), 0.0) for e in ids]
        eg = [jnp.exp(g) for g in gcum]
        kb = [k[e] * bb[e] for e in ids]
        vb = [v[e] * bb[e] for e in ids]
        kq = [_bdot_nt(jnp.concatenate([kb[e], q[e]], axis=0), k[e]) for e in ids]
        a_mat = [jnp.where(strict, kq[e][:c] * decay[e], 0.0) for e in ids]
        attn = [kq[e][c:] * decay[e] for e in ids]
        x = [eye - jnp.where(off_masks[0], a_mat[e], 0.0) for e in ids]
        for om in off_masks[1:]:
            inner = [_bdot(jnp.where(om, a_mat[e], 0.0), x[e]) for e in ids]
            x = [x[e] - _bdot(x[e], inner[e]) for e in ids]
        uw = [_bdot(x[e], jnp.concatenate([vb[e], kb[e] * eg[e]], axis=1)) for e in ids]
        g_last = [g[c - 1:c, :] for g in gcum]
        k_dec_t = [(k[e] * jnp.exp(g_last[e] - gcum[e])).T for e in ids]
        wq = [jnp.concatenate([uw[e][:, GDN_HEAD_DIM:], q[e] * eg[e]], axis=0) for e in ids]
        ak = [jnp.concatenate([attn[e], k_dec_t[e]], axis=0) for e in ids]
        return [dict(u=uw[e][:, :GDN_HEAD_DIM], wq=wq[e], ak=ak[e], s_dec=jnp.exp(g_last[e])) for e in ids]

    def state_phase(n, loc, states):
        heads = range(hp)
        wq_s = [_bdot(loc[h]["wq"], states[h]) for h in heads]
        v_new = [loc[h]["u"] - wq_s[h][:c] for h in heads]
        ak_v = [_bdot(loc[h]["ak"], v_new[h]) for h in heads]
        for h, hs in enumerate(slabs):
            o = wq_s[h][c:] + ak_v[h][:c]
            o_ref[rows_of(n), hs] = _gated_norm(o, gn, zg_ref[rows_of(n), hs]).astype(BF16)
        return tuple(states[h] * loc[h]["s_dec"] + ak_v[h][c:] for h in heads)

    def chunk_group(p, states):
        ns = [p * cpi + r for r in range(cpi)]
        loc = local_phase([(n, hs) for n in ns for hs in slabs])
        for r, n in enumerate(ns):
            states = state_phase(n, loc[r * hp:(r + 1) * hp], states)
        return states

    zero_state = jnp.zeros((GDN_HEAD_DIM, GDN_HEAD_DIM), F32)
    finals = lax.fori_loop(0, seq // (c * cpi), chunk_group, (zero_state,) * hp)
    for hh in range(hp):
        sfin_ref[hh] = finals[hh]


GDN_HEADS_PER_STEP = 4


def _gdn_prompt(a_log, dt_bias, qkv2d, conv_w, small, zg, gdn_norm, batch, seq):
    nh = N_GDN_HEADS
    hp = GDN_HEADS_PER_STEP
    ng = nh // hp
    t = batch * seq
    w = hp * LANES
    once = pl.Buffered(1)
    blk = lambda off: pl.BlockSpec((seq, w), lambda b, j: (b, off + j), pipeline_mode=once)
    wblk = lambda off: pl.BlockSpec((CONV_W, w), lambda b, j: (0, off + j))
    return pl.pallas_call(
        functools.partial(_gdn_prompt_body, seq=seq, hp=hp),
        grid=(batch, ng),
        in_specs=[pl.BlockSpec(memory_space=pltpu.SMEM), pl.BlockSpec(memory_space=pltpu.SMEM),
                  blk(0), blk(ng), blk(2 * ng), wblk(0), wblk(ng), wblk(2 * ng),
                  pl.BlockSpec((seq, LANES), lambda b, j: (b, 0)),
                  pl.BlockSpec((seq, w), lambda b, j: (b, j), pipeline_mode=once),
                  pl.BlockSpec((1, LANES), lambda b, j: (0, 0))],
        out_specs=[pl.BlockSpec((seq, w), lambda b, j: (b, j)),
                   pl.BlockSpec((None, hp, GDN_HEAD_DIM, GDN_HEAD_DIM), lambda b, j: (b, j, 0, 0))],
        out_shape=[jax.ShapeDtypeStruct((t, D_GDN), BF16),
                   jax.ShapeDtypeStruct((batch, nh, GDN_HEAD_DIM, GDN_HEAD_DIM), F32)],
        scratch_shapes=[pltpu.VMEM((seq + 2 * SUBLANES, w), F32)] + [pltpu.VMEM((seq, w), F32)] * 5,
        compiler_params=_cparams(("parallel", "arbitrary")),
        name="gdn_prompt",
    )(a_log, dt_bias, qkv2d, qkv2d, qkv2d, conv_w, conv_w, conv_w, small, zg, gdn_norm.reshape(1, LANES))


def _outproj_body(*refs, gate_att):
    if gate_att:
        x_ref, att_ref, za_ref, gdn_ref, w_ref, g_ref, y_ref = refs
        att = (att_ref[...] * _silu(za_ref[...])).astype(BF16)
    else:
        x_ref, att_ref, gdn_ref, w_ref, g_ref, y_ref = refs
        att = att_ref[...]
    y = x_ref[...] + jnp.dot(att, w_ref[0:D_ATT, :], preferred_element_type=F32) \
        + jnp.dot(gdn_ref[...], w_ref[D_ATT:D_ATT + D_GDN, :], preferred_element_type=F32)
    y = y * lax.rsqrt(jnp.mean(y * y, axis=-1, keepdims=True) + RMS_EPS)
    y_ref[...] = y * g_ref[...]


def _outproj(x2d, att, gdn, w_bf, norm_g, tm, za=None):
    t, d = x2d.shape
    row = lambda n: pl.BlockSpec((tm, n), lambda i: (i, 0))
    full = lambda a, b: pl.BlockSpec((a, b), lambda i: (0, 0))
    ins = [x2d, att] + ([za] if za is not None else []) + [gdn, w_bf, norm_g.reshape(1, d)]
    specs = [row(d), row(D_ATT)] + ([row(D_ATT)] if za is not None else []) + \
            [row(D_GDN), full(D_ATT + D_GDN, d), full(1, d)]
    return pl.pallas_call(
        functools.partial(_outproj_body, gate_att=za is not None),
        grid=(t // tm,),
        in_specs=specs,
        out_specs=row(d),
        out_shape=jax.ShapeDtypeStruct((t, d), F32),
        compiler_params=_cparams(("parallel",)),
        name="outproj_gated" if za is not None else "outproj",
    )(*ins)


IDX_PAGES_PER_STEP = 64
ATT_PAGES_PER_STEP = 32
SAMPLE_Q_ROWS = 16


def _idx_scores_body(pt_ref, qi_ref, wi_ref, *rest):
    page_refs = rest[:IDX_PAGES_PER_STEP]
    o_ref = rest[IDX_PAGES_PER_STEP]
    qi = qi_ref[0]
    wi = wi_ref[0] * (N_IDX_HEADS ** -0.5)
    for j, pr in enumerate(page_refs):
        s = _bdot(qi, pr[...])
        sc = jnp.sum(jnp.maximum(s, 0.0) * wi, axis=0, keepdims=True)
        o_ref[0, j:j + 1, :] = sc + 0.0


def _idx_scores(page_table, qi3, wi3, kidx_pages):
    db, n_pages = page_table.shape
    page = kidx_pages.shape[2]
    pg = IDX_PAGES_PER_STEP

    def page_spec(j):
        return pl.BlockSpec((None, IDX_DIM, page), lambda b, s, pt: (pt[b, s * pg + j], 0, 0))

    grid_spec = pltpu.PrefetchScalarGridSpec(
        num_scalar_prefetch=1,
        grid=(db, n_pages // pg),
        in_specs=[pl.BlockSpec((1, N_IDX_HEADS, IDX_DIM), lambda b, s, pt: (b, 0, 0)),
                  pl.BlockSpec((1, N_IDX_HEADS, 1), lambda b, s, pt: (b, 0, 0))]
                 + [page_spec(j) for j in range(pg)],
        out_specs=pl.BlockSpec((1, pg, page), lambda b, s, pt: (b, s, 0)),
    )
    return pl.pallas_call(
        _idx_scores_body,
        grid_spec=grid_spec,
        out_shape=jax.ShapeDtypeStruct((db, n_pages, page), F32),
        compiler_params=_cparams(("parallel", "arbitrary")),
        name="sample_idx_scores",
    )(page_table, qi3, wi3, *([kidx_pages] * pg))


def _sample_select_body(sc_ref, qi_ref, sm_ref, mask_ref, key_ref, *, n_tiles, topk):
    rows = sc_ref.shape[1]
    lane = lax.broadcasted_iota(I32, (rows, LANES), 1)

    def fill(t, carry):
        key_ref[t] = _sort_key(sc_ref[t])
        return carry

    lax.fori_loop(0, n_tiles, fill, 0)
    sm = sm_ref[...]
    ki_new = sm[:, 0:IDX_DIM].astype(BF16).astype(F32)
    acc = jnp.zeros((rows, 1), F32)
    for hd in range(N_IDX_HEADS):
        qh = qi_ref[:, hd * IDX_DIM:(hd + 1) * IDX_DIM].astype(F32)
        s = jnp.sum(qh * ki_new, axis=1, keepdims=True)
        acc = acc + jnp.maximum(s, 0.0) * (sm[:, SM_WI + hd:SM_WI + hd + 1] * (N_IDX_HEADS ** -0.5))
    sc_new = jnp.broadcast_to(acc + 0.0, (rows, LANES))
    key_ref[n_tiles] = jnp.where(lane == 0, _sort_key(sc_new), INT_MIN)

    def count_where(pred):
        def body(t, a):
            return a + pred(key_ref[t]).astype(F32)
        a = lax.fori_loop(0, n_tiles + 1, body, jnp.zeros((rows, LANES), F32))
        return jnp.sum(a, axis=1, keepdims=True)

    thr, _ = _kth_key_search(lambda cand: count_where(lambda kt: kt >= cand), (rows, 1), float(topk),
                             float(n_tiles * LANES + 1))
    need = float(topk) - count_where(lambda kt: kt > thr)
    cnt_eq = count_where(lambda kt: kt == thr)
    any_tie = jnp.max(jnp.where(cnt_eq > need, 1.0, 0.0)) > 0.0

    @pl.when(jnp.logical_not(any_tie))
    def _():
        def body(t, carry):
            mask_ref[t] = jnp.where(key_ref[t] >= thr, 0.0, NEG_BIG)
            return carry
        lax.fori_loop(0, n_tiles + 1, body, 0)

    @pl.when(any_tie)
    def _():
        sub = lax.broadcasted_iota(I32, (LANES, LANES), 0)
        lane2 = lax.broadcasted_iota(I32, (LANES, LANES), 1)
        upper = (sub <= lane2).astype(BF16)
        ones = jnp.ones((LANES, LANES), BF16)

        def body(t, offset):
            kt = key_ref[t]
            eq = (kt == thr).astype(BF16)
            prefix = jnp.dot(eq, upper, preferred_element_type=F32) + offset
            sel = (kt > thr) | ((eq > 0) & (prefix <= need))
            mask_ref[t] = jnp.where(sel, 0.0, NEG_BIG)
            return offset + jnp.dot(eq, ones, preferred_element_type=F32)

        lax.fori_loop(0, n_tiles + 1, body, jnp.zeros((rows, LANES), F32))


def _sample_select(scores_t, qi2d, small, topk):
    n_tiles, db, page = scores_t.shape
    vm = pl.BlockSpec(memory_space=pltpu.VMEM)
    return pl.pallas_call(
        functools.partial(_sample_select_body, n_tiles=n_tiles, topk=topk),
        in_specs=[vm, vm, vm],
        out_specs=vm,
        out_shape=jax.ShapeDtypeStruct((n_tiles + 1, db, page), F32),
        scratch_shapes=[pltpu.VMEM((n_tiles + 1, db, page), I32)],
        compiler_params=pltpu.CompilerParams(vmem_limit_bytes=VMEM_LIMIT),
        name="sample_select",
    )(scores_t, qi2d, small)


def _sample_attn_body(pt_ref, q_ref, mask_ref, mnew_ref, knew_ref, vnew_ref, relt_ref, *rest, past_len, page):
    pg = ATT_PAGES_PER_STEP
    k_refs = rest[:pg]
    v_refs = rest[pg:2 * pg]
    o_ref, m_s, l_s, acc_s = rest[2 * pg:]
    s = pl.program_id(1)
    n_steps = pl.num_programs(1)
    nh = SAMPLE_Q_ROWS
    far_bucket = _far_bucket_checked(MAX_DISTANCE + 1)

    @pl.when(s == 0)
    def _():
        m_s[...] = jnp.full(m_s.shape, NEG_BIG, F32)
        l_s[...] = jnp.zeros(l_s.shape, F32)
        acc_s[...] = jnp.zeros(acc_s.shape, F32)

    relt = relt_ref[...]
    c_far = relt[:, far_bucket:far_bucket + 1]

    def bias_of(dist):
        bucket = _rel_bucket(dist)
        acc = jnp.zeros((nh, dist.shape[1]), F32)
        for bk in range(N_BUCKETS):
            acc = jnp.where(bucket == bk, relt[:, bk:bk + 1], acc)
        return acc - c_far

    q = q_ref[0].astype(BF16)
    logits = []
    for j in range(pg):
        l = _bdot(q, k_refs[j][...]) + mask_ref[0, j:j + 1, :]
        logits.append(l)
    logits = jnp.concatenate(logits, axis=1)

    kpos = (s * pg) * page + lax.broadcasted_iota(I32, (1, pg * page), 1)
    near = past_len - ((s + 1) * pg * page - 1) <= MAX_DISTANCE
    logits = logits + lax.cond(near, lambda: bias_of(past_len - kpos),
                               lambda: jnp.zeros((nh, pg * page), F32))

    m_old = m_s[...]
    m_new = jnp.maximum(m_old, jnp.max(logits, axis=1, keepdims=True))
    alpha = jnp.exp(m_old - m_new)
    p = jnp.exp(logits - m_new)
    l_new = l_s[...] * alpha + jnp.sum(p, axis=1, keepdims=True)
    acc = acc_s[...] * alpha
    for j in range(pg):
        acc = acc + _bdot_nt(p[:, j * page:(j + 1) * page], v_refs[j][...])
    m_s[...] = m_new
    l_s[...] = l_new
    acc_s[...] = acc

    @pl.when(s == n_steps - 1)
    def _():
        kn = knew_ref[0].astype(BF16).astype(F32)
        vn = vnew_ref[0].astype(BF16).astype(F32)
        ln = jnp.sum(q.astype(F32) * kn, axis=1, keepdims=True) + bias_of(jnp.zeros((1, 1), I32)) \
            + mnew_ref[0][:, 0:1]
        m_fin = jnp.maximum(m_new, ln)
        a2 = jnp.exp(m_new - m_fin)
        pn = jnp.exp(ln - m_fin)
        l_fin = l_new * a2 + pn
        res = (acc * a2 + pn.astype(BF16).astype(F32) * vn) / l_fin
        row = lax.broadcasted_iota(I32, res.shape, 0)
        hpg = N_ATT_HEADS // N_KV_HEADS
        o_ref[0] = jnp.where((row >= hpg) & (row < 2 * hpg), pltpu.roll(res, ATT_HEAD_DIM, 1), res)


def _sample_attn(page_table, q_lh, mask_pages, mask_new, k_new, v_new, rel_t, k_pages, v_pages, past_len):
    db, n_pages = page_table.shape
    page = k_pages.shape[2]
    pg = ATT_PAGES_PER_STEP

    def page_spec(j):
        return pl.BlockSpec((None, KV_DIM, page), lambda b, s, pt: (pt[b, s * pg + j], 0, 0))

    row3 = lambda n: pl.BlockSpec((1, 1, n), lambda b, s, pt: (b, 0, 0))
    grid_spec = pltpu.PrefetchScalarGridSpec(
        num_scalar_prefetch=1,
        grid=(db, n_pages // pg),
        in_specs=[pl.BlockSpec((1, SAMPLE_Q_ROWS, LANES), lambda b, s, pt: (b, 0, 0)),
                  pl.BlockSpec((1, pg, page), lambda b, s, pt: (b, s, 0)),
                  row3(LANES), row3(KV_DIM), row3(KV_DIM),
                  pl.BlockSpec((SAMPLE_Q_ROWS, N_BUCKETS), lambda b, s, pt: (0, 0))]
                 + [page_spec(j) for j in range(pg)] * 2,
        out_specs=pl.BlockSpec((1, SAMPLE_Q_ROWS, LANES), lambda b, s, pt: (b, 0, 0)),
        scratch_shapes=[pltpu.VMEM((SAMPLE_Q_ROWS, 1), F32), pltpu.VMEM((SAMPLE_Q_ROWS, 1), F32),
                        pltpu.VMEM((SAMPLE_Q_ROWS, LANES), F32)],
    )
    return pl.pallas_call(
        functools.partial(_sample_attn_body, past_len=past_len, page=page),
        grid_spec=grid_spec,
        out_shape=jax.ShapeDtypeStruct((db, SAMPLE_Q_ROWS, LANES), F32),
        compiler_params=_cparams(("parallel", "arbitrary")),
        name="sample_attn",
    )(page_table, q_lh, mask_pages, mask_new, k_new, v_new, rel_t, *([k_pages] * pg), *([v_pages] * pg))


def _gdn_sample_body(alog_ref, dtb_ref, x_ref, cst_ref, w_ref, sm_ref, zg_ref, gn_ref, s0_ref,
                     o_ref, s_ref, cnew_ref):
    x = x_ref[0]
    cst = cst_ref[...]
    w = w_ref[...]
    acc = cst[0:1, :] * w[0:1, :]
    for j in range(1, CONV_W - 1):
        acc = acc + cst[j:j + 1, :] * w[j:j + 1, :]
    acc = acc + x * w[CONV_W - 1:CONV_W, :]
    xc = _silu(acc)
    cnew_ref[0:CONV_W - 2, :] = cst[1:CONV_W - 1, :]
    cnew_ref[CONV_W - 2:CONV_W - 1, :] = x
    sm = sm_ref[0]
    gn = gn_ref[...]
    d = GDN_HEAD_DIM
    for h in range(N_GDN_HEADS):
        q = xc[:, h * d:(h + 1) * d]
        k = xc[:, D_GDN + h * d:D_GDN + (h + 1) * d]
        v = xc[:, 2 * D_GDN + h * d:2 * D_GDN + (h + 1) * d]
        q = q * lax.rsqrt(jnp.sum(q * q, axis=-1, keepdims=True) + 1e-6) * (d ** -0.5)
        k = k * lax.rsqrt(jnp.sum(k * k, axis=-1, keepdims=True) + 1e-6)
        a_neg = -jnp.exp(jnp.zeros((1, 1), F32) + alog_ref[h])
        g = a_neg * _softplus(sm[:, SM_AG + h:SM_AG + h + 1] + dtb_ref[h])
        beta = 1.0 / (1.0 + jnp.exp(-sm[:, SM_BG + h:SM_BG + h + 1]))
        st = s0_ref[h] * jnp.exp(g)
        k_col = jnp.broadcast_to(k, (d, d)).T
        q_col = jnp.broadcast_to(q, (d, d)).T
        kv = jnp.sum(k_col * st, axis=0, keepdims=True)
        delta = (v - kv) * beta
        st = st + k_col * delta
        s_ref[h] = st
        o = jnp.sum(q_col * st, axis=0, keepdims=True)
        o_ref[0, :, h * d:(h + 1) * d] = _gated_norm(o, gn, zg_ref[0][:, h * d:(h + 1) * d]).astype(BF16)


def _gdn_sample(a_log, dt_bias, qkv3, state_conv_l, conv_w, small3, zg3, gdn_norm, state_ssm_l):
    db = qkv3.shape[0]
    d = GDN_HEAD_DIM
    nh = N_GDN_HEADS
    row3 = lambda n: pl.BlockSpec((1, 1, n), lambda b: (b, 0, 0))
    return pl.pallas_call(
        _gdn_sample_body,
        grid=(db,),
        in_specs=[pl.BlockSpec(memory_space=pltpu.SMEM), pl.BlockSpec(memory_space=pltpu.SMEM),
                  row3(3 * D_GDN),
                  pl.BlockSpec((None, CONV_W - 1, 3 * D_GDN), lambda b: (b, 0, 0)),
                  pl.BlockSpec((CONV_W, 3 * D_GDN), lambda b: (0, 0)),
                  row3(LANES), row3(D_GDN),
                  pl.BlockSpec((1, LANES), lambda b: (0, 0)),
                  pl.BlockSpec((None, nh, d, d), lambda b: (b, 0, 0, 0))],
        out_specs=[row3(D_GDN),
                   pl.BlockSpec((None, nh, d, d), lambda b: (b, 0, 0, 0)),
                   pl.BlockSpec((None, CONV_W - 1, 3 * D_GDN), lambda b: (b, 0, 0))],
        out_shape=[jax.ShapeDtypeStruct((db, 1, D_GDN), BF16),
                   jax.ShapeDtypeStruct((db, nh, d, d), F32),
                   jax.ShapeDtypeStruct((db, CONV_W - 1, 3 * D_GDN), F32)],
        compiler_params=_cparams(("parallel",)),
        name="gdn_sample",
    )(a_log, dt_bias, qkv3, state_conv_l, conv_w, small3, zg3, gdn_norm.reshape(1, LANES), state_ssm_l)


def kernel(x_prompt, x_sample, cache_k, cache_v, cache_kidx, state_ssm, state_conv, page_table, norm_in, w_in,
           conv_w, a_log, dt_bias, gdn_norm, w_out, rel_table, norm_final):
    depth = w_in.shape[0]
    assert depth == 1, "single-layer model"
    batch, seq, d_model = x_prompt.shape
    db, dec_seq, _ = x_sample.shape
    assert dec_seq == 1 and seq % KT == 0 and seq % (GDN_C * GDN_CHUNKS_PER_ITER) == 0 and seq % PROJ_ROWS == 0
    assert (batch * seq) % (2 * PROJ_ROWS) == 0
    n_pool, page = cache_k.shape[1], cache_k.shape[2]
    n_pages = page_table.shape[1]
    past_len = n_pages * page
    assert page == LANES and n_pages % IDX_PAGES_PER_STEP == 0 and n_pages % ATT_PAGES_PER_STEP == 0

    lyr = 0
    w_pad = _prep_w_in(w_in[lyr])
    w_out_bf = w_out[lyr].astype(BF16)

    xp = x_prompt.reshape(batch * seq, d_model)
    q_blk, k2d, v2d, za, qi_blk, small, qkv2d, zg, k_t, v_t, ki_t = _inproj(
        xp, norm_in[lyr], w_pad, tm=PROJ_ROWS, blocked=True, seq=seq)
    att_g = _dsa_prompt_t(rel_table, q_blk, qi_blk, small, za, k2d, v2d, batch, seq)
    gdn_g, s_fin = _gdn_prompt(a_log[lyr], dt_bias[lyr], qkv2d, conv_w[lyr], small, zg, gdn_norm[lyr], batch, seq)
    y_prompt = _outproj(xp, att_g, gdn_g, w_out_bf, norm_final, tm=2 * PROJ_ROWS).reshape(batch, seq, d_model)
    k_prompt = jnp.transpose(k_t.reshape(batch, N_KV_HEADS, ATT_HEAD_DIM, seq), (0, 3, 1, 2))[None]
    v_prompt = jnp.transpose(v_t.reshape(batch, N_KV_HEADS, ATT_HEAD_DIM, seq), (0, 3, 1, 2))[None]
    kidx_prompt = jnp.transpose(ki_t, (0, 2, 1))[None]
    ssm_prompt = s_fin[None]
    conv_prompt = qkv2d.reshape(batch, seq, 3 * D_GDN)[:, seq - (CONV_W - 1):][None]

    xs = x_sample.reshape(db, d_model)
    q_s, k_s, v_s, za_s, qi_s, small_s, qkv_s, zg_s = _inproj(xs, norm_in[lyr], w_pad, tm=db, blocked=False)
    kidx_t = jnp.transpose(cache_kidx[lyr], (0, 2, 1))
    k_pages_t = jnp.transpose(cache_k[lyr], (0, 2, 3, 1)).reshape(n_pool, KV_DIM, page)
    v_pages_t = jnp.transpose(cache_v[lyr], (0, 2, 3, 1)).reshape(n_pool, KV_DIM, page)
    scores = _idx_scores(page_table, qi_s.reshape(db, N_IDX_HEADS, IDX_DIM),
                         small_s[:, SM_WI:SM_WI + N_IDX_HEADS].reshape(db, N_IDX_HEADS, 1),
                         kidx_t)
    topk = min(TOPK_MAX, (past_len + dec_seq) // 4)
    mask_t = _sample_select(jnp.transpose(scores, (1, 0, 2)), qi_s, small_s, topk)
    mask_pages = jnp.transpose(mask_t[:n_pages], (1, 0, 2))
    mask_new = mask_t[n_pages].reshape(db, 1, page)
    hpg = N_ATT_HEADS // N_KV_HEADS
    q8 = q_s.reshape(db, N_ATT_HEADS, ATT_HEAD_DIM).astype(F32)
    zq = jnp.zeros((db, hpg, ATT_HEAD_DIM), F32)
    q_lh = jnp.concatenate([jnp.concatenate([q8[:, :hpg], zq], axis=2),
                            jnp.concatenate([zq, q8[:, hpg:]], axis=2),
                            jnp.zeros((db, SAMPLE_Q_ROWS - N_ATT_HEADS, LANES), F32)], axis=1)
    rel_t = jnp.concatenate([rel_table.T, jnp.zeros((SAMPLE_Q_ROWS - N_ATT_HEADS, N_BUCKETS), F32)], axis=0)
    att_raw = _sample_attn(page_table, q_lh, mask_pages, mask_new, k_s.reshape(db, 1, KV_DIM),
                           v_s.reshape(db, 1, KV_DIM), rel_t,
                           k_pages_t, v_pages_t, past_len)
    att_s = att_raw[:, :N_ATT_HEADS, :ATT_HEAD_DIM].reshape(db, D_ATT)
    gdn_s, s_new, conv_new = _gdn_sample(a_log[lyr], dt_bias[lyr], qkv_s.reshape(db, 1, 3 * D_GDN), state_conv[lyr],
                                         conv_w[lyr], small_s.reshape(db, 1, LANES), zg_s.reshape(db, 1, D_GDN),
                                         gdn_norm[lyr], state_ssm[lyr])
    y_sample = _outproj(xs, att_s, gdn_s.reshape(db, D_GDN), w_out_bf, norm_final, tm=db,
                        za=za_s).reshape(db, 1, d_model)
    k_sample = k_s.reshape(1, db, 1, N_KV_HEADS, ATT_HEAD_DIM)
    v_sample = v_s.reshape(1, db, 1, N_KV_HEADS, ATT_HEAD_DIM)
    kidx_sample = small_s[:, :IDX_DIM].reshape(1, db, 1, IDX_DIM)

    return (y_prompt, y_sample, k_prompt, v_prompt, kidx_prompt, ssm_prompt, conv_prompt,
            k_sample, v_sample, kidx_sample, s_new[None], conv_new[None])
```

```python
import functools
import math

import numpy as np
import jax
import jax.numpy as jnp
from jax import lax
from jax.experimental import pallas as pl
from jax.experimental.pallas import tpu as pltpu

F32 = jnp.float32
BF16 = jnp.bfloat16
I32 = jnp.int32

N_ATT_HEADS = 8
ATT_HEAD_DIM = 64
N_KV_HEADS = 2
D_ATT = N_ATT_HEADS * ATT_HEAD_DIM
KV_DIM = N_KV_HEADS * ATT_HEAD_DIM
N_IDX_HEADS = 16
IDX_DIM = 64
TOPK_MAX = 256
N_GDN_HEADS = 4
GDN_HEAD_DIM = 128
D_GDN = N_GDN_HEADS * GDN_HEAD_DIM
CONV_W = 4
N_BUCKETS = 32
MAX_DISTANCE = 128
Q_BLOCK = 128
RMS_EPS = 1e-6
NEG_BIG = -1e30
PROJ_SIZES = (D_ATT, KV_DIM, KV_DIM, D_ATT, N_IDX_HEADS * IDX_DIM, IDX_DIM, N_IDX_HEADS,
              3 * D_GDN, D_GDN, N_GDN_HEADS, N_GDN_HEADS)

LANES = 128
SUBLANES = 8
VMEM_LIMIT = 56 * 1024 * 1024

OFF_Q = 0
OFF_K = OFF_Q + D_ATT
OFF_V = OFF_K + KV_DIM
OFF_ZA = OFF_V + KV_DIM
OFF_QI = OFF_ZA + D_ATT
OFF_SM = OFF_QI + N_IDX_HEADS * IDX_DIM
OFF_QKV = OFF_SM + LANES
OFF_ZG = OFF_QKV + 3 * D_GDN
D_PROJ_PAD = OFF_ZG + D_GDN
SM_WI = IDX_DIM
SM_AG = SM_WI + N_IDX_HEADS
SM_BG = SM_AG + N_GDN_HEADS

PROJ_ROWS = 512
GDN_C = 128
GDN_CHUNKS_PER_ITER = 4
KEY_NEG_BIG = int(np.array(NEG_BIG, np.float32).view(np.int32)) ^ 0x7FFFFFFF
INT_MIN = -2 ** 31


def _cparams(sem):
    return pltpu.CompilerParams(dimension_semantics=sem, vmem_limit_bytes=VMEM_LIMIT)


def _silu(x):
    return x * (1.0 / (1.0 + jnp.exp(-x)))


def _bdot(a, b):
    return jnp.dot(a.astype(BF16), b.astype(BF16), preferred_element_type=F32)


def _bdot_nt(a, b):
    return lax.dot_general(a.astype(BF16), b.astype(BF16), (((1,), (1,)), ((), ())),
                           preferred_element_type=F32)


def _sort_key(x):
    i = pltpu.bitcast(x, I32)
    return jnp.where(i < 0, i ^ 0x7FFFFFFF, i)


def _inproj_body(x_ref, g_ref, w_ref, q_ref, k_ref, v_ref, za_ref, qi_ref, sm_ref, qkv_ref, zg_ref, *t_refs,
                 blocked):
    x = x_ref[...]
    ms = jnp.mean(x * x, axis=-1, keepdims=True)
    h = ((x * lax.rsqrt(ms + RMS_EPS)) * g_ref[...]).astype(BF16)

    def mm(a, b):
        return jnp.dot(h, w_ref[:, a:b], preferred_element_type=F32)

    q = mm(OFF_Q, OFF_K) * (ATT_HEAD_DIM ** -0.5)
    qi = mm(OFF_QI, OFF_SM) * (IDX_DIM ** -0.5)
    if blocked:
        for r in range(x.shape[0] // Q_BLOCK):
            rs = slice(r * Q_BLOCK, (r + 1) * Q_BLOCK)
            for j in range(D_ATT // LANES):
                cs = slice(j * LANES, (j + 1) * LANES)
                q_ref[r, :, cs] = q[rs, cs].T.astype(BF16)
            for j in range(N_IDX_HEADS * IDX_DIM // LANES):
                cs = slice(j * LANES, (j + 1) * LANES)
                qi_ref[r, :, cs] = qi[rs, cs].T.astype(BF16)
    else:
        q_ref[...] = q.astype(BF16)
        qi_ref[...] = qi.astype(BF16)
    k = mm(OFF_K, OFF_V)
    v = mm(OFF_V, OFF_ZA)
    sm = mm(OFF_SM, OFF_QKV)
    k_ref[...] = k
    v_ref[...] = v
    sm_ref[...] = sm
    za_ref[...] = mm(OFF_ZA, OFF_QI)
    qkv_ref[...] = mm(OFF_QKV, OFF_ZG)
    zg_ref[...] = mm(OFF_ZG, D_PROJ_PAD)
    if t_refs:
        kt_ref, vt_ref, kit_ref = t_refs
        kt_ref[...] = k.T
        vt_ref[...] = v.T
        kit_ref[...] = sm.T[0:IDX_DIM, :]


def _prep_w_in(w):
    splits = np.cumsum(PROJ_SIZES)[:-1].tolist()
    q, k, v, z_a, qi, ki, wi, qkv, z_g, a_g, b_g = jnp.split(w, splits, axis=1)
    pad = jnp.zeros((w.shape[0], LANES - (SM_BG + N_GDN_HEADS)), w.dtype)
    small = jnp.concatenate([ki, wi, a_g, b_g, pad], axis=1)
    return jnp.concatenate([q, k, v, z_a, qi, small, qkv, z_g], axis=1).astype(BF16)


def _inproj(x2d, norm_g, w_pad, tm, blocked, seq=None):
    t, d = x2d.shape
    nblk = t // tm
    t_shapes, t_specs = [], []
    if blocked:
        spb = seq // tm
        for n in (KV_DIM, KV_DIM, IDX_DIM):
            t_shapes.append(jax.ShapeDtypeStruct((t // seq, n, seq), F32))
            t_specs.append(pl.BlockSpec((None, n, tm), lambda i: (i // spb, 0, i % spb)))
        rb = tm // Q_BLOCK
        q_shape = jax.ShapeDtypeStruct((t // Q_BLOCK, LANES, D_ATT), BF16)
        qi_shape = jax.ShapeDtypeStruct((t // Q_BLOCK, LANES, N_IDX_HEADS * IDX_DIM), BF16)
        q_spec = pl.BlockSpec((rb, LANES, D_ATT), lambda i: (i, 0, 0))
        qi_spec = pl.BlockSpec((rb, LANES, N_IDX_HEADS * IDX_DIM), lambda i: (i, 0, 0))
    else:
        q_shape = jax.ShapeDtypeStruct((t, D_ATT), BF16)
        qi_shape = jax.ShapeDtypeStruct((t, N_IDX_HEADS * IDX_DIM), BF16)
        q_spec = pl.BlockSpec((tm, D_ATT), lambda i: (i, 0))
        qi_spec = pl.BlockSpec((tm, N_IDX_HEADS * IDX_DIM), lambda i: (i, 0))

    def row(n):
        return pl.BlockSpec((tm, n), lambda i: (i, 0))

    return pl.pallas_call(
        functools.partial(_inproj_body, blocked=blocked),
        grid=(nblk,),
        in_specs=[row(d), pl.BlockSpec((1, d), lambda i: (0, 0)),
                  pl.BlockSpec((d, D_PROJ_PAD), lambda i: (0, 0))],
        out_specs=[q_spec, row(KV_DIM), row(KV_DIM), row(D_ATT), qi_spec, row(LANES), row(3 * D_GDN), row(D_GDN)]
                  + t_specs,
        out_shape=[q_shape, jax.ShapeDtypeStruct((t, KV_DIM), F32), jax.ShapeDtypeStruct((t, KV_DIM), F32),
                   jax.ShapeDtypeStruct((t, D_ATT), F32), qi_shape, jax.ShapeDtypeStruct((t, LANES), F32),
                   jax.ShapeDtypeStruct((t, 3 * D_GDN), F32), jax.ShapeDtypeStruct((t, D_GDN), F32)] + t_shapes,
        compiler_params=_cparams(("parallel",)),
        name="inproj_blocked" if blocked else "inproj_rows",
    )(x2d, norm_g.reshape(1, d), w_pad)


def _rel_bucket(dist):
    n = jnp.maximum(dist, 0)
    max_exact = N_BUCKETS // 2
    nf = jnp.maximum(n, 1).astype(F32)
    large = max_exact + (jnp.log(nf / max_exact) / math.log(MAX_DISTANCE / max_exact)
                         * (N_BUCKETS - max_exact)).astype(I32)
    large = jnp.minimum(large, N_BUCKETS - 1)
    return jnp.where(n < max_exact, n, large)


def _far_bucket_checked(first_far):
    d = np.arange(first_far, 1 << 16, dtype=np.float32)
    b = 16 + (np.log(d / 16) / math.log(MAX_DISTANCE / 16) * 16).astype(np.int32)
    assert int(b.min()) >= N_BUCKETS - 1
    return N_BUCKETS - 1


def _bit_search(count_ge, t0, cnt0, k, nbits):
    def body(step, carry):
        t, cnt_t = carry
        cand = t + jnp.left_shift(jnp.int32(1), nbits - 1 - step)
        cnt = count_ge(cand)
        accept = cnt >= k
        return jnp.where(accept, cand, t), jnp.where(accept, cnt, cnt_t)

    return lax.fori_loop(0, nbits, body, (t0, cnt0))


def _kth_key_search(count_ge, shape, k, total):
    return _bit_search(count_ge, jnp.full(shape, INT_MIN, I32), jnp.full(shape, total, F32), k, 32)


KT = 2 * LANES


def _fold8(x, op):
    binop = {jnp.sum: jnp.add, jnp.max: jnp.maximum}[op]
    r = x.reshape(x.shape[0] // SUBLANES, SUBLANES, x.shape[1])
    while r.shape[0] > 1:
        half = r.shape[0] // 2
        r = binop(r[:half], r[half:])
    return r[0]


def _dsa_prompt_t_body(rel_ref, q_ref, qi_ref, smq_ref, za_ref, k_ref, v_ref, sms_ref, o_ref,
                       kk_ref, vvt_ref, ki_ref, bias_ref, key_ref, mask_ref, lg_ref,
                       *, seq, topk):
    b = pl.program_id(0)
    i = pl.program_id(1)
    n_t = seq // KT
    hpg = N_ATT_HEADS // N_KV_HEADS
    far_bucket = _far_bucket_checked(MAX_DISTANCE + 1)
    sub1 = lax.broadcasted_iota(I32, (LANES, LANES), 0)
    lane1 = lax.broadcasted_iota(I32, (LANES, LANES), 1)

    @pl.when((b == 0) & (i == 0))
    def _():
        for dt in range(2):
            bucket = _rel_bucket(dt * LANES + lane1 - sub1)
            for h in range(N_ATT_HEADS):
                acc = jnp.zeros((LANES, LANES), F32)
                for bk in range(N_BUCKETS):
                    acc = jnp.where(bucket == bk, rel_ref[bk, h], acc)
                bias_ref[h, dt] = acc - rel_ref[far_bucket, h]

    @pl.when(i == 0)
    def _():
        lo = lax.broadcasted_iota(I32, (seq, LANES), 1) < ATT_HEAD_DIM
        kf = k_ref[...]
        g0 = jnp.where(lo, kf, 0.0)
        g1 = jnp.where(lo, 0.0, kf)
        kk_ref[0] = g0.astype(BF16)
        kk_ref[1] = pltpu.roll(g0, ATT_HEAD_DIM, 1).astype(BF16)
        kk_ref[2] = pltpu.roll(g1, ATT_HEAD_DIM, 1).astype(BF16)
        kk_ref[3] = g1.astype(BF16)
        c0 = jnp.where(lo, sms_ref[...], 0.0)
        ki_ref[0] = c0.astype(BF16)
        ki_ref[1] = pltpu.roll(c0, IDX_DIM, 1).astype(BF16)
        lo_t = lax.broadcasted_iota(I32, (KT, LANES), 1) < ATT_HEAD_DIM
        for t in range(n_t):
            vf = v_ref[t * KT:(t + 1) * KT, :]
            w0 = jnp.where(lo_t, vf, 0.0)
            w1 = jnp.where(lo_t, 0.0, vf)
            vvt_ref[0, t] = w0.T.astype(BF16)
            vvt_ref[1, t] = pltpu.roll(w0, ATT_HEAD_DIM, 1).T.astype(BF16)
            vvt_ref[2, t] = pltpu.roll(w1, ATT_HEAD_DIM, 1).T.astype(BF16)
            vvt_ref[3, t] = w1.T.astype(BF16)

    n_ip = N_IDX_HEADS * IDX_DIM // LANES
    sm_t = smq_ref[...].T
    wrow = [sm_t[SM_WI + h:SM_WI + h + 1, :] * (N_IDX_HEADS ** -0.5) for h in range(N_IDX_HEADS)]

    n_live = i // 2 + 1
    n_dead = ((n_t - n_live) * KT).astype(F32)
    kidx = lax.broadcasted_iota(I32, (KT, LANES), 0)
    qpos = i * Q_BLOCK + lax.broadcasted_iota(I32, (KT, LANES), 1)

    def causal_of(t):
        return (t * KT + kidx) <= qpos

    def tile_rows(t):
        return pl.ds(pl.multiple_of(t * KT, KT), KT)

    def score_tile(t, carry):
        acc = jnp.zeros((KT, LANES), F32)
        for half in range(2):
            s_all = jnp.dot(ki_ref[half, tile_rows(t), :], qi_ref[...], preferred_element_type=F32)
            for j in range(n_ip):
                acc = acc + jnp.maximum(s_all[:, j * LANES:(j + 1) * LANES], 0.0) * wrow[2 * j + half]
        acc = acc + 0.0
        key_ref[t] = _sort_key(jnp.where(causal_of(t), acc, NEG_BIG))
        return carry

    def pair_loop(body, init):
        return lax.fori_loop(0, (n_live + 1) // 2, lambda p, c: body(2 * p + 1, body(2 * p, c)), init)

    pair_loop(score_tile, 0)

    @pl.when(n_live % 2 == 1)
    def _():
        key_ref[n_live] = jnp.full((KT, LANES), INT_MIN, I32)
        mask_ref[n_live] = jnp.full((KT, LANES), NEG_BIG, F32)

    slabs_per_tile = KT // LANES

    def causal_slab(t, r):
        return (t * KT + r * LANES + sub1) <= (i * Q_BLOCK + lane1)

    def count_where(pred):
        def body(t, acc):
            kt = key_ref[t]
            for r in range(slabs_per_tile):
                acc = jnp.where(pred(kt[r * LANES:(r + 1) * LANES], t, r), acc + 1.0, acc)
            return acc
        acc = pair_loop(body, jnp.zeros((LANES, LANES), F32))
        return jnp.sum(_fold8(acc, jnp.sum), axis=0, keepdims=True)

    def count_ge(cand):
        return count_where(lambda kt, t, r: kt >= cand) + jnp.where(cand <= KEY_NEG_BIG, n_dead, 0.0)

    def write_threshold_masks(thr):
        def body(t, carry):
            mask_ref[t] = jnp.where((key_ref[t] >= thr) & causal_of(t), 0.0, NEG_BIG)
            return carry
        lax.fori_loop(0, n_live, body, 0)

    few_keys = (i + 1) * Q_BLOCK <= topk

    @pl.when(few_keys)
    def _():
        write_threshold_masks(jnp.full((1, LANES), KEY_NEG_BIG, I32))

    @pl.when(jnp.logical_not(few_keys))
    def _():
        thr, cnt_thr = _kth_key_search(count_ge, (1, LANES), float(topk), float(seq))
        boundary_dup = jnp.max(jnp.where(cnt_thr > float(topk), 1.0, 0.0)) > 0.0

        @pl.when(jnp.logical_not(boundary_dup))
        def _():
            write_threshold_masks(thr)

        @pl.when(boundary_dup)
        def _():
            cnt_gt = count_where(lambda kt, t, r: kt > thr) + jnp.where(thr < KEY_NEG_BIG, n_dead, 0.0)
            need = float(topk) - cnt_gt
            cnt_ceq = count_where(lambda kt, t, r: (kt == thr) & causal_slab(t, r))
            any_tie = jnp.max(jnp.where(cnt_ceq > need, 1.0, 0.0)) > 0.0

            @pl.when(jnp.logical_not(any_tie))
            def _():
                write_threshold_masks(thr)

            @pl.when(any_tie)
            def _():
                rk = lax.broadcasted_iota(I32, (KT, KT), 0)
                ck = lax.broadcasted_iota(I32, (KT, KT), 1)
                lower = (ck <= rk).astype(BF16)

                def body(t, offset):
                    kt = key_ref[t]
                    cz = causal_of(t)
                    eq = ((kt == thr) & cz).astype(F32)
                    prefix = jnp.dot(lower, eq.astype(BF16), preferred_element_type=F32) + offset
                    sel = ((kt > thr) & cz) | ((eq > 0.0) & (prefix <= need))
                    mask_ref[t] = jnp.where(sel, 0.0, NEG_BIG)
                    return offset + jnp.sum(_fold8(eq, jnp.sum), axis=0, keepdims=True)

                lax.fori_loop(0, n_live, body, jnp.zeros((1, LANES), F32))


    za = za_ref[...]
    near_lo = jnp.maximum(i - 1, 0) // 2
    neg8 = jnp.full((SUBLANES, LANES), NEG_BIG, F32)
    zero8 = jnp.zeros((SUBLANES, LANES), F32)
    def logit_tile(t, mx):
        mx = list(mx)
        far = t < near_lo
        for g in range(N_KV_HEADS):
            qt_g = q_ref[:, g * 2 * LANES:(g + 1) * 2 * LANES]
            for half in range(2):
                l_all = jnp.dot(kk_ref[2 * g + half, tile_rows(t), :], qt_g, preferred_element_type=F32)
                for n in range(2):
                    h = g * hpg + 2 * n + half
                    l = l_all[:, n * LANES:(n + 1) * LANES] + mask_ref[t]
                    lg_ref[h, t] = l
                    mx[h] = jnp.maximum(mx[h], jnp.where(far, _fold8(l, jnp.max), NEG_BIG))
        return tuple(mx)

    mx = pair_loop(logit_tile, (neg8,) * N_ATT_HEADS)
    m_all = []
    for h in range(N_ATT_HEADS):
        r0 = pl.ds(pl.multiple_of((i % 2) * LANES, LANES), LANES)
        lg_ref[h, i // 2, r0, :] = lg_ref[h, i // 2, r0, :] + bias_ref[h, 0]
        im1 = jnp.maximum(i - 1, 0)
        r1 = pl.ds(pl.multiple_of((im1 % 2) * LANES, LANES), LANES)
        lg_ref[h, im1 // 2, r1, :] = lg_ref[h, im1 // 2, r1, :] + jnp.where(i > 0, bias_ref[h, 1], 0.0)
        m8 = jnp.maximum(mx[h], jnp.maximum(_fold8(lg_ref[h, near_lo], jnp.max),
                                            _fold8(lg_ref[h, i // 2], jnp.max)))
        m_all.append(jnp.max(m8, axis=0, keepdims=True))

    n_pairs = N_ATT_HEADS // 2

    def pv_tile(t, carry):
        ssum = list(carry[:N_ATT_HEADS])
        acc = list(carry[N_ATT_HEADS:])
        for pr in range(n_pairs):
            g = (2 * pr) // hpg
            for half in range(2):
                h = 2 * pr + half
                p = jnp.exp(lg_ref[h, t] - m_all[h])
                ssum[h] = ssum[h] + _fold8(p, jnp.sum)
                acc[pr] = acc[pr] + jnp.dot(vvt_ref[2 * g + half, t], p.astype(BF16), preferred_element_type=F32)
        return tuple(ssum) + tuple(acc)

    zacc = jnp.zeros((LANES, LANES), F32)
    res = pair_loop(pv_tile, (zero8,) * N_ATT_HEADS + (zacc,) * n_pairs)
    for pr in range(n_pairs):
        l_lo = jnp.sum(res[2 * pr], axis=0, keepdims=True)
        l_hi = jnp.sum(res[2 * pr + 1], axis=0, keepdims=True)
        inv = jnp.where(sub1 < ATT_HEAD_DIM, 1.0 / l_lo, 1.0 / l_hi)
        o_pair = (res[N_ATT_HEADS + pr] * inv).T
        cs = slice(pr * LANES, (pr + 1) * LANES)
        o_ref[:, cs] = (o_pair * _silu(za[:, cs])).astype(BF16)


def _dsa_prompt_t(rel_table, q_blk, qi_blk, small, za, k2d, v2d, batch, seq):
    assert N_KV_HEADS == 2 and N_ATT_HEADS // N_KV_HEADS == 4 and seq % KT == 0
    nb = seq // Q_BLOCK
    n_t = seq // KT
    topk = min(TOPK_MAX, seq // 4)
    assert seq >= 2 * topk
    t = batch * seq
    return pl.pallas_call(
        functools.partial(_dsa_prompt_t_body, seq=seq, topk=topk),
        grid=(batch, nb),
        in_specs=[
            pl.BlockSpec(memory_space=pltpu.SMEM),
            pl.BlockSpec((None, LANES, D_ATT), lambda b, i: (b * nb + i, 0, 0)),
            pl.BlockSpec((None, LANES, N_IDX_HEADS * IDX_DIM), lambda b, i: (b * nb + i, 0, 0)),
            pl.BlockSpec((Q_BLOCK, LANES), lambda b, i: (b * nb + i, 0)),
            pl.BlockSpec((Q_BLOCK, D_ATT), lambda b, i: (b * nb + i, 0)),
            pl.BlockSpec((seq, KV_DIM), lambda b, i: (b, 0)),
            pl.BlockSpec((seq, KV_DIM), lambda b, i: (b, 0)),
            pl.BlockSpec((seq, LANES), lambda b, i: (b, 0)),
        ],
        out_specs=pl.BlockSpec((Q_BLOCK, D_ATT), lambda b, i: (b * nb + i, 0)),
        out_shape=jax.ShapeDtypeStruct((t, D_ATT), BF16),
        scratch_shapes=[
            pltpu.VMEM((2 * N_KV_HEADS, seq, LANES), BF16),
            pltpu.VMEM((2 * N_KV_HEADS, n_t, LANES, KT), BF16),
            pltpu.VMEM((2, seq, LANES), BF16),
            pltpu.VMEM((N_ATT_HEADS, 2, LANES, LANES), F32),
            pltpu.VMEM((n_t, KT, LANES), I32),
            pltpu.VMEM((n_t, KT, LANES), F32),
            pltpu.VMEM((N_ATT_HEADS, n_t, KT, LANES), F32),
        ],
        compiler_params=_cparams(("arbitrary", "arbitrary")),
        name="dsa_prompt",
    )(rel_table, q_blk, qi_blk, small, za, k2d, v2d, small)


def _softplus(x):
    return jnp.maximum(x, 0.0) + jnp.log1p(jnp.exp(-jnp.abs(x)))


def _lane_pick(x, idx):
    lane = lax.broadcasted_iota(I32, x.shape, 1)
    return jnp.sum(jnp.where(lane == idx, x, 0.0), axis=1, keepdims=True)


def _gated_norm(o, gn, z):
    y = o * lax.rsqrt(jnp.mean(o * o, axis=-1, keepdims=True) + RMS_EPS)
    return (y * gn) * _silu(z)


def _gdn_prompt_body(alog_ref, dtb_ref, xq_ref, xk_ref, xv_ref, wq_ref, wk_ref, wv_ref, sm_ref, zg_ref, gn_ref,
                     o_ref, sfin_ref, xs_ref, q_s, k_s, v_s, g_s, b_s, *, seq, hp):
    h0 = pl.program_id(1) * hp
    hist = SUBLANES
    slabs = [slice(hh * LANES, (hh + 1) * LANES) for hh in range(hp)]

    def conv_into(x_ref, w_ref, dst, post):
        xs_ref[0:hist, :] = jnp.zeros((hist, hp * LANES), F32)
        xs_ref[hist:hist + seq, :] = x_ref[...]
        base = hist - (CONV_W - 1)
        for hs in slabs:
            acc = xs_ref[base:base + seq, hs] * w_ref[0:1, hs]
            for j in range(1, CONV_W):
                acc = acc + xs_ref[base + j:base + j + seq, hs] * w_ref[j:j + 1, hs]
            dst[:, hs] = post(_silu(acc))

    def l2n(x):
        return x * lax.rsqrt(jnp.sum(x * x, axis=-1, keepdims=True) + 1e-6)

    conv_into(xq_ref, wq_ref, q_s, lambda x: l2n(x) * (GDN_HEAD_DIM ** -0.5))
    conv_into(xk_ref, wk_ref, k_s, l2n)
    conv_into(xv_ref, wv_ref, v_s, lambda x: x)

    lane_row = lax.broadcasted_iota(I32, (1, LANES), 1)
    alog_row = jnp.zeros((1, LANES), F32)
    dtb_row = jnp.zeros((1, LANES), F32)
    for hh in range(hp):
        alog_row = jnp.where(lane_row == SM_AG + h0 + hh, alog_ref[h0 + hh], alog_row)
        dtb_row = jnp.where(lane_row == SM_AG + h0 + hh, dtb_ref[h0 + hh], dtb_row)
    sm = sm_ref[...]
    g_all = -jnp.exp(alog_row) * _softplus(sm + dtb_row)
    beta_all = 1.0 / (1.0 + jnp.exp(-sm))
    for hh, hs in enumerate(slabs):
        g_s[:, hs] = jnp.broadcast_to(_lane_pick(g_all, SM_AG + h0 + hh), (seq, LANES))
        b_s[:, hs] = jnp.broadcast_to(_lane_pick(beta_all, SM_BG + h0 + hh), (seq, LANES))

    c = GDN_C
    ri = lax.broadcasted_iota(I32, (c, c), 0)
    ci = lax.broadcasted_iota(I32, (c, c), 1)
    tril = ri >= ci
    strict = ri > ci
    tril_f = tril.astype(F32)
    eye = (ri == ci).astype(F32)
    gn = gn_ref[...]
    off_masks = []
    for lg in range(int(math.log2(c))):
        same_pair = (ri >> (lg + 1)) == (ci >> (lg + 1))
        off_masks.append(same_pair & (((ri >> lg) & 1) == 1) & (((ci >> lg) & 1) == 0))

    tril_b = tril.astype(BF16)

    def cumsum_rows(g):
        hi = g.astype(BF16)
        r1 = g - hi.astype(F32)
        mid = r1.astype(BF16)
        lo = (r1 - mid.astype(F32)).astype(BF16)
        return sum(jnp.dot(tril_b, piece, preferred_element_type=F32) for piece in (hi, mid, lo))

    cpi = GDN_CHUNKS_PER_ITER

    def rows_of(n):
        return pl.ds(pl.multiple_of(n * c, c), c)

    def local_phase(items):
        ids = range(len(items))
        q = [q_s[rows_of(n), hs] for n, hs in items]
        k = [k_s[rows_of(n), hs] for n, hs in items]
        v = [v_s[rows_of(n), hs] for n, hs in items]
        bb = [b_s[rows_of(n), hs] for n, hs in items]
        gcum = [cumsum_rows(g_s[rows_of(n), hs]) for n, hs in items]
        gcum_row = [g.T for g in gcum]
        decay = [jnp.where(tril, jnp.exp(jnp.where(tril, gcum[e] - gcum_row[e], 0.0)), 0.0) for e in ids]
        eg = [jnp.exp(g) for g in gcum]
        kb = [k[e] * bb[e] for e in ids]
        vb = [v[e] * bb[e] for e in ids]
        kq = [_bdot_nt(jnp.concatenate([kb[e], q[e]], axis=0), k[e]) for e in ids]
        a_mat = [jnp.where(strict, kq[e][:c] * decay[e], 0.0) for e in ids]
        attn = [kq[e][c:] * decay[e] for e in ids]
        x = [eye - jnp.where(off_masks[0], a_mat[e], 0.0) for e in ids]
        for om in off_masks[1:]:
            inner = [_bdot(jnp.where(om, a_mat[e], 0.0), x[e]) for e in ids]
            x = [x[e] - _bdot(x[e], inner[e]) for e in ids]
        uw = [_bdot(x[e], jnp.concatenate([vb[e], kb[e] * eg[e]], axis=1)) for e in ids]
        g_last = [g[c - 1:c, :] for g in gcum]
        k_dec_t = [(k[e] * jnp.exp(g_last[e] - gcum[e])).T for e in ids]
        wq = [jnp.concatenate([uw[e][:, GDN_HEAD_DIM:], q[e] * eg[e]], axis=0) for e in ids]
        ak = [jnp.concatenate([attn[e], k_dec_t[e]], axis=0) for e in ids]
        return [dict(u=uw[e][:, :GDN_HEAD_DIM], wq=wq[e], ak=ak[e], s_dec=jnp.exp(g_last[e])) for e in ids]

    def state_phase(n, loc, states):
        heads = range(hp)
        wq_s = [_bdot(loc[h]["wq"], states[h]) for h in heads]
        v_new = [loc[h]["u"] - wq_s[h][:c] for h in heads]
        ak_v = [_bdot(loc[h]["ak"], v_new[h]) for h in heads]
        for h, hs in enumerate(slabs):
            o = wq_s[h][c:] + ak_v[h][:c]
            o_ref[rows_of(n), hs] = _gated_norm(o, gn, zg_ref[rows_of(n), hs]).astype(BF16)
        return tuple(states[h] * loc[h]["s_dec"] + ak_v[h][c:] for h in heads)

    def chunk_group(p, states):
        ns = [p * cpi + r for r in range(cpi)]
        loc = local_phase([(n, hs) for n in ns for hs in slabs])
        for r, n in enumerate(ns):
            states = state_phase(n, loc[r * hp:(r + 1) * hp], states)
        return states

    zero_state = jnp.zeros((GDN_HEAD_DIM, GDN_HEAD_DIM), F32)
    finals = lax.fori_loop(0, seq // (c * cpi), chunk_group, (zero_state,) * hp)
    for hh in range(hp):
        sfin_ref[hh] = finals[hh]


GDN_HEADS_PER_STEP = 4


def _gdn_prompt(a_log, dt_bias, qkv2d, conv_w, small, zg, gdn_norm, batch, seq):
    nh = N_GDN_HEADS
    hp = GDN_HEADS_PER_STEP
    ng = nh // hp
    t = batch * seq
    w = hp * LANES
    once = pl.Buffered(1)
    blk = lambda off: pl.BlockSpec((seq, w), lambda b, j: (b, off + j), pipeline_mode=once)
    wblk = lambda off: pl.BlockSpec((CONV_W, w), lambda b, j: (0, off + j))
    return pl.pallas_call(
        functools.partial(_gdn_prompt_body, seq=seq, hp=hp),
        grid=(batch, ng),
        in_specs=[pl.BlockSpec(memory_space=pltpu.SMEM), pl.BlockSpec(memory_space=pltpu.SMEM),
                  blk(0), blk(ng), blk(2 * ng), wblk(0), wblk(ng), wblk(2 * ng),
                  pl.BlockSpec((seq, LANES), lambda b, j: (b, 0)),
                  pl.BlockSpec((seq, w), lambda b, j: (b, j), pipeline_mode=once),
                  pl.BlockSpec((1, LANES), lambda b, j: (0, 0))],
        out_specs=[pl.BlockSpec((seq, w), lambda b, j: (b, j)),
                   pl.BlockSpec((None, hp, GDN_HEAD_DIM, GDN_HEAD_DIM), lambda b, j: (b, j, 0, 0))],
        out_shape=[jax.ShapeDtypeStruct((t, D_GDN), BF16),
                   jax.ShapeDtypeStruct((batch, nh, GDN_HEAD_DIM, GDN_HEAD_DIM), F32)],
        scratch_shapes=[pltpu.VMEM((seq + 2 * SUBLANES, w), F32)] + [pltpu.VMEM((seq, w), F32)] * 5,
        compiler_params=_cparams(("parallel", "arbitrary")),
        name="gdn_prompt",
    )(a_log, dt_bias, qkv2d, qkv2d, qkv2d, conv_w, conv_w, conv_w, small, zg, gdn_norm.reshape(1, LANES))


def _outproj_body(*refs, gate_att):
    if gate_att:
        x_ref, att_ref, za_ref, gdn_ref, w_ref, g_ref, y_ref = refs
        att = (att_ref[...] * _silu(za_ref[...])).astype(BF16)
    else:
        x_ref, att_ref, gdn_ref, w_ref, g_ref, y_ref = refs
        att = att_ref[...]
    y = x_ref[...] + jnp.dot(att, w_ref[0:D_ATT, :], preferred_element_type=F32) \
        + jnp.dot(gdn_ref[...], w_ref[D_ATT:D_ATT + D_GDN, :], preferred_element_type=F32)
    y = y * lax.rsqrt(jnp.mean(y * y, axis=-1, keepdims=True) + RMS_EPS)
    y_ref[...] = y * g_ref[...]


def _outproj(x2d, att, gdn, w_bf, norm_g, tm, za=None):
    t, d = x2d.shape
    row = lambda n: pl.BlockSpec((tm, n), lambda i: (i, 0))
    full = lambda a, b: pl.BlockSpec((a, b), lambda i: (0, 0))
    ins = [x2d, att] + ([za] if za is not None else []) + [gdn, w_bf, norm_g.reshape(1, d)]
    specs = [row(d), row(D_ATT)] + ([row(D_ATT)] if za is not None else []) + \
            [row(D_GDN), full(D_ATT + D_GDN, d), full(1, d)]
    return pl.pallas_call(
        functools.partial(_outproj_body, gate_att=za is not None),
        grid=(t // tm,),
        in_specs=specs,
        out_specs=row(d),
        out_shape=jax.ShapeDtypeStruct((t, d), F32),
        compiler_params=_cparams(("parallel",)),
        name="outproj_gated" if za is not None else "outproj",
    )(*ins)


IDX_PAGES_PER_STEP = 64
ATT_PAGES_PER_STEP = 32
SAMPLE_Q_ROWS = 16


def _idx_scores_body(pt_ref, qi_ref, wi_ref, *rest):
    page_refs = rest[:IDX_PAGES_PER_STEP]
    o_ref = rest[IDX_PAGES_PER_STEP]
    qi = qi_ref[0]
    wi = wi_ref[0] * (N_IDX_HEADS ** -0.5)
    for j, pr in enumerate(page_refs):
        s = _bdot(qi, pr[...])
        sc = jnp.sum(jnp.maximum(s, 0.0) * wi, axis=0, keepdims=True)
        o_ref[0, j:j + 1, :] = sc + 0.0


def _idx_scores(page_table, qi3, wi3, kidx_pages):
    db, n_pages = page_table.shape
    page = kidx_pages.shape[2]
    pg = IDX_PAGES_PER_STEP

    def page_spec(j):
        return pl.BlockSpec((None, IDX_DIM, page), lambda b, s, pt: (pt[b, s * pg + j], 0, 0))

    grid_spec = pltpu.PrefetchScalarGridSpec(
        num_scalar_prefetch=1,
        grid=(db, n_pages // pg),
        in_specs=[pl.BlockSpec((1, N_IDX_HEADS, IDX_DIM), lambda b, s, pt: (b, 0, 0)),
                  pl.BlockSpec((1, N_IDX_HEADS, 1), lambda b, s, pt: (b, 0, 0))]
                 + [page_spec(j) for j in range(pg)],
        out_specs=pl.BlockSpec((1, pg, page), lambda b, s, pt: (b, s, 0)),
    )
    return pl.pallas_call(
        _idx_scores_body,
        grid_spec=grid_spec,
        out_shape=jax.ShapeDtypeStruct((db, n_pages, page), F32),
        compiler_params=_cparams(("parallel", "arbitrary")),
        name="sample_idx_scores",
    )(page_table, qi3, wi3, *([kidx_pages] * pg))


def _sample_select_body(sc_ref, qi_ref, sm_ref, mask_ref, key_ref, *, n_tiles, topk):
    rows = sc_ref.shape[1]
    lane = lax.broadcasted_iota(I32, (rows, LANES), 1)

    def fill(t, carry):
        key_ref[t] = _sort_key(sc_ref[t])
        return carry

    lax.fori_loop(0, n_tiles, fill, 0)
    sm = sm_ref[...]
    ki_new = sm[:, 0:IDX_DIM].astype(BF16).astype(F32)
    acc = jnp.zeros((rows, 1), F32)
    for hd in range(N_IDX_HEADS):
        qh = qi_ref[:, hd * IDX_DIM:(hd + 1) * IDX_DIM].astype(F32)
        s = jnp.sum(qh * ki_new, axis=1, keepdims=True)
        acc = acc + jnp.maximum(s, 0.0) * (sm[:, SM_WI + hd:SM_WI + hd + 1] * (N_IDX_HEADS ** -0.5))
    sc_new = jnp.broadcast_to(acc + 0.0, (rows, LANES))
    key_ref[n_tiles] = jnp.where(lane == 0, _sort_key(sc_new), INT_MIN)

    def count_where(pred):
        def body(t, a):
            return a + pred(key_ref[t]).astype(F32)
        a = lax.fori_loop(0, n_tiles + 1, body, jnp.zeros((rows, LANES), F32))
        return jnp.sum(a, axis=1, keepdims=True)

    thr, _ = _kth_key_search(lambda cand: count_where(lambda kt: kt >= cand), (rows, 1), float(topk),
                             float(n_tiles * LANES + 1))
    need = float(topk) - count_where(lambda kt: kt > thr)
    cnt_eq = count_where(lambda kt: kt == thr)
    any_tie = jnp.max(jnp.where(cnt_eq > need, 1.0, 0.0)) > 0.0

    @pl.when(jnp.logical_not(any_tie))
    def _():
        def body(t, carry):
            mask_ref[t] = jnp.where(key_ref[t] >= thr, 0.0, NEG_BIG)
            return carry
        lax.fori_loop(0, n_tiles + 1, body, 0)

    @pl.when(any_tie)
    def _():
        sub = lax.broadcasted_iota(I32, (LANES, LANES), 0)
        lane2 = lax.broadcasted_iota(I32, (LANES, LANES), 1)
        upper = (sub <= lane2).astype(BF16)
        ones = jnp.ones((LANES, LANES), BF16)

        def body(t, offset):
            kt = key_ref[t]
            eq = (kt == thr).astype(BF16)
            prefix = jnp.dot(eq, upper, preferred_element_type=F32) + offset
            sel = (kt > thr) | ((eq > 0) & (prefix <= need))
            mask_ref[t] = jnp.where(sel, 0.0, NEG_BIG)
            return offset + jnp.dot(eq, ones, preferred_element_type=F32)

        lax.fori_loop(0, n_tiles + 1, body, jnp.zeros((rows, LANES), F32))


def _sample_select(scores_t, qi2d, small, topk):
    n_tiles, db, page = scores_t.shape
    vm = pl.BlockSpec(memory_space=pltpu.VMEM)
    return pl.pallas_call(
        functools.partial(_sample_select_body, n_tiles=n_tiles, topk=topk),
        in_specs=[vm, vm, vm],
        out_specs=vm,
        out_shape=jax.ShapeDtypeStruct((n_tiles + 1, db, page), F32),
        scratch_shapes=[pltpu.VMEM((n_tiles + 1, db, page), I32)],
        compiler_params=pltpu.CompilerParams(vmem_limit_bytes=VMEM_LIMIT),
        name="sample_select",
    )(scores_t, qi2d, small)


def _sample_attn_body(pt_ref, q_ref, mask_ref, mnew_ref, knew_ref, vnew_ref, relt_ref, *rest, past_len, page):
    pg = ATT_PAGES_PER_STEP
    k_refs = rest[:pg]
    v_refs = rest[pg:2 * pg]
    o_ref, m_s, l_s, acc_s = rest[2 * pg:]
    s = pl.program_id(1)
    n_steps = pl.num_programs(1)
    nh = SAMPLE_Q_ROWS
    far_bucket = _far_bucket_checked(MAX_DISTANCE + 1)

    @pl.when(s == 0)
    def _():
        m_s[...] = jnp.full(m_s.shape, NEG_BIG, F32)
        l_s[...] = jnp.zeros(l_s.shape, F32)
        acc_s[...] = jnp.zeros(acc_s.shape, F32)

    relt = relt_ref[...]
    c_far = relt[:, far_bucket:far_bucket + 1]

    def bias_of(dist):
        bucket = _rel_bucket(dist)
        acc = jnp.zeros((nh, dist.shape[1]), F32)
        for bk in range(N_BUCKETS):
            acc = jnp.where(bucket == bk, relt[:, bk:bk + 1], acc)
        return acc - c_far

    q = q_ref[0].astype(BF16)
    logits = []
    for j in range(pg):
        l = _bdot(q, k_refs[j][...]) + mask_ref[0, j:j + 1, :]
        logits.append(l)
    logits = jnp.concatenate(logits, axis=1)

    kpos = (s * pg) * page + lax.broadcasted_iota(I32, (1, pg * page), 1)
    near = past_len - ((s + 1) * pg * page - 1) <= MAX_DISTANCE
    logits = logits + lax.cond(near, lambda: bias_of(past_len - kpos),
                               lambda: jnp.zeros((nh, pg * page), F32))

    m_old = m_s[...]
    m_new = jnp.maximum(m_old, jnp.max(logits, axis=1, keepdims=True))
    alpha = jnp.exp(m_old - m_new)
    p = jnp.exp(logits - m_new)
    l_new = l_s[...] * alpha + jnp.sum(p, axis=1, keepdims=True)
    acc = acc_s[...] * alpha
    for j in range(pg):
        acc = acc + _bdot_nt(p[:, j * page:(j + 1) * page], v_refs[j][...])
    m_s[...] = m_new
    l_s[...] = l_new
    acc_s[...] = acc

    @pl.when(s == n_steps - 1)
    def _():
        kn = knew_ref[0].astype(BF16).astype(F32)
        vn = vnew_ref[0].astype(BF16).astype(F32)
        ln = jnp.sum(q.astype(F32) * kn, axis=1, keepdims=True) + bias_of(jnp.zeros((1, 1), I32)) \
            + mnew_ref[0][:, 0:1]
        m_fin = jnp.maximum(m_new, ln)
        a2 = jnp.exp(m_new - m_fin)
        pn = jnp.exp(ln - m_fin)
        l_fin = l_new * a2 + pn
        res = (acc * a2 + pn.astype(BF16).astype(F32) * vn) / l_fin
        row = lax.broadcasted_iota(I32, res.shape, 0)
        hpg = N_ATT_HEADS // N_KV_HEADS
        o_ref[0] = jnp.where((row >= hpg) & (row < 2 * hpg), pltpu.roll(res, ATT_HEAD_DIM, 1), res)


def _sample_attn(page_table, q_lh, mask_pages, mask_new, k_new, v_new, rel_t, k_pages, v_pages, past_len):
    db, n_pages = page_table.shape
    page = k_pages.shape[2]
    pg = ATT_PAGES_PER_STEP

    def page_spec(j):
        return pl.BlockSpec((None, KV_DIM, page), lambda b, s, pt: (pt[b, s * pg + j], 0, 0))

    row3 = lambda n: pl.BlockSpec((1, 1, n), lambda b, s, pt: (b, 0, 0))
    grid_spec = pltpu.PrefetchScalarGridSpec(
        num_scalar_prefetch=1,
        grid=(db, n_pages // pg),
        in_specs=[pl.BlockSpec((1, SAMPLE_Q_ROWS, LANES), lambda b, s, pt: (b, 0, 0)),
                  pl.BlockSpec((1, pg, page), lambda b, s, pt: (b, s, 0)),
                  row3(LANES), row3(KV_DIM), row3(KV_DIM),
                  pl.BlockSpec((SAMPLE_Q_ROWS, N_BUCKETS), lambda b, s, pt: (0, 0))]
                 + [page_spec(j) for j in range(pg)] * 2,
        out_specs=pl.BlockSpec((1, SAMPLE_Q_ROWS, LANES), lambda b, s, pt: (b, 0, 0)),
        scratch_shapes=[pltpu.VMEM((SAMPLE_Q_ROWS, 1), F32), pltpu.VMEM((SAMPLE_Q_ROWS, 1), F32),
                        pltpu.VMEM((SAMPLE_Q_ROWS, LANES), F32)],
    )
    return pl.pallas_call(
        functools.partial(_sample_attn_body, past_len=past_len, page=page),
        grid_spec=grid_spec,
        out_shape=jax.ShapeDtypeStruct((db, SAMPLE_Q_ROWS, LANES), F32),
        compiler_params=_cparams(("parallel", "arbitrary")),
        name="sample_attn",
    )(page_table, q_lh, mask_pages, mask_new, k_new, v_new, rel_t, *([k_pages] * pg), *([v_pages] * pg))


def _gdn_sample_body(alog_ref, dtb_ref, x_ref, cst_ref, w_ref, sm_ref, zg_ref, gn_ref, s0_ref,
                     o_ref, s_ref, cnew_ref):
    x = x_ref[0]
    cst = cst_ref[...]
    w = w_ref[...]
    acc = cst[0:1, :] * w[0:1, :]
    for j in range(1, CONV_W - 1):
        acc = acc + cst[j:j + 1, :] * w[j:j + 1, :]
    acc = acc + x * w[CONV_W - 1:CONV_W, :]
    xc = _silu(acc)
    cnew_ref[0:CONV_W - 2, :] = cst[1:CONV_W - 1, :]
    cnew_ref[CONV_W - 2:CONV_W - 1, :] = x
    sm = sm_ref[0]
    gn = gn_ref[...]
    d = GDN_HEAD_DIM
    for h in range(N_GDN_HEADS):
        q = xc[:, h * d:(h + 1) * d]
        k = xc[:, D_GDN + h * d:D_GDN + (h + 1) * d]
        v = xc[:, 2 * D_GDN + h * d:2 * D_GDN + (h + 1) * d]
        q = q * lax.rsqrt(jnp.sum(q * q, axis=-1, keepdims=True) + 1e-6) * (d ** -0.5)
        k = k * lax.rsqrt(jnp.sum(k * k, axis=-1, keepdims=True) + 1e-6)
        a_neg = -jnp.exp(jnp.zeros((1, 1), F32) + alog_ref[h])
        g = a_neg * _softplus(sm[:, SM_AG + h:SM_AG + h + 1] + dtb_ref[h])
        beta = 1.0 / (1.0 + jnp.exp(-sm[:, SM_BG + h:SM_BG + h + 1]))
        st = s0_ref[h] * jnp.exp(g)
        k_col = jnp.broadcast_to(k, (d, d)).T
        q_col = jnp.broadcast_to(q, (d, d)).T
        kv = jnp.sum(k_col * st, axis=0, keepdims=True)
        delta = (v - kv) * beta
        st = st + k_col * delta
        s_ref[h] = st
        o = jnp.sum(q_col * st, axis=0, keepdims=True)
        o_ref[0, :, h * d:(h + 1) * d] = _gated_norm(o, gn, zg_ref[0][:, h * d:(h + 1) * d]).astype(BF16)


def _gdn_sample(a_log, dt_bias, qkv3, state_conv_l, conv_w, small3, zg3, gdn_norm, state_ssm_l):
    db = qkv3.shape[0]
    d = GDN_HEAD_DIM
    nh = N_GDN_HEADS
    row3 = lambda n: pl.BlockSpec((1, 1, n), lambda b: (b, 0, 0))
    return pl.pallas_call(
        _gdn_sample_body,
        grid=(db,),
        in_specs=[pl.BlockSpec(memory_space=pltpu.SMEM), pl.BlockSpec(memory_space=pltpu.SMEM),
                  row3(3 * D_GDN),
                  pl.BlockSpec((None, CONV_W - 1, 3 * D_GDN), lambda b: (b, 0, 0)),
                  pl.BlockSpec((CONV_W, 3 * D_GDN), lambda b: (0, 0)),
                  row3(LANES), row3(D_GDN),
                  pl.BlockSpec((1, LANES), lambda b: (0, 0)),
                  pl.BlockSpec((None, nh, d, d), lambda b: (b, 0, 0, 0))],
        out_specs=[row3(D_GDN),
                   pl.BlockSpec((None, nh, d, d), lambda b: (b, 0, 0, 0)),
                   pl.BlockSpec((None, CONV_W - 1, 3 * D_GDN), lambda b: (b, 0, 0))],
        out_shape=[jax.ShapeDtypeStruct((db, 1, D_GDN), BF16),
                   jax.ShapeDtypeStruct((db, nh, d, d), F32),
                   jax.ShapeDtypeStruct((db, CONV_W - 1, 3 * D_GDN), F32)],
        compiler_params=_cparams(("parallel",)),
        name="gdn_sample",
    )(a_log, dt_bias, qkv3, state_conv_l, conv_w, small3, zg3, gdn_norm.reshape(1, LANES), state_ssm_l)


def kernel(x_prompt, x_sample, cache_k, cache_v, cache_kidx, state_ssm, state_conv, page_table, norm_in, w_in,
           conv_w, a_log, dt_bias, gdn_norm, w_out, rel_table, norm_final):
    depth = w_in.shape[0]
    assert depth == 1, "single-layer model"
    batch, seq, d_model = x_prompt.shape
    db, dec_seq, _ = x_sample.shape
    assert dec_seq == 1 and seq % KT == 0 and seq % (GDN_C * GDN_CHUNKS_PER_ITER) == 0 and seq % PROJ_ROWS == 0
    assert (batch * seq) % (2 * PROJ_ROWS) == 0
    n_pool, page = cache_k.shape[1], cache_k.shape[2]
    n_pages = page_table.shape[1]
    past_len = n_pages * page
    assert page == LANES and n_pages % IDX_PAGES_PER_STEP == 0 and n_pages % ATT_PAGES_PER_STEP == 0

    lyr = 0
    w_pad = _prep_w_in(w_in[lyr])
    w_out_bf = w_out[lyr].astype(BF16)

    xp = x_prompt.reshape(batch * seq, d_model)
    q_blk, k2d, v2d, za, qi_blk, small, qkv2d, zg, k_t, v_t, ki_t = _inproj(
        xp, norm_in[lyr], w_pad, tm=PROJ_ROWS, blocked=True, seq=seq)
    att_g = _dsa_prompt_t(rel_table, q_blk, qi_blk, small, za, k2d, v2d, batch, seq)
    gdn_g, s_fin = _gdn_prompt(a_log[lyr], dt_bias[lyr], qkv2d, conv_w[lyr], small, zg, gdn_norm[lyr], batch, seq)
    y_prompt = _outproj(xp, att_g, gdn_g, w_out_bf, norm_final, tm=2 * PROJ_ROWS).reshape(batch, seq, d_model)
    k_prompt = jnp.transpose(k_t.reshape(batch, N_KV_HEADS, ATT_HEAD_DIM, seq), (0, 3, 1, 2))[None]
    v_prompt = jnp.transpose(v_t.reshape(batch, N_KV_HEADS, ATT_HEAD_DIM, seq), (0, 3, 1, 2))[None]
    kidx_prompt = jnp.transpose(ki_t, (0, 2, 1))[None]
    ssm_prompt = s_fin[None]
    conv_prompt = qkv2d.reshape(batch, seq, 3 * D_GDN)[:, seq - (CONV_W - 1):][None]

    xs = x_sample.reshape(db, d_model)
    q_s, k_s, v_s, za_s, qi_s, small_s, qkv_s, zg_s = _inproj(xs, norm_in[lyr], w_pad, tm=db, blocked=False)
    kidx_t = jnp.transpose(cache_kidx[lyr], (0, 2, 1))
    k_pages_t = jnp.transpose(cache_k[lyr], (0, 2, 3, 1)).reshape(n_pool, KV_DIM, page)
    v_pages_t = jnp.transpose(cache_v[lyr], (0, 2, 3, 1)).reshape(n_pool, KV_DIM, page)
    scores = _idx_scores(page_table, qi_s.reshape(db, N_IDX_HEADS, IDX_DIM),
                         small_s[:, SM_WI:SM_WI + N_IDX_HEADS].reshape(db, N_IDX_HEADS, 1),
                         kidx_t)
    topk = min(TOPK_MAX, (past_len + dec_seq) // 4)
    mask_t = _sample_select(jnp.transpose(scores, (1, 0, 2)), qi_s, small_s, topk)
    mask_pages = jnp.transpose(mask_t[:n_pages], (1, 0, 2))
    mask_new = mask_t[n_pages].reshape(db, 1, page)
    hpg = N_ATT_HEADS // N_KV_HEADS
    q8 = q_s.reshape(db, N_ATT_HEADS, ATT_HEAD_DIM).astype(F32)
    zq = jnp.zeros((db, hpg, ATT_HEAD_DIM), F32)
    q_lh = jnp.concatenate([jnp.concatenate([q8[:, :hpg], zq], axis=2),
                            jnp.concatenate([zq, q8[:, hpg:]], axis=2),
                            jnp.zeros((db, SAMPLE_Q_ROWS - N_ATT_HEADS, LANES), F32)], axis=1)
    rel_t = jnp.concatenate([rel_table.T, jnp.zeros((SAMPLE_Q_ROWS - N_ATT_HEADS, N_BUCKETS), F32)], axis=0)
    att_raw = _sample_attn(page_table, q_lh, mask_pages, mask_new, k_s.reshape(db, 1, KV_DIM),
                           v_s.reshape(db, 1, KV_DIM), rel_t,
                           k_pages_t, v_pages_t, past_len)
    att_s = att_raw[:, :N_ATT_HEADS, :ATT_HEAD_DIM].reshape(db, D_ATT)
    gdn_s, s_new, conv_new = _gdn_sample(a_log[lyr], dt_bias[lyr], qkv_s.reshape(db, 1, 3 * D_GDN), state_conv[lyr],
                                         conv_w[lyr], small_s.reshape(db, 1, LANES), zg_s.reshape(db, 1, D_GDN),
                                         gdn_norm[lyr], state_ssm[lyr])
    y_sample = _outproj(xs, att_s, gdn_s.reshape(db, D_GDN), w_out_bf, norm_final, tm=db,
                        za=za_s).reshape(db, 1, d_model)
    k_sample = k_s.reshape(1, db, 1, N_KV_HEADS, ATT_HEAD_DIM)
    v_sample = v_s.reshape(1, db, 1, N_KV_HEADS, ATT_HEAD_DIM)
    kidx_sample = small_s[:, :IDX_DIM].reshape(1, db, 1, IDX_DIM)

    return (y_prompt, y_sample, k_prompt, v_prompt, kidx_prompt, ssm_prompt, conv_prompt,
            k_sample, v_sample, kidx_sample, s_new[None], conv_new[None])
```

```python
import functools
import math

import numpy as np
import jax
import jax.numpy as jnp
from jax import lax
from jax.experimental import pallas as pl
from jax.experimental.pallas import tpu as pltpu

F32 = jnp.float32
BF16 = jnp.bfloat16
I32 = jnp.int32

N_ATT_HEADS = 8
ATT_HEAD_DIM = 64
N_KV_HEADS = 2
D_ATT = N_ATT_HEADS * ATT_HEAD_DIM
KV_DIM = N_KV_HEADS * ATT_HEAD_DIM
N_IDX_HEADS = 16
IDX_DIM = 64
TOPK_MAX = 256
N_GDN_HEADS = 4
GDN_HEAD_DIM = 128
D_GDN = N_GDN_HEADS * GDN_HEAD_DIM
CONV_W = 4
N_BUCKETS = 32
MAX_DISTANCE = 128
Q_BLOCK = 128
RMS_EPS = 1e-6
NEG_BIG = -1e30
PROJ_SIZES = (D_ATT, KV_DIM, KV_DIM, D_ATT, N_IDX_HEADS * IDX_DIM, IDX_DIM, N_IDX_HEADS,
              3 * D_GDN, D_GDN, N_GDN_HEADS, N_GDN_HEADS)

LANES = 128
SUBLANES = 8
VMEM_LIMIT = 56 * 1024 * 1024

OFF_Q = 0
OFF_K = OFF_Q + D_ATT
OFF_V = OFF_K + KV_DIM
OFF_ZA = OFF_V + KV_DIM
OFF_QI = OFF_ZA + D_ATT
OFF_SM = OFF_QI + N_IDX_HEADS * IDX_DIM
OFF_QKV = OFF_SM + LANES
OFF_ZG = OFF_QKV + 3 * D_GDN
D_PROJ_PAD = OFF_ZG + D_GDN
SM_WI = IDX_DIM
SM_AG = SM_WI + N_IDX_HEADS
SM_BG = SM_AG + N_GDN_HEADS

PROJ_ROWS = 512
GDN_C = 128
GDN_CHUNKS_PER_ITER = 4
KEY_NEG_BIG = int(np.array(NEG_BIG, np.float32).view(np.int32)) ^ 0x7FFFFFFF
INT_MIN = -2 ** 31


def _cparams(sem):
    return pltpu.CompilerParams(dimension_semantics=sem, vmem_limit_bytes=VMEM_LIMIT)


def _silu(x):
    return x * (1.0 / (1.0 + jnp.exp(-x)))


def _bdot(a, b):
    return jnp.dot(a.astype(BF16), b.astype(BF16), preferred_element_type=F32)


def _bdot_nt(a, b):
    return lax.dot_general(a.astype(BF16), b.astype(BF16), (((1,), (1,)), ((), ())),
                           preferred_element_type=F32)


def _sort_key(x):
    i = pltpu.bitcast(x, I32)
    return jnp.where(i < 0, i ^ 0x7FFFFFFF, i)


def _inproj_body(x_ref, g_ref, w_ref, q_ref, k_ref, v_ref, za_ref, qi_ref, sm_ref, qkv_ref, zg_ref, *t_refs,
                 blocked):
    x = x_ref[...]
    ms = jnp.mean(x * x, axis=-1, keepdims=True)
    h = ((x * lax.rsqrt(ms + RMS_EPS)) * g_ref[...]).astype(BF16)

    def mm(a, b):
        return jnp.dot(h, w_ref[:, a:b], preferred_element_type=F32)

    q = mm(OFF_Q, OFF_K) * (ATT_HEAD_DIM ** -0.5)
    qi = mm(OFF_QI, OFF_SM) * (IDX_DIM ** -0.5)
    if blocked:
        for r in range(x.shape[0] // Q_BLOCK):
            rs = slice(r * Q_BLOCK, (r + 1) * Q_BLOCK)
            for j in range(D_ATT // LANES):
                cs = slice(j * LANES, (j + 1) * LANES)
                q_ref[r, :, cs] = q[rs, cs].T.astype(BF16)
            for j in range(N_IDX_HEADS * IDX_DIM // LANES):
                cs = slice(j * LANES, (j + 1) * LANES)
                qi_ref[r, :, cs] = qi[rs, cs].T.astype(BF16)
    else:
        q_ref[...] = q.astype(BF16)
        qi_ref[...] = qi.astype(BF16)
    k = mm(OFF_K, OFF_V)
    v = mm(OFF_V, OFF_ZA)
    sm = mm(OFF_SM, OFF_QKV)
    k_ref[...] = k
    v_ref[...] = v
    sm_ref[...] = sm
    za_ref[...] = mm(OFF_ZA, OFF_QI)
    qkv_ref[...] = mm(OFF_QKV, OFF_ZG)
    zg_ref[...] = mm(OFF_ZG, D_PROJ_PAD)
    if t_refs:
        kt_ref, vt_ref, kit_ref = t_refs
        kt_ref[...] = k.T
        vt_ref[...] = v.T
        kit_ref[...] = sm.T[0:IDX_DIM, :]


def _prep_w_in(w):
    splits = np.cumsum(PROJ_SIZES)[:-1].tolist()
    q, k, v, z_a, qi, ki, wi, qkv, z_g, a_g, b_g = jnp.split(w, splits, axis=1)
    pad = jnp.zeros((w.shape[0], LANES - (SM_BG + N_GDN_HEADS)), w.dtype)
    small = jnp.concatenate([ki, wi, a_g, b_g, pad], axis=1)
    return jnp.concatenate([q, k, v, z_a, qi, small, qkv, z_g], axis=1).astype(BF16)


def _inproj(x2d, norm_g, w_pad, tm, blocked, seq=None):
    t, d = x2d.shape
    nblk = t // tm
    t_shapes, t_specs = [], []
    if blocked:
        spb = seq // tm
        for n in (KV_DIM, KV_DIM, IDX_DIM):
            t_shapes.append(jax.ShapeDtypeStruct((t // seq, n, seq), F32))
            t_specs.append(pl.BlockSpec((None, n, tm), lambda i: (i // spb, 0, i % spb)))
        rb = tm // Q_BLOCK
        q_shape = jax.ShapeDtypeStruct((t // Q_BLOCK, LANES, D_ATT), BF16)
        qi_shape = jax.ShapeDtypeStruct((t // Q_BLOCK, LANES, N_IDX_HEADS * IDX_DIM), BF16)
        q_spec = pl.BlockSpec((rb, LANES, D_ATT), lambda i: (i, 0, 0))
        qi_spec = pl.BlockSpec((rb, LANES, N_IDX_HEADS * IDX_DIM), lambda i: (i, 0, 0))
    else:
        q_shape = jax.ShapeDtypeStruct((t, D_ATT), BF16)
        qi_shape = jax.ShapeDtypeStruct((t, N_IDX_HEADS * IDX_DIM), BF16)
        q_spec = pl.BlockSpec((tm, D_ATT), lambda i: (i, 0))
        qi_spec = pl.BlockSpec((tm, N_IDX_HEADS * IDX_DIM), lambda i: (i, 0))

    def row(n):
        return pl.BlockSpec((tm, n), lambda i: (i, 0))

    return pl.pallas_call(
        functools.partial(_inproj_body, blocked=blocked),
        grid=(nblk,),
        in_specs=[row(d), pl.BlockSpec((1, d), lambda i: (0, 0)),
                  pl.BlockSpec((d, D_PROJ_PAD), lambda i: (0, 0))],
        out_specs=[q_spec, row(KV_DIM), row(KV_DIM), row(D_ATT), qi_spec, row(LANES), row(3 * D_GDN), row(D_GDN)]
                  + t_specs,
        out_shape=[q_shape, jax.ShapeDtypeStruct((t, KV_DIM), F32), jax.ShapeDtypeStruct((t, KV_DIM), F32),
                   jax.ShapeDtypeStruct((t, D_ATT), F32), qi_shape, jax.ShapeDtypeStruct((t, LANES), F32),
                   jax.ShapeDtypeStruct((t, 3 * D_GDN), F32), jax.ShapeDtypeStruct((t, D_GDN), F32)] + t_shapes,
        compiler_params=_cparams(("parallel",)),
        name="inproj_blocked" if blocked else "inproj_rows",
    )(x2d, norm_g.reshape(1, d), w_pad)


def _rel_bucket(dist):
    n = jnp.maximum(dist, 0)
    max_exact = N_BUCKETS // 2
    nf = jnp.maximum(n, 1).astype(F32)
    large = max_exact + (jnp.log(nf / max_exact) / math.log(MAX_DISTANCE / max_exact)
                         * (N_BUCKETS - max_exact)).astype(I32)
    large = jnp.minimum(large, N_BUCKETS - 1)
    return jnp.where(n < max_exact, n, large)


def _far_bucket_checked(first_far):
    d = np.arange(first_far, 1 << 16, dtype=np.float32)
    b = 16 + (np.log(d / 16) / math.log(MAX_DISTANCE / 16) * 16).astype(np.int32)
    assert int(b.min()) >= N_BUCKETS - 1
    return N_BUCKETS - 1


def _kth_key_search(count_ge, shape, k, total):
    def body(step, carry):
        t, cnt_t = carry
        cand = t + jnp.left_shift(jnp.int32(1), 31 - step)
        cnt = count_ge(cand)
        accept = cnt >= k
        return jnp.where(accept, cand, t), jnp.where(accept, cnt, cnt_t)

    return lax.fori_loop(0, 32, body, (jnp.full(shape, INT_MIN, I32), jnp.full(shape, total, F32)))


KT = 2 * LANES


def _fold8(x, op):
    binop = {jnp.sum: jnp.add, jnp.max: jnp.maximum}[op]
    r = x.reshape(x.shape[0] // SUBLANES, SUBLANES, x.shape[1])
    while r.shape[0] > 1:
        half = r.shape[0] // 2
        r = binop(r[:half], r[half:])
    return r[0]


def _dsa_prompt_t_body(rel_ref, q_ref, qi_ref, smq_ref, za_ref, k_ref, v_ref, sms_ref, o_ref,
                       kk_ref, vvt_ref, ki_ref, bias_ref, key_ref, mask_ref, lg_ref,
                       *, seq, topk):
    b = pl.program_id(0)
    i = pl.program_id(1)
    n_t = seq // KT
    hpg = N_ATT_HEADS // N_KV_HEADS
    far_bucket = _far_bucket_checked(MAX_DISTANCE + 1)
    sub1 = lax.broadcasted_iota(I32, (LANES, LANES), 0)
    lane1 = lax.broadcasted_iota(I32, (LANES, LANES), 1)

    @pl.when((b == 0) & (i == 0))
    def _():
        for dt in range(2):
            bucket = _rel_bucket(dt * LANES + lane1 - sub1)
            for h in range(N_ATT_HEADS):
                acc = jnp.zeros((LANES, LANES), F32)
                for bk in range(N_BUCKETS):
                    acc = jnp.where(bucket == bk, rel_ref[bk, h], acc)
                bias_ref[h, dt] = acc - rel_ref[far_bucket, h]

    @pl.when(i == 0)
    def _():
        lo = lax.broadcasted_iota(I32, (seq, LANES), 1) < ATT_HEAD_DIM
        kf = k_ref[...]
        g0 = jnp.where(lo, kf, 0.0)
        g1 = jnp.where(lo, 0.0, kf)
        kk_ref[0] = g0.astype(BF16)
        kk_ref[1] = pltpu.roll(g0, ATT_HEAD_DIM, 1).astype(BF16)
        kk_ref[2] = pltpu.roll(g1, ATT_HEAD_DIM, 1).astype(BF16)
        kk_ref[3] = g1.astype(BF16)
        c0 = jnp.where(lo, sms_ref[...], 0.0)
        ki_ref[0] = c0.astype(BF16)
        ki_ref[1] = pltpu.roll(c0, IDX_DIM, 1).astype(BF16)
        lo_t = lax.broadcasted_iota(I32, (KT, LANES), 1) < ATT_HEAD_DIM
        for t in range(n_t):
            vf = v_ref[t * KT:(t + 1) * KT, :]
            w0 = jnp.where(lo_t, vf, 0.0)
            w1 = jnp.where(lo_t, 0.0, vf)
            vvt_ref[0, t] = w0.T.astype(BF16)
            vvt_ref[1, t] = pltpu.roll(w0, ATT_HEAD_DIM, 1).T.astype(BF16)
            vvt_ref[2, t] = pltpu.roll(w1, ATT_HEAD_DIM, 1).T.astype(BF16)
            vvt_ref[3, t] = w1.T.astype(BF16)

    n_ip = N_IDX_HEADS * IDX_DIM // LANES
    sm_t = smq_ref[...].T
    wrow = [sm_t[SM_WI + h:SM_WI + h + 1, :] * (N_IDX_HEADS ** -0.5) for h in range(N_IDX_HEADS)]

    n_live = i // 2 + 1
    n_dead = ((n_t - n_live) * KT).astype(F32)
    kidx = lax.broadcasted_iota(I32, (KT, LANES), 0)
    qpos = i * Q_BLOCK + lax.broadcasted_iota(I32, (KT, LANES), 1)

    def causal_of(t):
        return (t * KT + kidx) <= qpos

    def tile_rows(t):
        return pl.ds(pl.multiple_of(t * KT, KT), KT)

    def score_tile(t, carry):
        acc = jnp.zeros((KT, LANES), F32)
        for half in range(2):
            s_all = jnp.dot(ki_ref[half, tile_rows(t), :], qi_ref[...], preferred_element_type=F32)
            for j in range(n_ip):
                acc = acc + jnp.maximum(s_all[:, j * LANES:(j + 1) * LANES], 0.0) * wrow[2 * j + half]
        acc = acc + 0.0
        key_ref[t] = _sort_key(jnp.where(causal_of(t), acc, NEG_BIG))
        return carry

    def pair_loop(body, init):
        return lax.fori_loop(0, (n_live + 1) // 2, lambda p, c: body(2 * p + 1, body(2 * p, c)), init)

    pair_loop(score_tile, 0)

    @pl.when(n_live % 2 == 1)
    def _():
        key_ref[n_live] = jnp.full((KT, LANES), INT_MIN, I32)
        mask_ref[n_live] = jnp.full((KT, LANES), NEG_BIG, F32)

    slabs_per_tile = KT // LANES

    def causal_slab(t, r):
        return (t * KT + r * LANES + sub1) <= (i * Q_BLOCK + lane1)

    def count_where(pred):
        def body(t, acc):
            kt = key_ref[t]
            for r in range(slabs_per_tile):
                acc = jnp.where(pred(kt[r * LANES:(r + 1) * LANES], t, r), acc + 1.0, acc)
            return acc
        acc = pair_loop(body, jnp.zeros((LANES, LANES), F32))
        return jnp.sum(_fold8(acc, jnp.sum), axis=0, keepdims=True)

    def count_ge(cand):
        return count_where(lambda kt, t, r: kt >= cand) + jnp.where(cand <= KEY_NEG_BIG, n_dead, 0.0)

    def write_threshold_masks(thr):
        def body(t, carry):
            mask_ref[t] = jnp.where((key_ref[t] >= thr) & causal_of(t), 0.0, NEG_BIG)
            return carry
        lax.fori_loop(0, n_live, body, 0)

    few_keys = (i + 1) * Q_BLOCK <= topk

    @pl.when(few_keys)
    def _():
        write_threshold_masks(jnp.full((1, LANES), KEY_NEG_BIG, I32))

    @pl.when(jnp.logical_not(few_keys))
    def _():
        thr, cnt_thr = _kth_key_search(count_ge, (1, LANES), float(topk), float(seq))
        boundary_dup = jnp.max(jnp.where(cnt_thr > float(topk), 1.0, 0.0)) > 0.0

        @pl.when(jnp.logical_not(boundary_dup))
        def _():
            write_threshold_masks(thr)

        @pl.when(boundary_dup)
        def _():
            cnt_gt = count_where(lambda kt, t, r: kt > thr) + jnp.where(thr < KEY_NEG_BIG, n_dead, 0.0)
            need = float(topk) - cnt_gt
            cnt_ceq = count_where(lambda kt, t, r: (kt == thr) & causal_slab(t, r))
            any_tie = jnp.max(jnp.where(cnt_ceq > need, 1.0, 0.0)) > 0.0

            @pl.when(jnp.logical_not(any_tie))
            def _():
                write_threshold_masks(thr)

            @pl.when(any_tie)
            def _():
                rk = lax.broadcasted_iota(I32, (KT, KT), 0)
                ck = lax.broadcasted_iota(I32, (KT, KT), 1)
                lower = (ck <= rk).astype(BF16)

                def body(t, offset):
                    kt = key_ref[t]
                    cz = causal_of(t)
                    eq = ((kt == thr) & cz).astype(F32)
                    prefix = jnp.dot(lower, eq.astype(BF16), preferred_element_type=F32) + offset
                    sel = ((kt > thr) & cz) | ((eq > 0.0) & (prefix <= need))
                    mask_ref[t] = jnp.where(sel, 0.0, NEG_BIG)
                    return offset + jnp.sum(_fold8(eq, jnp.sum), axis=0, keepdims=True)

                lax.fori_loop(0, n_live, body, jnp.zeros((1, LANES), F32))


    za = za_ref[...]
    near_lo = jnp.maximum(i - 1, 0) // 2
    neg8 = jnp.full((SUBLANES, LANES), NEG_BIG, F32)
    zero8 = jnp.zeros((SUBLANES, LANES), F32)
    def logit_tile(t, mx):
        mx = list(mx)
        far = t < near_lo
        for g in range(N_KV_HEADS):
            qt_g = q_ref[:, g * 2 * LANES:(g + 1) * 2 * LANES]
            for half in range(2):
                l_all = jnp.dot(kk_ref[2 * g + half, tile_rows(t), :], qt_g, preferred_element_type=F32)
                for n in range(2):
                    h = g * hpg + 2 * n + half
                    l = l_all[:, n * LANES:(n + 1) * LANES] + mask_ref[t]
                    lg_ref[h, t] = l
                    mx[h] = jnp.maximum(mx[h], jnp.where(far, _fold8(l, jnp.max), NEG_BIG))
        return tuple(mx)

    mx = pair_loop(logit_tile, (neg8,) * N_ATT_HEADS)
    m_all = []
    for h in range(N_ATT_HEADS):
        r0 = pl.ds(pl.multiple_of((i % 2) * LANES, LANES), LANES)
        lg_ref[h, i // 2, r0, :] = lg_ref[h, i // 2, r0, :] + bias_ref[h, 0]
        im1 = jnp.maximum(i - 1, 0)
        r1 = pl.ds(pl.multiple_of((im1 % 2) * LANES, LANES), LANES)
        lg_ref[h, im1 // 2, r1, :] = lg_ref[h, im1 // 2, r1, :] + jnp.where(i > 0, bias_ref[h, 1], 0.0)
        m8 = jnp.maximum(mx[h], jnp.maximum(_fold8(lg_ref[h, near_lo], jnp.max),
                                            _fold8(lg_ref[h, i // 2], jnp.max)))
        m_all.append(jnp.max(m8, axis=0, keepdims=True))

    n_pairs = N_ATT_HEADS // 2

    def pv_tile(t, carry):
        ssum = list(carry[:N_ATT_HEADS])
        acc = list(carry[N_ATT_HEADS:])
        for pr in range(n_pairs):
            g = (2 * pr) // hpg
            for half in range(2):
                h = 2 * pr + half
                p = jnp.exp(lg_ref[h, t] - m_all[h])
                ssum[h] = ssum[h] + _fold8(p, jnp.sum)
                acc[pr] = acc[pr] + jnp.dot(vvt_ref[2 * g + half, t], p.astype(BF16), preferred_element_type=F32)
        return tuple(ssum) + tuple(acc)

    zacc = jnp.zeros((LANES, LANES), F32)
    res = pair_loop(pv_tile, (zero8,) * N_ATT_HEADS + (zacc,) * n_pairs)
    for pr in range(n_pairs):
        l_lo = jnp.sum(res[2 * pr], axis=0, keepdims=True)
        l_hi = jnp.sum(res[2 * pr + 1], axis=0, keepdims=True)
        inv = jnp.where(sub1 < ATT_HEAD_DIM, 1.0 / l_lo, 1.0 / l_hi)
        o_pair = (res[N_ATT_HEADS + pr] * inv).T
        cs = slice(pr * LANES, (pr + 1) * LANES)
        o_ref[:, cs] = (o_pair * _silu(za[:, cs])).astype(BF16)


def _dsa_prompt_t(rel_table, q_blk, qi_blk, small, za, k2d, v2d, batch, seq):
    assert N_KV_HEADS == 2 and N_ATT_HEADS // N_KV_HEADS == 4 and seq % KT == 0
    nb = seq // Q_BLOCK
    n_t = seq // KT
    topk = min(TOPK_MAX, seq // 4)
    assert seq >= 2 * topk
    t = batch * seq
    return pl.pallas_call(
        functools.partial(_dsa_prompt_t_body, seq=seq, topk=topk),
        grid=(batch, nb),
        in_specs=[
            pl.BlockSpec(memory_space=pltpu.SMEM),
            pl.BlockSpec((None, LANES, D_ATT), lambda b, i: (b * nb + i, 0, 0)),
            pl.BlockSpec((None, LANES, N_IDX_HEADS * IDX_DIM), lambda b, i: (b * nb + i, 0, 0)),
            pl.BlockSpec((Q_BLOCK, LANES), lambda b, i: (b * nb + i, 0)),
            pl.BlockSpec((Q_BLOCK, D_ATT), lambda b, i: (b * nb + i, 0)),
            pl.BlockSpec((seq, KV_DIM), lambda b, i: (b, 0)),
            pl.BlockSpec((seq, KV_DIM), lambda b, i: (b, 0)),
            pl.BlockSpec((seq, LANES), lambda b, i: (b, 0)),
        ],
        out_specs=pl.BlockSpec((Q_BLOCK, D_ATT), lambda b, i: (b * nb + i, 0)),
        out_shape=jax.ShapeDtypeStruct((t, D_ATT), BF16),
        scratch_shapes=[
            pltpu.VMEM((2 * N_KV_HEADS, seq, LANES), BF16),
            pltpu.VMEM((2 * N_KV_HEADS, n_t, LANES, KT), BF16),
            pltpu.VMEM((2, seq, LANES), BF16),
            pltpu.VMEM((N_ATT_HEADS, 2, LANES, LANES), F32),
            pltpu.VMEM((n_t, KT, LANES), I32),
            pltpu.VMEM((n_t, KT, LANES), F32),
            pltpu.VMEM((N_ATT_HEADS, n_t, KT, LANES), F32),
        ],
        compiler_params=_cparams(("arbitrary", "arbitrary")),
        name="dsa_prompt",
    )(rel_table, q_blk, qi_blk, small, za, k2d, v2d, small)


def _softplus(x):
    return jnp.maximum(x, 0.0) + jnp.log1p(jnp.exp(-jnp.abs(x)))


def _lane_pick(x, idx):
    lane = lax.broadcasted_iota(I32, x.shape, 1)
    return jnp.sum(jnp.where(lane == idx, x, 0.0), axis=1, keepdims=True)


def _gated_norm(o, gn, z):
    y = o * lax.rsqrt(jnp.mean(o * o, axis=-1, keepdims=True) + RMS_EPS)
    return (y * gn) * _silu(z)


def _gdn_prompt_body(alog_ref, dtb_ref, xq_ref, xk_ref, xv_ref, wq_ref, wk_ref, wv_ref, sm_ref, zg_ref, gn_ref,
                     o_ref, sfin_ref, xs_ref, q_s, k_s, v_s, g_s, b_s, *, seq, hp):
    h0 = pl.program_id(1) * hp
    hist = SUBLANES
    slabs = [slice(hh * LANES, (hh + 1) * LANES) for hh in range(hp)]

    def conv_into(x_ref, w_ref, dst, post):
        xs_ref[0:hist, :] = jnp.zeros((hist, hp * LANES), F32)
        xs_ref[hist:hist + seq, :] = x_ref[...]
        base = hist - (CONV_W - 1)
        for hs in slabs:
            acc = xs_ref[base:base + seq, hs] * w_ref[0:1, hs]
            for j in range(1, CONV_W):
                acc = acc + xs_ref[base + j:base + j + seq, hs] * w_ref[j:j + 1, hs]
            dst[:, hs] = post(_silu(acc))

    def l2n(x):
        return x * lax.rsqrt(jnp.sum(x * x, axis=-1, keepdims=True) + 1e-6)

    conv_into(xq_ref, wq_ref, q_s, lambda x: l2n(x) * (GDN_HEAD_DIM ** -0.5))
    conv_into(xk_ref, wk_ref, k_s, l2n)
    conv_into(xv_ref, wv_ref, v_s, lambda x: x)

    lane_row = lax.broadcasted_iota(I32, (1, LANES), 1)
    alog_row = jnp.zeros((1, LANES), F32)
    dtb_row = jnp.zeros((1, LANES), F32)
    for hh in range(hp):
        alog_row = jnp.where(lane_row == SM_AG + h0 + hh, alog_ref[h0 + hh], alog_row)
        dtb_row = jnp.where(lane_row == SM_AG + h0 + hh, dtb_ref[h0 + hh], dtb_row)
    sm = sm_ref[...]
    g_all = -jnp.exp(alog_row) * _softplus(sm + dtb_row)
    beta_all = 1.0 / (1.0 + jnp.exp(-sm))
    for hh, hs in enumerate(slabs):
        g_s[:, hs] = jnp.broadcast_to(_lane_pick(g_all, SM_AG + h0 + hh), (seq, LANES))
        b_s[:, hs] = jnp.broadcast_to(_lane_pick(beta_all, SM_BG + h0 + hh), (seq, LANES))

    c = GDN_C
    ri = lax.broadcasted_iota(I32, (c, c), 0)
    ci = lax.broadcasted_iota(I32, (c, c), 1)
    tril = ri >= ci
    strict = ri > ci
    tril_f = tril.astype(F32)
    eye = (ri == ci).astype(F32)
    gn = gn_ref[...]
    off_masks = []
    for lg in range(int(math.log2(c))):
        same_pair = (ri >> (lg + 1)) == (ci >> (lg + 1))
        off_masks.append(same_pair & (((ri >> lg) & 1) == 1) & (((ci >> lg) & 1) == 0))

    tril_b = tril.astype(BF16)

    def cumsum_rows(g):
        hi = g.astype(BF16)
        r1 = g - hi.astype(F32)
        mid = r1.astype(BF16)
        lo = (r1 - mid.astype(F32)).astype(BF16)
        return sum(jnp.dot(tril_b, piece, preferred_element_type=F32) for piece in (hi, mid, lo))

    cpi = GDN_CHUNKS_PER_ITER

    def rows_of(n):
        return pl.ds(pl.multiple_of(n * c, c), c)

    def local_phase(items):
        ids = range(len(items))
        q = [q_s[rows_of(n), hs] for n, hs in items]
        k = [k_s[rows_of(n), hs] for n, hs in items]
        v = [v_s[rows_of(n), hs] for n, hs in items]
        bb = [b_s[rows_of(n), hs] for n, hs in items]
        gcum = [cumsum_rows(g_s[rows_of(n), hs]) for n, hs in items]
        gcum_row = [g.T for g in gcum]
        decay = [jnp.where(tril, jnp.exp(jnp.where(tril, gcum[e] - gcum_row[e], 0.0)), 0.0) for e in ids]
        eg = [jnp.exp(g) for g in gcum]
        kb = [k[e] * bb[e] for e in ids]
        vb = [v[e] * bb[e] for e in ids]
        kq = [_bdot_nt(jnp.concatenate([kb[e], q[e]], axis=0), k[e]) for e in ids]
        a_mat = [jnp.where(strict, kq[e][:c] * decay[e], 0.0) for e in ids]
        attn = [kq[e][c:] * decay[e] for e in ids]
        x = [eye - jnp.where(off_masks[0], a_mat[e], 0.0) for e in ids]
        for om in off_masks[1:]:
            inner = [_bdot(jnp.where(om, a_mat[e], 0.0), x[e]) for e in ids]
            x = [x[e] - _bdot(x[e], inner[e]) for e in ids]
        uw = [_bdot(x[e], jnp.concatenate([vb[e], kb[e] * eg[e]], axis=1)) for e in ids]
        g_last = [g[c - 1:c, :] for g in gcum]
        k_dec_t = [(k[e] * jnp.exp(g_last[e] - gcum[e])).T for e in ids]
        wq = [jnp.concatenate([uw[e][:, GDN_HEAD_DIM:], q[e] * eg[e]], axis=0) for e in ids]
        ak = [jnp.concatenate([attn[e], k_dec_t[e]], axis=0) for e in ids]
        return [dict(u=uw[e][:, :GDN_HEAD_DIM], wq=wq[e], ak=ak[e], s_dec=jnp.exp(g_last[e])) for e in ids]

    def state_phase(n, loc, states):
        heads = range(hp)
        wq_s = [_bdot(loc[h]["wq"], states[h]) for h in heads]
        v_new = [loc[h]["u"] - wq_s[h][:c] for h in heads]
        ak_v = [_bdot(loc[h]["ak"], v_new[h]) for h in heads]
        for h, hs in enumerate(slabs):
            o = wq_s[h][c:] + ak_v[h][:c]
            o_ref[rows_of(n), hs] = _gated_norm(o, gn, zg_ref[rows_of(n), hs]).astype(BF16)
        return tuple(states[h] * loc[h]["s_dec"] + ak_v[h][c:] for h in heads)

    def chunk_group(p, states):
        ns = [p * cpi + r for r in range(cpi)]
        loc = local_phase([(n, hs) for n in ns for hs in slabs])
        for r, n in enumerate(ns):
            states = state_phase(n, loc[r * hp:(r + 1) * hp], states)
        return states

    zero_state = jnp.zeros((GDN_HEAD_DIM, GDN_HEAD_DIM), F32)
    finals = lax.fori_loop(0, seq // (c * cpi), chunk_group, (zero_state,) * hp)
    for hh in range(hp):
        sfin_ref[hh] = finals[hh]


GDN_HEADS_PER_STEP = 4


def _gdn_prompt(a_log, dt_bias, qkv2d, conv_w, small, zg, gdn_norm, batch, seq):
    nh = N_GDN_HEADS
    hp = GDN_HEADS_PER_STEP
    ng = nh // hp
    t = batch * seq
    w = hp * LANES
    once = pl.Buffered(1)
    blk = lambda off: pl.BlockSpec((seq, w), lambda b, j: (b, off + j), pipeline_mode=once)
    wblk = lambda off: pl.BlockSpec((CONV_W, w), lambda b, j: (0, off + j))
    return pl.pallas_call(
        functools.partial(_gdn_prompt_body, seq=seq, hp=hp),
        grid=(batch, ng),
        in_specs=[pl.BlockSpec(memory_space=pltpu.SMEM), pl.BlockSpec(memory_space=pltpu.SMEM),
                  blk(0), blk(ng), blk(2 * ng), wblk(0), wblk(ng), wblk(2 * ng),
                  pl.BlockSpec((seq, LANES), lambda b, j: (b, 0)),
                  pl.BlockSpec((seq, w), lambda b, j: (b, j), pipeline_mode=once),
                  pl.BlockSpec((1, LANES), lambda b, j: (0, 0))],
        out_specs=[pl.BlockSpec((seq, w), lambda b, j: (b, j)),
                   pl.BlockSpec((None, hp, GDN_HEAD_DIM, GDN_HEAD_DIM), lambda b, j: (b, j, 0, 0))],
        out_shape=[jax.ShapeDtypeStruct((t, D_GDN), BF16),
                   jax.ShapeDtypeStruct((batch, nh, GDN_HEAD_DIM, GDN_HEAD_DIM), F32)],
        scratch_shapes=[pltpu.VMEM((seq + 2 * SUBLANES, w), F32)] + [pltpu.VMEM((seq, w), F32)] * 5,
        compiler_params=_cparams(("parallel", "arbitrary")),
        name="gdn_prompt",
    )(a_log, dt_bias, qkv2d, qkv2d, qkv2d, conv_w, conv_w, conv_w, small, zg, gdn_norm.reshape(1, LANES))


def _outproj_body(*refs, gate_att):
    if gate_att:
        x_ref, att_ref, za_ref, gdn_ref, w_ref, g_ref, y_ref = refs
        att = (att_ref[...] * _silu(za_ref[...])).astype(BF16)
    else:
        x_ref, att_ref, gdn_ref, w_ref, g_ref, y_ref = refs
        att = att_ref[...]
    y = x_ref[...] + jnp.dot(att, w_ref[0:D_ATT, :], preferred_element_type=F32) \
        + jnp.dot(gdn_ref[...], w_ref[D_ATT:D_ATT + D_GDN, :], preferred_element_type=F32)
    y = y * lax.rsqrt(jnp.mean(y * y, axis=-1, keepdims=True) + RMS_EPS)
    y_ref[...] = y * g_ref[...]


def _outproj(x2d, att, gdn, w_bf, norm_g, tm, za=None):
    t, d = x2d.shape
    row = lambda n: pl.BlockSpec((tm, n), lambda i: (i, 0))
    full = lambda a, b: pl.BlockSpec((a, b), lambda i: (0, 0))
    ins = [x2d, att] + ([za] if za is not None else []) + [gdn, w_bf, norm_g.reshape(1, d)]
    specs = [row(d), row(D_ATT)] + ([row(D_ATT)] if za is not None else []) + \
            [row(D_GDN), full(D_ATT + D_GDN, d), full(1, d)]
    return pl.pallas_call(
        functools.partial(_outproj_body, gate_att=za is not None),
        grid=(t // tm,),
        in_specs=specs,
        out_specs=row(d),
        out_shape=jax.ShapeDtypeStruct((t, d), F32),
        compiler_params=_cparams(("parallel",)),
        name="outproj_gated" if za is not None else "outproj",
    )(*ins)


IDX_PAGES_PER_STEP = 64
ATT_PAGES_PER_STEP = 64
SAMPLE_Q_ROWS = 16


def _idx_scores_body(pt_ref, qi_ref, wi_ref, *rest):
    page_refs = rest[:IDX_PAGES_PER_STEP]
    o_ref = rest[IDX_PAGES_PER_STEP]
    qi = qi_ref[0]
    wi = wi_ref[0] * (N_IDX_HEADS ** -0.5)
    for j, pr in enumerate(page_refs):
        s = _bdot(qi, pr[...])
        sc = jnp.sum(jnp.maximum(s, 0.0) * wi, axis=0, keepdims=True)
        o_ref[0, j:j + 1, :] = sc + 0.0


def _idx_scores(page_table, qi3, wi3, kidx_pages):
    db, n_pages = page_table.shape
    page = kidx_pages.shape[2]
    pg = IDX_PAGES_PER_STEP

    def page_spec(j):
        return pl.BlockSpec((None, IDX_DIM, page), lambda b, s, pt: (pt[b, s * pg + j], 0, 0))

    grid_spec = pltpu.PrefetchScalarGridSpec(
        num_scalar_prefetch=1,
        grid=(db, n_pages // pg),
        in_specs=[pl.BlockSpec((1, N_IDX_HEADS, IDX_DIM), lambda b, s, pt: (b, 0, 0)),
                  pl.BlockSpec((1, N_IDX_HEADS, 1), lambda b, s, pt: (b, 0, 0))]
                 + [page_spec(j) for j in range(pg)],
        out_specs=pl.BlockSpec((1, pg, page), lambda b, s, pt: (b, s, 0)),
    )
    return pl.pallas_call(
        _idx_scores_body,
        grid_spec=grid_spec,
        out_shape=jax.ShapeDtypeStruct((db, n_pages, page), F32),
        compiler_params=_cparams(("parallel", "arbitrary")),
        name="sample_idx_scores",
    )(page_table, qi3, wi3, *([kidx_pages] * pg))


def _sample_select_body(sc_ref, qi_ref, sm_ref, mask_ref, key_ref, *, n_tiles, topk):
    rows = sc_ref.shape[1]
    lane = lax.broadcasted_iota(I32, (rows, LANES), 1)

    def fill(t, carry):
        key_ref[t] = _sort_key(sc_ref[t])
        return carry

    lax.fori_loop(0, n_tiles, fill, 0)
    sm = sm_ref[...]
    ki_new = sm[:, 0:IDX_DIM].astype(BF16).astype(F32)
    acc = jnp.zeros((rows, 1), F32)
    for hd in range(N_IDX_HEADS):
        qh = qi_ref[:, hd * IDX_DIM:(hd + 1) * IDX_DIM].astype(F32)
        s = jnp.sum(qh * ki_new, axis=1, keepdims=True)
        acc = acc + jnp.maximum(s, 0.0) * (sm[:, SM_WI + hd:SM_WI + hd + 1] * (N_IDX_HEADS ** -0.5))
    sc_new = jnp.broadcast_to(acc + 0.0, (rows, LANES))
    key_ref[n_tiles] = jnp.where(lane == 0, _sort_key(sc_new), INT_MIN)

    def count_where(pred):
        def body(t, a):
            return a + pred(key_ref[t]).astype(F32)
        a = lax.fori_loop(0, n_tiles + 1, body, jnp.zeros((rows, LANES), F32))
        return jnp.sum(a, axis=1, keepdims=True)

    thr, _ = _kth_key_search(lambda cand: count_where(lambda kt: kt >= cand), (rows, 1), float(topk),
                             float(n_tiles * LANES + 1))
    need = float(topk) - count_where(lambda kt: kt > thr)
    cnt_eq = count_where(lambda kt: kt == thr)
    any_tie = jnp.max(jnp.where(cnt_eq > need, 1.0, 0.0)) > 0.0

    @pl.when(jnp.logical_not(any_tie))
    def _():
        def body(t, carry):
            mask_ref[t] = jnp.where(key_ref[t] >= thr, 0.0, NEG_BIG)
            return carry
        lax.fori_loop(0, n_tiles + 1, body, 0)

    @pl.when(any_tie)
    def _():
        sub = lax.broadcasted_iota(I32, (LANES, LANES), 0)
        lane2 = lax.broadcasted_iota(I32, (LANES, LANES), 1)
        upper = (sub <= lane2).astype(BF16)
        ones = jnp.ones((LANES, LANES), BF16)

        def body(t, offset):
            kt = key_ref[t]
            eq = (kt == thr).astype(BF16)
            prefix = jnp.dot(eq, upper, preferred_element_type=F32) + offset
            sel = (kt > thr) | ((eq > 0) & (prefix <= need))
            mask_ref[t] = jnp.where(sel, 0.0, NEG_BIG)
            return offset + jnp.dot(eq, ones, preferred_element_type=F32)

        lax.fori_loop(0, n_tiles + 1, body, jnp.zeros((rows, LANES), F32))


def _sample_select(scores_t, qi2d, small, topk):
    n_tiles, db, page = scores_t.shape
    vm = pl.BlockSpec(memory_space=pltpu.VMEM)
    return pl.pallas_call(
        functools.partial(_sample_select_body, n_tiles=n_tiles, topk=topk),
        in_specs=[vm, vm, vm],
        out_specs=vm,
        out_shape=jax.ShapeDtypeStruct((n_tiles + 1, db, page), F32),
        scratch_shapes=[pltpu.VMEM((n_tiles + 1, db, page), I32)],
        compiler_params=pltpu.CompilerParams(vmem_limit_bytes=VMEM_LIMIT),
        name="sample_select",
    )(scores_t, qi2d, small)


def _sample_attn_body(pt_ref, q_ref, mask_ref, mnew_ref, knew_ref, vnew_ref, relt_ref, *rest, past_len, page):
    pg = ATT_PAGES_PER_STEP
    k_refs = rest[:pg]
    v_refs = rest[pg:2 * pg]
    o_ref, m_s, l_s, acc_s = rest[2 * pg:]
    s = pl.program_id(1)
    n_steps = pl.num_programs(1)
    nh = SAMPLE_Q_ROWS
    far_bucket = _far_bucket_checked(MAX_DISTANCE + 1)

    @pl.when(s == 0)
    def _():
        m_s[...] = jnp.full(m_s.shape, NEG_BIG, F32)
        l_s[...] = jnp.zeros(l_s.shape, F32)
        acc_s[...] = jnp.zeros(acc_s.shape, F32)

    relt = relt_ref[...]
    c_far = relt[:, far_bucket:far_bucket + 1]

    def bias_of(dist):
        bucket = _rel_bucket(dist)
        acc = jnp.zeros((nh, dist.shape[1]), F32)
        for bk in range(N_BUCKETS):
            acc = jnp.where(bucket == bk, relt[:, bk:bk + 1], acc)
        return acc - c_far

    q = q_ref[0].astype(BF16)
    logits = []
    for j in range(pg):
        l = _bdot(q, k_refs[j][...]) + mask_ref[0, j:j + 1, :]
        logits.append(l)
    logits = jnp.concatenate(logits, axis=1)

    kpos = (s * pg) * page + lax.broadcasted_iota(I32, (1, pg * page), 1)
    near = past_len - ((s + 1) * pg * page - 1) <= MAX_DISTANCE
    logits = logits + lax.cond(near, lambda: bias_of(past_len - kpos),
                               lambda: jnp.zeros((nh, pg * page), F32))

    m_old = m_s[...]
    m_new = jnp.maximum(m_old, jnp.max(logits, axis=1, keepdims=True))
    alpha = jnp.exp(m_old - m_new)
    p = jnp.exp(logits - m_new)
    l_new = l_s[...] * alpha + jnp.sum(p, axis=1, keepdims=True)
    acc = acc_s[...] * alpha
    for j in range(pg):
        acc = acc + _bdot_nt(p[:, j * page:(j + 1) * page], v_refs[j][...])
    m_s[...] = m_new
    l_s[...] = l_new
    acc_s[...] = acc

    @pl.when(s == n_steps - 1)
    def _():
        kn = knew_ref[0].astype(BF16).astype(F32)
        vn = vnew_ref[0].astype(BF16).astype(F32)
        ln = jnp.sum(q.astype(F32) * kn, axis=1, keepdims=True) + bias_of(jnp.zeros((1, 1), I32)) \
            + mnew_ref[0][:, 0:1]
        m_fin = jnp.maximum(m_new, ln)
        a2 = jnp.exp(m_new - m_fin)
        pn = jnp.exp(ln - m_fin)
        l_fin = l_new * a2 + pn
        res = (acc * a2 + pn.astype(BF16).astype(F32) * vn) / l_fin
        row = lax.broadcasted_iota(I32, res.shape, 0)
        hpg = N_ATT_HEADS // N_KV_HEADS
        o_ref[0] = jnp.where((row >= hpg) & (row < 2 * hpg), pltpu.roll(res, ATT_HEAD_DIM, 1), res)


def _sample_attn(page_table, q_lh, mask_pages, mask_new, k_new, v_new, rel_t, k_pages, v_pages, past_len):
    db, n_pages = page_table.shape
    page = k_pages.shape[2]
    pg = ATT_PAGES_PER_STEP

    def page_spec(j):
        return pl.BlockSpec((None, KV_DIM, page), lambda b, s, pt: (pt[b, s * pg + j], 0, 0))

    row3 = lambda n: pl.BlockSpec((1, 1, n), lambda b, s, pt: (b, 0, 0))
    grid_spec = pltpu.PrefetchScalarGridSpec(
        num_scalar_prefetch=1,
        grid=(db, n_pages // pg),
        in_specs=[pl.BlockSpec((1, SAMPLE_Q_ROWS, LANES), lambda b, s, pt: (b, 0, 0)),
                  pl.BlockSpec((1, pg, page), lambda b, s, pt: (b, s, 0)),
                  row3(LANES), row3(KV_DIM), row3(KV_DIM),
                  pl.BlockSpec((SAMPLE_Q_ROWS, N_BUCKETS), lambda b, s, pt: (0, 0))]
                 + [page_spec(j) for j in range(pg)] * 2,
        out_specs=pl.BlockSpec((1, SAMPLE_Q_ROWS, LANES), lambda b, s, pt: (b, 0, 0)),
        scratch_shapes=[pltpu.VMEM((SAMPLE_Q_ROWS, 1), F32), pltpu.VMEM((SAMPLE_Q_ROWS, 1), F32),
                        pltpu.VMEM((SAMPLE_Q_ROWS, LANES), F32)],
    )
    return pl.pallas_call(
        functools.partial(_sample_attn_body, past_len=past_len, page=page),
        grid_spec=grid_spec,
        out_shape=jax.ShapeDtypeStruct((db, SAMPLE_Q_ROWS, LANES), F32),
        compiler_params=_cparams(("parallel", "arbitrary")),
        name="sample_attn",
    )(page_table, q_lh, mask_pages, mask_new, k_new, v_new, rel_t, *([k_pages] * pg), *([v_pages] * pg))


def _gdn_sample_body(alog_ref, dtb_ref, x_ref, cst_ref, w_ref, sm_ref, zg_ref, gn_ref, s0_ref,
                     o_ref, s_ref, cnew_ref):
    x = x_ref[0]
    cst = cst_ref[...]
    w = w_ref[...]
    acc = cst[0:1, :] * w[0:1, :]
    for j in range(1, CONV_W - 1):
        acc = acc + cst[j:j + 1, :] * w[j:j + 1, :]
    acc = acc + x * w[CONV_W - 1:CONV_W, :]
    xc = _silu(acc)
    cnew_ref[0:CONV_W - 2, :] = cst[1:CONV_W - 1, :]
    cnew_ref[CONV_W - 2:CONV_W - 1, :] = x
    sm = sm_ref[0]
    gn = gn_ref[...]
    d = GDN_HEAD_DIM
    for h in range(N_GDN_HEADS):
        q = xc[:, h * d:(h + 1) * d]
        k = xc[:, D_GDN + h * d:D_GDN + (h + 1) * d]
        v = xc[:, 2 * D_GDN + h * d:2 * D_GDN + (h + 1) * d]
        q = q * lax.rsqrt(jnp.sum(q * q, axis=-1, keepdims=True) + 1e-6) * (d ** -0.5)
        k = k * lax.rsqrt(jnp.sum(k * k, axis=-1, keepdims=True) + 1e-6)
        a_neg = -jnp.exp(jnp.zeros((1, 1), F32) + alog_ref[h])
        g = a_neg * _softplus(sm[:, SM_AG + h:SM_AG + h + 1] + dtb_ref[h])
        beta = 1.0 / (1.0 + jnp.exp(-sm[:, SM_BG + h:SM_BG + h + 1]))
        st = s0_ref[h] * jnp.exp(g)
        k_col = jnp.broadcast_to(k, (d, d)).T
        q_col = jnp.broadcast_to(q, (d, d)).T
        kv = jnp.sum(k_col * st, axis=0, keepdims=True)
        delta = (v - kv) * beta
        st = st + k_col * delta
        s_ref[h] = st
        o = jnp.sum(q_col * st, axis=0, keepdims=True)
        o_ref[0, :, h * d:(h + 1) * d] = _gated_norm(o, gn, zg_ref[0][:, h * d:(h + 1) * d]).astype(BF16)


def _gdn_sample(a_log, dt_bias, qkv3, state_conv_l, conv_w, small3, zg3, gdn_norm, state_ssm_l):
    db = qkv3.shape[0]
    d = GDN_HEAD_DIM
    nh = N_GDN_HEADS
    row3 = lambda n: pl.BlockSpec((1, 1, n), lambda b: (b, 0, 0))
    return pl.pallas_call(
        _gdn_sample_body,
        grid=(db,),
        in_specs=[pl.BlockSpec(memory_space=pltpu.SMEM), pl.BlockSpec(memory_space=pltpu.SMEM),
                  row3(3 * D_GDN),
                  pl.BlockSpec((None, CONV_W - 1, 3 * D_GDN), lambda b: (b, 0, 0)),
                  pl.BlockSpec((CONV_W, 3 * D_GDN), lambda b: (0, 0)),
                  row3(LANES), row3(D_GDN),
                  pl.BlockSpec((1, LANES), lambda b: (0, 0)),
                  pl.BlockSpec((None, nh, d, d), lambda b: (b, 0, 0, 0))],
        out_specs=[row3(D_GDN),
                   pl.BlockSpec((None, nh, d, d), lambda b: (b, 0, 0, 0)),
                   pl.BlockSpec((None, CONV_W - 1, 3 * D_GDN), lambda b: (b, 0, 0))],
        out_shape=[jax.ShapeDtypeStruct((db, 1, D_GDN), BF16),
                   jax.ShapeDtypeStruct((db, nh, d, d), F32),
                   jax.ShapeDtypeStruct((db, CONV_W - 1, 3 * D_GDN), F32)],
        compiler_params=_cparams(("parallel",)),
        name="gdn_sample",
    )(a_log, dt_bias, qkv3, state_conv_l, conv_w, small3, zg3, gdn_norm.reshape(1, LANES), state_ssm_l)


def kernel(x_prompt, x_sample, cache_k, cache_v, cache_kidx, state_ssm, state_conv, page_table, norm_in, w_in,
           conv_w, a_log, dt_bias, gdn_norm, w_out, rel_table, norm_final):
    depth = w_in.shape[0]
    assert depth == 1, "single-layer model"
    batch, seq, d_model = x_prompt.shape
    db, dec_seq, _ = x_sample.shape
    assert dec_seq == 1 and seq % KT == 0 and seq % (GDN_C * GDN_CHUNKS_PER_ITER) == 0 and seq % PROJ_ROWS == 0
    assert (batch * seq) % (2 * PROJ_ROWS) == 0
    n_pool, page = cache_k.shape[1], cache_k.shape[2]
    n_pages = page_table.shape[1]
    past_len = n_pages * page
    assert page == LANES and n_pages % IDX_PAGES_PER_STEP == 0 and n_pages % ATT_PAGES_PER_STEP == 0

    lyr = 0
    w_pad = _prep_w_in(w_in[lyr])
    w_out_bf = w_out[lyr].astype(BF16)

    xp = x_prompt.reshape(batch * seq, d_model)
    q_blk, k2d, v2d, za, qi_blk, small, qkv2d, zg, k_t, v_t, ki_t = _inproj(
        xp, norm_in[lyr], w_pad, tm=PROJ_ROWS, blocked=True, seq=seq)
    att_g = _dsa_prompt_t(rel_table, q_blk, qi_blk, small, za, k2d, v2d, batch, seq)
    gdn_g, s_fin = _gdn_prompt(a_log[lyr], dt_bias[lyr], qkv2d, conv_w[lyr], small, zg, gdn_norm[lyr], batch, seq)
    y_prompt = _outproj(xp, att_g, gdn_g, w_out_bf, norm_final, tm=2 * PROJ_ROWS).reshape(batch, seq, d_model)
    k_prompt = jnp.transpose(k_t.reshape(batch, N_KV_HEADS, ATT_HEAD_DIM, seq), (0, 3, 1, 2))[None]
    v_prompt = jnp.transpose(v_t.reshape(batch, N_KV_HEADS, ATT_HEAD_DIM, seq), (0, 3, 1, 2))[None]
    kidx_prompt = jnp.transpose(ki_t, (0, 2, 1))[None]
    ssm_prompt = s_fin[None]
    conv_prompt = qkv2d.reshape(batch, seq, 3 * D_GDN)[:, seq - (CONV_W - 1):][None]

    xs = x_sample.reshape(db, d_model)
    q_s, k_s, v_s, za_s, qi_s, small_s, qkv_s, zg_s = _inproj(xs, norm_in[lyr], w_pad, tm=db, blocked=False)
    kidx_t = jnp.transpose(cache_kidx[lyr], (0, 2, 1))
    k_pages_t = jnp.transpose(cache_k[lyr], (0, 2, 3, 1)).reshape(n_pool, KV_DIM, page)
    v_pages_t = jnp.transpose(cache_v[lyr], (0, 2, 3, 1)).reshape(n_pool, KV_DIM, page)
    scores = _idx_scores(page_table, qi_s.reshape(db, N_IDX_HEADS, IDX_DIM),
                         small_s[:, SM_WI:SM_WI + N_IDX_HEADS].reshape(db, N_IDX_HEADS, 1),
                         kidx_t)
    topk = min(TOPK_MAX, (past_len + dec_seq) // 4)
    mask_t = _sample_select(jnp.transpose(scores, (1, 0, 2)), qi_s, small_s, topk)
    mask_pages = jnp.transpose(mask_t[:n_pages], (1, 0, 2))
    mask_new = mask_t[n_pages].reshape(db, 1, page)
    hpg = N_ATT_HEADS // N_KV_HEADS
    q8 = q_s.reshape(db, N_ATT_HEADS, ATT_HEAD_DIM).astype(F32)
    zq = jnp.zeros((db, hpg, ATT_HEAD_DIM), F32)
    q_lh = jnp.concatenate([jnp.concatenate([q8[:, :hpg], zq], axis=2),
                            jnp.concatenate([zq, q8[:, hpg:]], axis=2),
                            jnp.zeros((db, SAMPLE_Q_ROWS - N_ATT_HEADS, LANES), F32)], axis=1)
    rel_t = jnp.concatenate([rel_table.T, jnp.zeros((SAMPLE_Q_ROWS - N_ATT_HEADS, N_BUCKETS), F32)], axis=0)
    att_raw = _sample_attn(page_table, q_lh, mask_pages, mask_new, k_s.reshape(db, 1, KV_DIM),
                           v_s.reshape(db, 1, KV_DIM), rel_t,
                           k_pages_t, v_pages_t, past_len)
    att_s = att_raw[:, :N_ATT_HEADS, :ATT_HEAD_DIM].reshape(db, D_ATT)
    gdn_s, s_new, conv_new = _gdn_sample(a_log[lyr], dt_bias[lyr], qkv_s.reshape(db, 1, 3 * D_GDN), state_conv[lyr],
                                         conv_w[lyr], small_s.reshape(db, 1, LANES), zg_s.reshape(db, 1, D_GDN),
                                         gdn_norm[lyr], state_ssm[lyr])
    y_sample = _outproj(xs, att_s, gdn_s.reshape(db, D_GDN), w_out_bf, norm_final, tm=db,
                        za=za_s).reshape(db, 1, d_model)
    k_sample = k_s.reshape(1, db, 1, N_KV_HEADS, ATT_HEAD_DIM)
    v_sample = v_s.reshape(1, db, 1, N_KV_HEADS, ATT_HEAD_DIM)
    kidx_sample = small_s[:, :IDX_DIM].reshape(1, db, 1, IDX_DIM)

    return (y_prompt, y_sample, k_prompt, v_prompt, kidx_prompt, ssm_prompt, conv_prompt,
            k_sample, v_sample, kidx_sample, s_new[None], conv_new[None])
```

```python
import functools
import math

import numpy as np
import jax
import jax.numpy as jnp
from jax import lax
from jax.experimental import pallas as pl
from jax.experimental.pallas import tpu as pltpu

F32 = jnp.float32
BF16 = jnp.bfloat16
I32 = jnp.int32

N_ATT_HEADS = 8
ATT_HEAD_DIM = 64
N_KV_HEADS = 2
D_ATT = N_ATT_HEADS * ATT_HEAD_DIM
KV_DIM = N_KV_HEADS * ATT_HEAD_DIM
N_IDX_HEADS = 16
IDX_DIM = 64
TOPK_MAX = 256
N_GDN_HEADS = 4
GDN_HEAD_DIM = 128
D_GDN = N_GDN_HEADS * GDN_HEAD_DIM
CONV_W = 4
N_BUCKETS = 32
MAX_DISTANCE = 128
Q_BLOCK = 128
RMS_EPS = 1e-6
NEG_BIG = -1e30
PROJ_SIZES = (D_ATT, KV_DIM, KV_DIM, D_ATT, N_IDX_HEADS * IDX_DIM, IDX_DIM, N_IDX_HEADS,
              3 * D_GDN, D_GDN, N_GDN_HEADS, N_GDN_HEADS)

LANES = 128
SUBLANES = 8
VMEM_LIMIT = 56 * 1024 * 1024

OFF_Q = 0
OFF_K = OFF_Q + D_ATT
OFF_V = OFF_K + KV_DIM
OFF_ZA = OFF_V + KV_DIM
OFF_QI = OFF_ZA + D_ATT
OFF_SM = OFF_QI + N_IDX_HEADS * IDX_DIM
OFF_QKV = OFF_SM + LANES
OFF_ZG = OFF_QKV + 3 * D_GDN
D_PROJ_PAD = OFF_ZG + D_GDN
SM_WI = IDX_DIM
SM_AG = SM_WI + N_IDX_HEADS
SM_BG = SM_AG + N_GDN_HEADS

PROJ_ROWS = 512
GDN_C = 128
GDN_CHUNKS_PER_ITER = 4
KEY_NEG_BIG = int(np.array(NEG_BIG, np.float32).view(np.int32)) ^ 0x7FFFFFFF
INT_MIN = -2 ** 31


def _cparams(sem):
    return pltpu.CompilerParams(dimension_semantics=sem, vmem_limit_bytes=VMEM_LIMIT)


def _silu(x):
    return x * (1.0 / (1.0 + jnp.exp(-x)))


def _bdot(a, b):
    return jnp.dot(a.astype(BF16), b.astype(BF16), preferred_element_type=F32)


def _bdot_nt(a, b):
    return lax.dot_general(a.astype(BF16), b.astype(BF16), (((1,), (1,)), ((), ())),
                           preferred_element_type=F32)


def _sort_key(x):
    i = pltpu.bitcast(x, I32)
    return jnp.where(i < 0, i ^ 0x7FFFFFFF, i)


def _inproj_body(x_ref, g_ref, w_ref, q_ref, k_ref, v_ref, za_ref, qi_ref, sm_ref, qkv_ref, zg_ref, *t_refs,
                 blocked):
    x = x_ref[...]
    ms = jnp.mean(x * x, axis=-1, keepdims=True)
    h = ((x * lax.rsqrt(ms + RMS_EPS)) * g_ref[...]).astype(BF16)

    def mm(a, b):
        return jnp.dot(h, w_ref[:, a:b], preferred_element_type=F32)

    q = mm(OFF_Q, OFF_K) * (ATT_HEAD_DIM ** -0.5)
    qi = mm(OFF_QI, OFF_SM) * (IDX_DIM ** -0.5)
    if blocked:
        for r in range(x.shape[0] // Q_BLOCK):
            rs = slice(r * Q_BLOCK, (r + 1) * Q_BLOCK)
            for j in range(D_ATT // LANES):
                cs = slice(j * LANES, (j + 1) * LANES)
                q_ref[r, :, cs] = q[rs, cs].T.astype(BF16)
            for j in range(N_IDX_HEADS * IDX_DIM // LANES):
                cs = slice(j * LANES, (j + 1) * LANES)
                qi_ref[r, :, cs] = qi[rs, cs].T.astype(BF16)
    else:
        q_ref[...] = q.astype(BF16)
        qi_ref[...] = qi.astype(BF16)
    k = mm(OFF_K, OFF_V)
    v = mm(OFF_V, OFF_ZA)
    sm = mm(OFF_SM, OFF_QKV)
    k_ref[...] = k
    v_ref[...] = v
    sm_ref[...] = sm
    za_ref[...] = mm(OFF_ZA, OFF_QI)
    qkv_ref[...] = mm(OFF_QKV, OFF_ZG)
    zg_ref[...] = mm(OFF_ZG, D_PROJ_PAD)
    if t_refs:
        kt_ref, vt_ref, kit_ref = t_refs
        kt_ref[...] = k.T
        vt_ref[...] = v.T
        kit_ref[...] = sm.T[0:IDX_DIM, :]


def _prep_w_in(w):
    splits = np.cumsum(PROJ_SIZES)[:-1].tolist()
    q, k, v, z_a, qi, ki, wi, qkv, z_g, a_g, b_g = jnp.split(w, splits, axis=1)
    pad = jnp.zeros((w.shape[0], LANES - (SM_BG + N_GDN_HEADS)), w.dtype)
    small = jnp.concatenate([ki, wi, a_g, b_g, pad], axis=1)
    return jnp.concatenate([q, k, v, z_a, qi, small, qkv, z_g], axis=1).astype(BF16)


def _inproj(x2d, norm_g, w_pad, tm, blocked, seq=None):
    t, d = x2d.shape
    nblk = t // tm
    t_shapes, t_specs = [], []
    if blocked:
        spb = seq // tm
        for n in (KV_DIM, KV_DIM, IDX_DIM):
            t_shapes.append(jax.ShapeDtypeStruct((t // seq, n, seq), F32))
            t_specs.append(pl.BlockSpec((None, n, tm), lambda i: (i // spb, 0, i % spb)))
        rb = tm // Q_BLOCK
        q_shape = jax.ShapeDtypeStruct((t // Q_BLOCK, LANES, D_ATT), BF16)
        qi_shape = jax.ShapeDtypeStruct((t // Q_BLOCK, LANES, N_IDX_HEADS * IDX_DIM), BF16)
        q_spec = pl.BlockSpec((rb, LANES, D_ATT), lambda i: (i, 0, 0))
        qi_spec = pl.BlockSpec((rb, LANES, N_IDX_HEADS * IDX_DIM), lambda i: (i, 0, 0))
    else:
        q_shape = jax.ShapeDtypeStruct((t, D_ATT), BF16)
        qi_shape = jax.ShapeDtypeStruct((t, N_IDX_HEADS * IDX_DIM), BF16)
        q_spec = pl.BlockSpec((tm, D_ATT), lambda i: (i, 0))
        qi_spec = pl.BlockSpec((tm, N_IDX_HEADS * IDX_DIM), lambda i: (i, 0))

    def row(n):
        return pl.BlockSpec((tm, n), lambda i: (i, 0))

    return pl.pallas_call(
        functools.partial(_inproj_body, blocked=blocked),
        grid=(nblk,),
        in_specs=[row(d), pl.BlockSpec((1, d), lambda i: (0, 0)),
                  pl.BlockSpec((d, D_PROJ_PAD), lambda i: (0, 0))],
        out_specs=[q_spec, row(KV_DIM), row(KV_DIM), row(D_ATT), qi_spec, row(LANES), row(3 * D_GDN), row(D_GDN)]
                  + t_specs,
        out_shape=[q_shape, jax.ShapeDtypeStruct((t, KV_DIM), F32), jax.ShapeDtypeStruct((t, KV_DIM), F32),
                   jax.ShapeDtypeStruct((t, D_ATT), F32), qi_shape, jax.ShapeDtypeStruct((t, LANES), F32),
                   jax.ShapeDtypeStruct((t, 3 * D_GDN), F32), jax.ShapeDtypeStruct((t, D_GDN), F32)] + t_shapes,
        compiler_params=_cparams(("parallel",)),
        name="inproj_blocked" if blocked else "inproj_rows",
    )(x2d, norm_g.reshape(1, d), w_pad)


def _rel_bucket(dist):
    n = jnp.maximum(dist, 0)
    max_exact = N_BUCKETS // 2
    nf = jnp.maximum(n, 1).astype(F32)
    large = max_exact + (jnp.log(nf / max_exact) / math.log(MAX_DISTANCE / max_exact)
                         * (N_BUCKETS - max_exact)).astype(I32)
    large = jnp.minimum(large, N_BUCKETS - 1)
    return jnp.where(n < max_exact, n, large)


def _far_bucket_checked(first_far):
    d = np.arange(first_far, 1 << 16, dtype=np.float32)
    b = 16 + (np.log(d / 16) / math.log(MAX_DISTANCE / 16) * 16).astype(np.int32)
    assert int(b.min()) >= N_BUCKETS - 1
    return N_BUCKETS - 1


def _kth_key_search(count_ge, shape, k, total):
    def body(step, carry):
        t, cnt_t = carry
        cand = t + jnp.left_shift(jnp.int32(1), 31 - step)
        cnt = count_ge(cand)
        accept = cnt >= k
        return jnp.where(accept, cand, t), jnp.where(accept, cnt, cnt_t)

    return lax.fori_loop(0, 32, body, (jnp.full(shape, INT_MIN, I32), jnp.full(shape, total, F32)))


KT = 2 * LANES


def _fold8(x, op):
    binop = {jnp.sum: jnp.add, jnp.max: jnp.maximum}[op]
    r = x.reshape(x.shape[0] // SUBLANES, SUBLANES, x.shape[1])
    while r.shape[0] > 1:
        half = r.shape[0] // 2
        r = binop(r[:half], r[half:])
    return r[0]


def _dsa_prompt_t_body(rel_ref, q_ref, qi_ref, smq_ref, za_ref, k_ref, v_ref, sms_ref, o_ref,
                       kk_ref, vvt_ref, ki_ref, bias_ref, key_ref, mask_ref, lg_ref,
                       *, seq, topk):
    b = pl.program_id(0)
    i = pl.program_id(1)
    n_t = seq // KT
    hpg = N_ATT_HEADS // N_KV_HEADS
    far_bucket = _far_bucket_checked(MAX_DISTANCE + 1)
    sub1 = lax.broadcasted_iota(I32, (LANES, LANES), 0)
    lane1 = lax.broadcasted_iota(I32, (LANES, LANES), 1)

    @pl.when((b == 0) & (i == 0))
    def _():
        for dt in range(2):
            bucket = _rel_bucket(dt * LANES + lane1 - sub1)
            for h in range(N_ATT_HEADS):
                acc = jnp.zeros((LANES, LANES), F32)
                for bk in range(N_BUCKETS):
                    acc = jnp.where(bucket == bk, rel_ref[bk, h], acc)
                bias_ref[h, dt] = acc - rel_ref[far_bucket, h]

    @pl.when(i == 0)
    def _():
        lo = lax.broadcasted_iota(I32, (seq, LANES), 1) < ATT_HEAD_DIM
        kf = k_ref[...]
        g0 = jnp.where(lo, kf, 0.0)
        g1 = jnp.where(lo, 0.0, kf)
        kk_ref[0] = g0.astype(BF16)
        kk_ref[1] = pltpu.roll(g0, ATT_HEAD_DIM, 1).astype(BF16)
        kk_ref[2] = pltpu.roll(g1, ATT_HEAD_DIM, 1).astype(BF16)
        kk_ref[3] = g1.astype(BF16)
        c0 = jnp.where(lo, sms_ref[...], 0.0)
        ki_ref[0] = c0.astype(BF16)
        ki_ref[1] = pltpu.roll(c0, IDX_DIM, 1).astype(BF16)
        lo_t = lax.broadcasted_iota(I32, (KT, LANES), 1) < ATT_HEAD_DIM
        for t in range(n_t):
            vf = v_ref[t * KT:(t + 1) * KT, :]
            w0 = jnp.where(lo_t, vf, 0.0)
            w1 = jnp.where(lo_t, 0.0, vf)
            vvt_ref[0, t] = w0.T.astype(BF16)
            vvt_ref[1, t] = pltpu.roll(w0, ATT_HEAD_DIM, 1).T.astype(BF16)
            vvt_ref[2, t] = pltpu.roll(w1, ATT_HEAD_DIM, 1).T.astype(BF16)
            vvt_ref[3, t] = w1.T.astype(BF16)

    n_ip = N_IDX_HEADS * IDX_DIM // LANES
    sm_t = smq_ref[...].T
    wrow = [sm_t[SM_WI + h:SM_WI + h + 1, :] * (N_IDX_HEADS ** -0.5) for h in range(N_IDX_HEADS)]

    n_live = i // 2 + 1
    n_dead = ((n_t - n_live) * KT).astype(F32)
    kidx = lax.broadcasted_iota(I32, (KT, LANES), 0)
    qpos = i * Q_BLOCK + lax.broadcasted_iota(I32, (KT, LANES), 1)

    def causal_of(t):
        return (t * KT + kidx) <= qpos

    def tile_rows(t):
        return pl.ds(pl.multiple_of(t * KT, KT), KT)

    def score_tile(t, carry):
        acc = jnp.zeros((KT, LANES), F32)
        for half in range(2):
            s_all = jnp.dot(ki_ref[half, tile_rows(t), :], qi_ref[...], preferred_element_type=F32)
            for j in range(n_ip):
                acc = acc + jnp.maximum(s_all[:, j * LANES:(j + 1) * LANES], 0.0) * wrow[2 * j + half]
        acc = acc + 0.0
        key_ref[t] = _sort_key(jnp.where(causal_of(t), acc, NEG_BIG))
        return carry

    def pair_loop(body, init):
        return lax.fori_loop(0, (n_live + 1) // 2, lambda p, c: body(2 * p + 1, body(2 * p, c)), init)

    pair_loop(score_tile, 0)

    @pl.when(n_live % 2 == 1)
    def _():
        key_ref[n_live] = jnp.full((KT, LANES), INT_MIN, I32)
        mask_ref[n_live] = jnp.full((KT, LANES), NEG_BIG, F32)

    slabs_per_tile = KT // LANES

    def causal_slab(t, r):
        return (t * KT + r * LANES + sub1) <= (i * Q_BLOCK + lane1)

    def count_where(pred):
        def body(t, acc):
            kt = key_ref[t]
            for r in range(slabs_per_tile):
                acc = jnp.where(pred(kt[r * LANES:(r + 1) * LANES], t, r), acc + 1.0, acc)
            return acc
        acc = pair_loop(body, jnp.zeros((LANES, LANES), F32))
        return jnp.sum(_fold8(acc, jnp.sum), axis=0, keepdims=True)

    def count_ge(cand):
        return count_where(lambda kt, t, r: kt >= cand) + jnp.where(cand <= KEY_NEG_BIG, n_dead, 0.0)

    def write_threshold_masks(thr):
        def body(t, carry):
            mask_ref[t] = jnp.where((key_ref[t] >= thr) & causal_of(t), 0.0, NEG_BIG)
            return carry
        lax.fori_loop(0, n_live, body, 0)

    few_keys = (i + 1) * Q_BLOCK <= topk

    @pl.when(few_keys)
    def _():
        write_threshold_masks(jnp.full((1, LANES), KEY_NEG_BIG, I32))

    @pl.when(jnp.logical_not(few_keys))
    def _():
        thr, cnt_thr = _kth_key_search(count_ge, (1, LANES), float(topk), float(seq))
        boundary_dup = jnp.max(jnp.where(cnt_thr > float(topk), 1.0, 0.0)) > 0.0

        @pl.when(jnp.logical_not(boundary_dup))
        def _():
            write_threshold_masks(thr)

        @pl.when(boundary_dup)
        def _():
            cnt_gt = count_where(lambda kt, t, r: kt > thr) + jnp.where(thr < KEY_NEG_BIG, n_dead, 0.0)
            need = float(topk) - cnt_gt
            cnt_ceq = count_where(lambda kt, t, r: (kt == thr) & causal_slab(t, r))
            any_tie = jnp.max(jnp.where(cnt_ceq > need, 1.0, 0.0)) > 0.0

            @pl.when(jnp.logical_not(any_tie))
            def _():
                write_threshold_masks(thr)

            @pl.when(any_tie)
            def _():
                rk = lax.broadcasted_iota(I32, (KT, KT), 0)
                ck = lax.broadcasted_iota(I32, (KT, KT), 1)
                lower = (ck <= rk).astype(BF16)

                def body(t, offset):
                    kt = key_ref[t]
                    cz = causal_of(t)
                    eq = ((kt == thr) & cz).astype(F32)
                    prefix = jnp.dot(lower, eq.astype(BF16), preferred_element_type=F32) + offset
                    sel = ((kt > thr) & cz) | ((eq > 0.0) & (prefix <= need))
                    mask_ref[t] = jnp.where(sel, 0.0, NEG_BIG)
                    return offset + jnp.sum(_fold8(eq, jnp.sum), axis=0, keepdims=True)

                lax.fori_loop(0, n_live, body, jnp.zeros((1, LANES), F32))

    near_lo = jnp.maximum(i - 1, 0) // 2
    neg8 = jnp.full((SUBLANES, LANES), NEG_BIG, F32)
    zero8 = jnp.zeros((SUBLANES, LANES), F32)
    def logit_tile(t, mx):
        mx = list(mx)
        far = t < near_lo
        for g in range(N_KV_HEADS):
            qt_g = q_ref[:, g * 2 * LANES:(g + 1) * 2 * LANES]
            for half in range(2):
                l_all = jnp.dot(kk_ref[2 * g + half, tile_rows(t), :], qt_g, preferred_element_type=F32)
                for n in range(2):
                    h = g * hpg + 2 * n + half
                    l = l_all[:, n * LANES:(n + 1) * LANES] + mask_ref[t]
                    lg_ref[h, t] = l
                    mx[h] = jnp.maximum(mx[h], jnp.where(far, _fold8(l, jnp.max), NEG_BIG))
        return tuple(mx)

    mx = pair_loop(logit_tile, (neg8,) * N_ATT_HEADS)
    m_all = []
    for h in range(N_ATT_HEADS):
        r0 = pl.ds(pl.multiple_of((i % 2) * LANES, LANES), LANES)
        l0 = lg_ref[h, i // 2, r0, :] + bias_ref[h, 0]
        lg_ref[h, i // 2, r0, :] = l0
        im1 = jnp.maximum(i - 1, 0)
        r1 = pl.ds(pl.multiple_of((im1 % 2) * LANES, LANES), LANES)
        l1 = lg_ref[h, im1 // 2, r1, :] + jnp.where(i > 0, bias_ref[h, 1], 0.0)
        lg_ref[h, im1 // 2, r1, :] = l1
        l2 = lg_ref[h, near_lo, 0:LANES, :]
        m8 = jnp.maximum(jnp.maximum(mx[h], _fold8(l2, jnp.max)),
                         jnp.maximum(_fold8(l0, jnp.max), _fold8(l1, jnp.max)))
        m_all.append(jnp.max(m8, axis=0, keepdims=True))
    gate = _silu(za_ref[...])

    n_pairs = N_ATT_HEADS // 2

    def pv_tile(t, carry):
        ssum = list(carry[:N_ATT_HEADS])
        acc = list(carry[N_ATT_HEADS:])
        for pr in range(n_pairs):
            g = (2 * pr) // hpg
            for half in range(2):
                h = 2 * pr + half
                p = jnp.exp(lg_ref[h, t] - m_all[h])
                ssum[h] = ssum[h] + _fold8(p, jnp.sum)
                acc[pr] = acc[pr] + jnp.dot(vvt_ref[2 * g + half, t], p.astype(BF16), preferred_element_type=F32)
        return tuple(ssum) + tuple(acc)

    zacc = jnp.zeros((LANES, LANES), F32)
    res = pair_loop(pv_tile, (zero8,) * N_ATT_HEADS + (zacc,) * n_pairs)
    for pr in range(n_pairs):
        l_lo = jnp.sum(res[2 * pr], axis=0, keepdims=True)
        l_hi = jnp.sum(res[2 * pr + 1], axis=0, keepdims=True)
        inv = jnp.where(sub1 < ATT_HEAD_DIM, 1.0 / l_lo, 1.0 / l_hi)
        o_pair = (res[N_ATT_HEADS + pr] * inv).T
        cs = slice(pr * LANES, (pr + 1) * LANES)
        o_ref[:, cs] = (o_pair * gate[:, cs]).astype(BF16)


def _dsa_prompt_t(rel_table, q_blk, qi_blk, small, za, k2d, v2d, batch, seq):
    assert N_KV_HEADS == 2 and N_ATT_HEADS // N_KV_HEADS == 4 and seq % KT == 0
    nb = seq // Q_BLOCK
    n_t = seq // KT
    topk = min(TOPK_MAX, seq // 4)
    assert seq >= 2 * topk
    t = batch * seq
    return pl.pallas_call(
        functools.partial(_dsa_prompt_t_body, seq=seq, topk=topk),
        grid=(batch, nb),
        in_specs=[
            pl.BlockSpec(memory_space=pltpu.SMEM),
            pl.BlockSpec((None, LANES, D_ATT), lambda b, i: (b * nb + i, 0, 0)),
            pl.BlockSpec((None, LANES, N_IDX_HEADS * IDX_DIM), lambda b, i: (b * nb + i, 0, 0)),
            pl.BlockSpec((Q_BLOCK, LANES), lambda b, i: (b * nb + i, 0)),
            pl.BlockSpec((Q_BLOCK, D_ATT), lambda b, i: (b * nb + i, 0)),
            pl.BlockSpec((seq, KV_DIM), lambda b, i: (b, 0)),
            pl.BlockSpec((seq, KV_DIM), lambda b, i: (b, 0)),
            pl.BlockSpec((seq, LANES), lambda b, i: (b, 0)),
        ],
        out_specs=pl.BlockSpec((Q_BLOCK, D_ATT), lambda b, i: (b * nb + i, 0)),
        out_shape=jax.ShapeDtypeStruct((t, D_ATT), BF16),
        scratch_shapes=[
            pltpu.VMEM((2 * N_KV_HEADS, seq, LANES), BF16),
            pltpu.VMEM((2 * N_KV_HEADS, n_t, LANES, KT), BF16),
            pltpu.VMEM((2, seq, LANES), BF16),
            pltpu.VMEM((N_ATT_HEADS, 2, LANES, LANES), F32),
            pltpu.VMEM((n_t, KT, LANES), I32),
            pltpu.VMEM((n_t, KT, LANES), F32),
            pltpu.VMEM((N_ATT_HEADS, n_t, KT, LANES), F32),
        ],
        compiler_params=_cparams(("arbitrary", "arbitrary")),
        name="dsa_prompt",
    )(rel_table, q_blk, qi_blk, small, za, k2d, v2d, small)


def _softplus(x):
    return jnp.maximum(x, 0.0) + jnp.log1p(jnp.exp(-jnp.abs(x)))


def _lane_pick(x, idx):
    lane = lax.broadcasted_iota(I32, x.shape, 1)
    return jnp.sum(jnp.where(lane == idx, x, 0.0), axis=1, keepdims=True)


def _gated_norm(o, gn, z):
    y = o * lax.rsqrt(jnp.mean(o * o, axis=-1, keepdims=True) + RMS_EPS)
    return (y * gn) * _silu(z)


def _gdn_prompt_body(alog_ref, dtb_ref, xq_ref, xk_ref, xv_ref, wq_ref, wk_ref, wv_ref, sm_ref, zg_ref, gn_ref,
                     o_ref, sfin_ref, xs_ref, q_s, k_s, v_s, g_s, b_s, *, seq, hp):
    h0 = pl.program_id(1) * hp
    hist = SUBLANES
    slabs = [slice(hh * LANES, (hh + 1) * LANES) for hh in range(hp)]

    def conv_into(x_ref, w_ref, dst, post):
        xs_ref[0:hist, :] = jnp.zeros((hist, hp * LANES), F32)
        xs_ref[hist:hist + seq, :] = x_ref[...]
        base = hist - (CONV_W - 1)
        for hs in slabs:
            acc = xs_ref[base:base + seq, hs] * w_ref[0:1, hs]
            for j in range(1, CONV_W):
                acc = acc + xs_ref[base + j:base + j + seq, hs] * w_ref[j:j + 1, hs]
            dst[:, hs] = post(_silu(acc))

    def l2n(x):
        return x * lax.rsqrt(jnp.sum(x * x, axis=-1, keepdims=True) + 1e-6)

    conv_into(xq_ref, wq_ref, q_s, lambda x: l2n(x) * (GDN_HEAD_DIM ** -0.5))
    conv_into(xk_ref, wk_ref, k_s, l2n)
    conv_into(xv_ref, wv_ref, v_s, lambda x: x)

    lane_row = lax.broadcasted_iota(I32, (1, LANES), 1)
    alog_row = jnp.zeros((1, LANES), F32)
    dtb_row = jnp.zeros((1, LANES), F32)
    for hh in range(hp):
        alog_row = jnp.where(lane_row == SM_AG + h0 + hh, alog_ref[h0 + hh], alog_row)
        dtb_row = jnp.where(lane_row == SM_AG + h0 + hh, dtb_ref[h0 + hh], dtb_row)
    sm = sm_ref[...]
    g_all = -jnp.exp(alog_row) * _softplus(sm + dtb_row)
    beta_all = 1.0 / (1.0 + jnp.exp(-sm))
    for hh, hs in enumerate(slabs):
        g_s[:, hs] = jnp.broadcast_to(_lane_pick(g_all, SM_AG + h0 + hh), (seq, LANES))
        b_s[:, hs] = jnp.broadcast_to(_lane_pick(beta_all, SM_BG + h0 + hh), (seq, LANES))

    c = GDN_C
    ri = lax.broadcasted_iota(I32, (c, c), 0)
    ci = lax.broadcasted_iota(I32, (c, c), 1)
    tril = ri >= ci
    strict = ri > ci
    tril_f = tril.astype(F32)
    eye = (ri == ci).astype(F32)
    gn = gn_ref[...]
    off_masks = []
    for lg in range(int(math.log2(c))):
        same_pair = (ri >> (lg + 1)) == (ci >> (lg + 1))
        off_masks.append(same_pair & (((ri >> lg) & 1) == 1) & (((ci >> lg) & 1) == 0))

    tril_b = tril.astype(BF16)

    def cumsum_rows(g):
        hi = g.astype(BF16)
        r1 = g - hi.astype(F32)
        mid = r1.astype(BF16)
        lo = (r1 - mid.astype(F32)).astype(BF16)
        return sum(jnp.dot(tril_b, piece, preferred_element_type=F32) for piece in (hi, mid, lo))

    cpi = GDN_CHUNKS_PER_ITER

    def rows_of(n):
        return pl.ds(pl.multiple_of(n * c, c), c)

    def local_phase(items):
        ids = range(len(items))
        q = [q_s[rows_of(n), hs] for n, hs in items]
        k = [k_s[rows_of(n), hs] for n, hs in items]
        v = [v_s[rows_of(n), hs] for n, hs in items]
        bb = [b_s[rows_of(n), hs] for n, hs in items]
        gcum = [cumsum_rows(g_s[rows_of(n), hs]) for n, hs in items]
        gcum_row = [g.T for g in gcum]
        decay = [jnp.where(tril, jnp.exp(jnp.where(tril, gcum[e] - gcum_row[e], 0.0)), 0.0) for e in ids]
        eg = [jnp.exp(g) for g in gcum]
        kb = [k[e] * bb[e] for e in ids]
        vb = [v[e] * bb[e] for e in ids]
        kq = [_bdot_nt(jnp.concatenate([kb[e], q[e]], axis=0), k[e]) for e in ids]
        a_mat = [jnp.where(strict, kq[e][:c] * decay[e], 0.0) for e in ids]
        attn = [kq[e][c:] * decay[e] for e in ids]
        x = [eye - jnp.where(off_masks[0], a_mat[e], 0.0) for e in ids]
        for om in off_masks[1:]:
            inner = [_bdot(jnp.where(om, a_mat[e], 0.0), x[e]) for e in ids]
            x = [x[e] - _bdot(x[e], inner[e]) for e in ids]
        uw = [_bdot(x[e], jnp.concatenate([vb[e], kb[e] * eg[e]], axis=1)) for e in ids]
        g_last = [g[c - 1:c, :] for g in gcum]
        k_dec_t = [(k[e] * jnp.exp(g_last[e] - gcum[e])).T for e in ids]
        wq = [jnp.concatenate([uw[e][:, GDN_HEAD_DIM:], q[e] * eg[e]], axis=0) for e in ids]
        ak = [jnp.concatenate([attn[e], k_dec_t[e]], axis=0) for e in ids]
        return [dict(u=uw[e][:, :GDN_HEAD_DIM], wq=wq[e], ak=ak[e], s_dec=jnp.exp(g_last[e])) for e in ids]

    def state_phase(n, loc, states):
        heads = range(hp)
        wq_s = [_bdot(loc[h]["wq"], states[h]) for h in heads]
        v_new = [loc[h]["u"] - wq_s[h][:c] for h in heads]
        ak_v = [_bdot(loc[h]["ak"], v_new[h]) for h in heads]
        for h, hs in enumerate(slabs):
            o = wq_s[h][c:] + ak_v[h][:c]
            o_ref[rows_of(n), hs] = _gated_norm(o, gn, zg_ref[rows_of(n), hs]).astype(BF16)
        return tuple(states[h] * loc[h]["s_dec"] + ak_v[h][c:] for h in heads)

    def chunk_group(p, states):
        ns = [p * cpi + r for r in range(cpi)]
        loc = local_phase([(n, hs) for n in ns for hs in slabs])
        for r, n in enumerate(ns):
            states = state_phase(n, loc[r * hp:(r + 1) * hp], states)
        return states

    zero_state = jnp.zeros((GDN_HEAD_DIM, GDN_HEAD_DIM), F32)
    finals = lax.fori_loop(0, seq // (c * cpi), chunk_group, (zero_state,) * hp)
    for hh in range(hp):
        sfin_ref[hh] = finals[hh]


GDN_HEADS_PER_STEP = 4


def _gdn_prompt(a_log, dt_bias, qkv2d, conv_w, small, zg, gdn_norm, batch, seq):
    nh = N_GDN_HEADS
    hp = GDN_HEADS_PER_STEP
    ng = nh // hp
    t = batch * seq
    w = hp * LANES
    once = pl.Buffered(1)
    blk = lambda off: pl.BlockSpec((seq, w), lambda b, j: (b, off + j), pipeline_mode=once)
    wblk = lambda off: pl.BlockSpec((CONV_W, w), lambda b, j: (0, off + j))
    return pl.pallas_call(
        functools.partial(_gdn_prompt_body, seq=seq, hp=hp),
        grid=(batch, ng),
        in_specs=[pl.BlockSpec(memory_space=pltpu.SMEM), pl.BlockSpec(memory_space=pltpu.SMEM),
                  blk(0), blk(ng), blk(2 * ng), wblk(0), wblk(ng), wblk(2 * ng),
                  pl.BlockSpec((seq, LANES), lambda b, j: (b, 0)),
                  pl.BlockSpec((seq, w), lambda b, j: (b, j), pipeline_mode=once),
                  pl.BlockSpec((1, LANES), lambda b, j: (0, 0))],
        out_specs=[pl.BlockSpec((seq, w), lambda b, j: (b, j)),
                   pl.BlockSpec((None, hp, GDN_HEAD_DIM, GDN_HEAD_DIM), lambda b, j: (b, j, 0, 0))],
        out_shape=[jax.ShapeDtypeStruct((t, D_GDN), BF16),
                   jax.ShapeDtypeStruct((batch, nh, GDN_HEAD_DIM, GDN_HEAD_DIM), F32)],
        scratch_shapes=[pltpu.VMEM((seq + 2 * SUBLANES, w), F32)] + [pltpu.VMEM((seq, w), F32)] * 5,
        compiler_params=_cparams(("parallel", "arbitrary")),
        name="gdn_prompt",
    )(a_log, dt_bias, qkv2d, qkv2d, qkv2d, conv_w, conv_w, conv_w, small, zg, gdn_norm.reshape(1, LANES))


def _outproj_body(*refs, gate_att):
    if gate_att:
        x_ref, att_ref, za_ref, gdn_ref, w_ref, g_ref, y_ref = refs
        att = (att_ref[...] * _silu(za_ref[...])).astype(BF16)
    else:
        x_ref, att_ref, gdn_ref, w_ref, g_ref, y_ref = refs
        att = att_ref[...]
    y = x_ref[...] + jnp.dot(att, w_ref[0:D_ATT, :], preferred_element_type=F32) \
        + jnp.dot(gdn_ref[...], w_ref[D_ATT:D_ATT + D_GDN, :], preferred_element_type=F32)
    y = y * lax.rsqrt(jnp.mean(y * y, axis=-1, keepdims=True) + RMS_EPS)
    y_ref[...] = y * g_ref[...]


def _outproj(x2d, att, gdn, w_bf, norm_g, tm, za=None):
    t, d = x2d.shape
    row = lambda n: pl.BlockSpec((tm, n), lambda i: (i, 0))
    full = lambda a, b: pl.BlockSpec((a, b), lambda i: (0, 0))
    ins = [x2d, att] + ([za] if za is not None else []) + [gdn, w_bf, norm_g.reshape(1, d)]
    specs = [row(d), row(D_ATT)] + ([row(D_ATT)] if za is not None else []) + \
            [row(D_GDN), full(D_ATT + D_GDN, d), full(1, d)]
    return pl.pallas_call(
        functools.partial(_outproj_body, gate_att=za is not None),
        grid=(t // tm,),
        in_specs=specs,
        out_specs=row(d),
        out_shape=jax.ShapeDtypeStruct((t, d), F32),
        compiler_params=_cparams(("parallel",)),
        name="outproj_gated" if za is not None else "outproj",
    )(*ins)


IDX_PAGES_PER_STEP = 64
ATT_PAGES_PER_STEP = 64
SAMPLE_Q_ROWS = 16


def _idx_scores_body(pt_ref, qi_ref, wi_ref, *rest):
    page_refs = rest[:IDX_PAGES_PER_STEP]
    o_ref = rest[IDX_PAGES_PER_STEP]
    qi = qi_ref[0]
    wi = wi_ref[0] * (N_IDX_HEADS ** -0.5)
    for j, pr in enumerate(page_refs):
        s = _bdot(qi, pr[...])
        sc = jnp.sum(jnp.maximum(s, 0.0) * wi, axis=0, keepdims=True)
        o_ref[0, j:j + 1, :] = sc + 0.0


def _idx_scores(page_table, qi3, wi3, kidx_pages):
    db, n_pages = page_table.shape
    page = kidx_pages.shape[2]
    pg = IDX_PAGES_PER_STEP

    def page_spec(j):
        return pl.BlockSpec((None, IDX_DIM, page), lambda b, s, pt: (pt[b, s * pg + j], 0, 0))

    grid_spec = pltpu.PrefetchScalarGridSpec(
        num_scalar_prefetch=1,
        grid=(db, n_pages // pg),
        in_specs=[pl.BlockSpec((1, N_IDX_HEADS, IDX_DIM), lambda b, s, pt: (b, 0, 0)),
                  pl.BlockSpec((1, N_IDX_HEADS, 1), lambda b, s, pt: (b, 0, 0))]
                 + [page_spec(j) for j in range(pg)],
        out_specs=pl.BlockSpec((1, pg, page), lambda b, s, pt: (b, s, 0)),
    )
    return pl.pallas_call(
        _idx_scores_body,
        grid_spec=grid_spec,
        out_shape=jax.ShapeDtypeStruct((db, n_pages, page), F32),
        compiler_params=_cparams(("parallel", "arbitrary")),
        name="sample_idx_scores",
    )(page_table, qi3, wi3, *([kidx_pages] * pg))


def _sample_select_body(sc_ref, qi_ref, sm_ref, mask_ref, key_ref, *, n_tiles, topk):
    rows = sc_ref.shape[1]
    lane = lax.broadcasted_iota(I32, (rows, LANES), 1)

    def fill(t, carry):
        key_ref[t] = _sort_key(sc_ref[t])
        return carry

    lax.fori_loop(0, n_tiles, fill, 0)
    sm = sm_ref[...]
    ki_new = sm[:, 0:IDX_DIM].astype(BF16).astype(F32)
    acc = jnp.zeros((rows, 1), F32)
    for hd in range(N_IDX_HEADS):
        qh = qi_ref[:, hd * IDX_DIM:(hd + 1) * IDX_DIM].astype(F32)
        s = jnp.sum(qh * ki_new, axis=1, keepdims=True)
        acc = acc + jnp.maximum(s, 0.0) * (sm[:, SM_WI + hd:SM_WI + hd + 1] * (N_IDX_HEADS ** -0.5))
    sc_new = jnp.broadcast_to(acc + 0.0, (rows, LANES))
    key_ref[n_tiles] = jnp.where(lane == 0, _sort_key(sc_new), INT_MIN)

    def count_where(pred):
        def body(t, a):
            return a + pred(key_ref[t]).astype(F32)
        a = lax.fori_loop(0, n_tiles + 1, body, jnp.zeros((rows, LANES), F32))
        return jnp.sum(a, axis=1, keepdims=True)

    thr, _ = _kth_key_search(lambda cand: count_where(lambda kt: kt >= cand), (rows, 1), float(topk),
                             float(n_tiles * LANES + 1))
    need = float(topk) - count_where(lambda kt: kt > thr)
    cnt_eq = count_where(lambda kt: kt == thr)
    any_tie = jnp.max(jnp.where(cnt_eq > need, 1.0, 0.0)) > 0.0

    @pl.when(jnp.logical_not(any_tie))
    def _():
        def body(t, carry):
            mask_ref[t] = jnp.where(key_ref[t] >= thr, 0.0, NEG_BIG)
            return carry
        lax.fori_loop(0, n_tiles + 1, body, 0)

    @pl.when(any_tie)
    def _():
        sub = lax.broadcasted_iota(I32, (LANES, LANES), 0)
        lane2 = lax.broadcasted_iota(I32, (LANES, LANES), 1)
        upper = (sub <= lane2).astype(BF16)
        ones = jnp.ones((LANES, LANES), BF16)

        def body(t, offset):
            kt = key_ref[t]
            eq = (kt == thr).astype(BF16)
            prefix = jnp.dot(eq, upper, preferred_element_type=F32) + offset
            sel = (kt > thr) | ((eq > 0) & (prefix <= need))
            mask_ref[t] = jnp.where(sel, 0.0, NEG_BIG)
            return offset + jnp.dot(eq, ones, preferred_element_type=F32)

        lax.fori_loop(0, n_tiles + 1, body, jnp.zeros((rows, LANES), F32))


def _sample_select(scores_t, qi2d, small, topk):
    n_tiles, db, page = scores_t.shape
    vm = pl.BlockSpec(memory_space=pltpu.VMEM)
    return pl.pallas_call(
        functools.partial(_sample_select_body, n_tiles=n_tiles, topk=topk),
        in_specs=[vm, vm, vm],
        out_specs=vm,
        out_shape=jax.ShapeDtypeStruct((n_tiles + 1, db, page), F32),
        scratch_shapes=[pltpu.VMEM((n_tiles + 1, db, page), I32)],
        compiler_params=pltpu.CompilerParams(vmem_limit_bytes=VMEM_LIMIT),
        name="sample_select",
    )(scores_t, qi2d, small)


def _sample_attn_body(pt_ref, q_ref, mask_ref, mnew_ref, knew_ref, vnew_ref, relt_ref, *rest, past_len, page):
    pg = ATT_PAGES_PER_STEP
    k_refs = rest[:pg]
    v_refs = rest[pg:2 * pg]
    o_ref, m_s, l_s, acc_s = rest[2 * pg:]
    s = pl.program_id(1)
    n_steps = pl.num_programs(1)
    nh = SAMPLE_Q_ROWS
    far_bucket = _far_bucket_checked(MAX_DISTANCE + 1)

    @pl.when(s == 0)
    def _():
        m_s[...] = jnp.full(m_s.shape, NEG_BIG, F32)
        l_s[...] = jnp.zeros(l_s.shape, F32)
        acc_s[...] = jnp.zeros(acc_s.shape, F32)

    relt = relt_ref[...]
    c_far = relt[:, far_bucket:far_bucket + 1]

    def bias_of(dist):
        bucket = _rel_bucket(dist)
        acc = jnp.zeros((nh, dist.shape[1]), F32)
        for bk in range(N_BUCKETS):
            acc = jnp.where(bucket == bk, relt[:, bk:bk + 1], acc)
        return acc - c_far

    q = q_ref[0].astype(BF16)
    logits = []
    for j in range(pg):
        l = _bdot(q, k_refs[j][...]) + mask_ref[0, j:j + 1, :]
        logits.append(l)
    logits = jnp.concatenate(logits, axis=1)

    kpos = (s * pg) * page + lax.broadcasted_iota(I32, (1, pg * page), 1)
    near = past_len - ((s + 1) * pg * page - 1) <= MAX_DISTANCE
    logits = logits + lax.cond(near, lambda: bias_of(past_len - kpos),
                               lambda: jnp.zeros((nh, pg * page), F32))

    m_old = m_s[...]
    m_new = jnp.maximum(m_old, jnp.max(logits, axis=1, keepdims=True))
    alpha = jnp.exp(m_old - m_new)
    p = jnp.exp(logits - m_new)
    l_new = l_s[...] * alpha + jnp.sum(p, axis=1, keepdims=True)
    acc = acc_s[...] * alpha
    for j in range(pg):
        acc = acc + _bdot_nt(p[:, j * page:(j + 1) * page], v_refs[j][...])
    m_s[...] = m_new
    l_s[...] = l_new
    acc_s[...] = acc

    @pl.when(s == n_steps - 1)
    def _():
        kn = knew_ref[0].astype(BF16).astype(F32)
        vn = vnew_ref[0].astype(BF16).astype(F32)
        ln = jnp.sum(q.astype(F32) * kn, axis=1, keepdims=True) + bias_of(jnp.zeros((1, 1), I32)) \
            + mnew_ref[0][:, 0:1]
        m_fin = jnp.maximum(m_new, ln)
        a2 = jnp.exp(m_new - m_fin)
        pn = jnp.exp(ln - m_fin)
        l_fin = l_new * a2 + pn
        res = (acc * a2 + pn.astype(BF16).astype(F32) * vn) / l_fin
        row = lax.broadcasted_iota(I32, res.shape, 0)
        hpg = N_ATT_HEADS // N_KV_HEADS
        o_ref[0] = jnp.where((row >= hpg) & (row < 2 * hpg), pltpu.roll(res, ATT_HEAD_DIM, 1), res)


def _sample_attn(page_table, q_lh, mask_pages, mask_new, k_new, v_new, rel_t, k_pages, v_pages, past_len):
    db, n_pages = page_table.shape
    page = k_pages.shape[2]
    pg = ATT_PAGES_PER_STEP

    def page_spec(j):
        return pl.BlockSpec((None, KV_DIM, page), lambda b, s, pt: (pt[b, s * pg + j], 0, 0))

    row3 = lambda n: pl.BlockSpec((1, 1, n), lambda b, s, pt: (b, 0, 0))
    grid_spec = pltpu.PrefetchScalarGridSpec(
        num_scalar_prefetch=1,
        grid=(db, n_pages // pg),
        in_specs=[pl.BlockSpec((1, SAMPLE_Q_ROWS, LANES), lambda b, s, pt: (b, 0, 0)),
                  pl.BlockSpec((1, pg, page), lambda b, s, pt: (b, s, 0)),
                  row3(LANES), row3(KV_DIM), row3(KV_DIM),
                  pl.BlockSpec((SAMPLE_Q_ROWS, N_BUCKETS), lambda b, s, pt: (0, 0))]
                 + [page_spec(j) for j in range(pg)] * 2,
        out_specs=pl.BlockSpec((1, SAMPLE_Q_ROWS, LANES), lambda b, s, pt: (b, 0, 0)),
        scratch_shapes=[pltpu.VMEM((SAMPLE_Q_ROWS, 1), F32), pltpu.VMEM((SAMPLE_Q_ROWS, 1), F32),
                        pltpu.VMEM((SAMPLE_Q_ROWS, LANES), F32)],
    )
    return pl.pallas_call(
        functools.partial(_sample_attn_body, past_len=past_len, page=page),
        grid_spec=grid_spec,
        out_shape=jax.ShapeDtypeStruct((db, SAMPLE_Q_ROWS, LANES), F32),
        compiler_params=_cparams(("parallel", "arbitrary")),
        name="sample_attn",
    )(page_table, q_lh, mask_pages, mask_new, k_new, v_new, rel_t, *([k_pages] * pg), *([v_pages] * pg))


def _gdn_sample_body(alog_ref, dtb_ref, x_ref, cst_ref, w_ref, sm_ref, zg_ref, gn_ref, s0_ref,
                     o_ref, s_ref, cnew_ref):
    x = x_ref[0]
    cst = cst_ref[...]
    w = w_ref[...]
    acc = cst[0:1, :] * w[0:1, :]
    for j in range(1, CONV_W - 1):
        acc = acc + cst[j:j + 1, :] * w[j:j + 1, :]
    acc = acc + x * w[CONV_W - 1:CONV_W, :]
    xc = _silu(acc)
    cnew_ref[0:CONV_W - 2, :] = cst[1:CONV_W - 1, :]
    cnew_ref[CONV_W - 2:CONV_W - 1, :] = x
    sm = sm_ref[0]
    gn = gn_ref[...]
    d = GDN_HEAD_DIM
    for h in range(N_GDN_HEADS):
        q = xc[:, h * d:(h + 1) * d]
        k = xc[:, D_GDN + h * d:D_GDN + (h + 1) * d]
        v = xc[:, 2 * D_GDN + h * d:2 * D_GDN + (h + 1) * d]
        q = q * lax.rsqrt(jnp.sum(q * q, axis=-1, keepdims=True) + 1e-6) * (d ** -0.5)
        k = k * lax.rsqrt(jnp.sum(k * k, axis=-1, keepdims=True) + 1e-6)
        a_neg = -jnp.exp(jnp.zeros((1, 1), F32) + alog_ref[h])
        g = a_neg * _softplus(sm[:, SM_AG + h:SM_AG + h + 1] + dtb_ref[h])
        beta = 1.0 / (1.0 + jnp.exp(-sm[:, SM_BG + h:SM_BG + h + 1]))
        st = s0_ref[h] * jnp.exp(g)
        k_col = jnp.broadcast_to(k, (d, d)).T
        q_col = jnp.broadcast_to(q, (d, d)).T
        kv = jnp.sum(k_col * st, axis=0, keepdims=True)
        delta = (v - kv) * beta
        st = st + k_col * delta
        s_ref[h] = st
        o = jnp.sum(q_col * st, axis=0, keepdims=True)
        o_ref[0, :, h * d:(h + 1) * d] = _gated_norm(o, gn, zg_ref[0][:, h * d:(h + 1) * d]).astype(BF16)


def _gdn_sample(a_log, dt_bias, qkv3, state_conv_l, conv_w, small3, zg3, gdn_norm, state_ssm_l):
    db = qkv3.shape[0]
    d = GDN_HEAD_DIM
    nh = N_GDN_HEADS
    row3 = lambda n: pl.BlockSpec((1, 1, n), lambda b: (b, 0, 0))
    return pl.pallas_call(
        _gdn_sample_body,
        grid=(db,),
        in_specs=[pl.BlockSpec(memory_space=pltpu.SMEM), pl.BlockSpec(memory_space=pltpu.SMEM),
                  row3(3 * D_GDN),
                  pl.BlockSpec((None, CONV_W - 1, 3 * D_GDN), lambda b: (b, 0, 0)),
                  pl.BlockSpec((CONV_W, 3 * D_GDN), lambda b: (0, 0)),
                  row3(LANES), row3(D_GDN),
                  pl.BlockSpec((1, LANES), lambda b: (0, 0)),
                  pl.BlockSpec((None, nh, d, d), lambda b: (b, 0, 0, 0))],
        out_specs=[row3(D_GDN),
                   pl.BlockSpec((None, nh, d, d), lambda b: (b, 0, 0, 0)),
                   pl.BlockSpec((None, CONV_W - 1, 3 * D_GDN), lambda b: (b, 0, 0))],
        out_shape=[jax.ShapeDtypeStruct((db, 1, D_GDN), BF16),
                   jax.ShapeDtypeStruct((db, nh, d, d), F32),
                   jax.ShapeDtypeStruct((db, CONV_W - 1, 3 * D_GDN), F32)],
        compiler_params=_cparams(("parallel",)),
        name="gdn_sample",
    )(a_log, dt_bias, qkv3, state_conv_l, conv_w, small3, zg3, gdn_norm.reshape(1, LANES), state_ssm_l)


def kernel(x_prompt, x_sample, cache_k, cache_v, cache_kidx, state_ssm, state_conv, page_table, norm_in, w_in,
           conv_w, a_log, dt_bias, gdn_norm, w_out, rel_table, norm_final):
    depth = w_in.shape[0]
    assert depth == 1, "single-layer model"
    batch, seq, d_model = x_prompt.shape
    db, dec_seq, _ = x_sample.shape
    assert dec_seq == 1 and seq % KT == 0 and seq % (GDN_C * GDN_CHUNKS_PER_ITER) == 0 and seq % PROJ_ROWS == 0
    assert (batch * seq) % (2 * PROJ_ROWS) == 0
    n_pool, page = cache_k.shape[1], cache_k.shape[2]
    n_pages = page_table.shape[1]
    past_len = n_pages * page
    assert page == LANES and n_pages % IDX_PAGES_PER_STEP == 0 and n_pages % ATT_PAGES_PER_STEP == 0

    lyr = 0
    w_pad = _prep_w_in(w_in[lyr])
    w_out_bf = w_out[lyr].astype(BF16)

    xp = x_prompt.reshape(batch * seq, d_model)
    q_blk, k2d, v2d, za, qi_blk, small, qkv2d, zg, k_t, v_t, ki_t = _inproj(
        xp, norm_in[lyr], w_pad, tm=PROJ_ROWS, blocked=True, seq=seq)
    att_g = _dsa_prompt_t(rel_table, q_blk, qi_blk, small, za, k2d, v2d, batch, seq)
    gdn_g, s_fin = _gdn_prompt(a_log[lyr], dt_bias[lyr], qkv2d, conv_w[lyr], small, zg, gdn_norm[lyr], batch, seq)
    y_prompt = _outproj(xp, att_g, gdn_g, w_out_bf, norm_final, tm=2 * PROJ_ROWS).reshape(batch, seq, d_model)
    k_prompt = jnp.transpose(k_t.reshape(batch, N_KV_HEADS, ATT_HEAD_DIM, seq), (0, 3, 1, 2))[None]
    v_prompt = jnp.transpose(v_t.reshape(batch, N_KV_HEADS, ATT_HEAD_DIM, seq), (0, 3, 1, 2))[None]
    kidx_prompt = jnp.transpose(ki_t, (0, 2, 1))[None]
    ssm_prompt = s_fin[None]
    conv_prompt = qkv2d.reshape(batch, seq, 3 * D_GDN)[:, seq - (CONV_W - 1):][None]

    xs = x_sample.reshape(db, d_model)
    q_s, k_s, v_s, za_s, qi_s, small_s, qkv_s, zg_s = _inproj(xs, norm_in[lyr], w_pad, tm=db, blocked=False)
    kidx_t = jnp.transpose(cache_kidx[lyr], (0, 2, 1))
    k_pages_t = jnp.transpose(cache_k[lyr], (0, 2, 3, 1)).reshape(n_pool, KV_DIM, page)
    v_pages_t = jnp.transpose(cache_v[lyr], (0, 2, 3, 1)).reshape(n_pool, KV_DIM, page)
    scores = _idx_scores(page_table, qi_s.reshape(db, N_IDX_HEADS, IDX_DIM),
                         small_s[:, SM_WI:SM_WI + N_IDX_HEADS].reshape(db, N_IDX_HEADS, 1),
                         kidx_t)
    topk = min(TOPK_MAX, (past_len + dec_seq) // 4)
    mask_t = _sample_select(jnp.transpose(scores, (1, 0, 2)), qi_s, small_s, topk)
    mask_pages = jnp.transpose(mask_t[:n_pages], (1, 0, 2))
    mask_new = mask_t[n_pages].reshape(db, 1, page)
    hpg = N_ATT_HEADS // N_KV_HEADS
    q8 = q_s.reshape(db, N_ATT_HEADS, ATT_HEAD_DIM).astype(F32)
    zq = jnp.zeros((db, hpg, ATT_HEAD_DIM), F32)
    q_lh = jnp.concatenate([jnp.concatenate([q8[:, :hpg], zq], axis=2),
                            jnp.concatenate([zq, q8[:, hpg:]], axis=2),
                            jnp.zeros((db, SAMPLE_Q_ROWS - N_ATT_HEADS, LANES), F32)], axis=1)
    rel_t = jnp.concatenate([rel_table.T, jnp.zeros((SAMPLE_Q_ROWS - N_ATT_HEADS, N_BUCKETS), F32)], axis=0)
    att_raw = _sample_attn(page_table, q_lh, mask_pages, mask_new, k_s.reshape(db, 1, KV_DIM),
                           v_s.reshape(db, 1, KV_DIM), rel_t,
                           k_pages_t, v_pages_t, past_len)
    att_s = att_raw[:, :N_ATT_HEADS, :ATT_HEAD_DIM].reshape(db, D_ATT)
    gdn_s, s_new, conv_new = _gdn_sample(a_log[lyr], dt_bias[lyr], qkv_s.reshape(db, 1, 3 * D_GDN), state_conv[lyr],
                                         conv_w[lyr], small_s.reshape(db, 1, LANES), zg_s.reshape(db, 1, D_GDN),
                                         gdn_norm[lyr], state_ssm[lyr])
    y_sample = _outproj(xs, att_s, gdn_s.reshape(db, D_GDN), w_out_bf, norm_final, tm=db,
                        za=za_s).reshape(db, 1, d_model)
    k_sample = k_s.reshape(1, db, 1, N_KV_HEADS, ATT_HEAD_DIM)
    v_sample = v_s.reshape(1, db, 1, N_KV_HEADS, ATT_HEAD_DIM)
    kidx_sample = small_s[:, :IDX_DIM].reshape(1, db, 1, IDX_DIM)

    return (y_prompt, y_sample, k_prompt, v_prompt, kidx_prompt, ssm_prompt, conv_prompt,
            k_sample, v_sample, kidx_sample, s_new[None], conv_new[None])
```

```python
import functools
import math

import numpy as np
import jax
import jax.numpy as jnp
from jax import lax
from jax.experimental import pallas as pl
from jax.experimental.pallas import tpu as pltpu

F32 = jnp.float32
BF16 = jnp.bfloat16
I32 = jnp.int32

N_ATT_HEADS = 8
ATT_HEAD_DIM = 64
N_KV_HEADS = 2
D_ATT = N_ATT_HEADS * ATT_HEAD_DIM
KV_DIM = N_KV_HEADS * ATT_HEAD_DIM
N_IDX_HEADS = 16
IDX_DIM = 64
TOPK_MAX = 256
N_GDN_HEADS = 4
GDN_HEAD_DIM = 128
D_GDN = N_GDN_HEADS * GDN_HEAD_DIM
CONV_W = 4
N_BUCKETS = 32
MAX_DISTANCE = 128
Q_BLOCK = 128
RMS_EPS = 1e-6
NEG_BIG = -1e30
PROJ_SIZES = (D_ATT, KV_DIM, KV_DIM, D_ATT, N_IDX_HEADS * IDX_DIM, IDX_DIM, N_IDX_HEADS,
              3 * D_GDN, D_GDN, N_GDN_HEADS, N_GDN_HEADS)

LANES = 128
SUBLANES = 8
VMEM_LIMIT = 56 * 1024 * 1024

OFF_Q = 0
OFF_K = OFF_Q + D_ATT
OFF_V = OFF_K + KV_DIM
OFF_ZA = OFF_V + KV_DIM
OFF_QI = OFF_ZA + D_ATT
OFF_SM = OFF_QI + N_IDX_HEADS * IDX_DIM
OFF_QKV = OFF_SM + LANES
OFF_ZG = OFF_QKV + 3 * D_GDN
D_PROJ_PAD = OFF_ZG + D_GDN
SM_WI = IDX_DIM
SM_AG = SM_WI + N_IDX_HEADS
SM_BG = SM_AG + N_GDN_HEADS

PROJ_ROWS = 512
GDN_C = 128
GDN_CHUNKS_PER_ITER = 4
KEY_NEG_BIG = int(np.array(NEG_BIG, np.float32).view(np.int32)) ^ 0x7FFFFFFF
INT_MIN = -2 ** 31


def _cparams(sem):
    return pltpu.CompilerParams(dimension_semantics=sem, vmem_limit_bytes=VMEM_LIMIT)


def _silu(x):
    return x * (1.0 / (1.0 + jnp.exp(-x)))


def _bdot(a, b):
    return jnp.dot(a.astype(BF16), b.astype(BF16), preferred_element_type=F32)


def _bdot_nt(a, b):
    return lax.dot_general(a.astype(BF16), b.astype(BF16), (((1,), (1,)), ((), ())),
                           preferred_element_type=F32)


def _sort_key(x):
    i = pltpu.bitcast(x, I32)
    return jnp.where(i < 0, i ^ 0x7FFFFFFF, i)


def _inproj_body(x_ref, g_ref, w_ref, q_ref, k_ref, v_ref, za_ref, qi_ref, sm_ref, qkv_ref, zg_ref, *t_refs,
                 blocked):
    x = x_ref[...]
    ms = jnp.mean(x * x, axis=-1, keepdims=True)
    h = ((x * lax.rsqrt(ms + RMS_EPS)) * g_ref[...]).astype(BF16)

    def mm(a, b):
        return jnp.dot(h, w_ref[:, a:b], preferred_element_type=F32)

    q = mm(OFF_Q, OFF_K) * (ATT_HEAD_DIM ** -0.5)
    qi = mm(OFF_QI, OFF_SM) * (IDX_DIM ** -0.5)
    if blocked:
        for r in range(x.shape[0] // Q_BLOCK):
            rs = slice(r * Q_BLOCK, (r + 1) * Q_BLOCK)
            for j in range(D_ATT // LANES):
                cs = slice(j * LANES, (j + 1) * LANES)
                q_ref[r, :, cs] = q[rs, cs].T.astype(BF16)
            for j in range(N_IDX_HEADS * IDX_DIM // LANES):
                cs = slice(j * LANES, (j + 1) * LANES)
                qi_ref[r, :, cs] = qi[rs, cs].T.astype(BF16)
    else:
        q_ref[...] = q.astype(BF16)
        qi_ref[...] = qi.astype(BF16)
    k = mm(OFF_K, OFF_V)
    v = mm(OFF_V, OFF_ZA)
    sm = mm(OFF_SM, OFF_QKV)
    k_ref[...] = k
    v_ref[...] = v
    sm_ref[...] = sm
    za_ref[...] = mm(OFF_ZA, OFF_QI)
    qkv_ref[...] = mm(OFF_QKV, OFF_ZG)
    zg_ref[...] = mm(OFF_ZG, D_PROJ_PAD)
    if t_refs:
        kt_ref, vt_ref, kit_ref = t_refs
        kt_ref[...] = k.T
        vt_ref[...] = v.T
        kit_ref[...] = sm.T[0:IDX_DIM, :]


def _prep_w_in(w):
    splits = np.cumsum(PROJ_SIZES)[:-1].tolist()
    q, k, v, z_a, qi, ki, wi, qkv, z_g, a_g, b_g = jnp.split(w, splits, axis=1)
    pad = jnp.zeros((w.shape[0], LANES - (SM_BG + N_GDN_HEADS)), w.dtype)
    small = jnp.concatenate([ki, wi, a_g, b_g, pad], axis=1)
    return jnp.concatenate([q, k, v, z_a, qi, small, qkv, z_g], axis=1).astype(BF16)


def _inproj(x2d, norm_g, w_pad, tm, blocked, seq=None):
    t, d = x2d.shape
    nblk = t // tm
    t_shapes, t_specs = [], []
    if blocked:
        spb = seq // tm
        for n in (KV_DIM, KV_DIM, IDX_DIM):
            t_shapes.append(jax.ShapeDtypeStruct((t // seq, n, seq), F32))
            t_specs.append(pl.BlockSpec((None, n, tm), lambda i: (i // spb, 0, i % spb)))
        rb = tm // Q_BLOCK
        q_shape = jax.ShapeDtypeStruct((t // Q_BLOCK, LANES, D_ATT), BF16)
        qi_shape = jax.ShapeDtypeStruct((t // Q_BLOCK, LANES, N_IDX_HEADS * IDX_DIM), BF16)
        q_spec = pl.BlockSpec((rb, LANES, D_ATT), lambda i: (i, 0, 0))
        qi_spec = pl.BlockSpec((rb, LANES, N_IDX_HEADS * IDX_DIM), lambda i: (i, 0, 0))
    else:
        q_shape = jax.ShapeDtypeStruct((t, D_ATT), BF16)
        qi_shape = jax.ShapeDtypeStruct((t, N_IDX_HEADS * IDX_DIM), BF16)
        q_spec = pl.BlockSpec((tm, D_ATT), lambda i: (i, 0))
        qi_spec = pl.BlockSpec((tm, N_IDX_HEADS * IDX_DIM), lambda i: (i, 0))

    def row(n):
        return pl.BlockSpec((tm, n), lambda i: (i, 0))

    return pl.pallas_call(
        functools.partial(_inproj_body, blocked=blocked),
        grid=(nblk,),
        in_specs=[row(d), pl.BlockSpec((1, d), lambda i: (0, 0)),
                  pl.BlockSpec((d, D_PROJ_PAD), lambda i: (0, 0))],
        out_specs=[q_spec, row(KV_DIM), row(KV_DIM), row(D_ATT), qi_spec, row(LANES), row(3 * D_GDN), row(D_GDN)]
                  + t_specs,
        out_shape=[q_shape, jax.ShapeDtypeStruct((t, KV_DIM), F32), jax.ShapeDtypeStruct((t, KV_DIM), F32),
                   jax.ShapeDtypeStruct((t, D_ATT), F32), qi_shape, jax.ShapeDtypeStruct((t, LANES), F32),
                   jax.ShapeDtypeStruct((t, 3 * D_GDN), F32), jax.ShapeDtypeStruct((t, D_GDN), F32)] + t_shapes,
        compiler_params=_cparams(("parallel",)),
        name="inproj_blocked" if blocked else "inproj_rows",
    )(x2d, norm_g.reshape(1, d), w_pad)


def _rel_bucket(dist):
    n = jnp.maximum(dist, 0)
    max_exact = N_BUCKETS // 2
    nf = jnp.maximum(n, 1).astype(F32)
    large = max_exact + (jnp.log(nf / max_exact) / math.log(MAX_DISTANCE / max_exact)
                         * (N_BUCKETS - max_exact)).astype(I32)
    large = jnp.minimum(large, N_BUCKETS - 1)
    return jnp.where(n < max_exact, n, large)


def _far_bucket_checked(first_far):
    d = np.arange(first_far, 1 << 16, dtype=np.float32)
    b = 16 + (np.log(d / 16) / math.log(MAX_DISTANCE / 16) * 16).astype(np.int32)
    assert int(b.min()) >= N_BUCKETS - 1
    return N_BUCKETS - 1


def _kth_key_search(count_ge, shape, k, total):
    def body(step, carry):
        t, cnt_t = carry
        cand = t + jnp.left_shift(jnp.int32(1), 31 - step)
        cnt = count_ge(cand)
        accept = cnt >= k
        return jnp.where(accept, cand, t), jnp.where(accept, cnt, cnt_t)

    return lax.fori_loop(0, 32, body, (jnp.full(shape, INT_MIN, I32), jnp.full(shape, total, F32)))


KT = 2 * LANES


def _fold8(x, op):
    binop = {jnp.sum: jnp.add, jnp.max: jnp.maximum}[op]
    r = x.reshape(x.shape[0] // SUBLANES, SUBLANES, x.shape[1])
    while r.shape[0] > 1:
        half = r.shape[0] // 2
        r = binop(r[:half], r[half:])
    return r[0]


def _dsa_prompt_t_body(rel_ref, q_ref, qi_ref, smq_ref, za_ref, k_ref, v_ref, sms_ref, o_ref,
                       kk_ref, vvt_ref, ki_ref, bias_ref, key_ref, mask_ref, lg_ref,
                       *, seq, topk):
    b = pl.program_id(0)
    i = pl.program_id(1)
    n_t = seq // KT
    hpg = N_ATT_HEADS // N_KV_HEADS
    far_bucket = _far_bucket_checked(MAX_DISTANCE + 1)
    sub1 = lax.broadcasted_iota(I32, (LANES, LANES), 0)
    lane1 = lax.broadcasted_iota(I32, (LANES, LANES), 1)

    @pl.when((b == 0) & (i == 0))
    def _():
        for dt in range(2):
            bucket = _rel_bucket(dt * LANES + lane1 - sub1)
            for h in range(N_ATT_HEADS):
                acc = jnp.zeros((LANES, LANES), F32)
                for bk in range(N_BUCKETS):
                    acc = jnp.where(bucket == bk, rel_ref[bk, h], acc)
                bias_ref[h, dt] = acc - rel_ref[far_bucket, h]

    @pl.when(i == 0)
    def _():
        lo = lax.broadcasted_iota(I32, (seq, LANES), 1) < ATT_HEAD_DIM
        kf = k_ref[...]
        g0 = jnp.where(lo, kf, 0.0)
        g1 = jnp.where(lo, 0.0, kf)
        kk_ref[0] = g0.astype(BF16)
        kk_ref[1] = pltpu.roll(g0, ATT_HEAD_DIM, 1).astype(BF16)
        kk_ref[2] = pltpu.roll(g1, ATT_HEAD_DIM, 1).astype(BF16)
        kk_ref[3] = g1.astype(BF16)
        c0 = jnp.where(lo, sms_ref[...], 0.0)
        ki_ref[0] = c0.astype(BF16)
        ki_ref[1] = pltpu.roll(c0, IDX_DIM, 1).astype(BF16)
        lo_t = lax.broadcasted_iota(I32, (KT, LANES), 1) < ATT_HEAD_DIM
        for t in range(n_t):
            vf = v_ref[t * KT:(t + 1) * KT, :]
            w0 = jnp.where(lo_t, vf, 0.0)
            w1 = jnp.where(lo_t, 0.0, vf)
            vvt_ref[0, t] = w0.T.astype(BF16)
            vvt_ref[1, t] = pltpu.roll(w0, ATT_HEAD_DIM, 1).T.astype(BF16)
            vvt_ref[2, t] = pltpu.roll(w1, ATT_HEAD_DIM, 1).T.astype(BF16)
            vvt_ref[3, t] = w1.T.astype(BF16)

    n_ip = N_IDX_HEADS * IDX_DIM // LANES
    sm_t = smq_ref[...].T
    wrow = [sm_t[SM_WI + h:SM_WI + h + 1, :] * (N_IDX_HEADS ** -0.5) for h in range(N_IDX_HEADS)]

    n_live = i // 2 + 1
    n_dead = ((n_t - n_live) * KT).astype(F32)
    kidx = lax.broadcasted_iota(I32, (KT, LANES), 0)
    qpos = i * Q_BLOCK + lax.broadcasted_iota(I32, (KT, LANES), 1)

    def causal_of(t):
        return (t * KT + kidx) <= qpos

    def tile_rows(t):
        return pl.ds(pl.multiple_of(t * KT, KT), KT)

    def score_tile(t, carry):
        acc = jnp.zeros((KT, LANES), F32)
        for half in range(2):
            s_all = jnp.dot(ki_ref[half, tile_rows(t), :], qi_ref[...], preferred_element_type=F32)
            for j in range(n_ip):
                acc = acc + jnp.maximum(s_all[:, j * LANES:(j + 1) * LANES], 0.0) * wrow[2 * j + half]
        acc = acc + 0.0
        key_ref[t] = _sort_key(jnp.where(causal_of(t), acc, NEG_BIG))
        return carry

    def pair_loop(body, init):
        return lax.fori_loop(0, (n_live + 1) // 2, lambda p, c: body(2 * p + 1, body(2 * p, c)), init)

    pair_loop(score_tile, 0)

    @pl.when(n_live % 2 == 1)
    def _():
        key_ref[n_live] = jnp.full((KT, LANES), INT_MIN, I32)
        mask_ref[n_live] = jnp.full((KT, LANES), NEG_BIG, F32)

    slabs_per_tile = KT // LANES

    def causal_slab(t, r):
        return (t * KT + r * LANES + sub1) <= (i * Q_BLOCK + lane1)

    def count_where(pred):
        def body(t, acc):
            kt = key_ref[t]
            for r in range(slabs_per_tile):
                acc = jnp.where(pred(kt[r * LANES:(r + 1) * LANES], t, r), acc + 1.0, acc)
            return acc
        acc = pair_loop(body, jnp.zeros((LANES, LANES), F32))
        return jnp.sum(_fold8(acc, jnp.sum), axis=0, keepdims=True)

    def count_ge(cand):
        return count_where(lambda kt, t, r: kt >= cand) + jnp.where(cand <= KEY_NEG_BIG, n_dead, 0.0)

    def write_threshold_masks(thr):
        def body(t, carry):
            mask_ref[t] = jnp.where((key_ref[t] >= thr) & causal_of(t), 0.0, NEG_BIG)
            return carry
        lax.fori_loop(0, n_live, body, 0)

    few_keys = (i + 1) * Q_BLOCK <= topk

    @pl.when(few_keys)
    def _():
        write_threshold_masks(jnp.full((1, LANES), KEY_NEG_BIG, I32))

    @pl.when(jnp.logical_not(few_keys))
    def _():
        thr, cnt_thr = _kth_key_search(count_ge, (1, LANES), float(topk), float(seq))
        boundary_dup = jnp.max(jnp.where(cnt_thr > float(topk), 1.0, 0.0)) > 0.0

        @pl.when(jnp.logical_not(boundary_dup))
        def _():
            write_threshold_masks(thr)

        @pl.when(boundary_dup)
        def _():
            cnt_gt = count_where(lambda kt, t, r: kt > thr) + jnp.where(thr < KEY_NEG_BIG, n_dead, 0.0)
            need = float(topk) - cnt_gt
            cnt_ceq = count_where(lambda kt, t, r: (kt == thr) & causal_slab(t, r))
            any_tie = jnp.max(jnp.where(cnt_ceq > need, 1.0, 0.0)) > 0.0

            @pl.when(jnp.logical_not(any_tie))
            def _():
                write_threshold_masks(thr)

            @pl.when(any_tie)
            def _():
                rk = lax.broadcasted_iota(I32, (KT, KT), 0)
                ck = lax.broadcasted_iota(I32, (KT, KT), 1)
                lower = (ck <= rk).astype(BF16)

                def body(t, offset):
                    kt = key_ref[t]
                    cz = causal_of(t)
                    eq = ((kt == thr) & cz).astype(F32)
                    prefix = jnp.dot(lower, eq.astype(BF16), preferred_element_type=F32) + offset
                    sel = ((kt > thr) & cz) | ((eq > 0.0) & (prefix <= need))
                    mask_ref[t] = jnp.where(sel, 0.0, NEG_BIG)
                    return offset + jnp.sum(_fold8(eq, jnp.sum), axis=0, keepdims=True)

                lax.fori_loop(0, n_live, body, jnp.zeros((1, LANES), F32))

    near_lo = jnp.maximum(i - 1, 0) // 2
    neg8 = jnp.full((SUBLANES, LANES), NEG_BIG, F32)
    zero8 = jnp.zeros((SUBLANES, LANES), F32)
    def logit_tile(t, mx):
        mx = list(mx)
        far = t < near_lo
        for g in range(N_KV_HEADS):
            qt_g = q_ref[:, g * 2 * LANES:(g + 1) * 2 * LANES]
            for half in range(2):
                l_all = jnp.dot(kk_ref[2 * g + half, tile_rows(t), :], qt_g, preferred_element_type=F32)
                for n in range(2):
                    h = g * hpg + 2 * n + half
                    l = l_all[:, n * LANES:(n + 1) * LANES] + mask_ref[t]
                    lg_ref[h, t] = l
                    mx[h] = jnp.maximum(mx[h], jnp.where(far, _fold8(l, jnp.max), NEG_BIG))
        return tuple(mx)

    mx = pair_loop(logit_tile, (neg8,) * N_ATT_HEADS)
    m_all = []
    for h in range(N_ATT_HEADS):
        r0 = pl.ds(pl.multiple_of((i % 2) * LANES, LANES), LANES)
        l0 = lg_ref[h, i // 2, r0, :] + bias_ref[h, 0]
        lg_ref[h, i // 2, r0, :] = l0
        im1 = jnp.maximum(i - 1, 0)
        r1 = pl.ds(pl.multiple_of((im1 % 2) * LANES, LANES), LANES)
        l1 = lg_ref[h, im1 // 2, r1, :] + jnp.where(i > 0, bias_ref[h, 1], 0.0)
        lg_ref[h, im1 // 2, r1, :] = l1
        l2 = lg_ref[h, near_lo, 0:LANES, :]
        m8 = jnp.maximum(jnp.maximum(mx[h], _fold8(l2, jnp.max)),
                         jnp.maximum(_fold8(l0, jnp.max), _fold8(l1, jnp.max)))
        m_all.append(jnp.max(m8, axis=0, keepdims=True))
    gate = _silu(za_ref[...])

    n_pairs = N_ATT_HEADS // 2

    def pv_tile(t, carry):
        ssum = list(carry[:N_ATT_HEADS])
        acc = list(carry[N_ATT_HEADS:])
        for pr in range(n_pairs):
            g = (2 * pr) // hpg
            for half in range(2):
                h = 2 * pr + half
                p = jnp.exp(lg_ref[h, t] - m_all[h])
                ssum[h] = ssum[h] + _fold8(p, jnp.sum)
                acc[pr] = acc[pr] + jnp.dot(vvt_ref[2 * g + half, t], p.astype(BF16), preferred_element_type=F32)
        return tuple(ssum) + tuple(acc)

    zacc = jnp.zeros((LANES, LANES), F32)
    res = pair_loop(pv_tile, (zero8,) * N_ATT_HEADS + (zacc,) * n_pairs)
    for pr in range(n_pairs):
        l_lo = jnp.sum(res[2 * pr], axis=0, keepdims=True)
        l_hi = jnp.sum(res[2 * pr + 1], axis=0, keepdims=True)
        inv = jnp.where(sub1 < ATT_HEAD_DIM, 1.0 / l_lo, 1.0 / l_hi)
        o_pair = (res[N_ATT_HEADS + pr] * inv).T
        cs = slice(pr * LANES, (pr + 1) * LANES)
        o_ref[:, cs] = (o_pair * gate[:, cs]).astype(BF16)


def _dsa_prompt_t(rel_table, q_blk, qi_blk, small, za, k2d, v2d, batch, seq):
    assert N_KV_HEADS == 2 and N_ATT_HEADS // N_KV_HEADS == 4 and seq % KT == 0
    nb = seq // Q_BLOCK
    n_t = seq // KT
    topk = min(TOPK_MAX, seq // 4)
    assert seq >= 2 * topk
    t = batch * seq
    return pl.pallas_call(
        functools.partial(_dsa_prompt_t_body, seq=seq, topk=topk),
        grid=(batch, nb),
        in_specs=[
            pl.BlockSpec(memory_space=pltpu.SMEM),
            pl.BlockSpec((None, LANES, D_ATT), lambda b, i: (b * nb + i, 0, 0)),
            pl.BlockSpec((None, LANES, N_IDX_HEADS * IDX_DIM), lambda b, i: (b * nb + i, 0, 0)),
            pl.BlockSpec((Q_BLOCK, LANES), lambda b, i: (b * nb + i, 0)),
            pl.BlockSpec((Q_BLOCK, D_ATT), lambda b, i: (b * nb + i, 0)),
            pl.BlockSpec((seq, KV_DIM), lambda b, i: (b, 0)),
            pl.BlockSpec((seq, KV_DIM), lambda b, i: (b, 0)),
            pl.BlockSpec((seq, LANES), lambda b, i: (b, 0)),
        ],
        out_specs=pl.BlockSpec((Q_BLOCK, D_ATT), lambda b, i: (b * nb + i, 0)),
        out_shape=jax.ShapeDtypeStruct((t, D_ATT), BF16),
        scratch_shapes=[
            pltpu.VMEM((2 * N_KV_HEADS, seq, LANES), BF16),
            pltpu.VMEM((2 * N_KV_HEADS, n_t, LANES, KT), BF16),
            pltpu.VMEM((2, seq, LANES), BF16),
            pltpu.VMEM((N_ATT_HEADS, 2, LANES, LANES), F32),
            pltpu.VMEM((n_t, KT, LANES), I32),
            pltpu.VMEM((n_t, KT, LANES), F32),
            pltpu.VMEM((N_ATT_HEADS, n_t, KT, LANES), F32),
        ],
        compiler_params=_cparams(("arbitrary", "arbitrary")),
        name="dsa_prompt",
    )(rel_table, q_blk, qi_blk, small, za, k2d, v2d, small)


def _softplus(x):
    return jnp.maximum(x, 0.0) + jnp.log1p(jnp.exp(-jnp.abs(x)))


def _lane_pick(x, idx):
    lane = lax.broadcasted_iota(I32, x.shape, 1)
    return jnp.sum(jnp.where(lane == idx, x, 0.0), axis=1, keepdims=True)


def _gated_norm(o, gn, z):
    y = o * lax.rsqrt(jnp.mean(o * o, axis=-1, keepdims=True) + RMS_EPS)
    return (y * gn) * _silu(z)


GDN_FEATURE_ROWS = 512


def _gdn_features_body(xq_ref, xk_ref, xv_ref, hq_ref, hk_ref, hv_ref, wq_ref, wk_ref, wv_ref,
                       q_ref, k_ref, v_ref):
    first = pl.program_id(1) == 0
    hist = SUBLANES
    rows = xq_ref.shape[0]
    base = hist - (CONV_W - 1)

    def l2n(x):
        return x * lax.rsqrt(jnp.sum(x * x, axis=-1, keepdims=True) + 1e-6)

    for x_ref, h_ref, w_ref, dst, post in (
            (xq_ref, hq_ref, wq_ref, q_ref, lambda x: l2n(x) * (GDN_HEAD_DIM ** -0.5)),
            (xk_ref, hk_ref, wk_ref, k_ref, l2n),
            (xv_ref, hv_ref, wv_ref, v_ref, lambda x: x)):
        for hh in range(x_ref.shape[1] // LANES):
            hs = slice(hh * LANES, (hh + 1) * LANES)
            win = jnp.concatenate([jnp.where(first, 0.0, h_ref[:, hs]), x_ref[:, hs]], axis=0)
            acc = win[base:base + rows] * w_ref[0:1, hs]
            for j in range(1, CONV_W):
                acc = acc + win[base + j:base + j + rows] * w_ref[j:j + 1, hs]
            dst[:, hs] = post(_silu(acc))


def _gdn_features(qkv2d, conv_w, batch, seq):
    t = batch * seq
    r = GDN_FEATURE_ROWS
    nr = seq // r
    hb = r // SUBLANES
    w = D_GDN
    ng = 1
    tile = lambda off: pl.BlockSpec((r, w), lambda b, j: (b * nr + j, off))
    prev = lambda off: pl.BlockSpec((SUBLANES, w), lambda b, j: (jnp.maximum((b * nr + j) * hb - 1, 0), off))
    wblk = lambda off: pl.BlockSpec((CONV_W, w), lambda b, j: (0, off))
    out = pl.BlockSpec((r, w), lambda b, j: (b * nr + j, 0))
    return pl.pallas_call(
        _gdn_features_body,
        grid=(batch, nr),
        in_specs=[tile(0), tile(ng), tile(2 * ng), prev(0), prev(ng), prev(2 * ng),
                  wblk(0), wblk(ng), wblk(2 * ng)],
        out_specs=[out, out, out],
        out_shape=[jax.ShapeDtypeStruct((t, w), F32)] * 3,
        compiler_params=_cparams(("parallel", "arbitrary")),
        name="gdn_features",
    )(qkv2d, qkv2d, qkv2d, qkv2d, qkv2d, qkv2d, conv_w, conv_w, conv_w)


def _gdn_prompt_body(alog_ref, dtb_ref, q_s, k_s, v_s, sm_ref, zg_ref, gn_ref,
                     o_ref, sfin_ref, g_s, b_s, *, seq, hp):
    h0 = pl.program_id(1) * hp
    slabs = [slice(hh * LANES, (hh + 1) * LANES) for hh in range(hp)]

    lane_row = lax.broadcasted_iota(I32, (1, LANES), 1)
    alog_row = jnp.zeros((1, LANES), F32)
    dtb_row = jnp.zeros((1, LANES), F32)
    for hh in range(hp):
        alog_row = jnp.where(lane_row == SM_AG + h0 + hh, alog_ref[h0 + hh], alog_row)
        dtb_row = jnp.where(lane_row == SM_AG + h0 + hh, dtb_ref[h0 + hh], dtb_row)
    sm = sm_ref[...]
    g_all = -jnp.exp(alog_row) * _softplus(sm + dtb_row)
    beta_all = 1.0 / (1.0 + jnp.exp(-sm))
    for hh, hs in enumerate(slabs):
        g_s[:, hs] = jnp.broadcast_to(_lane_pick(g_all, SM_AG + h0 + hh), (seq, LANES))
        b_s[:, hs] = jnp.broadcast_to(_lane_pick(beta_all, SM_BG + h0 + hh), (seq, LANES))

    c = GDN_C
    ri = lax.broadcasted_iota(I32, (c, c), 0)
    ci = lax.broadcasted_iota(I32, (c, c), 1)
    tril = ri >= ci
    strict = ri > ci
    tril_f = tril.astype(F32)
    eye = (ri == ci).astype(F32)
    gn = gn_ref[...]
    off_masks = []
    for lg in range(int(math.log2(c))):
        same_pair = (ri >> (lg + 1)) == (ci >> (lg + 1))
        off_masks.append(same_pair & (((ri >> lg) & 1) == 1) & (((ci >> lg) & 1) == 0))

    tril_b = tril.astype(BF16)

    def cumsum_rows(g):
        hi = g.astype(BF16)
        r1 = g - hi.astype(F32)
        mid = r1.astype(BF16)
        lo = (r1 - mid.astype(F32)).astype(BF16)
        return sum(jnp.dot(tril_b, piece, preferred_element_type=F32) for piece in (hi, mid, lo))

    cpi = GDN_CHUNKS_PER_ITER

    def rows_of(n):
        return pl.ds(pl.multiple_of(n * c, c), c)

    def local_phase(items):
        ids = range(len(items))
        q = [q_s[rows_of(n), hs] for n, hs in items]
        k = [k_s[rows_of(n), hs] for n, hs in items]
        v = [v_s[rows_of(n), hs] for n, hs in items]
        bb = [b_s[rows_of(n), hs] for n, hs in items]
        gcum = [cumsum_rows(g_s[rows_of(n), hs]) for n, hs in items]
        gcum_row = [g.T for g in gcum]
        decay = [jnp.where(tril, jnp.exp(jnp.where(tril, gcum[e] - gcum_row[e], 0.0)), 0.0) for e in ids]
        eg = [jnp.exp(g) for g in gcum]
        kb = [k[e] * bb[e] for e in ids]
        vb = [v[e] * bb[e] for e in ids]
        kq = [_bdot_nt(jnp.concatenate([kb[e], q[e]], axis=0), k[e]) for e in ids]
        a_mat = [jnp.where(strict, kq[e][:c] * decay[e], 0.0) for e in ids]
        attn = [kq[e][c:] * decay[e] for e in ids]
        x = [eye - jnp.where(off_masks[0], a_mat[e], 0.0) for e in ids]
        for om in off_masks[1:]:
            inner = [_bdot(jnp.where(om, a_mat[e], 0.0), x[e]) for e in ids]
            x = [x[e] - _bdot(x[e], inner[e]) for e in ids]
        uw = [_bdot(x[e], jnp.concatenate([vb[e], kb[e] * eg[e]], axis=1)) for e in ids]
        g_last = [g[c - 1:c, :] for g in gcum]
        k_dec_t = [(k[e] * jnp.exp(g_last[e] - gcum[e])).T for e in ids]
        wq = [jnp.concatenate([uw[e][:, GDN_HEAD_DIM:], q[e] * eg[e]], axis=0) for e in ids]
        ak = [jnp.concatenate([attn[e], k_dec_t[e]], axis=0) for e in ids]
        return [dict(u=uw[e][:, :GDN_HEAD_DIM], wq=wq[e], ak=ak[e], s_dec=jnp.exp(g_last[e])) for e in ids]

    def state_phase(n, loc, states):
        heads = range(hp)
        wq_s = [_bdot(loc[h]["wq"], states[h]) for h in heads]
        v_new = [loc[h]["u"] - wq_s[h][:c] for h in heads]
        ak_v = [_bdot(loc[h]["ak"], v_new[h]) for h in heads]
        for h, hs in enumerate(slabs):
            o = wq_s[h][c:] + ak_v[h][:c]
            o_ref[rows_of(n), hs] = _gated_norm(o, gn, zg_ref[rows_of(n), hs]).astype(BF16)
        return tuple(states[h] * loc[h]["s_dec"] + ak_v[h][c:] for h in heads)

    def chunk_group(p, states):
        ns = [p * cpi + r for r in range(cpi)]
        loc = local_phase([(n, hs) for n in ns for hs in slabs])
        for r, n in enumerate(ns):
            states = state_phase(n, loc[r * hp:(r + 1) * hp], states)
        return states

    zero_state = jnp.zeros((GDN_HEAD_DIM, GDN_HEAD_DIM), F32)
    finals = lax.fori_loop(0, seq // (c * cpi), chunk_group, (zero_state,) * hp)
    for hh in range(hp):
        sfin_ref[hh] = finals[hh]


GDN_HEADS_PER_STEP = 4


def _gdn_prompt(a_log, dt_bias, qkv2d, conv_w, small, zg, gdn_norm, batch, seq):
    nh = N_GDN_HEADS
    hp = GDN_HEADS_PER_STEP
    ng = nh // hp
    t = batch * seq
    w = hp * LANES
    assert ng == 1
    q_f, k_f, v_f = _gdn_features(qkv2d, conv_w, batch, seq)
    once = pl.Buffered(1)
    blk = pl.BlockSpec((seq, w), lambda b, j: (b, j))
    return pl.pallas_call(
        functools.partial(_gdn_prompt_body, seq=seq, hp=hp),
        grid=(batch, ng),
        in_specs=[pl.BlockSpec(memory_space=pltpu.SMEM), pl.BlockSpec(memory_space=pltpu.SMEM),
                  blk, blk, blk,
                  pl.BlockSpec((seq, LANES), lambda b, j: (b, 0)),
                  pl.BlockSpec((seq, w), lambda b, j: (b, j), pipeline_mode=once),
                  pl.BlockSpec((1, LANES), lambda b, j: (0, 0))],
        out_specs=[pl.BlockSpec((seq, w), lambda b, j: (b, j)),
                   pl.BlockSpec((None, hp, GDN_HEAD_DIM, GDN_HEAD_DIM), lambda b, j: (b, j, 0, 0))],
        out_shape=[jax.ShapeDtypeStruct((t, D_GDN), BF16),
                   jax.ShapeDtypeStruct((batch, nh, GDN_HEAD_DIM, GDN_HEAD_DIM), F32)],
        scratch_shapes=[pltpu.VMEM((seq, w), F32)] * 2,
        compiler_params=_cparams(("parallel", "arbitrary")),
        name="gdn_prompt",
    )(a_log, dt_bias, q_f, k_f, v_f, small, zg, gdn_norm.reshape(1, LANES))


def _outproj_body(*refs, gate_att):
    if gate_att:
        x_ref, att_ref, za_ref, gdn_ref, w_ref, g_ref, y_ref = refs
        att = (att_ref[...] * _silu(za_ref[...])).astype(BF16)
    else:
        x_ref, att_ref, gdn_ref, w_ref, g_ref, y_ref = refs
        att = att_ref[...]
    y = x_ref[...] + jnp.dot(att, w_ref[0:D_ATT, :], preferred_element_type=F32) \
        + jnp.dot(gdn_ref[...], w_ref[D_ATT:D_ATT + D_GDN, :], preferred_element_type=F32)
    y = y * lax.rsqrt(jnp.mean(y * y, axis=-1, keepdims=True) + RMS_EPS)
    y_ref[...] = y * g_ref[...]


def _outproj(x2d, att, gdn, w_bf, norm_g, tm, za=None):
    t, d = x2d.shape
    row = lambda n: pl.BlockSpec((tm, n), lambda i: (i, 0))
    full = lambda a, b: pl.BlockSpec((a, b), lambda i: (0, 0))
    ins = [x2d, att] + ([za] if za is not None else []) + [gdn, w_bf, norm_g.reshape(1, d)]
    specs = [row(d), row(D_ATT)] + ([row(D_ATT)] if za is not None else []) + \
            [row(D_GDN), full(D_ATT + D_GDN, d), full(1, d)]
    return pl.pallas_call(
        functools.partial(_outproj_body, gate_att=za is not None),
        grid=(t // tm,),
        in_specs=specs,
        out_specs=row(d),
        out_shape=jax.ShapeDtypeStruct((t, d), F32),
        compiler_params=_cparams(("parallel",)),
        name="outproj_gated" if za is not None else "outproj",
    )(*ins)


IDX_PAGES_PER_STEP = 64
ATT_PAGES_PER_STEP = 64
SAMPLE_Q_ROWS = 16


def _idx_scores_body(pt_ref, qi_ref, wi_ref, *rest):
    page_refs = rest[:IDX_PAGES_PER_STEP]
    o_ref = rest[IDX_PAGES_PER_STEP]
    qi = qi_ref[0]
    wi = wi_ref[0] * (N_IDX_HEADS ** -0.5)
    for j, pr in enumerate(page_refs):
        s = _bdot(qi, pr[...])
        sc = jnp.sum(jnp.maximum(s, 0.0) * wi, axis=0, keepdims=True)
        o_ref[0, j:j + 1, :] = sc + 0.0


def _idx_scores(page_table, qi3, wi3, kidx_pages):
    db, n_pages = page_table.shape
    page = kidx_pages.shape[2]
    pg = IDX_PAGES_PER_STEP

    def page_spec(j):
        return pl.BlockSpec((None, IDX_DIM, page), lambda b, s, pt: (pt[b, s * pg + j], 0, 0))

    grid_spec = pltpu.PrefetchScalarGridSpec(
        num_scalar_prefetch=1,
        grid=(db, n_pages // pg),
        in_specs=[pl.BlockSpec((1, N_IDX_HEADS, IDX_DIM), lambda b, s, pt: (b, 0, 0)),
                  pl.BlockSpec((1, N_IDX_HEADS, 1), lambda b, s, pt: (b, 0, 0))]
                 + [page_spec(j) for j in range(pg)],
        out_specs=pl.BlockSpec((1, pg, page), lambda b, s, pt: (b, s, 0)),
    )
    return pl.pallas_call(
        _idx_scores_body,
        grid_spec=grid_spec,
        out_shape=jax.ShapeDtypeStruct((db, n_pages, page), F32),
        compiler_params=_cparams(("parallel", "arbitrary")),
        name="sample_idx_scores",
    )(page_table, qi3, wi3, *([kidx_pages] * pg))


def _sample_select_body(sc_ref, qi_ref, sm_ref, mask_ref, key_ref, *, n_tiles, topk):
    rows = sc_ref.shape[1]
    lane = lax.broadcasted_iota(I32, (rows, LANES), 1)

    def fill(t, carry):
        key_ref[t] = _sort_key(sc_ref[t])
        return carry

    lax.fori_loop(0, n_tiles, fill, 0)
    sm = sm_ref[...]
    ki_new = sm[:, 0:IDX_DIM].astype(BF16).astype(F32)
    acc = jnp.zeros((rows, 1), F32)
    for hd in range(N_IDX_HEADS):
        qh = qi_ref[:, hd * IDX_DIM:(hd + 1) * IDX_DIM].astype(F32)
        s = jnp.sum(qh * ki_new, axis=1, keepdims=True)
        acc = acc + jnp.maximum(s, 0.0) * (sm[:, SM_WI + hd:SM_WI + hd + 1] * (N_IDX_HEADS ** -0.5))
    sc_new = jnp.broadcast_to(acc + 0.0, (rows, LANES))
    key_ref[n_tiles] = jnp.where(lane == 0, _sort_key(sc_new), INT_MIN)

    def count_where(pred):
        def body(t, a):
            return a + pred(key_ref[t]).astype(F32)
        a = lax.fori_loop(0, n_tiles + 1, body, jnp.zeros((rows, LANES), F32))
        return jnp.sum(a, axis=1, keepdims=True)

    thr, _ = _kth_key_search(lambda cand: count_where(lambda kt: kt >= cand), (rows, 1), float(topk),
                             float(n_tiles * LANES + 1))
    need = float(topk) - count_where(lambda kt: kt > thr)
    cnt_eq = count_where(lambda kt: kt == thr)
    any_tie = jnp.max(jnp.where(cnt_eq > need, 1.0, 0.0)) > 0.0

    @pl.when(jnp.logical_not(any_tie))
    def _():
        def body(t, carry):
            mask_ref[t] = jnp.where(key_ref[t] >= thr, 0.0, NEG_BIG)
            return carry
        lax.fori_loop(0, n_tiles + 1, body, 0)

    @pl.when(any_tie)
    def _():
        sub = lax.broadcasted_iota(I32, (LANES, LANES), 0)
        lane2 = lax.broadcasted_iota(I32, (LANES, LANES), 1)
        upper = (sub <= lane2).astype(BF16)
        ones = jnp.ones((LANES, LANES), BF16)

        def body(t, offset):
            kt = key_ref[t]
            eq = (kt == thr).astype(BF16)
            prefix = jnp.dot(eq, upper, preferred_element_type=F32) + offset
            sel = (kt > thr) | ((eq > 0) & (prefix <= need))
            mask_ref[t] = jnp.where(sel, 0.0, NEG_BIG)
            return offset + jnp.dot(eq, ones, preferred_element_type=F32)

        lax.fori_loop(0, n_tiles + 1, body, jnp.zeros((rows, LANES), F32))


def _sample_select(scores_t, qi2d, small, topk):
    n_tiles, db, page = scores_t.shape
    vm = pl.BlockSpec(memory_space=pltpu.VMEM)
    return pl.pallas_call(
        functools.partial(_sample_select_body, n_tiles=n_tiles, topk=topk),
        in_specs=[vm, vm, vm],
        out_specs=vm,
        out_shape=jax.ShapeDtypeStruct((n_tiles + 1, db, page), F32),
        scratch_shapes=[pltpu.VMEM((n_tiles + 1, db, page), I32)],
        compiler_params=pltpu.CompilerParams(vmem_limit_bytes=VMEM_LIMIT),
        name="sample_select",
    )(scores_t, qi2d, small)


def _sample_attn_body(pt_ref, q_ref, mask_ref, mnew_ref, knew_ref, vnew_ref, relt_ref, *rest, past_len, page):
    pg = ATT_PAGES_PER_STEP
    k_refs = rest[:pg]
    v_refs = rest[pg:2 * pg]
    o_ref, m_s, l_s, acc_s = rest[2 * pg:]
    s = pl.program_id(1)
    n_steps = pl.num_programs(1)
    nh = SAMPLE_Q_ROWS
    far_bucket = _far_bucket_checked(MAX_DISTANCE + 1)

    @pl.when(s == 0)
    def _():
        m_s[...] = jnp.full(m_s.shape, NEG_BIG, F32)
        l_s[...] = jnp.zeros(l_s.shape, F32)
        acc_s[...] = jnp.zeros(acc_s.shape, F32)

    relt = relt_ref[...]
    c_far = relt[:, far_bucket:far_bucket + 1]

    def bias_of(dist):
        bucket = _rel_bucket(dist)
        acc = jnp.zeros((nh, dist.shape[1]), F32)
        for bk in range(N_BUCKETS):
            acc = jnp.where(bucket == bk, relt[:, bk:bk + 1], acc)
        return acc - c_far

    q = q_ref[0].astype(BF16)
    logits = []
    for j in range(pg):
        l = _bdot(q, k_refs[j][...]) + mask_ref[0, j:j + 1, :]
        logits.append(l)
    logits = jnp.concatenate(logits, axis=1)

    kpos = (s * pg) * page + lax.broadcasted_iota(I32, (1, pg * page), 1)
    near = past_len - ((s + 1) * pg * page - 1) <= MAX_DISTANCE
    logits = logits + lax.cond(near, lambda: bias_of(past_len - kpos),
                               lambda: jnp.zeros((nh, pg * page), F32))

    m_old = m_s[...]
    m_new = jnp.maximum(m_old, jnp.max(logits, axis=1, keepdims=True))
    alpha = jnp.exp(m_old - m_new)
    p = jnp.exp(logits - m_new)
    l_new = l_s[...] * alpha + jnp.sum(p, axis=1, keepdims=True)
    acc = acc_s[...] * alpha
    for j in range(pg):
        acc = acc + _bdot_nt(p[:, j * page:(j + 1) * page], v_refs[j][...])
    m_s[...] = m_new
    l_s[...] = l_new
    acc_s[...] = acc

    @pl.when(s == n_steps - 1)
    def _():
        kn = knew_ref[0].astype(BF16).astype(F32)
        vn = vnew_ref[0].astype(BF16).astype(F32)
        ln = jnp.sum(q.astype(F32) * kn, axis=1, keepdims=True) + bias_of(jnp.zeros((1, 1), I32)) \
            + mnew_ref[0][:, 0:1]
        m_fin = jnp.maximum(m_new, ln)
        a2 = jnp.exp(m_new - m_fin)
        pn = jnp.exp(ln - m_fin)
        l_fin = l_new * a2 + pn
        res = (acc * a2 + pn.astype(BF16).astype(F32) * vn) / l_fin
        row = lax.broadcasted_iota(I32, res.shape, 0)
        hpg = N_ATT_HEADS // N_KV_HEADS
        o_ref[0] = jnp.where((row >= hpg) & (row < 2 * hpg), pltpu.roll(res, ATT_HEAD_DIM, 1), res)


def _sample_attn(page_table, q_lh, mask_pages, mask_new, k_new, v_new, rel_t, k_pages, v_pages, past_len):
    db, n_pages = page_table.shape
    page = k_pages.shape[2]
    pg = ATT_PAGES_PER_STEP

    def page_spec(j):
        return pl.BlockSpec((None, KV_DIM, page), lambda b, s, pt: (pt[b, s * pg + j], 0, 0))

    row3 = lambda n: pl.BlockSpec((1, 1, n), lambda b, s, pt: (b, 0, 0))
    grid_spec = pltpu.PrefetchScalarGridSpec(
        num_scalar_prefetch=1,
        grid=(db, n_pages // pg),
        in_specs=[pl.BlockSpec((1, SAMPLE_Q_ROWS, LANES), lambda b, s, pt: (b, 0, 0)),
                  pl.BlockSpec((1, pg, page), lambda b, s, pt: (b, s, 0)),
                  row3(LANES), row3(KV_DIM), row3(KV_DIM),
                  pl.BlockSpec((SAMPLE_Q_ROWS, N_BUCKETS), lambda b, s, pt: (0, 0))]
                 + [page_spec(j) for j in range(pg)] * 2,
        out_specs=pl.BlockSpec((1, SAMPLE_Q_ROWS, LANES), lambda b, s, pt: (b, 0, 0)),
        scratch_shapes=[pltpu.VMEM((SAMPLE_Q_ROWS, 1), F32), pltpu.VMEM((SAMPLE_Q_ROWS, 1), F32),
                        pltpu.VMEM((SAMPLE_Q_ROWS, LANES), F32)],
    )
    return pl.pallas_call(
        functools.partial(_sample_attn_body, past_len=past_len, page=page),
        grid_spec=grid_spec,
        out_shape=jax.ShapeDtypeStruct((db, SAMPLE_Q_ROWS, LANES), F32),
        compiler_params=_cparams(("parallel", "arbitrary")),
        name="sample_attn",
    )(page_table, q_lh, mask_pages, mask_new, k_new, v_new, rel_t, *([k_pages] * pg), *([v_pages] * pg))


def _gdn_sample_body(alog_ref, dtb_ref, x_ref, cst_ref, w_ref, sm_ref, zg_ref, gn_ref, s0_ref,
                     o_ref, s_ref, cnew_ref):
    x = x_ref[0]
    cst = cst_ref[...]
    w = w_ref[...]
    acc = cst[0:1, :] * w[0:1, :]
    for j in range(1, CONV_W - 1):
        acc = acc + cst[j:j + 1, :] * w[j:j + 1, :]
    acc = acc + x * w[CONV_W - 1:CONV_W, :]
    xc = _silu(acc)
    cnew_ref[0:CONV_W - 2, :] = cst[1:CONV_W - 1, :]
    cnew_ref[CONV_W - 2:CONV_W - 1, :] = x
    sm = sm_ref[0]
    gn = gn_ref[...]
    d = GDN_HEAD_DIM
    for h in range(N_GDN_HEADS):
        q = xc[:, h * d:(h + 1) * d]
        k = xc[:, D_GDN + h * d:D_GDN + (h + 1) * d]
        v = xc[:, 2 * D_GDN + h * d:2 * D_GDN + (h + 1) * d]
        q = q * lax.rsqrt(jnp.sum(q * q, axis=-1, keepdims=True) + 1e-6) * (d ** -0.5)
        k = k * lax.rsqrt(jnp.sum(k * k, axis=-1, keepdims=True) + 1e-6)
        a_neg = -jnp.exp(jnp.zeros((1, 1), F32) + alog_ref[h])
        g = a_neg * _softplus(sm[:, SM_AG + h:SM_AG + h + 1] + dtb_ref[h])
        beta = 1.0 / (1.0 + jnp.exp(-sm[:, SM_BG + h:SM_BG + h + 1]))
        st = s0_ref[h] * jnp.exp(g)
        k_col = jnp.broadcast_to(k, (d, d)).T
        q_col = jnp.broadcast_to(q, (d, d)).T
        kv = jnp.sum(k_col * st, axis=0, keepdims=True)
        delta = (v - kv) * beta
        st = st + k_col * delta
        s_ref[h] = st
        o = jnp.sum(q_col * st, axis=0, keepdims=True)
        o_ref[0, :, h * d:(h + 1) * d] = _gated_norm(o, gn, zg_ref[0][:, h * d:(h + 1) * d]).astype(BF16)


def _gdn_sample(a_log, dt_bias, qkv3, state_conv_l, conv_w, small3, zg3, gdn_norm, state_ssm_l):
    db = qkv3.shape[0]
    d = GDN_HEAD_DIM
    nh = N_GDN_HEADS
    row3 = lambda n: pl.BlockSpec((1, 1, n), lambda b: (b, 0, 0))
    return pl.pallas_call(
        _gdn_sample_body,
        grid=(db,),
        in_specs=[pl.BlockSpec(memory_space=pltpu.SMEM), pl.BlockSpec(memory_space=pltpu.SMEM),
                  row3(3 * D_GDN),
                  pl.BlockSpec((None, CONV_W - 1, 3 * D_GDN), lambda b: (b, 0, 0)),
                  pl.BlockSpec((CONV_W, 3 * D_GDN), lambda b: (0, 0)),
                  row3(LANES), row3(D_GDN),
                  pl.BlockSpec((1, LANES), lambda b: (0, 0)),
                  pl.BlockSpec((None, nh, d, d), lambda b: (b, 0, 0, 0))],
        out_specs=[row3(D_GDN),
                   pl.BlockSpec((None, nh, d, d), lambda b: (b, 0, 0, 0)),
                   pl.BlockSpec((None, CONV_W - 1, 3 * D_GDN), lambda b: (b, 0, 0))],
        out_shape=[jax.ShapeDtypeStruct((db, 1, D_GDN), BF16),
                   jax.ShapeDtypeStruct((db, nh, d, d), F32),
                   jax.ShapeDtypeStruct((db, CONV_W - 1, 3 * D_GDN), F32)],
        compiler_params=_cparams(("parallel",)),
        name="gdn_sample",
    )(a_log, dt_bias, qkv3, state_conv_l, conv_w, small3, zg3, gdn_norm.reshape(1, LANES), state_ssm_l)


def kernel(x_prompt, x_sample, cache_k, cache_v, cache_kidx, state_ssm, state_conv, page_table, norm_in, w_in,
           conv_w, a_log, dt_bias, gdn_norm, w_out, rel_table, norm_final):
    depth = w_in.shape[0]
    assert depth == 1, "single-layer model"
    batch, seq, d_model = x_prompt.shape
    db, dec_seq, _ = x_sample.shape
    assert dec_seq == 1 and seq % KT == 0 and seq % (GDN_C * GDN_CHUNKS_PER_ITER) == 0 and seq % PROJ_ROWS == 0
    assert (batch * seq) % (2 * PROJ_ROWS) == 0
    n_pool, page = cache_k.shape[1], cache_k.shape[2]
    n_pages = page_table.shape[1]
    past_len = n_pages * page
    assert page == LANES and n_pages % IDX_PAGES_PER_STEP == 0 and n_pages % ATT_PAGES_PER_STEP == 0

    lyr = 0
    w_pad = _prep_w_in(w_in[lyr])
    w_out_bf = w_out[lyr].astype(BF16)

    xp = x_prompt.reshape(batch * seq, d_model)
    q_blk, k2d, v2d, za, qi_blk, small, qkv2d, zg, k_t, v_t, ki_t = _inproj(
        xp, norm_in[lyr], w_pad, tm=PROJ_ROWS, blocked=True, seq=seq)
    att_g = _dsa_prompt_t(rel_table, q_blk, qi_blk, small, za, k2d, v2d, batch, seq)
    gdn_g, s_fin = _gdn_prompt(a_log[lyr], dt_bias[lyr], qkv2d, conv_w[lyr], small, zg, gdn_norm[lyr], batch, seq)
    y_prompt = _outproj(xp, att_g, gdn_g, w_out_bf, norm_final, tm=2 * PROJ_ROWS).reshape(batch, seq, d_model)
    k_prompt = jnp.transpose(k_t.reshape(batch, N_KV_HEADS, ATT_HEAD_DIM, seq), (0, 3, 1, 2))[None]
    v_prompt = jnp.transpose(v_t.reshape(batch, N_KV_HEADS, ATT_HEAD_DIM, seq), (0, 3, 1, 2))[None]
    kidx_prompt = jnp.transpose(ki_t, (0, 2, 1))[None]
    ssm_prompt = s_fin[None]
    conv_prompt = qkv2d.reshape(batch, seq, 3 * D_GDN)[:, seq - (CONV_W - 1):][None]

    xs = x_sample.reshape(db, d_model)
    q_s, k_s, v_s, za_s, qi_s, small_s, qkv_s, zg_s = _inproj(xs, norm_in[lyr], w_pad, tm=db, blocked=False)
    kidx_t = jnp.transpose(cache_kidx[lyr], (0, 2, 1))
    k_pages_t = jnp.transpose(cache_k[lyr], (0, 2, 3, 1)).reshape(n_pool, KV_DIM, page)
    v_pages_t = jnp.transpose(cache_v[lyr], (0, 2, 3, 1)).reshape(n_pool, KV_DIM, page)
    scores = _idx_scores(page_table, qi_s.reshape(db, N_IDX_HEADS, IDX_DIM),
                         small_s[:, SM_WI:SM_WI + N_IDX_HEADS].reshape(db, N_IDX_HEADS, 1),
                         kidx_t)
    topk = min(TOPK_MAX, (past_len + dec_seq) // 4)
    mask_t = _sample_select(jnp.transpose(scores, (1, 0, 2)), qi_s, small_s, topk)
    mask_pages = jnp.transpose(mask_t[:n_pages], (1, 0, 2))
    mask_new = mask_t[n_pages].reshape(db, 1, page)
    hpg = N_ATT_HEADS // N_KV_HEADS
    q8 = q_s.reshape(db, N_ATT_HEADS, ATT_HEAD_DIM).astype(F32)
    zq = jnp.zeros((db, hpg, ATT_HEAD_DIM), F32)
    q_lh = jnp.concatenate([jnp.concatenate([q8[:, :hpg], zq], axis=2),
                            jnp.concatenate([zq, q8[:, hpg:]], axis=2),
                            jnp.zeros((db, SAMPLE_Q_ROWS - N_ATT_HEADS, LANES), F32)], axis=1)
    rel_t = jnp.concatenate([rel_table.T, jnp.zeros((SAMPLE_Q_ROWS - N_ATT_HEADS, N_BUCKETS), F32)], axis=0)
    att_raw = _sample_attn(page_table, q_lh, mask_pages, mask_new, k_s.reshape(db, 1, KV_DIM),
                           v_s.reshape(db, 1, KV_DIM), rel_t,
                           k_pages_t, v_pages_t, past_len)
    att_s = att_raw[:, :N_ATT_HEADS, :ATT_HEAD_DIM].reshape(db, D_ATT)
    gdn_s, s_new, conv_new = _gdn_sample(a_log[lyr], dt_bias[lyr], qkv_s.reshape(db, 1, 3 * D_GDN), state_conv[lyr],
                                         conv_w[lyr], small_s.reshape(db, 1, LANES), zg_s.reshape(db, 1, D_GDN),
                                         gdn_norm[lyr], state_ssm[lyr])
    y_sample = _outproj(xs, att_s, gdn_s.reshape(db, D_GDN), w_out_bf, norm_final, tm=db,
                        za=za_s).reshape(db, 1, d_model)
    k_sample = k_s.reshape(1, db, 1, N_KV_HEADS, ATT_HEAD_DIM)
    v_sample = v_s.reshape(1, db, 1, N_KV_HEADS, ATT_HEAD_DIM)
    kidx_sample = small_s[:, :IDX_DIM].reshape(1, db, 1, IDX_DIM)

    return (y_prompt, y_sample, k_prompt, v_prompt, kidx_prompt, ssm_prompt, conv_prompt,
            k_sample, v_sample, kidx_sample, s_new[None], conv_new[None])
```

```python
import functools
import math

import numpy as np
import jax
import jax.numpy as jnp
from jax import lax
from jax.experimental import pallas as pl
from jax.experimental.pallas import tpu as pltpu

F32 = jnp.float32
BF16 = jnp.bfloat16
I32 = jnp.int32

N_ATT_HEADS = 8
ATT_HEAD_DIM = 64
N_KV_HEADS = 2
D_ATT = N_ATT_HEADS * ATT_HEAD_DIM
KV_DIM = N_KV_HEADS * ATT_HEAD_DIM
N_IDX_HEADS = 16
IDX_DIM = 64
TOPK_MAX = 256
N_GDN_HEADS = 4
GDN_HEAD_DIM = 128
D_GDN = N_GDN_HEADS * GDN_HEAD_DIM
CONV_W = 4
N_BUCKETS = 32
MAX_DISTANCE = 128
Q_BLOCK = 128
RMS_EPS = 1e-6
NEG_BIG = -1e30
PROJ_SIZES = (D_ATT, KV_DIM, KV_DIM, D_ATT, N_IDX_HEADS * IDX_DIM, IDX_DIM, N_IDX_HEADS,
              3 * D_GDN, D_GDN, N_GDN_HEADS, N_GDN_HEADS)

LANES = 128
SUBLANES = 8
VMEM_LIMIT = 56 * 1024 * 1024

OFF_Q = 0
OFF_K = OFF_Q + D_ATT
OFF_V = OFF_K + KV_DIM
OFF_ZA = OFF_V + KV_DIM
OFF_QI = OFF_ZA + D_ATT
OFF_SM = OFF_QI + N_IDX_HEADS * IDX_DIM
OFF_QKV = OFF_SM + LANES
OFF_ZG = OFF_QKV + 3 * D_GDN
D_PROJ_PAD = OFF_ZG + D_GDN
SM_WI = IDX_DIM
SM_AG = SM_WI + N_IDX_HEADS
SM_BG = SM_AG + N_GDN_HEADS

PROJ_ROWS = 512
GDN_C = 128
GDN_CHUNKS_PER_ITER = 4
KEY_NEG_BIG = int(np.array(NEG_BIG, np.float32).view(np.int32)) ^ 0x7FFFFFFF
INT_MIN = -2 ** 31


def _cparams(sem):
    return pltpu.CompilerParams(dimension_semantics=sem, vmem_limit_bytes=VMEM_LIMIT)


def _silu(x):
    return x * (1.0 / (1.0 + jnp.exp(-x)))


def _bdot(a, b):
    return jnp.dot(a.astype(BF16), b.astype(BF16), preferred_element_type=F32)


def _bdot_nt(a, b):
    return lax.dot_general(a.astype(BF16), b.astype(BF16), (((1,), (1,)), ((), ())),
                           preferred_element_type=F32)


def _sort_key(x):
    i = pltpu.bitcast(x, I32)
    return jnp.where(i < 0, i ^ 0x7FFFFFFF, i)


def _inproj_body(x_ref, g_ref, w_ref, q_ref, k_ref, v_ref, za_ref, qi_ref, sm_ref, qkv_ref, zg_ref, *t_refs,
                 blocked):
    x = x_ref[...]
    ms = jnp.mean(x * x, axis=-1, keepdims=True)
    h = ((x * lax.rsqrt(ms + RMS_EPS)) * g_ref[...]).astype(BF16)

    def mm(a, b):
        return jnp.dot(h, w_ref[:, a:b], preferred_element_type=F32)

    q = mm(OFF_Q, OFF_K) * (ATT_HEAD_DIM ** -0.5)
    qi = mm(OFF_QI, OFF_SM) * (IDX_DIM ** -0.5)
    if blocked:
        for r in range(x.shape[0] // Q_BLOCK):
            rs = slice(r * Q_BLOCK, (r + 1) * Q_BLOCK)
            for j in range(D_ATT // LANES):
                cs = slice(j * LANES, (j + 1) * LANES)
                q_ref[r, :, cs] = q[rs, cs].T.astype(BF16)
            for j in range(N_IDX_HEADS * IDX_DIM // LANES):
                cs = slice(j * LANES, (j + 1) * LANES)
                qi_ref[r, :, cs] = qi[rs, cs].T.astype(BF16)
    else:
        q_ref[...] = q.astype(BF16)
        qi_ref[...] = qi.astype(BF16)
    k = mm(OFF_K, OFF_V)
    v = mm(OFF_V, OFF_ZA)
    sm = mm(OFF_SM, OFF_QKV)
    k_ref[...] = k
    v_ref[...] = v
    sm_ref[...] = sm
    za_ref[...] = mm(OFF_ZA, OFF_QI)
    qkv_ref[...] = mm(OFF_QKV, OFF_ZG)
    zg_ref[...] = mm(OFF_ZG, D_PROJ_PAD)
    if t_refs:
        kt_ref, vt_ref, kit_ref = t_refs
        kt_ref[...] = k.T
        vt_ref[...] = v.T
        kit_ref[...] = sm.T[0:IDX_DIM, :]


def _prep_w_in(w):
    splits = np.cumsum(PROJ_SIZES)[:-1].tolist()
    q, k, v, z_a, qi, ki, wi, qkv, z_g, a_g, b_g = jnp.split(w, splits, axis=1)
    pad = jnp.zeros((w.shape[0], LANES - (SM_BG + N_GDN_HEADS)), w.dtype)
    small = jnp.concatenate([ki, wi, a_g, b_g, pad], axis=1)
    return jnp.concatenate([q, k, v, z_a, qi, small, qkv, z_g], axis=1).astype(BF16)


def _inproj(x2d, norm_g, w_pad, tm, blocked, seq=None):
    t, d = x2d.shape
    nblk = t // tm
    t_shapes, t_specs = [], []
    if blocked:
        spb = seq // tm
        for n in (KV_DIM, KV_DIM, IDX_DIM):
            t_shapes.append(jax.ShapeDtypeStruct((t // seq, n, seq), F32))
            t_specs.append(pl.BlockSpec((None, n, tm), lambda i: (i // spb, 0, i % spb)))
        rb = tm // Q_BLOCK
        q_shape = jax.ShapeDtypeStruct((t // Q_BLOCK, LANES, D_ATT), BF16)
        qi_shape = jax.ShapeDtypeStruct((t // Q_BLOCK, LANES, N_IDX_HEADS * IDX_DIM), BF16)
        q_spec = pl.BlockSpec((rb, LANES, D_ATT), lambda i: (i, 0, 0))
        qi_spec = pl.BlockSpec((rb, LANES, N_IDX_HEADS * IDX_DIM), lambda i: (i, 0, 0))
    else:
        q_shape = jax.ShapeDtypeStruct((t, D_ATT), BF16)
        qi_shape = jax.ShapeDtypeStruct((t, N_IDX_HEADS * IDX_DIM), BF16)
        q_spec = pl.BlockSpec((tm, D_ATT), lambda i: (i, 0))
        qi_spec = pl.BlockSpec((tm, N_IDX_HEADS * IDX_DIM), lambda i: (i, 0))

    def row(n):
        return pl.BlockSpec((tm, n), lambda i: (i, 0))

    return pl.pallas_call(
        functools.partial(_inproj_body, blocked=blocked),
        grid=(nblk,),
        in_specs=[row(d), pl.BlockSpec((1, d), lambda i: (0, 0)),
                  pl.BlockSpec((d, D_PROJ_PAD), lambda i: (0, 0))],
        out_specs=[q_spec, row(KV_DIM), row(KV_DIM), row(D_ATT), qi_spec, row(LANES), row(3 * D_GDN), row(D_GDN)]
                  + t_specs,
        out_shape=[q_shape, jax.ShapeDtypeStruct((t, KV_DIM), F32), jax.ShapeDtypeStruct((t, KV_DIM), F32),
                   jax.ShapeDtypeStruct((t, D_ATT), F32), qi_shape, jax.ShapeDtypeStruct((t, LANES), F32),
                   jax.ShapeDtypeStruct((t, 3 * D_GDN), F32), jax.ShapeDtypeStruct((t, D_GDN), F32)] + t_shapes,
        compiler_params=_cparams(("parallel",)),
        name="inproj_blocked" if blocked else "inproj_rows",
    )(x2d, norm_g.reshape(1, d), w_pad)


def _rel_bucket(dist):
    n = jnp.maximum(dist, 0)
    max_exact = N_BUCKETS // 2
    nf = jnp.maximum(n, 1).astype(F32)
    large = max_exact + (jnp.log(nf / max_exact) / math.log(MAX_DISTANCE / max_exact)
                         * (N_BUCKETS - max_exact)).astype(I32)
    large = jnp.minimum(large, N_BUCKETS - 1)
    return jnp.where(n < max_exact, n, large)


def _far_bucket_checked(first_far):
    d = np.arange(first_far, 1 << 16, dtype=np.float32)
    b = 16 + (np.log(d / 16) / math.log(MAX_DISTANCE / 16) * 16).astype(np.int32)
    assert int(b.min()) >= N_BUCKETS - 1
    return N_BUCKETS - 1


def _kth_key_search(count_ge, shape, k, total):
    def body(step, carry):
        t, cnt_t = carry
        cand = t + jnp.left_shift(jnp.int32(1), 31 - step)
        cnt = count_ge(cand)
        accept = cnt >= k
        return jnp.where(accept, cand, t), jnp.where(accept, cnt, cnt_t)

    return lax.fori_loop(0, 32, body, (jnp.full(shape, INT_MIN, I32), jnp.full(shape, total, F32)))


KT = 2 * LANES


def _fold8(x, op):
    binop = {jnp.sum: jnp.add, jnp.max: jnp.maximum}[op]
    r = x.reshape(x.shape[0] // SUBLANES, SUBLANES, x.shape[1])
    while r.shape[0] > 1:
        half = r.shape[0] // 2
        r = binop(r[:half], r[half:])
    return r[0]


def _dsa_prompt_t_body(rel_ref, q_ref, qi_ref, smq_ref, za_ref, k_ref, v_ref, sms_ref, o_ref,
                       kk_ref, vvt_ref, ki_ref, bias_ref, key_ref, mask_ref, lg_ref,
                       *, seq, topk):
    b = pl.program_id(0)
    i = pl.program_id(1)
    n_t = seq // KT
    hpg = N_ATT_HEADS // N_KV_HEADS
    far_bucket = _far_bucket_checked(MAX_DISTANCE + 1)
    sub1 = lax.broadcasted_iota(I32, (LANES, LANES), 0)
    lane1 = lax.broadcasted_iota(I32, (LANES, LANES), 1)

    @pl.when((b == 0) & (i == 0))
    def _():
        for dt in range(2):
            bucket = _rel_bucket(dt * LANES + lane1 - sub1)
            for h in range(N_ATT_HEADS):
                acc = jnp.zeros((LANES, LANES), F32)
                for bk in range(N_BUCKETS):
                    acc = jnp.where(bucket == bk, rel_ref[bk, h], acc)
                bias_ref[h, dt] = acc - rel_ref[far_bucket, h]

    @pl.when(i == 0)
    def _():
        lo = lax.broadcasted_iota(I32, (seq, LANES), 1) < ATT_HEAD_DIM
        kf = k_ref[...]
        g0 = jnp.where(lo, kf, 0.0)
        g1 = jnp.where(lo, 0.0, kf)
        kk_ref[0] = g0.astype(BF16)
        kk_ref[1] = pltpu.roll(g0, ATT_HEAD_DIM, 1).astype(BF16)
        kk_ref[2] = pltpu.roll(g1, ATT_HEAD_DIM, 1).astype(BF16)
        kk_ref[3] = g1.astype(BF16)
        c0 = jnp.where(lo, sms_ref[...], 0.0)
        ki_ref[0] = c0.astype(BF16)
        ki_ref[1] = pltpu.roll(c0, IDX_DIM, 1).astype(BF16)
        lo_t = lax.broadcasted_iota(I32, (KT, LANES), 1) < ATT_HEAD_DIM
        for t in range(n_t):
            vf = v_ref[t * KT:(t + 1) * KT, :]
            w0 = jnp.where(lo_t, vf, 0.0)
            w1 = jnp.where(lo_t, 0.0, vf)
            vvt_ref[0, t] = w0.T.astype(BF16)
            vvt_ref[1, t] = pltpu.roll(w0, ATT_HEAD_DIM, 1).T.astype(BF16)
            vvt_ref[2, t] = pltpu.roll(w1, ATT_HEAD_DIM, 1).T.astype(BF16)
            vvt_ref[3, t] = w1.T.astype(BF16)

    n_ip = N_IDX_HEADS * IDX_DIM // LANES
    sm_t = smq_ref[...].T
    wrow = [sm_t[SM_WI + h:SM_WI + h + 1, :] * (N_IDX_HEADS ** -0.5) for h in range(N_IDX_HEADS)]

    n_live = i // 2 + 1
    n_dead = ((n_t - n_live) * KT).astype(F32)
    kidx = lax.broadcasted_iota(I32, (KT, LANES), 0)
    qpos = i * Q_BLOCK + lax.broadcasted_iota(I32, (KT, LANES), 1)

    def causal_of(t):
        return (t * KT + kidx) <= qpos

    def tile_rows(t):
        return pl.ds(pl.multiple_of(t * KT, KT), KT)

    def score_tile(t, carry):
        acc = jnp.zeros((KT, LANES), F32)
        for half in range(2):
            s_all = jnp.dot(ki_ref[half, tile_rows(t), :], qi_ref[...], preferred_element_type=F32)
            for j in range(n_ip):
                acc = acc + jnp.maximum(s_all[:, j * LANES:(j + 1) * LANES], 0.0) * wrow[2 * j + half]
        acc = acc + 0.0
        key_ref[t] = _sort_key(jnp.where(causal_of(t), acc, NEG_BIG))
        return carry

    def pair_loop(body, init):
        return lax.fori_loop(0, (n_live + 1) // 2, lambda p, c: body(2 * p + 1, body(2 * p, c)), init)

    pair_loop(score_tile, 0)

    @pl.when(n_live % 2 == 1)
    def _():
        key_ref[n_live] = jnp.full((KT, LANES), INT_MIN, I32)
        mask_ref[n_live] = jnp.full((KT, LANES), NEG_BIG, F32)

    slabs_per_tile = KT // LANES

    def causal_slab(t, r):
        return (t * KT + r * LANES + sub1) <= (i * Q_BLOCK + lane1)

    def count_where(pred):
        def body(t, acc):
            kt = key_ref[t]
            for r in range(slabs_per_tile):
                acc = jnp.where(pred(kt[r * LANES:(r + 1) * LANES], t, r), acc + 1.0, acc)
            return acc
        acc = pair_loop(body, jnp.zeros((LANES, LANES), F32))
        return jnp.sum(_fold8(acc, jnp.sum), axis=0, keepdims=True)

    def count_ge(cand):
        return count_where(lambda kt, t, r: kt >= cand) + jnp.where(cand <= KEY_NEG_BIG, n_dead, 0.0)

    def write_threshold_masks(thr):
        def body(t, carry):
            mask_ref[t] = jnp.where((key_ref[t] >= thr) & causal_of(t), 0.0, NEG_BIG)
            return carry
        lax.fori_loop(0, n_live, body, 0)

    few_keys = (i + 1) * Q_BLOCK <= topk

    @pl.when(few_keys)
    def _():
        write_threshold_masks(jnp.full((1, LANES), KEY_NEG_BIG, I32))

    @pl.when(jnp.logical_not(few_keys))
    def _():
        thr, cnt_thr = _kth_key_search(count_ge, (1, LANES), float(topk), float(seq))
        boundary_dup = jnp.max(jnp.where(cnt_thr > float(topk), 1.0, 0.0)) > 0.0

        @pl.when(jnp.logical_not(boundary_dup))
        def _():
            write_threshold_masks(thr)

        @pl.when(boundary_dup)
        def _():
            cnt_gt = count_where(lambda kt, t, r: kt > thr) + jnp.where(thr < KEY_NEG_BIG, n_dead, 0.0)
            need = float(topk) - cnt_gt
            cnt_ceq = count_where(lambda kt, t, r: (kt == thr) & causal_slab(t, r))
            any_tie = jnp.max(jnp.where(cnt_ceq > need, 1.0, 0.0)) > 0.0

            @pl.when(jnp.logical_not(any_tie))
            def _():
                write_threshold_masks(thr)

            @pl.when(any_tie)
            def _():
                rk = lax.broadcasted_iota(I32, (KT, KT), 0)
                ck = lax.broadcasted_iota(I32, (KT, KT), 1)
                lower = (ck <= rk).astype(BF16)

                def body(t, offset):
                    kt = key_ref[t]
                    cz = causal_of(t)
                    eq = ((kt == thr) & cz).astype(F32)
                    prefix = jnp.dot(lower, eq.astype(BF16), preferred_element_type=F32) + offset
                    sel = ((kt > thr) & cz) | ((eq > 0.0) & (prefix <= need))
                    mask_ref[t] = jnp.where(sel, 0.0, NEG_BIG)
                    return offset + jnp.sum(_fold8(eq, jnp.sum), axis=0, keepdims=True)

                lax.fori_loop(0, n_live, body, jnp.zeros((1, LANES), F32))

    near_lo = jnp.maximum(i - 1, 0) // 2
    neg8 = jnp.full((SUBLANES, LANES), NEG_BIG, F32)
    zero8 = jnp.zeros((SUBLANES, LANES), F32)
    def logit_tile(t, mx):
        mx = list(mx)
        far = t < near_lo
        for g in range(N_KV_HEADS):
            qt_g = q_ref[:, g * 2 * LANES:(g + 1) * 2 * LANES]
            for half in range(2):
                l_all = jnp.dot(kk_ref[2 * g + half, tile_rows(t), :], qt_g, preferred_element_type=F32)
                for n in range(2):
                    h = g * hpg + 2 * n + half
                    l = l_all[:, n * LANES:(n + 1) * LANES] + mask_ref[t]
                    lg_ref[h, t] = l
                    mx[h] = jnp.maximum(mx[h], jnp.where(far, _fold8(l, jnp.max), NEG_BIG))
        return tuple(mx)

    mx = pair_loop(logit_tile, (neg8,) * N_ATT_HEADS)
    m_all = []
    for h in range(N_ATT_HEADS):
        r0 = pl.ds(pl.multiple_of((i % 2) * LANES, LANES), LANES)
        l0 = lg_ref[h, i // 2, r0, :] + bias_ref[h, 0]
        lg_ref[h, i // 2, r0, :] = l0
        im1 = jnp.maximum(i - 1, 0)
        r1 = pl.ds(pl.multiple_of((im1 % 2) * LANES, LANES), LANES)
        l1 = lg_ref[h, im1 // 2, r1, :] + jnp.where(i > 0, bias_ref[h, 1], 0.0)
        lg_ref[h, im1 // 2, r1, :] = l1
        l2 = lg_ref[h, near_lo, 0:LANES, :]
        m8 = jnp.maximum(jnp.maximum(mx[h], _fold8(l2, jnp.max)),
                         jnp.maximum(_fold8(l0, jnp.max), _fold8(l1, jnp.max)))
        m_all.append(jnp.max(m8, axis=0, keepdims=True))
    gate = _silu(za_ref[...])

    n_pairs = N_ATT_HEADS // 2

    def pv_tile(t, carry):
        ssum = list(carry[:N_ATT_HEADS])
        acc = list(carry[N_ATT_HEADS:])
        for pr in range(n_pairs):
            g = (2 * pr) // hpg
            for half in range(2):
                h = 2 * pr + half
                p = jnp.exp(lg_ref[h, t] - m_all[h])
                ssum[h] = ssum[h] + _fold8(p, jnp.sum)
                acc[pr] = acc[pr] + jnp.dot(vvt_ref[2 * g + half, t], p.astype(BF16), preferred_element_type=F32)
        return tuple(ssum) + tuple(acc)

    zacc = jnp.zeros((LANES, LANES), F32)
    res = pair_loop(pv_tile, (zero8,) * N_ATT_HEADS + (zacc,) * n_pairs)
    for pr in range(n_pairs):
        l_lo = jnp.sum(res[2 * pr], axis=0, keepdims=True)
        l_hi = jnp.sum(res[2 * pr + 1], axis=0, keepdims=True)
        inv = jnp.where(sub1 < ATT_HEAD_DIM, 1.0 / l_lo, 1.0 / l_hi)
        o_pair = (res[N_ATT_HEADS + pr] * inv).T
        cs = slice(pr * LANES, (pr + 1) * LANES)
        o_ref[:, cs] = (o_pair * gate[:, cs]).astype(BF16)


def _dsa_prompt_t(rel_table, q_blk, qi_blk, small, za, k2d, v2d, batch, seq):
    assert N_KV_HEADS == 2 and N_ATT_HEADS // N_KV_HEADS == 4 and seq % KT == 0
    nb = seq // Q_BLOCK
    n_t = seq // KT
    topk = min(TOPK_MAX, seq // 4)
    assert seq >= 2 * topk
    t = batch * seq
    return pl.pallas_call(
        functools.partial(_dsa_prompt_t_body, seq=seq, topk=topk),
        grid=(batch, nb),
        in_specs=[
            pl.BlockSpec(memory_space=pltpu.SMEM),
            pl.BlockSpec((None, LANES, D_ATT), lambda b, i: (b * nb + i, 0, 0)),
            pl.BlockSpec((None, LANES, N_IDX_HEADS * IDX_DIM), lambda b, i: (b * nb + i, 0, 0)),
            pl.BlockSpec((Q_BLOCK, LANES), lambda b, i: (b * nb + i, 0)),
            pl.BlockSpec((Q_BLOCK, D_ATT), lambda b, i: (b * nb + i, 0)),
            pl.BlockSpec((seq, KV_DIM), lambda b, i: (b, 0)),
            pl.BlockSpec((seq, KV_DIM), lambda b, i: (b, 0)),
            pl.BlockSpec((seq, LANES), lambda b, i: (b, 0)),
        ],
        out_specs=pl.BlockSpec((Q_BLOCK, D_ATT), lambda b, i: (b * nb + i, 0)),
        out_shape=jax.ShapeDtypeStruct((t, D_ATT), BF16),
        scratch_shapes=[
            pltpu.VMEM((2 * N_KV_HEADS, seq, LANES), BF16),
            pltpu.VMEM((2 * N_KV_HEADS, n_t, LANES, KT), BF16),
            pltpu.VMEM((2, seq, LANES), BF16),
            pltpu.VMEM((N_ATT_HEADS, 2, LANES, LANES), F32),
            pltpu.VMEM((n_t, KT, LANES), I32),
            pltpu.VMEM((n_t, KT, LANES), F32),
            pltpu.VMEM((N_ATT_HEADS, n_t, KT, LANES), F32),
        ],
        compiler_params=_cparams(("arbitrary", "arbitrary")),
        name="dsa_prompt",
    )(rel_table, q_blk, qi_blk, small, za, k2d, v2d, small)


def _softplus(x):
    return jnp.maximum(x, 0.0) + jnp.log1p(jnp.exp(-jnp.abs(x)))


def _lane_pick(x, idx):
    lane = lax.broadcasted_iota(I32, x.shape, 1)
    return jnp.sum(jnp.where(lane == idx, x, 0.0), axis=1, keepdims=True)


def _gated_norm(o, gn, z):
    y = o * lax.rsqrt(jnp.mean(o * o, axis=-1, keepdims=True) + RMS_EPS)
    return (y * gn) * _silu(z)


GDN_FEATURE_ROWS = 512


def _gdn_features_body(xq_ref, xk_ref, xv_ref, hq_ref, hk_ref, hv_ref, wq_ref, wk_ref, wv_ref,
                       q_ref, k_ref, v_ref):
    first = pl.program_id(1) == 0
    hist = SUBLANES
    rows = xq_ref.shape[0]
    base = hist - (CONV_W - 1)

    def l2n(x):
        return x * lax.rsqrt(jnp.sum(x * x, axis=-1, keepdims=True) + 1e-6)

    for x_ref, h_ref, w_ref, dst, post in (
            (xq_ref, hq_ref, wq_ref, q_ref, lambda x: l2n(x) * (GDN_HEAD_DIM ** -0.5)),
            (xk_ref, hk_ref, wk_ref, k_ref, l2n),
            (xv_ref, hv_ref, wv_ref, v_ref, lambda x: x)):
        for hh in range(x_ref.shape[1] // LANES):
            hs = slice(hh * LANES, (hh + 1) * LANES)
            win = jnp.concatenate([jnp.where(first, 0.0, h_ref[:, hs]), x_ref[:, hs]], axis=0)
            acc = win[base:base + rows] * w_ref[0:1, hs]
            for j in range(1, CONV_W):
                acc = acc + win[base + j:base + j + rows] * w_ref[j:j + 1, hs]
            dst[:, hs] = post(_silu(acc))


def _gdn_features(qkv2d, conv_w, batch, seq):
    t = batch * seq
    r = GDN_FEATURE_ROWS
    nr = seq // r
    hb = r // SUBLANES
    w = D_GDN
    ng = 1
    tile = lambda off: pl.BlockSpec((r, w), lambda b, j: (b * nr + j, off))
    prev = lambda off: pl.BlockSpec((SUBLANES, w), lambda b, j: (jnp.maximum((b * nr + j) * hb - 1, 0), off))
    wblk = lambda off: pl.BlockSpec((CONV_W, w), lambda b, j: (0, off))
    out = pl.BlockSpec((r, w), lambda b, j: (b * nr + j, 0))
    return pl.pallas_call(
        _gdn_features_body,
        grid=(batch, nr),
        in_specs=[tile(0), tile(ng), tile(2 * ng), prev(0), prev(ng), prev(2 * ng),
                  wblk(0), wblk(ng), wblk(2 * ng)],
        out_specs=[out, out, out],
        out_shape=[jax.ShapeDtypeStruct((t, w), F32)] * 3,
        compiler_params=_cparams(("parallel", "arbitrary")),
        name="gdn_features",
    )(qkv2d, qkv2d, qkv2d, qkv2d, qkv2d, qkv2d, conv_w, conv_w, conv_w)


def _gdn_prompt_body(alog_ref, dtb_ref, q_s, k_s, v_s, sm_ref, zg_ref, gn_ref,
                     o_ref, sfin_ref, g_s, b_s, *, seq, hp):
    h0 = pl.program_id(1) * hp
    slabs = [slice(hh * LANES, (hh + 1) * LANES) for hh in range(hp)]

    lane_row = lax.broadcasted_iota(I32, (1, LANES), 1)
    alog_row = jnp.zeros((1, LANES), F32)
    dtb_row = jnp.zeros((1, LANES), F32)
    for hh in range(hp):
        alog_row = jnp.where(lane_row == SM_AG + h0 + hh, alog_ref[h0 + hh], alog_row)
        dtb_row = jnp.where(lane_row == SM_AG + h0 + hh, dtb_ref[h0 + hh], dtb_row)
    sm = sm_ref[...]
    g_all = -jnp.exp(alog_row) * _softplus(sm + dtb_row)
    beta_all = 1.0 / (1.0 + jnp.exp(-sm))
    for hh, hs in enumerate(slabs):
        g_s[:, hs] = jnp.broadcast_to(_lane_pick(g_all, SM_AG + h0 + hh), (seq, LANES))
        b_s[:, hs] = jnp.broadcast_to(_lane_pick(beta_all, SM_BG + h0 + hh), (seq, LANES))

    c = GDN_C
    ri = lax.broadcasted_iota(I32, (c, c), 0)
    ci = lax.broadcasted_iota(I32, (c, c), 1)
    tril = ri >= ci
    strict = ri > ci
    tril_f = tril.astype(F32)
    eye = (ri == ci).astype(F32)
    gn = gn_ref[...]
    off_masks = []
    for lg in range(int(math.log2(c))):
        same_pair = (ri >> (lg + 1)) == (ci >> (lg + 1))
        off_masks.append(same_pair & (((ri >> lg) & 1) == 1) & (((ci >> lg) & 1) == 0))

    tril_b = tril.astype(BF16)

    def cumsum_rows(g):
        hi = g.astype(BF16)
        r1 = g - hi.astype(F32)
        mid = r1.astype(BF16)
        lo = (r1 - mid.astype(F32)).astype(BF16)
        return sum(jnp.dot(tril_b, piece, preferred_element_type=F32) for piece in (hi, mid, lo))

    cpi = GDN_CHUNKS_PER_ITER

    def rows_of(n):
        return pl.ds(pl.multiple_of(n * c, c), c)

    def local_phase(items):
        ids = range(len(items))
        q = [q_s[rows_of(n), hs] for n, hs in items]
        k = [k_s[rows_of(n), hs] for n, hs in items]
        v = [v_s[rows_of(n), hs] for n, hs in items]
        bb = [b_s[rows_of(n), hs] for n, hs in items]
        gcum = [cumsum_rows(g_s[rows_of(n), hs]) for n, hs in items]
        gcum_row = [g.T for g in gcum]
        decay = [jnp.where(tril, jnp.exp(jnp.where(tril, gcum[e] - gcum_row[e], 0.0)), 0.0) for e in ids]
        eg = [jnp.exp(g) for g in gcum]
        kb = [k[e] * bb[e] for e in ids]
        vb = [v[e] * bb[e] for e in ids]
        kq = [_bdot_nt(jnp.concatenate([kb[e], q[e]], axis=0), k[e]) for e in ids]
        a_mat = [jnp.where(strict, kq[e][:c] * decay[e], 0.0) for e in ids]
        attn = [kq[e][c:] * decay[e] for e in ids]
        x = [eye - jnp.where(off_masks[0], a_mat[e], 0.0) for e in ids]
        for om in off_masks[1:]:
            inner = [_bdot(jnp.where(om, a_mat[e], 0.0), x[e]) for e in ids]
            x = [x[e] - _bdot(x[e], inner[e]) for e in ids]
        uw = [_bdot(x[e], jnp.concatenate([vb[e], kb[e] * eg[e]], axis=1)) for e in ids]
        g_last = [g[c - 1:c, :] for g in gcum]
        k_dec_t = [(k[e] * jnp.exp(g_last[e] - gcum[e])).T for e in ids]
        wq = [jnp.concatenate([uw[e][:, GDN_HEAD_DIM:], q[e] * eg[e]], axis=0) for e in ids]
        ak = [jnp.concatenate([attn[e], k_dec_t[e]], axis=0) for e in ids]
        return [dict(u=uw[e][:, :GDN_HEAD_DIM], wq=wq[e], ak=ak[e], s_dec=jnp.exp(g_last[e])) for e in ids]

    def state_phase(n, loc, states):
        heads = range(hp)
        wq_s = [_bdot(loc[h]["wq"], states[h]) for h in heads]
        v_new = [loc[h]["u"] - wq_s[h][:c] for h in heads]
        ak_v = [_bdot(loc[h]["ak"], v_new[h]) for h in heads]
        for h, hs in enumerate(slabs):
            o = wq_s[h][c:] + ak_v[h][:c]
            o_ref[rows_of(n), hs] = _gated_norm(o, gn, zg_ref[rows_of(n), hs]).astype(BF16)
        return tuple(states[h] * loc[h]["s_dec"] + ak_v[h][c:] for h in heads)

    def chunk_group(p, states):
        ns = [p * cpi + r for r in range(cpi)]
        loc = local_phase([(n, hs) for n in ns for hs in slabs])
        for r, n in enumerate(ns):
            states = state_phase(n, loc[r * hp:(r + 1) * hp], states)
        return states

    zero_state = jnp.zeros((GDN_HEAD_DIM, GDN_HEAD_DIM), F32)
    finals = lax.fori_loop(0, seq // (c * cpi), chunk_group, (zero_state,) * hp)
    for hh in range(hp):
        sfin_ref[hh] = finals[hh]


GDN_HEADS_PER_STEP = 4


def _gdn_prompt(a_log, dt_bias, qkv2d, conv_w, small, zg, gdn_norm, batch, seq):
    nh = N_GDN_HEADS
    hp = GDN_HEADS_PER_STEP
    ng = nh // hp
    t = batch * seq
    w = hp * LANES
    assert ng == 1
    q_f, k_f, v_f = _gdn_features(qkv2d, conv_w, batch, seq)
    blk = pl.BlockSpec((seq, w), lambda b, j: (b, j))
    return pl.pallas_call(
        functools.partial(_gdn_prompt_body, seq=seq, hp=hp),
        grid=(batch, ng),
        in_specs=[pl.BlockSpec(memory_space=pltpu.SMEM), pl.BlockSpec(memory_space=pltpu.SMEM),
                  blk, blk, blk,
                  pl.BlockSpec((seq, LANES), lambda b, j: (b, 0)),
                  pl.BlockSpec((seq, w), lambda b, j: (b, j)),
                  pl.BlockSpec((1, LANES), lambda b, j: (0, 0))],
        out_specs=[pl.BlockSpec((seq, w), lambda b, j: (b, j)),
                   pl.BlockSpec((None, hp, GDN_HEAD_DIM, GDN_HEAD_DIM), lambda b, j: (b, j, 0, 0))],
        out_shape=[jax.ShapeDtypeStruct((t, D_GDN), BF16),
                   jax.ShapeDtypeStruct((batch, nh, GDN_HEAD_DIM, GDN_HEAD_DIM), F32)],
        scratch_shapes=[pltpu.VMEM((seq, w), F32)] * 2,
        compiler_params=_cparams(("parallel", "arbitrary")),
        name="gdn_prompt",
    )(a_log, dt_bias, q_f, k_f, v_f, small, zg, gdn_norm.reshape(1, LANES))


def _outproj_body(*refs, gate_att):
    if gate_att:
        x_ref, att_ref, za_ref, gdn_ref, w_ref, g_ref, y_ref = refs
        att = (att_ref[...] * _silu(za_ref[...])).astype(BF16)
    else:
        x_ref, att_ref, gdn_ref, w_ref, g_ref, y_ref = refs
        att = att_ref[...]
    y = x_ref[...] + jnp.dot(att, w_ref[0:D_ATT, :], preferred_element_type=F32) \
        + jnp.dot(gdn_ref[...], w_ref[D_ATT:D_ATT + D_GDN, :], preferred_element_type=F32)
    y = y * lax.rsqrt(jnp.mean(y * y, axis=-1, keepdims=True) + RMS_EPS)
    y_ref[...] = y * g_ref[...]


def _outproj(x2d, att, gdn, w_bf, norm_g, tm, za=None):
    t, d = x2d.shape
    row = lambda n: pl.BlockSpec((tm, n), lambda i: (i, 0))
    full = lambda a, b: pl.BlockSpec((a, b), lambda i: (0, 0))
    ins = [x2d, att] + ([za] if za is not None else []) + [gdn, w_bf, norm_g.reshape(1, d)]
    specs = [row(d), row(D_ATT)] + ([row(D_ATT)] if za is not None else []) + \
            [row(D_GDN), full(D_ATT + D_GDN, d), full(1, d)]
    return pl.pallas_call(
        functools.partial(_outproj_body, gate_att=za is not None),
        grid=(t // tm,),
        in_specs=specs,
        out_specs=row(d),
        out_shape=jax.ShapeDtypeStruct((t, d), F32),
        compiler_params=_cparams(("parallel",)),
        name="outproj_gated" if za is not None else "outproj",
    )(*ins)


IDX_PAGES_PER_STEP = 64
ATT_PAGES_PER_STEP = 64
SAMPLE_Q_ROWS = 16


def _idx_scores_body(pt_ref, qi_ref, wi_ref, *rest):
    page_refs = rest[:IDX_PAGES_PER_STEP]
    o_ref = rest[IDX_PAGES_PER_STEP]
    qi = qi_ref[0]
    wi = wi_ref[0] * (N_IDX_HEADS ** -0.5)
    for j, pr in enumerate(page_refs):
        s = _bdot(qi, pr[...])
        sc = jnp.sum(jnp.maximum(s, 0.0) * wi, axis=0, keepdims=True)
        o_ref[0, j:j + 1, :] = sc + 0.0


def _idx_scores(page_table, qi3, wi3, kidx_pages):
    db, n_pages = page_table.shape
    page = kidx_pages.shape[2]
    pg = IDX_PAGES_PER_STEP

    def page_spec(j):
        return pl.BlockSpec((None, IDX_DIM, page), lambda b, s, pt: (pt[b, s * pg + j], 0, 0))

    grid_spec = pltpu.PrefetchScalarGridSpec(
        num_scalar_prefetch=1,
        grid=(db, n_pages // pg),
        in_specs=[pl.BlockSpec((1, N_IDX_HEADS, IDX_DIM), lambda b, s, pt: (b, 0, 0)),
                  pl.BlockSpec((1, N_IDX_HEADS, 1), lambda b, s, pt: (b, 0, 0))]
                 + [page_spec(j) for j in range(pg)],
        out_specs=pl.BlockSpec((1, pg, page), lambda b, s, pt: (b, s, 0)),
    )
    return pl.pallas_call(
        _idx_scores_body,
        grid_spec=grid_spec,
        out_shape=jax.ShapeDtypeStruct((db, n_pages, page), F32),
        compiler_params=_cparams(("parallel", "arbitrary")),
        name="sample_idx_scores",
    )(page_table, qi3, wi3, *([kidx_pages] * pg))


def _sample_select_body(sc_ref, qi_ref, sm_ref, mask_ref, key_ref, *, n_tiles, topk):
    rows = sc_ref.shape[1]
    lane = lax.broadcasted_iota(I32, (rows, LANES), 1)

    def fill(t, carry):
        key_ref[t] = _sort_key(sc_ref[t])
        return carry

    lax.fori_loop(0, n_tiles, fill, 0)
    sm = sm_ref[...]
    ki_new = sm[:, 0:IDX_DIM].astype(BF16).astype(F32)
    acc = jnp.zeros((rows, 1), F32)
    for hd in range(N_IDX_HEADS):
        qh = qi_ref[:, hd * IDX_DIM:(hd + 1) * IDX_DIM].astype(F32)
        s = jnp.sum(qh * ki_new, axis=1, keepdims=True)
        acc = acc + jnp.maximum(s, 0.0) * (sm[:, SM_WI + hd:SM_WI + hd + 1] * (N_IDX_HEADS ** -0.5))
    sc_new = jnp.broadcast_to(acc + 0.0, (rows, LANES))
    key_ref[n_tiles] = jnp.where(lane == 0, _sort_key(sc_new), INT_MIN)

    def count_where(pred):
        def body(t, a):
            return a + pred(key_ref[t]).astype(F32)
        a = lax.fori_loop(0, n_tiles + 1, body, jnp.zeros((rows, LANES), F32))
        return jnp.sum(a, axis=1, keepdims=True)

    thr, _ = _kth_key_search(lambda cand: count_where(lambda kt: kt >= cand), (rows, 1), float(topk),
                             float(n_tiles * LANES + 1))
    need = float(topk) - count_where(lambda kt: kt > thr)
    cnt_eq = count_where(lambda kt: kt == thr)
    any_tie = jnp.max(jnp.where(cnt_eq > need, 1.0, 0.0)) > 0.0

    @pl.when(jnp.logical_not(any_tie))
    def _():
        def body(t, carry):
            mask_ref[t] = jnp.where(key_ref[t] >= thr, 0.0, NEG_BIG)
            return carry
        lax.fori_loop(0, n_tiles + 1, body, 0)

    @pl.when(any_tie)
    def _():
        sub = lax.broadcasted_iota(I32, (LANES, LANES), 0)
        lane2 = lax.broadcasted_iota(I32, (LANES, LANES), 1)
        upper = (sub <= lane2).astype(BF16)
        ones = jnp.ones((LANES, LANES), BF16)

        def body(t, offset):
            kt = key_ref[t]
            eq = (kt == thr).astype(BF16)
            prefix = jnp.dot(eq, upper, preferred_element_type=F32) + offset
            sel = (kt > thr) | ((eq > 0) & (prefix <= need))
            mask_ref[t] = jnp.where(sel, 0.0, NEG_BIG)
            return offset + jnp.dot(eq, ones, preferred_element_type=F32)

        lax.fori_loop(0, n_tiles + 1, body, jnp.zeros((rows, LANES), F32))


def _sample_select(scores_t, qi2d, small, topk):
    n_tiles, db, page = scores_t.shape
    vm = pl.BlockSpec(memory_space=pltpu.VMEM)
    return pl.pallas_call(
        functools.partial(_sample_select_body, n_tiles=n_tiles, topk=topk),
        in_specs=[vm, vm, vm],
        out_specs=vm,
        out_shape=jax.ShapeDtypeStruct((n_tiles + 1, db, page), F32),
        scratch_shapes=[pltpu.VMEM((n_tiles + 1, db, page), I32)],
        compiler_params=pltpu.CompilerParams(vmem_limit_bytes=VMEM_LIMIT),
        name="sample_select",
    )(scores_t, qi2d, small)


def _sample_attn_body(pt_ref, q_ref, mask_ref, mnew_ref, knew_ref, vnew_ref, relt_ref, *rest, past_len, page):
    pg = ATT_PAGES_PER_STEP
    k_refs = rest[:pg]
    v_refs = rest[pg:2 * pg]
    o_ref, m_s, l_s, acc_s = rest[2 * pg:]
    s = pl.program_id(1)
    n_steps = pl.num_programs(1)
    nh = SAMPLE_Q_ROWS
    far_bucket = _far_bucket_checked(MAX_DISTANCE + 1)

    @pl.when(s == 0)
    def _():
        m_s[...] = jnp.full(m_s.shape, NEG_BIG, F32)
        l_s[...] = jnp.zeros(l_s.shape, F32)
        acc_s[...] = jnp.zeros(acc_s.shape, F32)

    relt = relt_ref[...]
    c_far = relt[:, far_bucket:far_bucket + 1]

    def bias_of(dist):
        bucket = _rel_bucket(dist)
        acc = jnp.zeros((nh, dist.shape[1]), F32)
        for bk in range(N_BUCKETS):
            acc = jnp.where(bucket == bk, relt[:, bk:bk + 1], acc)
        return acc - c_far

    q = q_ref[0].astype(BF16)
    logits = []
    for j in range(pg):
        l = _bdot(q, k_refs[j][...]) + mask_ref[0, j:j + 1, :]
        logits.append(l)
    logits = jnp.concatenate(logits, axis=1)

    kpos = (s * pg) * page + lax.broadcasted_iota(I32, (1, pg * page), 1)
    near = past_len - ((s + 1) * pg * page - 1) <= MAX_DISTANCE
    logits = logits + lax.cond(near, lambda: bias_of(past_len - kpos),
                               lambda: jnp.zeros((nh, pg * page), F32))

    m_old = m_s[...]
    m_new = jnp.maximum(m_old, jnp.max(logits, axis=1, keepdims=True))
    alpha = jnp.exp(m_old - m_new)
    p = jnp.exp(logits - m_new)
    l_new = l_s[...] * alpha + jnp.sum(p, axis=1, keepdims=True)
    acc = acc_s[...] * alpha
    for j in range(pg):
        acc = acc + _bdot_nt(p[:, j * page:(j + 1) * page], v_refs[j][...])
    m_s[...] = m_new
    l_s[...] = l_new
    acc_s[...] = acc

    @pl.when(s == n_steps - 1)
    def _():
        kn = knew_ref[0].astype(BF16).astype(F32)
        vn = vnew_ref[0].astype(BF16).astype(F32)
        ln = jnp.sum(q.astype(F32) * kn, axis=1, keepdims=True) + bias_of(jnp.zeros((1, 1), I32)) \
            + mnew_ref[0][:, 0:1]
        m_fin = jnp.maximum(m_new, ln)
        a2 = jnp.exp(m_new - m_fin)
        pn = jnp.exp(ln - m_fin)
        l_fin = l_new * a2 + pn
        res = (acc * a2 + pn.astype(BF16).astype(F32) * vn) / l_fin
        row = lax.broadcasted_iota(I32, res.shape, 0)
        hpg = N_ATT_HEADS // N_KV_HEADS
        o_ref[0] = jnp.where((row >= hpg) & (row < 2 * hpg), pltpu.roll(res, ATT_HEAD_DIM, 1), res)


def _sample_attn(page_table, q_lh, mask_pages, mask_new, k_new, v_new, rel_t, k_pages, v_pages, past_len):
    db, n_pages = page_table.shape
    page = k_pages.shape[2]
    pg = ATT_PAGES_PER_STEP

    def page_spec(j):
        return pl.BlockSpec((None, KV_DIM, page), lambda b, s, pt: (pt[b, s * pg + j], 0, 0))

    row3 = lambda n: pl.BlockSpec((1, 1, n), lambda b, s, pt: (b, 0, 0))
    grid_spec = pltpu.PrefetchScalarGridSpec(
        num_scalar_prefetch=1,
        grid=(db, n_pages // pg),
        in_specs=[pl.BlockSpec((1, SAMPLE_Q_ROWS, LANES), lambda b, s, pt: (b, 0, 0)),
                  pl.BlockSpec((1, pg, page), lambda b, s, pt: (b, s, 0)),
                  row3(LANES), row3(KV_DIM), row3(KV_DIM),
                  pl.BlockSpec((SAMPLE_Q_ROWS, N_BUCKETS), lambda b, s, pt: (0, 0))]
                 + [page_spec(j) for j in range(pg)] * 2,
        out_specs=pl.BlockSpec((1, SAMPLE_Q_ROWS, LANES), lambda b, s, pt: (b, 0, 0)),
        scratch_shapes=[pltpu.VMEM((SAMPLE_Q_ROWS, 1), F32), pltpu.VMEM((SAMPLE_Q_ROWS, 1), F32),
                        pltpu.VMEM((SAMPLE_Q_ROWS, LANES), F32)],
    )
    return pl.pallas_call(
        functools.partial(_sample_attn_body, past_len=past_len, page=page),
        grid_spec=grid_spec,
        out_shape=jax.ShapeDtypeStruct((db, SAMPLE_Q_ROWS, LANES), F32),
        compiler_params=_cparams(("parallel", "arbitrary")),
        name="sample_attn",
    )(page_table, q_lh, mask_pages, mask_new, k_new, v_new, rel_t, *([k_pages] * pg), *([v_pages] * pg))


def _gdn_sample_body(alog_ref, dtb_ref, x_ref, cst_ref, w_ref, sm_ref, zg_ref, gn_ref, s0_ref,
                     o_ref, s_ref, cnew_ref):
    x = x_ref[0]
    cst = cst_ref[...]
    w = w_ref[...]
    acc = cst[0:1, :] * w[0:1, :]
    for j in range(1, CONV_W - 1):
        acc = acc + cst[j:j + 1, :] * w[j:j + 1, :]
    acc = acc + x * w[CONV_W - 1:CONV_W, :]
    xc = _silu(acc)
    cnew_ref[0:CONV_W - 2, :] = cst[1:CONV_W - 1, :]
    cnew_ref[CONV_W - 2:CONV_W - 1, :] = x
    sm = sm_ref[0]
    gn = gn_ref[...]
    d = GDN_HEAD_DIM
    for h in range(N_GDN_HEADS):
        q = xc[:, h * d:(h + 1) * d]
        k = xc[:, D_GDN + h * d:D_GDN + (h + 1) * d]
        v = xc[:, 2 * D_GDN + h * d:2 * D_GDN + (h + 1) * d]
        q = q * lax.rsqrt(jnp.sum(q * q, axis=-1, keepdims=True) + 1e-6) * (d ** -0.5)
        k = k * lax.rsqrt(jnp.sum(k * k, axis=-1, keepdims=True) + 1e-6)
        a_neg = -jnp.exp(jnp.zeros((1, 1), F32) + alog_ref[h])
        g = a_neg * _softplus(sm[:, SM_AG + h:SM_AG + h + 1] + dtb_ref[h])
        beta = 1.0 / (1.0 + jnp.exp(-sm[:, SM_BG + h:SM_BG + h + 1]))
        st = s0_ref[h] * jnp.exp(g)
        k_col = jnp.broadcast_to(k, (d, d)).T
        q_col = jnp.broadcast_to(q, (d, d)).T
        kv = jnp.sum(k_col * st, axis=0, keepdims=True)
        delta = (v - kv) * beta
        st = st + k_col * delta
        s_ref[h] = st
        o = jnp.sum(q_col * st, axis=0, keepdims=True)
        o_ref[0, :, h * d:(h + 1) * d] = _gated_norm(o, gn, zg_ref[0][:, h * d:(h + 1) * d]).astype(BF16)


def _gdn_sample(a_log, dt_bias, qkv3, state_conv_l, conv_w, small3, zg3, gdn_norm, state_ssm_l):
    db = qkv3.shape[0]
    d = GDN_HEAD_DIM
    nh = N_GDN_HEADS
    row3 = lambda n: pl.BlockSpec((1, 1, n), lambda b: (b, 0, 0))
    return pl.pallas_call(
        _gdn_sample_body,
        grid=(db,),
        in_specs=[pl.BlockSpec(memory_space=pltpu.SMEM), pl.BlockSpec(memory_space=pltpu.SMEM),
                  row3(3 * D_GDN),
                  pl.BlockSpec((None, CONV_W - 1, 3 * D_GDN), lambda b: (b, 0, 0)),
                  pl.BlockSpec((CONV_W, 3 * D_GDN), lambda b: (0, 0)),
                  row3(LANES), row3(D_GDN),
                  pl.BlockSpec((1, LANES), lambda b: (0, 0)),
                  pl.BlockSpec((None, nh, d, d), lambda b: (b, 0, 0, 0))],
        out_specs=[row3(D_GDN),
                   pl.BlockSpec((None, nh, d, d), lambda b: (b, 0, 0, 0)),
                   pl.BlockSpec((None, CONV_W - 1, 3 * D_GDN), lambda b: (b, 0, 0))],
        out_shape=[jax.ShapeDtypeStruct((db, 1, D_GDN), BF16),
                   jax.ShapeDtypeStruct((db, nh, d, d), F32),
                   jax.ShapeDtypeStruct((db, CONV_W - 1, 3 * D_GDN), F32)],
        compiler_params=_cparams(("parallel",)),
        name="gdn_sample",
    )(a_log, dt_bias, qkv3, state_conv_l, conv_w, small3, zg3, gdn_norm.reshape(1, LANES), state_ssm_l)


def kernel(x_prompt, x_sample, cache_k, cache_v, cache_kidx, state_ssm, state_conv, page_table, norm_in, w_in,
           conv_w, a_log, dt_bias, gdn_norm, w_out, rel_table, norm_final):
    depth = w_in.shape[0]
    assert depth == 1, "single-layer model"
    batch, seq, d_model = x_prompt.shape
    db, dec_seq, _ = x_sample.shape
    assert dec_seq == 1 and seq % KT == 0 and seq % (GDN_C * GDN_CHUNKS_PER_ITER) == 0 and seq % PROJ_ROWS == 0
    assert (batch * seq) % (2 * PROJ_ROWS) == 0
    n_pool, page = cache_k.shape[1], cache_k.shape[2]
    n_pages = page_table.shape[1]
    past_len = n_pages * page
    assert page == LANES and n_pages % IDX_PAGES_PER_STEP == 0 and n_pages % ATT_PAGES_PER_STEP == 0

    lyr = 0
    w_pad = _prep_w_in(w_in[lyr])
    w_out_bf = w_out[lyr].astype(BF16)

    xp = x_prompt.reshape(batch * seq, d_model)
    q_blk, k2d, v2d, za, qi_blk, small, qkv2d, zg, k_t, v_t, ki_t = _inproj(
        xp, norm_in[lyr], w_pad, tm=PROJ_ROWS, blocked=True, seq=seq)
    att_g = _dsa_prompt_t(rel_table, q_blk, qi_blk, small, za, k2d, v2d, batch, seq)
    gdn_g, s_fin = _gdn_prompt(a_log[lyr], dt_bias[lyr], qkv2d, conv_w[lyr], small, zg, gdn_norm[lyr], batch, seq)
    y_prompt = _outproj(xp, att_g, gdn_g, w_out_bf, norm_final, tm=2 * PROJ_ROWS).reshape(batch, seq, d_model)
    k_prompt = jnp.transpose(k_t.reshape(batch, N_KV_HEADS, ATT_HEAD_DIM, seq), (0, 3, 1, 2))[None]
    v_prompt = jnp.transpose(v_t.reshape(batch, N_KV_HEADS, ATT_HEAD_DIM, seq), (0, 3, 1, 2))[None]
    kidx_prompt = jnp.transpose(ki_t, (0, 2, 1))[None]
    ssm_prompt = s_fin[None]
    conv_prompt = qkv2d.reshape(batch, seq, 3 * D_GDN)[:, seq - (CONV_W - 1):][None]

    xs = x_sample.reshape(db, d_model)
    q_s, k_s, v_s, za_s, qi_s, small_s, qkv_s, zg_s = _inproj(xs, norm_in[lyr], w_pad, tm=db, blocked=False)
    kidx_t = jnp.transpose(cache_kidx[lyr], (0, 2, 1))
    k_pages_t = jnp.transpose(cache_k[lyr], (0, 2, 3, 1)).reshape(n_pool, KV_DIM, page)
    v_pages_t = jnp.transpose(cache_v[lyr], (0, 2, 3, 1)).reshape(n_pool, KV_DIM, page)
    scores = _idx_scores(page_table, qi_s.reshape(db, N_IDX_HEADS, IDX_DIM),
                         small_s[:, SM_WI:SM_WI + N_IDX_HEADS].reshape(db, N_IDX_HEADS, 1),
                         kidx_t)
    topk = min(TOPK_MAX, (past_len + dec_seq) // 4)
    mask_t = _sample_select(jnp.transpose(scores, (1, 0, 2)), qi_s, small_s, topk)
    mask_pages = jnp.transpose(mask_t[:n_pages], (1, 0, 2))
    mask_new = mask_t[n_pages].reshape(db, 1, page)
    hpg = N_ATT_HEADS // N_KV_HEADS
    q8 = q_s.reshape(db, N_ATT_HEADS, ATT_HEAD_DIM).astype(F32)
    zq = jnp.zeros((db, hpg, ATT_HEAD_DIM), F32)
    q_lh = jnp.concatenate([jnp.concatenate([q8[:, :hpg], zq], axis=2),
                            jnp.concatenate([zq, q8[:, hpg:]], axis=2),
                            jnp.zeros((db, SAMPLE_Q_ROWS - N_ATT_HEADS, LANES), F32)], axis=1)
    rel_t = jnp.concatenate([rel_table.T, jnp.zeros((SAMPLE_Q_ROWS - N_ATT_HEADS, N_BUCKETS), F32)], axis=0)
    att_raw = _sample_attn(page_table, q_lh, mask_pages, mask_new, k_s.reshape(db, 1, KV_DIM),
                           v_s.reshape(db, 1, KV_DIM), rel_t,
                           k_pages_t, v_pages_t, past_len)
    att_s = att_raw[:, :N_ATT_HEADS, :ATT_HEAD_DIM].reshape(db, D_ATT)
    gdn_s, s_new, conv_new = _gdn_sample(a_log[lyr], dt_bias[lyr], qkv_s.reshape(db, 1, 3 * D_GDN), state_conv[lyr],
                                         conv_w[lyr], small_s.reshape(db, 1, LANES), zg_s.reshape(db, 1, D_GDN),
                                         gdn_norm[lyr], state_ssm[lyr])
    y_sample = _outproj(xs, att_s, gdn_s.reshape(db, D_GDN), w_out_bf, norm_final, tm=db,
                        za=za_s).reshape(db, 1, d_model)
    k_sample = k_s.reshape(1, db, 1, N_KV_HEADS, ATT_HEAD_DIM)
    v_sample = v_s.reshape(1, db, 1, N_KV_HEADS, ATT_HEAD_DIM)
    kidx_sample = small_s[:, :IDX_DIM].reshape(1, db, 1, IDX_DIM)

    return (y_prompt, y_sample, k_prompt, v_prompt, kidx_prompt, ssm_prompt, conv_prompt,
            k_sample, v_sample, kidx_sample, s_new[None], conv_new[None])
```
